```python
import math
import jax, jax.numpy as jnp
from jax import lax
import numpy as np


D_MODEL = 1024
BATCH = 8
SEQ = 4096
DEPTH = 2

MIX_WIDTH = D_MODEL
A_WIDTH = MIX_WIDTH // 2
B_WIDTH = MIX_WIDTH - A_WIDTH
C_WIDTH = MIX_WIDTH // 2
D_WIDTH = MIX_WIDTH - C_WIDTH
LRU_HEADS = 8
LRU_HEAD_DIM = A_WIDTH // LRU_HEADS
LRU_C = 8.0
CONV_WIDTH = 4
S5_GROUP_CH = 16
S5_GROUPS = B_WIDTH // S5_GROUP_CH
S5_STATE = 64
MLSTM_HEADS = 4
MLSTM_HEAD_DIM = C_WIDTH // MLSTM_HEADS
MLSTM_CHUNK = 64
DSW_HEADS = 4
DSW_HEAD_DIM = D_WIDTH // DSW_HEADS
DSW_CONFIGS = ((128, 1), (512, 4), (2048, 16))
DSW_BLOCK = 128
ROPE_THETA = 500000.0
ROPE_DIMS = DSW_HEAD_DIM // 4
N_EXPERTS = 32
TOP_K = 4
D_EXPERT = D_MODEL
SWIGLU_LIMIT = 7.0
SWIGLU_ALPHA = 1.702
MOE_BLOCK = 128
DEEPNORM_ALPHA = (2 * DEPTH) ** 0.25
DEEPNORM_BETA = (8 * DEPTH) ** -0.25
LN_EPS = 1e-5
N_AB = (DEPTH + 1) // 2
N_CD = DEPTH // 2
AB_IN = 2 * A_WIDTH + B_WIDTH
CD_IN = 2 * C_WIDTH + 3 * D_WIDTH

kernel_name = 'hybrid_rglru_s5_mlstm_dilated_moe'


def _layer_norm(x, g, b):
    xf = x.astype(jnp.float32)
    mu = xf.mean(-1, keepdims=True)
    var = jnp.square(xf - mu).mean(-1, keepdims=True)
    return ((xf - mu) * lax.rsqrt(var + LN_EPS) * g + b).astype(x.dtype)


def _causal_dwconv(x, w, b):
    seq = x.shape[1]
    xp = jnp.pad(x, ((0, 0), (CONV_WIDTH - 1, 0), (0, 0)))
    out = b
    for tap in range(CONV_WIDTH):
        out = out + xp[:, tap:tap + seq] * w[tap]
    return out


def _linear_combine(e1, e2):
    a1, b1 = e1
    a2, b2 = e2
    return (a1 * a2, a2 * b1 + b2)


def _complex_linear_combine(e1, e2):
    a1r, a1i, b1r, b1i = e1
    a2r, a2i, b2r, b2i = e2
    return (a2r * a1r - a2i * a1i, a2r * a1i + a2i * a1r,
            a2r * b1r - a2i * b1i + b2r, a2r * b1i + a2i * b1r + b2i)


def _rg_lru(x, gate_r_w, gate_r_b, gate_i_w, gate_i_b, lru_lambda):
    bsz, seq, width = x.shape
    xh = x.reshape(bsz, seq, LRU_HEADS, LRU_HEAD_DIM)
    r = jax.nn.sigmoid(jnp.einsum('blhi,hij->blhj', xh, gate_r_w).reshape(bsz, seq, width) + gate_r_b)
    i = jax.nn.sigmoid(jnp.einsum('blhi,hij->blhj', xh, gate_i_w).reshape(bsz, seq, width) + gate_i_b)
    log_a = -LRU_C * r.astype(jnp.float32) * jax.nn.softplus(-lru_lambda.astype(jnp.float32))
    a = jnp.exp(log_a)
    u = jnp.sqrt(-jnp.expm1(2.0 * log_a)) * (i * x).astype(jnp.float32)
    _, h = lax.associative_scan(_linear_combine, (a, u), axis=1)
    return h.astype(x.dtype)


def _s5(u, lam_re, lam_im, log_step, b_re, b_im, c_re, c_im, d_skip):
    bsz, seq, width = u.shape
    f32 = jnp.float32
    ug = u.astype(f32).reshape(bsz, seq, S5_GROUPS, S5_GROUP_CH)
    lr = jnp.minimum(lam_re.astype(f32), -1e-4)
    li = lam_im.astype(f32)
    step = jnp.exp(log_step.astype(f32))[:, None]
    mag = jnp.exp(lr * step)
    ar = mag * jnp.cos(li * step)
    ai = mag * jnp.sin(li * step)
    inv = 1.0 / (lr * lr + li * li)
    zr = ((ar - 1.0) * lr + ai * li) * inv
    zi = (ai * lr - (ar - 1.0) * li) * inv
    bbr = zr[..., None] * b_re.astype(f32) - zi[..., None] * b_im.astype(f32)
    bbi = zr[..., None] * b_im.astype(f32) + zi[..., None] * b_re.astype(f32)
    bu_r = jnp.einsum('blgh,gph->blgp', ug, bbr)
    bu_i = jnp.einsum('blgh,gph->blgp', ug, bbi)
    a_r = jnp.broadcast_to(ar, (1, seq) + ar.shape)
    a_i = jnp.broadcast_to(ai, (1, seq) + ai.shape)
    _, _, xr, xi = lax.associative_scan(_complex_linear_combine, (a_r, a_i, bu_r, bu_i), axis=1)
    y = (jnp.einsum('blgp,ghp->blgh', xr, c_re.astype(f32))
         - jnp.einsum('blgp,ghp->blgh', xi, c_im.astype(f32))
         + d_skip.astype(f32) * ug)
    return y.reshape(bsz, seq, width).astype(u.dtype)


def _rglru_s5_mixer(x, w_in, conv_w, conv_b, gate_r_w, gate_r_b, gate_i_w, gate_i_b, lru_lambda,
                    lam_re, lam_im, log_step, b_re, b_im, c_re, c_im, d_skip, glu_w, glu_b, w_out):
    proj = x @ w_in
    xa = proj[..., :A_WIDTH]
    ga = proj[..., A_WIDTH:2 * A_WIDTH]
    ub = proj[..., 2 * A_WIDTH:]
    ya = _rg_lru(_causal_dwconv(xa, conv_w, conv_b), gate_r_w, gate_r_b, gate_i_w, gate_i_b,
                 lru_lambda) * jax.nn.gelu(ga)
    s = jax.nn.gelu(_s5(ub, lam_re, lam_im, log_step, b_re, b_im, c_re, c_im, d_skip))
    yb = s * jax.nn.sigmoid(s @ glu_w + glu_b)
    return jnp.concatenate([ya, yb], axis=-1) @ w_out


def _mlstm_chunkwise(q, k, v, i_pre, f_pre):
    bsz, seq, heads, dk = q.shape
    dv = v.shape[-1]
    cs = MLSTM_CHUNK
    nc = seq // cs
    f32 = jnp.float32
    qc = q.astype(f32).reshape(bsz, nc, cs, heads, dk)
    kc = k.astype(f32).reshape(bsz, nc, cs, heads, dk)
    vc = v.astype(f32).reshape(bsz, nc, cs, heads, dv)
    log_f = jax.nn.log_sigmoid(f_pre.astype(f32)).reshape(bsz, nc, cs, heads).transpose(0, 1, 3, 2)
    log_i = i_pre.astype(f32).reshape(bsz, nc, cs, heads).transpose(0, 1, 3, 2)
    cum_f = jnp.cumsum(log_f, axis=-1)
    chunk_f = cum_f[..., -1]
    to_end = chunk_f[..., None] - cum_f + log_i
    m_chunk = to_end.max(-1)
    w_end = jnp.exp(to_end - m_chunk[..., None])
    kv_chunk = jnp.einsum('bnhc,bnchk,bnchv->bnhkv', w_end, kc, vc)
    n_chunk = jnp.einsum('bnhc,bnchk->bnhk', w_end, kc)

    def step(carry, inp):
        c_st, n_st, m_st = carry
        f_c, m_c, kv_c, nk_c = inp
        m_new = jnp.maximum(f_c + m_st, m_c)
        s_old = jnp.exp(f_c + m_st - m_new)
        s_new = jnp.exp(m_c - m_new)
        c_new = s_old[..., None, None] * c_st + s_new[..., None, None] * kv_c
        n_new = s_old[..., None] * n_st + s_new[..., None] * nk_c
        return (c_new, n_new, m_new), (c_st, n_st, m_st)

    init = (jnp.zeros((bsz, heads, dk, dv), f32), jnp.zeros((bsz, heads, dk), f32),
            jnp.full((bsz, heads), -jnp.inf, f32))
    xs = (chunk_f.transpose(1, 0, 2), m_chunk.transpose(1, 0, 2),
          kv_chunk.transpose(1, 0, 2, 3, 4), n_chunk.transpose(1, 0, 2, 3))
    _, (c_prev, n_prev, m_prev) = lax.scan(step, init, xs)
    c_prev = c_prev.transpose(1, 0, 2, 3, 4)
    n_prev = n_prev.transpose(1, 0, 2, 3)
    m_prev = m_prev.transpose(1, 0, 2)

    causal = jnp.tril(jnp.ones((cs, cs), dtype=bool))
    log_d = jnp.where(causal, cum_f[..., :, None] - cum_f[..., None, :] + log_i[..., None, :], -jnp.inf)
    log_inter = cum_f + m_prev[..., None]
    m_s = jnp.maximum(log_d.max(-1), log_inter)
    scores = jnp.einsum('bnshk,bnjhk->bnhsj', qc, kc) * jnp.exp(log_d - m_s[..., None])
    inter = jnp.exp(log_inter - m_s)
    num = (jnp.einsum('bnhsj,bnjhv->bnhsv', scores, vc)
           + inter[..., None] * jnp.einsum('bnshk,bnhkv->bnhsv', qc, c_prev))
    den = scores.sum(-1) + inter * jnp.einsum('bnshk,bnhk->bnhs', qc, n_prev)
    h = num / jnp.maximum(jnp.abs(den), jnp.exp(-m_s))[..., None]
    return h.transpose(0, 1, 3, 2, 4).reshape(bsz, seq, heads, dv)


def _partial_rope(t, positions):
    half = ROPE_DIMS // 2
    inv_freq = ROPE_THETA ** (-jnp.arange(half, dtype=jnp.float32) / half)
    ang = positions.astype(jnp.float32)[..., None] * inv_freq
    cos = jnp.cos(ang)[:, :, None, :]
    sin = jnp.sin(ang)[:, :, None, :]
    tf = t.astype(jnp.float32)
    t1, t2, rest = tf[..., :half], tf[..., half:ROPE_DIMS], tf[..., ROPE_DIMS:]
    return jnp.concatenate([t1 * cos - t2 * sin, t2 * cos + t1 * sin, rest], axis=-1).astype(t.dtype)


def _strided_band_attention(q, k, v, span, dil):
    bsz, seq, heads, dh = q.shape
    n = seq // dil
    n_pad = -(-n // DSW_BLOCK) * DSW_BLOCK
    nb = n_pad // DSW_BLOCK

    def blocks(t):
        t = t.reshape(bsz, n, dil, heads, dh).transpose(0, 2, 1, 3, 4).reshape(bsz * dil, n, heads, dh)
        t = jnp.pad(t, ((0, 0), (0, n_pad - n), (0, 0), (0, 0)))
        return t.reshape(bsz * dil, nb, DSW_BLOCK, heads, dh)

    def with_prev(t):
        prev = jnp.pad(t, ((0, 0), (1, 0), (0, 0), (0, 0), (0, 0)))[:, :-1]
        return jnp.concatenate([prev, t], axis=2)

    qb = blocks(q)
    kk = with_prev(blocks(k))
    vv = with_prev(blocks(v))
    s = jnp.einsum('znqhd,znkhd->znhqk', qb, kk).astype(jnp.float32) * dh ** -0.5
    blk = jnp.arange(nb)[:, None] * DSW_BLOCK
    qpos = blk + jnp.arange(DSW_BLOCK)[None, :]
    kpos = blk - DSW_BLOCK + jnp.arange(2 * DSW_BLOCK)[None, :]
    dist = qpos[:, :, None] - kpos[:, None, :]
    valid = (dist >= 0) & (dist <= span) & (kpos[:, None, :] >= 0)
    s = jnp.where(valid[None, :, None], s, -jnp.inf)
    m = s.max(-1, keepdims=True)
    p = jnp.exp(s - m)
    l = p.sum(-1, keepdims=True)
    o = jnp.einsum('znhqk,znkhd->znqhd', p / l, vv.astype(jnp.float32))
    lse = (m + jnp.log(l))[..., 0]
    o = o.reshape(bsz, dil, n_pad, heads, dh)[:, :, :n].transpose(0, 2, 1, 3, 4).reshape(bsz, seq, heads, dh)
    lse = lse.transpose(0, 1, 3, 2).reshape(bsz, dil, n_pad, heads)[:, :, :n]
    lse = lse.transpose(0, 2, 1, 3).reshape(bsz, seq, heads)
    return o, lse


def _dilated_window_attention(q, k, v):
    outs, lses = [], []
    for window, dil in DSW_CONFIGS:
        o, lse = _strided_band_attention(q, k, v, window // dil, dil)
        outs.append(o)
        lses.append(lse)
    wts = jax.nn.softmax(jnp.stack(lses), axis=0)
    return jnp.einsum('gblh,gblhd->blhd', wts, jnp.stack(outs))


def _mlstm_dsw_mixer(x, positions, w_in, conv_w, conv_b, w_q, w_k, w_v, w_if, b_if, w_out):
    bsz, seq, _ = x.shape
    proj = x @ w_in
    xc = proj[..., :C_WIDTH]
    o_pre = proj[..., C_WIDTH:2 * C_WIDTH]
    qkv = proj[..., 2 * C_WIDTH:].reshape(bsz, seq, 3, DSW_HEADS, DSW_HEAD_DIM)
    xconv = jax.nn.silu(_causal_dwconv(xc, conv_w, conv_b)).reshape(bsz, seq, MLSTM_HEADS, MLSTM_HEAD_DIM)
    xraw = xc.reshape(bsz, seq, MLSTM_HEADS, MLSTM_HEAD_DIM)
    q = jnp.einsum('blhi,hij->blhj', xconv, w_q)
    k = jnp.einsum('blhi,hij->blhj', xconv, w_k) * MLSTM_HEAD_DIM ** -0.5
    v = jnp.einsum('blhi,hij->blhj', xraw, w_v)
    gate_in = jnp.concatenate([q.reshape(bsz, seq, C_WIDTH), k.reshape(bsz, seq, C_WIDTH),
                               v.reshape(bsz, seq, C_WIDTH)], axis=-1)
    gates = gate_in @ w_if + b_if
    h_c = _mlstm_chunkwise(q, k, v, gates[..., :MLSTM_HEADS], gates[..., MLSTM_HEADS:]).astype(x.dtype)
    y_c = jax.nn.sigmoid(o_pre) * h_c.reshape(bsz, seq, C_WIDTH)
    qd = _partial_rope(qkv[:, :, 0], positions)
    kd = _partial_rope(qkv[:, :, 1], positions)
    y_d = _dilated_window_attention(qd, kd, qkv[:, :, 2]).astype(x.dtype).reshape(bsz, seq, D_WIDTH)
    return jnp.concatenate([y_c, y_d], axis=-1) @ w_out


def _moe_ffn(x, router_w, router_b, w_in, b_in, w_out, b_out):
    bsz, seq, d = x.shape
    xt = x.reshape(-1, d)
    t = xt.shape[0]
    logits = (xt @ router_w + router_b).astype(jnp.float32)
    top_logit, top_idx = lax.top_k(logits, TOP_K)
    gate = jax.nn.softmax(top_logit, axis=-1)
    flat_e = top_idx.reshape(-1)
    order = jnp.argsort(flat_e)
    e_sorted = flat_e[order]
    tok_sorted = order // TOP_K
    gate_sorted = gate.reshape(-1)[order]
    counts = jnp.bincount(flat_e, length=N_EXPERTS)
    blocks_per_e = (counts + MOE_BLOCK - 1) // MOE_BLOCK
    blk_end = jnp.cumsum(blocks_per_e)
    blk_start = blk_end - blocks_per_e
    row_start = jnp.cumsum(counts) - counts
    rank = jnp.arange(t * TOP_K) - row_start[e_sorted]
    dest = blk_start[e_sorted] * MOE_BLOCK + rank
    n_blocks = -(-(t * TOP_K) // MOE_BLOCK) + N_EXPERTS
    n_rows = n_blocks * MOE_BLOCK
    row_tok = jnp.full((n_rows,), t, dtype=jnp.int32).at[dest].set(tok_sorted.astype(jnp.int32))
    row_gate = jnp.zeros((n_rows,), jnp.float32).at[dest].set(gate_sorted)
    block_expert = jnp.minimum(jnp.searchsorted(blk_end, jnp.arange(n_blocks), side='right'), N_EXPERTS - 1)
    x_pad = jnp.concatenate([xt, jnp.zeros((1, d), xt.dtype)], axis=0)
    xb = x_pad[row_tok].reshape(n_blocks, MOE_BLOCK, d)

    def expert_block(args):
        xblk, e = args
        h = xblk @ w_in[e] + b_in[e]
        g = jnp.minimum(h[:, :D_EXPERT], SWIGLU_LIMIT)
        lin = jnp.clip(h[:, D_EXPERT:], -SWIGLU_LIMIT, SWIGLU_LIMIT)
        y = g * jax.nn.sigmoid(SWIGLU_ALPHA * g) * (lin + 1.0)
        return y @ w_out[e] + b_out[e]

    yb = lax.map(expert_block, (xb, block_expert)).reshape(n_rows, d)
    y = jax.ops.segment_sum(yb * row_gate[:, None].astype(yb.dtype), row_tok, num_segments=t + 1)[:t]
    return y.reshape(bsz, seq, d)


def setup_inputs(seed: int = 0) -> dict:
    key = jax.random.key(seed)
    keys = iter(jax.random.split(key, 64))
    f32 = jnp.float32

    def nrm(shape, scale):
        return jax.random.normal(next(keys), shape, f32) * scale

    x = jax.random.normal(next(keys), (BATCH, SEQ, D_MODEL), f32)
    offs = jax.random.randint(next(keys), (BATCH, 1), 0, SEQ, dtype=jnp.int32)
    positions = offs + jnp.arange(SEQ, dtype=jnp.int32)[None, :]
    ab_w_in = nrm((N_AB, D_MODEL, AB_IN), D_MODEL ** -0.5)
    ab_conv_w = nrm((N_AB, CONV_WIDTH, A_WIDTH), CONV_WIDTH ** -0.5)
    ab_conv_b = nrm((N_AB, A_WIDTH), 0.02)
    ab_gate_r_w = nrm((N_AB, LRU_HEADS, LRU_HEAD_DIM, LRU_HEAD_DIM), LRU_HEAD_DIM ** -0.5)
    ab_gate_r_b = nrm((N_AB, A_WIDTH), 0.02)
    ab_gate_i_w = nrm((N_AB, LRU_HEADS, LRU_HEAD_DIM, LRU_HEAD_DIM), LRU_HEAD_DIM ** -0.5)
    ab_gate_i_b = nrm((N_AB, A_WIDTH), 0.02)
    a_pow = jax.random.uniform(next(keys), (N_AB, A_WIDTH), f32, 0.9, 0.999)
    a_base = a_pow ** (1.0 / LRU_C)
    ab_lru_lambda = jnp.log(a_base) - jnp.log1p(-a_base)
    n_idx = jnp.arange(S5_STATE, dtype=f32)
    ab_s5_lambda_re = -0.5 + nrm((N_AB, S5_GROUPS, S5_STATE), 0.01)
    ab_s5_lambda_im = jnp.pi * n_idx + nrm((N_AB, S5_GROUPS, S5_STATE), 0.01)
    ab_s5_log_step = jax.random.uniform(next(keys), (N_AB, S5_GROUPS), f32, math.log(1e-3), math.log(1e-1))
    ab_s5_b_re = nrm((N_AB, S5_GROUPS, S5_STATE, S5_GROUP_CH), (2 * S5_GROUP_CH) ** -0.5)
    ab_s5_b_im = nrm((N_AB, S5_GROUPS, S5_STATE, S5_GROUP_CH), (2 * S5_GROUP_CH) ** -0.5)
    ab_s5_c_re = nrm((N_AB, S5_GROUPS, S5_GROUP_CH, S5_STATE), (2 * S5_STATE) ** -0.5)
    ab_s5_c_im = nrm((N_AB, S5_GROUPS, S5_GROUP_CH, S5_STATE), (2 * S5_STATE) ** -0.5)
    ab_s5_d = nrm((N_AB, S5_GROUPS, S5_GROUP_CH), 1.0)
    ab_glu_w = nrm((N_AB, B_WIDTH, B_WIDTH), B_WIDTH ** -0.5)
    ab_glu_b = nrm((N_AB, B_WIDTH), 0.02)
    ab_w_out = nrm((N_AB, MIX_WIDTH, D_MODEL), MIX_WIDTH ** -0.5 * DEEPNORM_BETA)
    cd_w_in = nrm((N_CD, D_MODEL, CD_IN), D_MODEL ** -0.5)
    cd_conv_w = nrm((N_CD, CONV_WIDTH, C_WIDTH), CONV_WIDTH ** -0.5)
    cd_conv_b = nrm((N_CD, C_WIDTH), 0.02)
    cd_w_q = nrm((N_CD, MLSTM_HEADS, MLSTM_HEAD_DIM, MLSTM_HEAD_DIM), MLSTM_HEAD_DIM ** -0.5)
    cd_w_k = nrm((N_CD, MLSTM_HEADS, MLSTM_HEAD_DIM, MLSTM_HEAD_DIM), MLSTM_HEAD_DIM ** -0.5)
    cd_w_v = nrm((N_CD, MLSTM_HEADS, MLSTM_HEAD_DIM, MLSTM_HEAD_DIM), MLSTM_HEAD_DIM ** -0.5)
    cd_w_if = nrm((N_CD, 3 * C_WIDTH, 2 * MLSTM_HEADS), 0.1 * (3 * C_WIDTH) ** -0.5)
    cd_b_if = jnp.concatenate([nrm((N_CD, MLSTM_HEADS), 0.1),
                               jnp.linspace(3.0, 6.0, MLSTM_HEADS, dtype=f32)[None, :]
                               + nrm((N_CD, MLSTM_HEADS), 0.01)], axis=-1)
    cd_w_out = nrm((N_CD, MIX_WIDTH, D_MODEL), MIX_WIDTH ** -0.5 * DEEPNORM_BETA)
    ln_mix_g = 1.0 + nrm((DEPTH, D_MODEL), 0.02)
    ln_mix_b = nrm((DEPTH, D_MODEL), 0.02)
    ln_ffn_g = 1.0 + nrm((DEPTH, D_MODEL), 0.02)
    ln_ffn_b = nrm((DEPTH, D_MODEL), 0.02)
    moe_router_w = nrm((DEPTH, D_MODEL, N_EXPERTS), D_MODEL ** -0.5)
    moe_router_b = nrm((DEPTH, N_EXPERTS), 0.01)
    moe_w_in = nrm((DEPTH, N_EXPERTS, D_MODEL, 2 * D_EXPERT), D_MODEL ** -0.5)
    moe_b_in = nrm((DEPTH, N_EXPERTS, 2 * D_EXPERT), 0.02)
    moe_w_out = nrm((DEPTH, N_EXPERTS, D_EXPERT, D_MODEL), D_EXPERT ** -0.5 * DEEPNORM_BETA)
    moe_b_out = nrm((DEPTH, N_EXPERTS, D_MODEL), 0.02)
    return {'x': x, 'positions': positions,
            'ab_w_in': ab_w_in, 'ab_conv_w': ab_conv_w, 'ab_conv_b': ab_conv_b,
            'ab_gate_r_w': ab_gate_r_w, 'ab_gate_r_b': ab_gate_r_b,
            'ab_gate_i_w': ab_gate_i_w, 'ab_gate_i_b': ab_gate_i_b, 'ab_lru_lambda': ab_lru_lambda,
            'ab_s5_lambda_re': ab_s5_lambda_re, 'ab_s5_lambda_im': ab_s5_lambda_im,
            'ab_s5_log_step': ab_s5_log_step, 'ab_s5_b_re': ab_s5_b_re, 'ab_s5_b_im': ab_s5_b_im,
            'ab_s5_c_re': ab_s5_c_re, 'ab_s5_c_im': ab_s5_c_im, 'ab_s5_d': ab_s5_d,
            'ab_glu_w': ab_glu_w, 'ab_glu_b': ab_glu_b, 'ab_w_out': ab_w_out,
            'cd_w_in': cd_w_in, 'cd_conv_w': cd_conv_w, 'cd_conv_b': cd_conv_b,
            'cd_w_q': cd_w_q, 'cd_w_k': cd_w_k, 'cd_w_v': cd_w_v,
            'cd_w_if': cd_w_if, 'cd_b_if': cd_b_if, 'cd_w_out': cd_w_out,
            'ln_mix_g': ln_mix_g, 'ln_mix_b': ln_mix_b, 'ln_ffn_g': ln_ffn_g, 'ln_ffn_b': ln_ffn_b,
            'moe_router_w': moe_router_w, 'moe_router_b': moe_router_b,
            'moe_w_in': moe_w_in, 'moe_b_in': moe_b_in, 'moe_w_out': moe_w_out, 'moe_b_out': moe_b_out}


def reference(x, positions, ab_w_in, ab_conv_w, ab_conv_b, ab_gate_r_w, ab_gate_r_b, ab_gate_i_w,
              ab_gate_i_b, ab_lru_lambda, ab_s5_lambda_re, ab_s5_lambda_im, ab_s5_log_step,
              ab_s5_b_re, ab_s5_b_im, ab_s5_c_re, ab_s5_c_im, ab_s5_d, ab_glu_w, ab_glu_b, ab_w_out,
              cd_w_in, cd_conv_w, cd_conv_b, cd_w_q, cd_w_k, cd_w_v, cd_w_if, cd_b_if, cd_w_out,
              ln_mix_g, ln_mix_b, ln_ffn_g, ln_ffn_b, moe_router_w, moe_router_b,
              moe_w_in, moe_b_in, moe_w_out, moe_b_out):
    for layer in range(DEPTH):
        j = layer // 2
        if layer % 2 == 0:
            mix = _rglru_s5_mixer(x, ab_w_in[j], ab_conv_w[j], ab_conv_b[j], ab_gate_r_w[j], ab_gate_r_b[j],
                                  ab_gate_i_w[j], ab_gate_i_b[j], ab_lru_lambda[j], ab_s5_lambda_re[j],
                                  ab_s5_lambda_im[j], ab_s5_log_step[j], ab_s5_b_re[j], ab_s5_b_im[j],
                                  ab_s5_c_re[j], ab_s5_c_im[j], ab_s5_d[j], ab_glu_w[j], ab_glu_b[j],
                                  ab_w_out[j])
        else:
            mix = _mlstm_dsw_mixer(x, positions, cd_w_in[j], cd_conv_w[j], cd_conv_b[j], cd_w_q[j],
                                   cd_w_k[j], cd_w_v[j], cd_w_if[j], cd_b_if[j], cd_w_out[j])
        x = _layer_norm(DEEPNORM_ALPHA * x + mix, ln_mix_g[layer], ln_mix_b[layer])
        ffn = _moe_ffn(x, moe_router_w[layer], moe_router_b[layer], moe_w_in[layer], moe_b_in[layer],
                       moe_w_out[layer], moe_b_out[layer])
        x = _layer_norm(DEEPNORM_ALPHA * x + ffn, ln_ffn_g[layer], ln_ffn_b[layer])
    return x
```

```python
import functools
import math

import jax
import jax.numpy as jnp
from jax import lax
from jax.experimental import pallas as pl
from jax.experimental.pallas import tpu as pltpu

F32 = jnp.float32
BF16 = jnp.bfloat16

D_MODEL = 1024
DEPTH = 2
A_WIDTH = 512
B_WIDTH = 512
C_WIDTH = 512
D_WIDTH = 512
LRU_HEADS = 8
LRU_HEAD_DIM = A_WIDTH // LRU_HEADS
LRU_C = 8.0
CONV_WIDTH = 4
S5_GROUP_CH = 16
S5_GROUPS = B_WIDTH // S5_GROUP_CH
S5_STATE = 64
MLSTM_HEADS = 4
MLSTM_HEAD_DIM = C_WIDTH // MLSTM_HEADS
MLSTM_CHUNK = 64
DSW_HEADS = 4
DSW_HEAD_DIM = D_WIDTH // DSW_HEADS
DSW_CONFIGS = ((128, 1), (512, 4), (2048, 16))
DSW_BLOCK = 128
ROPE_THETA = 500000.0
ROPE_DIMS = DSW_HEAD_DIM // 4
N_EXPERTS = 32
TOP_K = 4
D_EXPERT = D_MODEL
SWIGLU_LIMIT = 7.0
SWIGLU_ALPHA = 1.702
DEEPNORM_ALPHA = (2 * DEPTH) ** 0.25
LN_EPS = 1e-5

VMEM_LIMIT_BYTES = 56 * 1024 * 1024
MOE_TM = 256


def _cparams(sem):
    return pltpu.CompilerParams(dimension_semantics=sem, vmem_limit_bytes=VMEM_LIMIT_BYTES)


def _matmul_kernel(x_ref, w_ref, o_ref):
    o_ref[...] = jnp.dot(x_ref[...].astype(BF16), w_ref[...], preferred_element_type=F32)


def _matmul(x, w_bf16, tm=512):
    m, k = x.shape
    n = w_bf16.shape[1]
    return pl.pallas_call(
        _matmul_kernel,
        grid=(m // tm,),
        in_specs=[pl.BlockSpec((tm, k), lambda i: (i, 0)),
                  pl.BlockSpec((k, n), lambda i: (0, 0))],
        out_specs=pl.BlockSpec((tm, n), lambda i: (i, 0)),
        out_shape=jax.ShapeDtypeStruct((m, n), F32),
        compiler_params=_cparams(("arbitrary",)),
        name="in_proj",
    )(x, w_bf16)


def _layer_norm_rows(z, g, b):
    mu = jnp.mean(z, axis=-1, keepdims=True)
    zc = z - mu
    var = jnp.mean(zc * zc, axis=-1, keepdims=True)
    return zc * lax.rsqrt(var + LN_EPS) * g + b


def _outproj_ln_router_kernel(y_ref, w_ref, x_ref, g_ref, b_ref, rw_ref, rb_ref, xn_ref, lg_ref):
    mix = jnp.dot(y_ref[...].astype(BF16), w_ref[...], preferred_element_type=F32)
    xn = _layer_norm_rows(DEEPNORM_ALPHA * x_ref[...] + mix, g_ref[...], b_ref[...])
    xn_ref[...] = xn
    lg_ref[...] = jnp.dot(xn, rw_ref[...], precision=lax.Precision.HIGHEST,
                          preferred_element_type=F32) + rb_ref[...]


def _outproj_ln_router(y, w_bf16, x, g, b, rw, rb, tm=512):
    m, k = y.shape
    d = w_bf16.shape[1]
    ne = rw.shape[1]
    row = lambda i: (i, 0)
    fixed = lambda i: (0, 0)
    return pl.pallas_call(
        _outproj_ln_router_kernel,
        grid=(m // tm,),
        in_specs=[pl.BlockSpec((tm, k), row), pl.BlockSpec((k, d), fixed), pl.BlockSpec((tm, d), row),
                  pl.BlockSpec((1, d), fixed), pl.BlockSpec((1, d), fixed),
                  pl.BlockSpec((d, ne), fixed), pl.BlockSpec((1, ne), fixed)],
        out_specs=[pl.BlockSpec((tm, d), row), pl.BlockSpec((tm, ne), row)],
        out_shape=[jax.ShapeDtypeStruct((m, d), F32), jax.ShapeDtypeStruct((m, ne), F32)],
        compiler_params=_cparams(("arbitrary",)),
        name="out_proj_ln_router",
    )(y, w_bf16, x, g.reshape(1, d), b.reshape(1, d), rw, rb.reshape(1, ne))


def _residual_ln_kernel(x_ref, f_ref, g_ref, b_ref, o_ref):
    o_ref[...] = _layer_norm_rows(DEEPNORM_ALPHA * x_ref[...] + f_ref[...], g_ref[...], b_ref[...])


def _residual_ln(x, f, g, b, tm=1024):
    m, d = x.shape
    row = lambda i: (i, 0)
    fixed = lambda i: (0, 0)
    return pl.pallas_call(
        _residual_ln_kernel,
        grid=(m // tm,),
        in_specs=[pl.BlockSpec((tm, d), row), pl.BlockSpec((tm, d), row),
                  pl.BlockSpec((1, d), fixed), pl.BlockSpec((1, d), fixed)],
        out_specs=pl.BlockSpec((tm, d), row),
        out_shape=jax.ShapeDtypeStruct((m, d), F32),
        compiler_params=_cparams(("arbitrary",)),
        name="residual_ln",
    )(x, f, g.reshape(1, d), b.reshape(1, d))


def _moe_kernel(be_ref, nreal_ref, xs_ref, win_ref, bin_ref, wout_ref, bout_ref, o_ref, win_bf, wout_bf):
    i = pl.program_id(0)
    e = be_ref[i]
    prev = be_ref[jnp.maximum(i - 1, 0)]

    @pl.when((i == 0) | (e != prev))
    def _load_expert():
        win_bf[...] = win_ref[...].astype(BF16)
        wout_bf[...] = wout_ref[...].astype(BF16)

    @pl.when(i < nreal_ref[0])
    def _compute():
        h = jnp.dot(xs_ref[...], win_bf[...], preferred_element_type=F32) + bin_ref[...]
        g = jnp.minimum(h[:, :D_EXPERT], SWIGLU_LIMIT)
        lin = jnp.clip(h[:, D_EXPERT:], -SWIGLU_LIMIT, SWIGLU_LIMIT)
        y = g * jax.nn.sigmoid(SWIGLU_ALPHA * g) * (lin + 1.0)
        o_ref[...] = jnp.dot(y.astype(BF16), wout_bf[...], preferred_element_type=F32) + bout_ref[...]

    @pl.when(i >= nreal_ref[0])
    def _unused_block():
        o_ref[...] = jnp.zeros_like(o_ref)


def _moe_experts(xs, block_expert, n_real, w_in, b_in, w_out, b_out):
    n_rows, d = xs.shape
    tm = MOE_TM
    n_blocks = n_rows // tm
    ne, _, dh2 = w_in.shape
    grid_spec = pltpu.PrefetchScalarGridSpec(
        num_scalar_prefetch=2,
        grid=(n_blocks,),
        in_specs=[
            pl.BlockSpec((tm, d), lambda i, be, nr: (i, 0)),
            pl.BlockSpec((None, d, dh2), lambda i, be, nr: (be[i], 0, 0)),
            pl.BlockSpec((None, 1, dh2), lambda i, be, nr: (be[i], 0, 0)),
            pl.BlockSpec((None, D_EXPERT, d), lambda i, be, nr: (be[i], 0, 0)),
            pl.BlockSpec((None, 1, d), lambda i, be, nr: (be[i], 0, 0)),
        ],
        out_specs=pl.BlockSpec((tm, d), lambda i, be, nr: (i, 0)),
        scratch_shapes=[pltpu.VMEM((d, dh2), BF16), pltpu.VMEM((D_EXPERT, d), BF16)],
    )
    return pl.pallas_call(
        _moe_kernel,
        grid_spec=grid_spec,
        out_shape=jax.ShapeDtypeStruct((n_rows, d), F32),
        compiler_params=_cparams(("arbitrary",)),
        name="moe_experts",
    )(block_expert, n_real, xs, w_in, b_in.reshape(ne, 1, dh2), w_out, b_out.reshape(ne, 1, d))


def _moe_ffn(xn, logits, w_in, b_in, w_out, b_out):
    t, d = xn.shape
    tm = MOE_TM
    top_logit, top_idx = lax.top_k(logits, TOP_K)
    gate = jax.nn.softmax(top_logit, axis=-1)
    flat_e = top_idx.reshape(-1).astype(jnp.int32)
    onehot = (flat_e[:, None] == jnp.arange(N_EXPERTS, dtype=jnp.int32)[None, :]).astype(jnp.int32)
    csum = jnp.cumsum(onehot, axis=0)
    rank = jnp.sum(csum * onehot, axis=1) - 1
    counts = csum[-1]
    blocks_per_e = (counts + tm - 1) // tm
    blk_end = jnp.cumsum(blocks_per_e)
    blk_start = blk_end - blocks_per_e
    dest = blk_start[flat_e] * tm + rank
    n_blocks = (t * TOP_K) // tm + N_EXPERTS
    n_rows = n_blocks * tm
    row_tok = jnp.zeros((n_rows,), jnp.int32).at[dest].set(jnp.arange(t * TOP_K, dtype=jnp.int32) // TOP_K)
    block_expert = jnp.minimum(jnp.searchsorted(blk_end, jnp.arange(n_blocks), side='right'),
                               N_EXPERTS - 1).astype(jnp.int32)
    n_real = blk_end[-1:].astype(jnp.int32)
    xs = xn.astype(BF16)[row_tok]
    yb = _moe_experts(xs, block_expert, n_real, w_in, b_in, w_out, b_out)
    picked = yb[dest].reshape(t, TOP_K, d)
    return jnp.sum(picked * gate[:, :, None], axis=1)


def _causal_dwconv(x, w, b):
    seq = x.shape[1]
    xp = jnp.pad(x, ((0, 0), (CONV_WIDTH - 1, 0), (0, 0)))
    out = b
    for tap in range(CONV_WIDTH):
        out = out + xp[:, tap:tap + seq] * w[tap]
    return out


def _linear_combine(e1, e2):
    a1, b1 = e1
    a2, b2 = e2
    return (a1 * a2, a2 * b1 + b2)


def _complex_linear_combine(e1, e2):
    a1r, a1i, b1r, b1i = e1
    a2r, a2i, b2r, b2i = e2
    return (a2r * a1r - a2i * a1i, a2r * a1i + a2i * a1r,
            a2r * b1r - a2i * b1i + b2r, a2r * b1i + a2i * b1r + b2i)


def _rg_lru(x, gate_r_w, gate_r_b, gate_i_w, gate_i_b, lru_lambda):
    bsz, seq, width = x.shape
    xh = x.reshape(bsz, seq, LRU_HEADS, LRU_HEAD_DIM)
    r = jax.nn.sigmoid(jnp.einsum('blhi,hij->blhj', xh, gate_r_w).reshape(bsz, seq, width) + gate_r_b)
    i = jax.nn.sigmoid(jnp.einsum('blhi,hij->blhj', xh, gate_i_w).reshape(bsz, seq, width) + gate_i_b)
    log_a = -LRU_C * r * jax.nn.softplus(-lru_lambda)
    a = jnp.exp(log_a)
    u = jnp.sqrt(-jnp.expm1(2.0 * log_a)) * (i * x)
    _, h = lax.associative_scan(_linear_combine, (a, u), axis=1)
    return h


def _s5(u, lam_re, lam_im, log_step, b_re, b_im, c_re, c_im, d_skip):
    bsz, seq, width = u.shape
    ug = u.reshape(bsz, seq, S5_GROUPS, S5_GROUP_CH)
    lr = jnp.minimum(lam_re, -1e-4)
    li = lam_im
    step = jnp.exp(log_step)[:, None]
    mag = jnp.exp(lr * step)
    ar = mag * jnp.cos(li * step)
    ai = mag * jnp.sin(li * step)
    inv = 1.0 / (lr * lr + li * li)
    zr = ((ar - 1.0) * lr + ai * li) * inv
    zi = (ai * lr - (ar - 1.0) * li) * inv
    bbr = zr[..., None] * b_re - zi[..., None] * b_im
    bbi = zr[..., None] * b_im + zi[..., None] * b_re
    bu_r = jnp.einsum('blgh,gph->blgp', ug, bbr)
    bu_i = jnp.einsum('blgh,gph->blgp', ug, bbi)
    a_r = jnp.broadcast_to(ar, (1, seq) + ar.shape)
    a_i = jnp.broadcast_to(ai, (1, seq) + ai.shape)
    _, _, xr, xi = lax.associative_scan(_complex_linear_combine, (a_r, a_i, bu_r, bu_i), axis=1)
    y = (jnp.einsum('blgp,ghp->blgh', xr, c_re) - jnp.einsum('blgp,ghp->blgh', xi, c_im) + d_skip * ug)
    return y.reshape(bsz, seq, width)


def _rglru_s5_mixer(proj, conv_w, conv_b, gate_r_w, gate_r_b, gate_i_w, gate_i_b, lru_lambda,
                    lam_re, lam_im, log_step, b_re, b_im, c_re, c_im, d_skip, glu_w, glu_b):
    xa = proj[..., :A_WIDTH]
    ga = proj[..., A_WIDTH:2 * A_WIDTH]
    ub = proj[..., 2 * A_WIDTH:]
    ya = _rg_lru(_causal_dwconv(xa, conv_w, conv_b), gate_r_w, gate_r_b, gate_i_w, gate_i_b,
                 lru_lambda) * jax.nn.gelu(ga)
    s = jax.nn.gelu(_s5(ub, lam_re, lam_im, log_step, b_re, b_im, c_re, c_im, d_skip))
    yb = s * jax.nn.sigmoid(s @ glu_w + glu_b)
    return jnp.concatenate([ya, yb], axis=-1)


def _mlstm_chunkwise(q, k, v, i_pre, f_pre):
    bsz, seq, heads, dk = q.shape
    dv = v.shape[-1]
    cs = MLSTM_CHUNK
    nc = seq // cs
    qc = q.reshape(bsz, nc, cs, heads, dk)
    kc = k.reshape(bsz, nc, cs, heads, dk)
    vc = v.reshape(bsz, nc, cs, heads, dv)
    log_f = jax.nn.log_sigmoid(f_pre).reshape(bsz, nc, cs, heads).transpose(0, 1, 3, 2)
    log_i = i_pre.reshape(bsz, nc, cs, heads).transpose(0, 1, 3, 2)
    cum_f = jnp.cumsum(log_f, axis=-1)
    chunk_f = cum_f[..., -1]
    to_end = chunk_f[..., None] - cum_f + log_i
    m_chunk = to_end.max(-1)
    w_end = jnp.exp(to_end - m_chunk[..., None])
    kv_chunk = jnp.einsum('bnhc,bnchk,bnchv->bnhkv', w_end, kc, vc)
    n_chunk = jnp.einsum('bnhc,bnchk->bnhk', w_end, kc)

    def step(carry, inp):
        c_st, n_st, m_st = carry
        f_c, m_c, kv_c, nk_c = inp
        m_new = jnp.maximum(f_c + m_st, m_c)
        s_old = jnp.exp(f_c + m_st - m_new)
        s_new = jnp.exp(m_c - m_new)
        c_new = s_old[..., None, None] * c_st + s_new[..., None, None] * kv_c
        n_new = s_old[..., None] * n_st + s_new[..., None] * nk_c
        return (c_new, n_new, m_new), (c_st, n_st, m_st)

    init = (jnp.zeros((bsz, heads, dk, dv), F32), jnp.zeros((bsz, heads, dk), F32),
            jnp.full((bsz, heads), -jnp.inf, F32))
    xs = (chunk_f.transpose(1, 0, 2), m_chunk.transpose(1, 0, 2),
          kv_chunk.transpose(1, 0, 2, 3, 4), n_chunk.transpose(1, 0, 2, 3))
    _, (c_prev, n_prev, m_prev) = lax.scan(step, init, xs)
    c_prev = c_prev.transpose(1, 0, 2, 3, 4)
    n_prev = n_prev.transpose(1, 0, 2, 3)
    m_prev = m_prev.transpose(1, 0, 2)

    causal = jnp.tril(jnp.ones((cs, cs), dtype=bool))
    log_d = jnp.where(causal, cum_f[..., :, None] - cum_f[..., None, :] + log_i[..., None, :], -jnp.inf)
    log_inter = cum_f + m_prev[..., None]
    m_s = jnp.maximum(log_d.max(-1), log_inter)
    scores = jnp.einsum('bnshk,bnjhk->bnhsj', qc, kc) * jnp.exp(log_d - m_s[..., None])
    inter = jnp.exp(log_inter - m_s)
    num = (jnp.einsum('bnhsj,bnjhv->bnhsv', scores, vc)
           + inter[..., None] * jnp.einsum('bnshk,bnhkv->bnhsv', qc, c_prev))
    den = scores.sum(-1) + inter * jnp.einsum('bnshk,bnhk->bnhs', qc, n_prev)
    h = num / jnp.maximum(jnp.abs(den), jnp.exp(-m_s))[..., None]
    return h.transpose(0, 1, 3, 2, 4).reshape(bsz, seq, heads, dv)


def _partial_rope(t, positions):
    half = ROPE_DIMS // 2
    inv_freq = ROPE_THETA ** (-jnp.arange(half, dtype=F32) / half)
    ang = positions.astype(F32)[..., None] * inv_freq
    cos = jnp.cos(ang)[:, :, None, :]
    sin = jnp.sin(ang)[:, :, None, :]
    t1, t2, rest = t[..., :half], t[..., half:ROPE_DIMS], t[..., ROPE_DIMS:]
    return jnp.concatenate([t1 * cos - t2 * sin, t2 * cos + t1 * sin, rest], axis=-1)


def _strided_band_attention(q, k, v, span, dil):
    bsz, seq, heads, dh = q.shape
    n = seq // dil
    n_pad = -(-n // DSW_BLOCK) * DSW_BLOCK
    nb = n_pad // DSW_BLOCK

    def blocks(t):
        t = t.reshape(bsz, n, dil, heads, dh).transpose(0, 2, 1, 3, 4).reshape(bsz * dil, n, heads, dh)
        t = jnp.pad(t, ((0, 0), (0, n_pad - n), (0, 0), (0, 0)))
        return t.reshape(bsz * dil, nb, DSW_BLOCK, heads, dh)

    def with_prev(t):
        prev = jnp.pad(t, ((0, 0), (1, 0), (0, 0), (0, 0), (0, 0)))[:, :-1]
        return jnp.concatenate([prev, t], axis=2)

    qb = blocks(q)
    kk = with_prev(blocks(k))
    vv = with_prev(blocks(v))
    s = jnp.einsum('znqhd,znkhd->znhqk', qb, kk) * dh ** -0.5
    blk = jnp.arange(nb)[:, None] * DSW_BLOCK
    qpos = blk + jnp.arange(DSW_BLOCK)[None, :]
    kpos = blk - DSW_BLOCK + jnp.arange(2 * DSW_BLOCK)[None, :]
    dist = qpos[:, :, None] - kpos[:, None, :]
    valid = (dist >= 0) & (dist <= span) & (kpos[:, None, :] >= 0)
    s = jnp.where(valid[None, :, None], s, -jnp.inf)
    m = s.max(-1, keepdims=True)
    p = jnp.exp(s - m)
    l = p.sum(-1, keepdims=True)
    o = jnp.einsum('znhqk,znkhd->znqhd', p / l, vv)
    lse = (m + jnp.log(l))[..., 0]
    o = o.reshape(bsz, dil, n_pad, heads, dh)[:, :, :n].transpose(0, 2, 1, 3, 4).reshape(bsz, seq, heads, dh)
    lse = lse.transpose(0, 1, 3, 2).reshape(bsz, dil, n_pad, heads)[:, :, :n]
    lse = lse.transpose(0, 2, 1, 3).reshape(bsz, seq, heads)
    return o, lse


def _dilated_window_attention(q, k, v):
    outs, lses = [], []
    for window, dil in DSW_CONFIGS:
        o, lse = _strided_band_attention(q, k, v, window // dil, dil)
        outs.append(o)
        lses.append(lse)
    wts = jax.nn.softmax(jnp.stack(lses), axis=0)
    return jnp.einsum('gblh,gblhd->blhd', wts, jnp.stack(outs))


def _mlstm_dsw_mixer(proj, positions, conv_w, conv_b, w_q, w_k, w_v, w_if, b_if):
    bsz, seq, _ = proj.shape
    xc = proj[..., :C_WIDTH]
    o_pre = proj[..., C_WIDTH:2 * C_WIDTH]
    qkv = proj[..., 2 * C_WIDTH:].reshape(bsz, seq, 3, DSW_HEADS, DSW_HEAD_DIM)
    xconv = jax.nn.silu(_causal_dwconv(xc, conv_w, conv_b)).reshape(bsz, seq, MLSTM_HEADS, MLSTM_HEAD_DIM)
    xraw = xc.reshape(bsz, seq, MLSTM_HEADS, MLSTM_HEAD_DIM)
    q = jnp.einsum('blhi,hij->blhj', xconv, w_q)
    k = jnp.einsum('blhi,hij->blhj', xconv, w_k) * MLSTM_HEAD_DIM ** -0.5
    v = jnp.einsum('blhi,hij->blhj', xraw, w_v)
    gate_in = jnp.concatenate([q.reshape(bsz, seq, C_WIDTH), k.reshape(bsz, seq, C_WIDTH),
                               v.reshape(bsz, seq, C_WIDTH)], axis=-1)
    gates = gate_in @ w_if + b_if
    h_c = _mlstm_chunkwise(q, k, v, gates[..., :MLSTM_HEADS], gates[..., MLSTM_HEADS:])
    y_c = jax.nn.sigmoid(o_pre) * h_c.reshape(bsz, seq, C_WIDTH)
    qd = _partial_rope(qkv[:, :, 0], positions)
    kd = _partial_rope(qkv[:, :, 1], positions)
    y_d = _dilated_window_attention(qd, kd, qkv[:, :, 2]).reshape(bsz, seq, D_WIDTH)
    return jnp.concatenate([y_c, y_d], axis=-1)


def kernel(x, positions, ab_w_in, ab_conv_w, ab_conv_b, ab_gate_r_w, ab_gate_r_b, ab_gate_i_w, ab_gate_i_b, ab_lru_lambda, ab_s5_lambda_re, ab_s5_lambda_im, ab_s5_log_step, ab_s5_b_re, ab_s5_b_im, ab_s5_c_re, ab_s5_c_im, ab_s5_d, ab_glu_w, ab_glu_b, ab_w_out, cd_w_in, cd_conv_w, cd_conv_b, cd_w_q, cd_w_k, cd_w_v, cd_w_if, cd_b_if, cd_w_out, ln_mix_g, ln_mix_b, ln_ffn_g, ln_ffn_b, moe_router_w, moe_router_b, moe_w_in, moe_b_in, moe_w_out, moe_b_out):
    bsz, seq, d = x.shape
    t = bsz * seq
    xt = x.reshape(t, d)
    for layer in range(DEPTH):
        j = layer // 2
        if layer % 2 == 0:
            proj = _matmul(xt, ab_w_in[j].astype(BF16)).reshape(bsz, seq, -1)
            y = _rglru_s5_mixer(proj, ab_conv_w[j], ab_conv_b[j], ab_gate_r_w[j], ab_gate_r_b[j],
                                ab_gate_i_w[j], ab_gate_i_b[j], ab_lru_lambda[j], ab_s5_lambda_re[j],
                                ab_s5_lambda_im[j], ab_s5_log_step[j], ab_s5_b_re[j], ab_s5_b_im[j],
                                ab_s5_c_re[j], ab_s5_c_im[j], ab_s5_d[j], ab_glu_w[j], ab_glu_b[j])
            w_out = ab_w_out[j]
        else:
            proj = _matmul(xt, cd_w_in[j].astype(BF16)).reshape(bsz, seq, -1)
            y = _mlstm_dsw_mixer(proj, positions, cd_conv_w[j], cd_conv_b[j], cd_w_q[j], cd_w_k[j],
                                 cd_w_v[j], cd_w_if[j], cd_b_if[j])
            w_out = cd_w_out[j]
        xn, logits = _outproj_ln_router(y.reshape(t, -1), w_out.astype(BF16), xt, ln_mix_g[layer],
                                        ln_mix_b[layer], moe_router_w[layer], moe_router_b[layer])
        ffn = _moe_ffn(xn, logits, moe_w_in[layer], moe_b_in[layer], moe_w_out[layer], moe_b_out[layer])
        xt = _residual_ln(xn, ffn, ln_ffn_g[layer], ln_ffn_b[layer])
    return xt.reshape(bsz, seq, d)
```

```python
import functools

import jax
import jax.numpy as jnp
from jax import lax
from jax.experimental import pallas as pl
from jax.experimental.pallas import tpu as pltpu

F32 = jnp.float32
BF16 = jnp.bfloat16

D_MODEL = 1024
DEPTH = 2
A_WIDTH = 512
B_WIDTH = 512
C_WIDTH = 512
D_WIDTH = 512
LRU_HEADS = 8
LRU_HEAD_DIM = A_WIDTH // LRU_HEADS
LRU_C = 8.0
CONV_WIDTH = 4
S5_GROUP_CH = 16
S5_GROUPS = B_WIDTH // S5_GROUP_CH
S5_STATE = 64
MLSTM_HEADS = 4
MLSTM_HEAD_DIM = C_WIDTH // MLSTM_HEADS
DSW_HEADS = 4
DSW_HEAD_DIM = D_WIDTH // DSW_HEADS
DSW_CONFIGS = ((128, 1), (512, 4), (2048, 16))
DSW_BLOCK = 128
ROPE_THETA = 500000.0
ROPE_DIMS = DSW_HEAD_DIM // 4
N_EXPERTS = 32
TOP_K = 4
D_EXPERT = D_MODEL
SWIGLU_LIMIT = 7.0
SWIGLU_ALPHA = 1.702
DEEPNORM_ALPHA = (2 * DEPTH) ** 0.25
LN_EPS = 1e-5

VMEM_LIMIT_BYTES = 56 * 1024 * 1024
LANES = 128
MOE_TM = 256
PROJ_TL = 512
LRU_TC = 128
S5_TC = 64
S5_HALF = 2
MLSTM_CS = 256
NEG_INF = float("-inf")


def _cparams(*sem):
    return pltpu.CompilerParams(dimension_semantics=sem, vmem_limit_bytes=VMEM_LIMIT_BYTES)


def _dot(a, b):
    return jnp.dot(a, b, preferred_element_type=F32)


def _dot_nt(a, b):
    return lax.dot_general(a, b, (((1,), (1,)), ((), ())), preferred_element_type=F32)


def _matmul_kernel(x_ref, w_ref, o_ref):
    o_ref[...] = _dot(x_ref[...].astype(BF16), w_ref[...])


def _inproj_time_major(x3, w_bf16):
    bsz, seq, k = x3.shape
    n = w_bf16.shape[1]
    out = pl.pallas_call(
        _matmul_kernel,
        grid=(bsz, seq // PROJ_TL),
        in_specs=[pl.BlockSpec((None, PROJ_TL, k), lambda b, i: (b, i, 0)),
                  pl.BlockSpec((k, n), lambda b, i: (0, 0))],
        out_specs=pl.BlockSpec((PROJ_TL, n), lambda b, i: (i, b)),
        out_shape=jax.ShapeDtypeStruct((seq, bsz * n), F32),
        compiler_params=_cparams("arbitrary", "arbitrary"),
        name="in_proj_time_major",
    )(x3, w_bf16)
    return out.reshape(seq, bsz, n)


def _inproj(x3, w_bf16):
    bsz, seq, k = x3.shape
    n = w_bf16.shape[1]
    return pl.pallas_call(
        _matmul_kernel,
        grid=(bsz, seq // PROJ_TL),
        in_specs=[pl.BlockSpec((None, PROJ_TL, k), lambda b, i: (b, i, 0)),
                  pl.BlockSpec((k, n), lambda b, i: (0, 0))],
        out_specs=pl.BlockSpec((None, PROJ_TL, n), lambda b, i: (b, i, 0)),
        out_shape=jax.ShapeDtypeStruct((bsz, seq, n), F32),
        compiler_params=_cparams("arbitrary", "arbitrary"),
        name="in_proj",
    )(x3, w_bf16)


def _layer_norm_rows(z, g, b):
    mu = jnp.mean(z, axis=-1, keepdims=True)
    zc = z - mu
    var = jnp.mean(zc * zc, axis=-1, keepdims=True)
    return zc * lax.rsqrt(var + LN_EPS) * g + b


def _outproj_ln_router_kernel(ya_ref, yb_ref, wa_ref, wb_ref, x_ref, g_ref, b_ref, rw_ref, rb_ref,
                              xn_ref, lg_ref):
    mix = _dot(ya_ref[...], wa_ref[...]) + _dot(yb_ref[...], wb_ref[...])
    xn = _layer_norm_rows(DEEPNORM_ALPHA * x_ref[...] + mix, g_ref[...], b_ref[...])
    xn_ref[...] = xn
    lg_ref[...] = jnp.dot(xn, rw_ref[...], precision=lax.Precision.HIGHEST,
                          preferred_element_type=F32) + rb_ref[...]


def _outproj_ln_router(ya, yb, time_major, w_bf16, x3, g, b, rw, rb):
    bsz, seq, d = x3.shape
    wa, wb = w_bf16[:ya.shape[-1]], w_bf16[ya.shape[-1]:]
    ne = rw.shape[1]
    tl = PROJ_TL
    if time_major:
        ya, yb = ya.reshape(seq, -1), yb.reshape(seq, -1)
        y_specs = [pl.BlockSpec((tl, wa.shape[0]), lambda bi, i: (i, bi)),
                   pl.BlockSpec((tl, wb.shape[0]), lambda bi, i: (i, bi))]
    else:
        y_specs = [pl.BlockSpec((None, tl, wa.shape[0]), lambda bi, i: (bi, i, 0)),
                   pl.BlockSpec((None, tl, wb.shape[0]), lambda bi, i: (bi, i, 0))]
    fixed = lambda bi, i: (0, 0)
    tok = lambda bi, i: (bi, i, 0)
    return pl.pallas_call(
        _outproj_ln_router_kernel,
        grid=(bsz, seq // tl),
        in_specs=y_specs + [pl.BlockSpec(wa.shape, fixed), pl.BlockSpec(wb.shape, fixed),
                            pl.BlockSpec((None, tl, d), tok),
                            pl.BlockSpec((1, d), fixed), pl.BlockSpec((1, d), fixed),
                            pl.BlockSpec((d, ne), fixed), pl.BlockSpec((1, ne), fixed)],
        out_specs=[pl.BlockSpec((None, tl, d), tok), pl.BlockSpec((None, tl, ne), tok)],
        out_shape=[jax.ShapeDtypeStruct((bsz, seq, d), F32), jax.ShapeDtypeStruct((bsz, seq, ne), F32)],
        compiler_params=_cparams("arbitrary", "arbitrary"),
        name="out_proj_ln_router",
    )(ya, yb, wa, wb, x3, g.reshape(1, d), b.reshape(1, d), rw, rb.reshape(1, ne))


def _residual_ln_kernel(x_ref, f_ref, g_ref, b_ref, o_ref):
    o_ref[...] = _layer_norm_rows(DEEPNORM_ALPHA * x_ref[...] + f_ref[...], g_ref[...], b_ref[...])


def _residual_ln(x, f, g, b, tm=1024):
    m, d = x.shape
    row = lambda i: (i, 0)
    fixed = lambda i: (0, 0)
    return pl.pallas_call(
        _residual_ln_kernel,
        grid=(m // tm,),
        in_specs=[pl.BlockSpec((tm, d), row), pl.BlockSpec((tm, d), row),
                  pl.BlockSpec((1, d), fixed), pl.BlockSpec((1, d), fixed)],
        out_specs=pl.BlockSpec((tm, d), row),
        out_shape=jax.ShapeDtypeStruct((m, d), F32),
        compiler_params=_cparams("arbitrary"),
        name="residual_ln",
    )(x, f, g.reshape(1, d), b.reshape(1, d))


def _moe_kernel(be_ref, nreal_ref, xs_ref, win_ref, bin_ref, wout_ref, bout_ref, o_ref, win_bf, wout_bf):
    i = pl.program_id(0)
    e = be_ref[i]
    prev = be_ref[jnp.maximum(i - 1, 0)]

    @pl.when((i == 0) | (e != prev))
    def _load_expert():
        win_bf[...] = win_ref[...].astype(BF16)
        wout_bf[...] = wout_ref[...].astype(BF16)

    @pl.when(i < nreal_ref[0])
    def _compute():
        h = _dot(xs_ref[...], win_bf[...]) + bin_ref[...]
        g = jnp.minimum(h[:, :D_EXPERT], SWIGLU_LIMIT)
        lin = jnp.clip(h[:, D_EXPERT:], -SWIGLU_LIMIT, SWIGLU_LIMIT)
        y = g * jax.nn.sigmoid(SWIGLU_ALPHA * g) * (lin + 1.0)
        o_ref[...] = _dot(y.astype(BF16), wout_bf[...]) + bout_ref[...]

    @pl.when(i >= nreal_ref[0])
    def _unused_block():
        o_ref[...] = jnp.zeros_like(o_ref)


def _moe_experts(xs, block_expert, n_real, w_in, b_in, w_out, b_out):
    n_rows, d = xs.shape
    tm = MOE_TM
    n_blocks = n_rows // tm
    ne, _, dh2 = w_in.shape
    grid_spec = pltpu.PrefetchScalarGridSpec(
        num_scalar_prefetch=2,
        grid=(n_blocks,),
        in_specs=[
            pl.BlockSpec((tm, d), lambda i, be, nr: (i, 0)),
            pl.BlockSpec((None, d, dh2), lambda i, be, nr: (be[i], 0, 0)),
            pl.BlockSpec((None, 1, dh2), lambda i, be, nr: (be[i], 0, 0)),
            pl.BlockSpec((None, D_EXPERT, d), lambda i, be, nr: (be[i], 0, 0)),
            pl.BlockSpec((None, 1, d), lambda i, be, nr: (be[i], 0, 0)),
        ],
        out_specs=pl.BlockSpec((tm, d), lambda i, be, nr: (i, 0)),
        scratch_shapes=[pltpu.VMEM((d, dh2), BF16), pltpu.VMEM((D_EXPERT, d), BF16)],
    )
    return pl.pallas_call(
        _moe_kernel,
        grid_spec=grid_spec,
        out_shape=jax.ShapeDtypeStruct((n_rows, d), F32),
        compiler_params=_cparams("arbitrary"),
        name="moe_experts",
    )(block_expert, n_real, xs, w_in, b_in.reshape(ne, 1, dh2), w_out, b_out.reshape(ne, 1, d))


def _moe_ffn(xn, logits, w_in, b_in, w_out, b_out):
    t, d = xn.shape
    tm = MOE_TM
    top_logit, top_idx = lax.top_k(logits, TOP_K)
    gate = jax.nn.softmax(top_logit, axis=-1)
    flat_e = top_idx.reshape(-1).astype(jnp.int32)
    onehot = (flat_e[:, None] == jnp.arange(N_EXPERTS, dtype=jnp.int32)[None, :]).astype(jnp.int32)
    csum = jnp.cumsum(onehot, axis=0)
    rank = jnp.sum(csum * onehot, axis=1) - 1
    counts = csum[-1]
    blocks_per_e = (counts + tm - 1) // tm
    blk_end = jnp.cumsum(blocks_per_e)
    blk_start = blk_end - blocks_per_e
    dest = blk_start[flat_e] * tm + rank
    n_blocks = (t * TOP_K) // tm + N_EXPERTS
    n_rows = n_blocks * tm
    row_tok = jnp.zeros((n_rows,), jnp.int32).at[dest].set(jnp.arange(t * TOP_K, dtype=jnp.int32) // TOP_K)
    block_expert = jnp.minimum(jnp.searchsorted(blk_end, jnp.arange(n_blocks), side='right'),
                               N_EXPERTS - 1).astype(jnp.int32)
    n_real = blk_end[-1:].astype(jnp.int32)
    xs = xn.astype(BF16)[row_tok]
    yb = _moe_experts(xs, block_expert, n_real, w_in, b_in, w_out, b_out)
    picked = yb[dest].reshape(t, TOP_K, d)
    return jnp.sum(picked * gate[:, :, None], axis=1)


def _softplus(x):
    return jnp.maximum(x, 0.0) + jnp.log(1.0 + jnp.exp(-jnp.abs(x)))


def _gelu_tanh(x):
    return 0.5 * x * (1.0 + jnp.tanh(0.7978845608028654 * (x + 0.044715 * (x * x * x))))


def _lru_kernel(xa_ref, ga_ref, cw_ref, cb_ref, wr_ref, br_ref, wi_ref, bi_ref, lam_ref, o_ref,
                tail_ref, h_ref, a_ref, u_ref):
    tc, bsz, w = xa_ref.shape

    @pl.when(pl.program_id(0) == 0)
    def _init():
        tail_ref[...] = jnp.zeros_like(tail_ref)
        h_ref[...] = jnp.zeros_like(h_ref)

    xa = xa_ref[...]
    ext = jnp.concatenate([tail_ref[...], xa], axis=0)
    xc = cb_ref[...] + ext[0:tc] * cw_ref[0]
    for tap in range(1, CONV_WIDTH):
        xc = xc + ext[tap:tap + tc] * cw_ref[tap]
    tail_ref[...] = xa[tc - (CONV_WIDTH - 1):]
    x2 = xc.reshape(tc * bsz, w)
    xb = x2.astype(BF16)
    r = jax.nn.sigmoid(_dot(xb, wr_ref[...]) + br_ref[...])
    ig = jax.nn.sigmoid(_dot(xb, wi_ref[...]) + bi_ref[...])
    log_a = (-LRU_C) * r * _softplus(-lam_ref[...])
    a_ref[...] = jnp.exp(log_a).reshape(tc, bsz, w)
    u_ref[...] = (jnp.sqrt(1.0 - jnp.exp(2.0 * log_a)) * (ig * x2)).reshape(tc, bsz, w)

    def step(t, h):
        h = a_ref[t] * h + u_ref[t]
        u_ref[t] = h
        return h

    h_ref[...] = lax.fori_loop(0, tc, step, h_ref[...], unroll=8)
    o_ref[...] = (u_ref[...] * _gelu_tanh(ga_ref[...])).astype(o_ref.dtype)


def _block_diag(w):
    h, i, j = w.shape
    return jnp.einsum('hij,hk->hikj', w, jnp.eye(h, dtype=w.dtype)).reshape(h * i, h * j)


def _rg_lru(proj, conv_w, conv_b, gate_r_w, gate_r_b, gate_i_w, gate_i_b, lru_lambda):
    seq, bsz, _ = proj.shape
    w = A_WIDTH
    tc = LRU_TC
    fixed2 = lambda i: (0, 0)
    fixed3 = lambda i: (0, 0, 0)
    return pl.pallas_call(
        _lru_kernel,
        grid=(seq // tc,),
        in_specs=[pl.BlockSpec((tc, bsz, w), lambda i: (i, 0, 0)),
                  pl.BlockSpec((tc, bsz, w), lambda i: (i, 0, 1)),
                  pl.BlockSpec((CONV_WIDTH, 1, w), fixed3), pl.BlockSpec((1, w), fixed2),
                  pl.BlockSpec((w, w), fixed2), pl.BlockSpec((1, w), fixed2),
                  pl.BlockSpec((w, w), fixed2), pl.BlockSpec((1, w), fixed2),
                  pl.BlockSpec((1, w), fixed2)],
        out_specs=pl.BlockSpec((tc, bsz, w), lambda i: (i, 0, 0)),
        out_shape=jax.ShapeDtypeStruct((seq, bsz, w), BF16),
        scratch_shapes=[pltpu.VMEM((CONV_WIDTH - 1, bsz, w), F32), pltpu.VMEM((bsz, w), F32),
                        pltpu.VMEM((tc, bsz, w), F32), pltpu.VMEM((tc, bsz, w), F32)],
        compiler_params=_cparams("arbitrary"),
        name="rg_lru",
    )(proj, proj, conv_w.reshape(CONV_WIDTH, 1, w), conv_b.reshape(1, w),
      _block_diag(gate_r_w).astype(BF16), gate_r_b.reshape(1, w),
      _block_diag(gate_i_w).astype(BF16), gate_i_b.reshape(1, w), lru_lambda.reshape(1, w))


def _s5_kernel(u_ref, wbr_ref, wbi_ref, ar_ref, ai_ref, ccr_ref, cci_ref, d_ref, gw_ref, gb_ref, o_ref,
               xr_ref, xi_ref, sr_ref, si_ref):
    tc, bsz, w = u_ref.shape
    nstate = xr_ref.shape[-1]
    sh = nstate // S5_HALF
    wh = w // S5_HALF

    @pl.when(pl.program_id(0) == 0)
    def _init():
        sr_ref[...] = jnp.zeros_like(sr_ref)
        si_ref[...] = jnp.zeros_like(si_ref)

    u2 = u_ref[...].reshape(tc * bsz, w)
    ub = u2.astype(BF16)
    for hf in range(S5_HALF):
        uh = ub[:, hf * wh:(hf + 1) * wh]
        xr_ref[:, :, hf * sh:(hf + 1) * sh] = _dot(uh, wbr_ref[hf]).reshape(tc, bsz, sh)
        xi_ref[:, :, hf * sh:(hf + 1) * sh] = _dot(uh, wbi_ref[hf]).reshape(tc, bsz, sh)

    for hf in range(S5_HALF):
        lo, hi = hf * sh, (hf + 1) * sh
        ar = jnp.broadcast_to(ar_ref[:, lo:hi], (bsz, sh))
        ai = jnp.broadcast_to(ai_ref[:, lo:hi], (bsz, sh))

        def step(t, carry, lo=lo, hi=hi, ar=ar, ai=ai):
            xr, xi = carry
            nxr = ar * xr - ai * xi + xr_ref[t, :, lo:hi]
            nxi = ar * xi + ai * xr + xi_ref[t, :, lo:hi]
            xr_ref[t, :, lo:hi] = nxr
            xi_ref[t, :, lo:hi] = nxi
            return nxr, nxi

        xr, xi = lax.fori_loop(0, tc, step, (sr_ref[:, lo:hi], si_ref[:, lo:hi]), unroll=4)
        sr_ref[:, lo:hi] = xr
        si_ref[:, lo:hi] = xi

    ys = []
    for hf in range(S5_HALF):
        lo, hi = hf * sh, (hf + 1) * sh
        xrh = xr_ref[:, :, lo:hi].reshape(tc * bsz, sh).astype(BF16)
        xih = xi_ref[:, :, lo:hi].reshape(tc * bsz, sh).astype(BF16)
        ys.append(_dot(xrh, ccr_ref[hf]) - _dot(xih, cci_ref[hf]))
    y = jnp.concatenate(ys, axis=1) + d_ref[...] * u2
    s = _gelu_tanh(y)
    yb = s * jax.nn.sigmoid(_dot(s.astype(BF16), gw_ref[...]) + gb_ref[...])
    o_ref[...] = yb.reshape(tc, bsz, w).astype(o_ref.dtype)


def _s5_glu(proj, lam_re, lam_im, log_step, b_re, b_im, c_re, c_im, d_skip, glu_w, glu_b):
    seq, bsz, _ = proj.shape
    w = B_WIDTH
    tc = S5_TC
    nstate = S5_GROUPS * S5_STATE
    gh = S5_GROUPS // S5_HALF
    lr = jnp.minimum(lam_re, -1e-4)
    li = lam_im
    step = jnp.exp(log_step)[:, None]
    mag = jnp.exp(lr * step)
    ar = mag * jnp.cos(li * step)
    ai = mag * jnp.sin(li * step)
    inv = 1.0 / (lr * lr + li * li)
    zr = ((ar - 1.0) * lr + ai * li) * inv
    zi = (ai * lr - (ar - 1.0) * li) * inv
    bbr = zr[..., None] * b_re - zi[..., None] * b_im
    bbi = zr[..., None] * b_im + zi[..., None] * b_re
    eye = jnp.eye(gh, dtype=F32)

    def expand_in(bb):
        bb = bb.reshape(S5_HALF, gh, S5_STATE, S5_GROUP_CH)
        return jnp.einsum('fgph,gk->fghkp', bb, eye).reshape(
            S5_HALF, gh * S5_GROUP_CH, gh * S5_STATE).astype(BF16)

    def expand_out(c):
        c = c.reshape(S5_HALF, gh, S5_GROUP_CH, S5_STATE)
        return jnp.einsum('fghp,gk->fgpkh', c, eye).reshape(
            S5_HALF, gh * S5_STATE, gh * S5_GROUP_CH).astype(BF16)

    fixed2 = lambda i: (0, 0)
    fixed3 = lambda i: (0, 0, 0)
    return pl.pallas_call(
        _s5_kernel,
        grid=(seq // tc,),
        in_specs=[pl.BlockSpec((tc, bsz, w), lambda i: (i, 0, 2)),
                  pl.BlockSpec((S5_HALF, w // S5_HALF, nstate // S5_HALF), fixed3),
                  pl.BlockSpec((S5_HALF, w // S5_HALF, nstate // S5_HALF), fixed3),
                  pl.BlockSpec((1, nstate), fixed2), pl.BlockSpec((1, nstate), fixed2),
                  pl.BlockSpec((S5_HALF, nstate // S5_HALF, w // S5_HALF), fixed3),
                  pl.BlockSpec((S5_HALF, nstate // S5_HALF, w // S5_HALF), fixed3),
                  pl.BlockSpec((1, w), fixed2), pl.BlockSpec((w, w), fixed2), pl.BlockSpec((1, w), fixed2)],
        out_specs=pl.BlockSpec((tc, bsz, w), lambda i: (i, 0, 0)),
        out_shape=jax.ShapeDtypeStruct((seq, bsz, w), BF16),
        scratch_shapes=[pltpu.VMEM((tc, bsz, nstate), F32), pltpu.VMEM((tc, bsz, nstate), F32),
                        pltpu.VMEM((bsz, nstate), F32), pltpu.VMEM((bsz, nstate), F32)],
        compiler_params=_cparams("arbitrary"),
        name="s5_glu",
    )(proj, expand_in(bbr), expand_in(bbi), ar.reshape(1, nstate), ai.reshape(1, nstate),
      expand_out(c_re), expand_out(c_im), d_skip.reshape(1, w), glu_w.astype(BF16), glu_b.reshape(1, w))


def _cd_prep_kernel(xc_ref, q_ref, k_ref, v_ref, pos_ref, cw_ref, cb_ref, wq_ref, wk_ref, wv_ref, wif_ref,
                    bif_ref, invf_ref, mq_ref, mk_ref, mv_ref, gt_ref, qd_ref, kd_ref, vd_ref, tail_ref):
    tl, w = xc_ref.shape
    pad = tail_ref.shape[0]

    @pl.when(pl.program_id(1) == 0)
    def _init():
        tail_ref[...] = jnp.zeros_like(tail_ref)

    xc = xc_ref[...]
    ext = jnp.concatenate([tail_ref[...], xc], axis=0)
    conv = cb_ref[...]
    for tap in range(CONV_WIDTH):
        off = pad - (CONV_WIDTH - 1) + tap
        conv = conv + ext[off:off + tl] * cw_ref[tap]
    tail_ref[...] = xc[tl - pad:]
    xconv = (conv * jax.nn.sigmoid(conv)).astype(BF16)
    q = _dot(xconv, wq_ref[...]).astype(BF16)
    k = (_dot(xconv, wk_ref[...]) * (MLSTM_HEAD_DIM ** -0.5)).astype(BF16)
    v = _dot(xc.astype(BF16), wv_ref[...]).astype(BF16)
    mq_ref[...] = q
    mk_ref[...] = k
    mv_ref[...] = v
    gt_ref[...] = (_dot(q, wif_ref[0]) + _dot(k, wif_ref[1]) + _dot(v, wif_ref[2]) + bif_ref[...])

    ang = pos_ref[...].astype(F32) * invf_ref[...]
    cos = jnp.cos(ang)
    sin = jnp.sin(ang)
    lane = lax.broadcasted_iota(jnp.int32, (1, LANES), 1)
    half = ROPE_DIMS // 2
    sin_lo = jnp.where(lane < half, -sin, 0.0)
    sin_hi = jnp.where((lane >= half) & (lane < ROPE_DIMS), sin, 0.0)
    for src, dst in ((q_ref, qd_ref), (k_ref, kd_ref)):
        for h in range(DSW_HEADS):
            sl = slice(h * DSW_HEAD_DIM, (h + 1) * DSW_HEAD_DIM)
            t = src[:, sl]
            rot = (t * cos + pltpu.roll(t, LANES - half, axis=1) * sin_lo
                   + pltpu.roll(t, half, axis=1) * sin_hi)
            dst[:, sl] = rot.astype(BF16)
    vd_ref[...] = v_ref[...].astype(BF16)


def _cd_prep(proj, positions, conv_w, conv_b, w_q, w_k, w_v, w_if, b_if):
    bsz, seq, _ = proj.shape
    w = C_WIDTH
    tl = PROJ_TL
    half = ROPE_DIMS // 2
    inv_freq = ROPE_THETA ** (-jnp.arange(half, dtype=F32) / half)
    invf = jnp.zeros((1, LANES), F32).at[0, :half].set(inv_freq).at[0, half:ROPE_DIMS].set(inv_freq)
    wif = jnp.zeros((3, w, LANES), F32).at[:, :, :2 * MLSTM_HEADS].set(w_if.reshape(3, w, 2 * MLSTM_HEADS))
    bif = jnp.zeros((1, LANES), F32).at[0, :2 * MLSTM_HEADS].set(b_if)
    col = lambda c: (lambda b, i: (b, i, c))
    fixed2 = lambda b, i: (0, 0)
    fixed3 = lambda b, i: (0, 0, 0)
    tok_bf = jax.ShapeDtypeStruct((bsz, seq, w), BF16)
    return pl.pallas_call(
        _cd_prep_kernel,
        grid=(bsz, seq // tl),
        in_specs=[pl.BlockSpec((None, tl, w), col(0)), pl.BlockSpec((None, tl, w), col(2)),
                  pl.BlockSpec((None, tl, w), col(3)), pl.BlockSpec((None, tl, w), col(4)),
                  pl.BlockSpec((None, tl, 1), col(0)),
                  pl.BlockSpec((CONV_WIDTH, 1, w), fixed3), pl.BlockSpec((1, w), fixed2),
                  pl.BlockSpec((w, w), fixed2), pl.BlockSpec((w, w), fixed2), pl.BlockSpec((w, w), fixed2),
                  pl.BlockSpec((3, w, LANES), fixed3), pl.BlockSpec((1, LANES), fixed2),
                  pl.BlockSpec((1, LANES), fixed2)],
        out_specs=[pl.BlockSpec((None, tl, w), col(0))] * 3 + [pl.BlockSpec((None, tl, LANES), col(0))]
                  + [pl.BlockSpec((None, tl, w), col(0))] * 3,
        out_shape=[tok_bf, tok_bf, tok_bf, jax.ShapeDtypeStruct((bsz, seq, LANES), F32),
                   tok_bf, tok_bf, tok_bf],
        scratch_shapes=[pltpu.VMEM((8, w), F32)],
        compiler_params=_cparams("arbitrary", "arbitrary"),
        name="cd_prep",
    )(proj, proj, proj, proj, positions.reshape(bsz, seq, 1), conv_w.reshape(CONV_WIDTH, 1, w),
      conv_b.reshape(1, w), _block_diag(w_q).astype(BF16), _block_diag(w_k).astype(BF16),
      _block_diag(w_v).astype(BF16), wif.astype(BF16), bif, invf)


def _cumsum_rows(x):
    n = x.shape[0]
    row = lax.broadcasted_iota(jnp.int32, x.shape, 0)
    shift = 1
    while shift < n:
        x = x + jnp.where(row >= shift, pltpu.roll(x, shift, axis=0), 0.0)
        shift *= 2
    return x


def _mlstm_kernel(q_ref, k_ref, v_ref, g_ref, op_ref, o_ref, c_ref, n_ref, m_ref):
    cs = q_ref.shape[0]
    dh = MLSTM_HEAD_DIM

    @pl.when(pl.program_id(1) == 0)
    def _init():
        c_ref[...] = jnp.zeros_like(c_ref)
        n_ref[...] = jnp.zeros_like(n_ref)
        m_ref[...] = jnp.full_like(m_ref, NEG_INF)

    gates = g_ref[...]
    cum = _cumsum_rows(jax.nn.log_sigmoid(gates))
    gates_t = gates.T
    cum_t = cum.T
    causal = (lax.broadcasted_iota(jnp.int32, (cs, cs), 0) >= lax.broadcasted_iota(jnp.int32, (cs, cs), 1))
    for h in range(MLSTM_HEADS):
        sl = slice(h * dh, (h + 1) * dh)
        fcol = MLSTM_HEADS + h
        li_c, li_r = gates[:, h:h + 1], gates_t[h:h + 1, :]
        cum_c, cum_r = cum[:, fcol:fcol + 1], cum_t[fcol:fcol + 1, :]
        m_prev = m_ref[h][:, 0:1]
        q, k, v = q_ref[:, sl], k_ref[:, sl], v_ref[:, sl]
        log_d = jnp.where(causal, cum_c - cum_r + li_r, NEG_INF)
        log_inter = cum_c + m_prev
        m_s = jnp.maximum(jnp.max(log_d, axis=1, keepdims=True), log_inter)
        s = _dot_nt(q, k) * jnp.exp(log_d - m_s)
        inter = jnp.exp(log_inter - m_s)
        c_prev = c_ref[h]
        n_prev = n_ref[h]
        num = _dot(s.astype(BF16), v) + inter * _dot(q, c_prev.astype(BF16))
        den = (jnp.sum(s, axis=1, keepdims=True)
               + inter * jnp.sum(q.astype(F32) * n_prev, axis=1, keepdims=True))
        hh = num / jnp.maximum(jnp.abs(den), jnp.exp(-m_s))
        o_ref[:, sl] = (jax.nn.sigmoid(op_ref[:, sl]) * hh).astype(o_ref.dtype)
        chunk_f = cum_c[cs - 1:cs, :]
        to_end = chunk_f - cum_c + li_c
        m_c = jnp.max(to_end, axis=0, keepdims=True)
        m_new = jnp.maximum(chunk_f + m_prev, m_c)
        s_old = jnp.exp(chunk_f + m_prev - m_new)
        s_new = jnp.exp(m_c - m_new)
        kw = k.astype(F32) * jnp.exp(to_end - m_c)
        c_ref[h] = s_old * c_prev + s_new * _dot(kw.T.astype(BF16), v)
        n_ref[h] = s_old * n_prev + s_new * jnp.sum(kw, axis=0, keepdims=True)
        m_ref[h] = jnp.broadcast_to(m_new, (1, LANES))


def _mlstm(mq, mk, mv, gates, proj):
    bsz, seq, w = mq.shape
    cs = MLSTM_CS
    tok = lambda b, i: (b, i, 0)
    return pl.pallas_call(
        _mlstm_kernel,
        grid=(bsz, seq // cs),
        in_specs=[pl.BlockSpec((None, cs, w), tok)] * 3
                 + [pl.BlockSpec((None, cs, LANES), tok), pl.BlockSpec((None, cs, w), lambda b, i: (b, i, 1))],
        out_specs=pl.BlockSpec((None, cs, w), tok),
        out_shape=jax.ShapeDtypeStruct((bsz, seq, w), BF16),
        scratch_shapes=[pltpu.VMEM((MLSTM_HEADS, MLSTM_HEAD_DIM, MLSTM_HEAD_DIM), F32),
                        pltpu.VMEM((MLSTM_HEADS, 1, MLSTM_HEAD_DIM), F32),
                        pltpu.VMEM((MLSTM_HEADS, 1, LANES), F32)],
        compiler_params=_cparams("arbitrary", "arbitrary"),
        name="mlstm",
    )(mq, mk, mv, gates, proj)


def _dsw_kernel(q_ref, k_ref, v_ref, kp_ref, vp_ref, o_ref, l_ref, *, nres, nsub, span):
    blk = DSW_BLOCK
    dh = DSW_HEAD_DIM
    j = pl.program_id(2)
    qi = lax.broadcasted_iota(jnp.int32, (blk, 2 * blk), 0)
    ki = lax.broadcasted_iota(jnp.int32, (blk, 2 * blk), 1)
    dist = qi + blk - ki
    band = (dist >= 0) & (dist <= span)
    band_first = band & ((ki >= blk) | (j > 0))
    lane = lax.broadcasted_iota(jnp.int32, (blk, LANES), 1)
    scale = dh ** -0.5
    for r in range(nres):
        for sb in range(nsub):
            rows = slice(sb * blk, (sb + 1) * blk)
            lse_tile = jnp.zeros((blk, LANES), F32)
            for h in range(DSW_HEADS):
                cols = slice(r * D_WIDTH + h * dh, r * D_WIDTH + (h + 1) * dh)
                if sb == 0:
                    k_prev, v_prev = kp_ref[:, cols], vp_ref[:, cols]
                else:
                    prev_rows = slice((sb - 1) * blk, sb * blk)
                    k_prev, v_prev = k_ref[prev_rows, cols], v_ref[prev_rows, cols]
                kk = jnp.concatenate([k_prev, k_ref[rows, cols]], axis=0)
                vv = jnp.concatenate([v_prev, v_ref[rows, cols]], axis=0)
                s = _dot_nt(q_ref[rows, cols], kk) * scale
                s = jnp.where(band_first if sb == 0 else band, s, NEG_INF)
                m = jnp.max(s, axis=1, keepdims=True)
                p = jnp.exp(s - m)
                l = jnp.sum(p, axis=1, keepdims=True)
                o_ref[rows, cols] = _dot(p.astype(BF16), vv) / l
                lse_tile = jnp.where(lane == h, m + jnp.log(l), lse_tile)
            l_ref[rows, r * LANES:(r + 1) * LANES] = lse_tile


def _dsw_config(qd, kd, vd, window, dil):
    bsz, seq, w = qd.shape
    n = seq // dil
    tq = min(512, n)
    nres = 1 if n >= 512 else 512 // n
    nsub = tq // DSW_BLOCK
    view = lambda a: a.reshape(bsz, n, dil * w)
    cur = lambda b, r, j: (b, j, r)
    prev = lambda b, r, j: (b, jnp.maximum(j * nsub - 1, 0), r)
    o, lse = pl.pallas_call(
        functools.partial(_dsw_kernel, nres=nres, nsub=nsub, span=window // dil),
        grid=(bsz, dil // nres, n // tq),
        in_specs=[pl.BlockSpec((None, tq, nres * w), cur)] * 3
                 + [pl.BlockSpec((None, DSW_BLOCK, nres * w), prev)] * 2,
        out_specs=[pl.BlockSpec((None, tq, nres * w), cur), pl.BlockSpec((None, tq, nres * LANES), cur)],
        out_shape=[jax.ShapeDtypeStruct((bsz, n, dil * w), F32),
                   jax.ShapeDtypeStruct((bsz, n, dil * LANES), F32)],
        compiler_params=_cparams("arbitrary", "arbitrary", "arbitrary"),
        name=f"dsw_dil{dil}",
    )(view(qd), view(kd), view(vd), view(kd), view(vd))
    return o.reshape(bsz, seq, w), lse.reshape(bsz, seq, LANES)


def _dsw_combine_kernel(o0_ref, o1_ref, o2_ref, l0_ref, l1_ref, l2_ref, y_ref):
    dh = DSW_HEAD_DIM
    outs = (o0_ref, o1_ref, o2_ref)
    lses = (l0_ref[...], l1_ref[...], l2_ref[...])
    mx = jnp.maximum(jnp.maximum(lses[0], lses[1]), lses[2])
    ws = [jnp.exp(l - mx) for l in lses]
    tot = ws[0] + ws[1] + ws[2]
    for h in range(DSW_HEADS):
        sl = slice(h * dh, (h + 1) * dh)
        acc = sum(ws[g][:, h:h + 1] * outs[g][:, sl] for g in range(3))
        y_ref[:, sl] = (acc / tot[:, h:h + 1]).astype(y_ref.dtype)


def _dsw_attention(qd, kd, vd):
    bsz, seq, w = qd.shape
    outs, lses = [], []
    for window, dil in DSW_CONFIGS:
        o, lse = _dsw_config(qd, kd, vd, window, dil)
        outs.append(o)
        lses.append(lse)
    tl = PROJ_TL
    tok = lambda b, i: (b, i, 0)
    return pl.pallas_call(
        _dsw_combine_kernel,
        grid=(bsz, seq // tl),
        in_specs=[pl.BlockSpec((None, tl, w), tok)] * 3 + [pl.BlockSpec((None, tl, LANES), tok)] * 3,
        out_specs=pl.BlockSpec((None, tl, w), tok),
        out_shape=jax.ShapeDtypeStruct((bsz, seq, w), BF16),
        compiler_params=_cparams("arbitrary", "arbitrary"),
        name="dsw_combine",
    )(*outs, *lses)


def kernel(x, positions, ab_w_in, ab_conv_w, ab_conv_b, ab_gate_r_w, ab_gate_r_b, ab_gate_i_w, ab_gate_i_b, ab_lru_lambda, ab_s5_lambda_re, ab_s5_lambda_im, ab_s5_log_step, ab_s5_b_re, ab_s5_b_im, ab_s5_c_re, ab_s5_c_im, ab_s5_d, ab_glu_w, ab_glu_b, ab_w_out, cd_w_in, cd_conv_w, cd_conv_b, cd_w_q, cd_w_k, cd_w_v, cd_w_if, cd_b_if, cd_w_out, ln_mix_g, ln_mix_b, ln_ffn_g, ln_ffn_b, moe_router_w, moe_router_b, moe_w_in, moe_b_in, moe_w_out, moe_b_out):
    bsz, seq, d = x.shape
    t = bsz * seq
    for layer in range(DEPTH):
        j = layer // 2
        if layer % 2 == 0:
            proj = _inproj_time_major(x, ab_w_in[j].astype(BF16))
            ya = _rg_lru(proj, ab_conv_w[j], ab_conv_b[j], ab_gate_r_w[j], ab_gate_r_b[j],
                         ab_gate_i_w[j], ab_gate_i_b[j], ab_lru_lambda[j])
            yb = _s5_glu(proj, ab_s5_lambda_re[j], ab_s5_lambda_im[j], ab_s5_log_step[j], ab_s5_b_re[j],
                         ab_s5_b_im[j], ab_s5_c_re[j], ab_s5_c_im[j], ab_s5_d[j], ab_glu_w[j], ab_glu_b[j])
            time_major, w_out = True, ab_w_out[j]
        else:
            proj = _inproj(x, cd_w_in[j].astype(BF16))
            mq, mk, mv, gates, qd, kd, vd = _cd_prep(proj, positions, cd_conv_w[j], cd_conv_b[j], cd_w_q[j],
                                                     cd_w_k[j], cd_w_v[j], cd_w_if[j], cd_b_if[j])
            ya = _mlstm(mq, mk, mv, gates, proj)
            yb = _dsw_attention(qd, kd, vd)
            time_major, w_out = False, cd_w_out[j]
        xn, logits = _outproj_ln_router(ya, yb, time_major, w_out.astype(BF16), x, ln_mix_g[layer],
                                        ln_mix_b[layer], moe_router_w[layer], moe_router_b[layer])
        ffn = _moe_ffn(xn.reshape(t, d), logits.reshape(t, N_EXPERTS), moe_w_in[layer], moe_b_in[layer],
                       moe_w_out[layer], moe_b_out[layer])
        x = _residual_ln(xn.reshape(t, d), ffn, ln_ffn_g[layer], ln_ffn_b[layer]).reshape(bsz, seq, d)
    return x
```

```python
import functools

import jax
import jax.numpy as jnp
from jax import lax
from jax.experimental import pallas as pl
from jax.experimental.pallas import tpu as pltpu

F32 = jnp.float32
BF16 = jnp.bfloat16

D_MODEL = 1024
DEPTH = 2
A_WIDTH = 512
B_WIDTH = 512
C_WIDTH = 512
D_WIDTH = 512
LRU_HEADS = 8
LRU_HEAD_DIM = A_WIDTH // LRU_HEADS
LRU_C = 8.0
CONV_WIDTH = 4
S5_GROUP_CH = 16
S5_GROUPS = B_WIDTH // S5_GROUP_CH
S5_STATE = 64
MLSTM_HEADS = 4
MLSTM_HEAD_DIM = C_WIDTH // MLSTM_HEADS
DSW_HEADS = 4
DSW_HEAD_DIM = D_WIDTH // DSW_HEADS
DSW_CONFIGS = ((128, 1), (512, 4), (2048, 16))
DSW_BLOCK = 128
ROPE_THETA = 500000.0
ROPE_DIMS = DSW_HEAD_DIM // 4
N_EXPERTS = 32
TOP_K = 4
D_EXPERT = D_MODEL
SWIGLU_LIMIT = 7.0
SWIGLU_ALPHA = 1.702
DEEPNORM_ALPHA = (2 * DEPTH) ** 0.25
LN_EPS = 1e-5

VMEM_LIMIT_BYTES = 56 * 1024 * 1024
LANES = 128
MOE_TM = 256
PROJ_TL = 512
LRU_TC = 128
S5_TC = 64
S5_HALF = 2
MLSTM_CS = 256
NEG_INF = float("-inf")


def _cparams(*sem):
    return pltpu.CompilerParams(dimension_semantics=sem, vmem_limit_bytes=VMEM_LIMIT_BYTES)


def _dot(a, b):
    return jnp.dot(a, b, preferred_element_type=F32)


def _dot_nt(a, b):
    return lax.dot_general(a, b, (((1,), (1,)), ((), ())), preferred_element_type=F32)


def _matmul_kernel(x_ref, w_ref, o_ref):
    o_ref[...] = _dot(x_ref[...].astype(BF16), w_ref[...])


def _inproj_time_major(x3, w_bf16):
    bsz, seq, k = x3.shape
    n = w_bf16.shape[1]
    out = pl.pallas_call(
        _matmul_kernel,
        grid=(bsz, seq // PROJ_TL),
        in_specs=[pl.BlockSpec((None, PROJ_TL, k), lambda b, i: (b, i, 0)),
                  pl.BlockSpec((k, n), lambda b, i: (0, 0))],
        out_specs=pl.BlockSpec((PROJ_TL, n), lambda b, i: (i, b)),
        out_shape=jax.ShapeDtypeStruct((seq, bsz * n), F32),
        compiler_params=_cparams("arbitrary", "arbitrary"),
        name="in_proj_time_major",
    )(x3, w_bf16)
    return out.reshape(seq, bsz, n)


def _inproj(x3, w_bf16):
    bsz, seq, k = x3.shape
    n = w_bf16.shape[1]
    return pl.pallas_call(
        _matmul_kernel,
        grid=(bsz, seq // PROJ_TL),
        in_specs=[pl.BlockSpec((None, PROJ_TL, k), lambda b, i: (b, i, 0)),
                  pl.BlockSpec((k, n), lambda b, i: (0, 0))],
        out_specs=pl.BlockSpec((None, PROJ_TL, n), lambda b, i: (b, i, 0)),
        out_shape=jax.ShapeDtypeStruct((bsz, seq, n), F32),
        compiler_params=_cparams("arbitrary", "arbitrary"),
        name="in_proj",
    )(x3, w_bf16)


def _layer_norm_rows(z, g, b):
    mu = jnp.mean(z, axis=-1, keepdims=True)
    zc = z - mu
    var = jnp.mean(zc * zc, axis=-1, keepdims=True)
    return zc * lax.rsqrt(var + LN_EPS) * g + b


def _outproj_ln_router_kernel(ya_ref, yb_ref, wa_ref, wb_ref, x_ref, g_ref, b_ref, rwh_ref, rwl_ref, rb_ref,
                              xn_ref, lg_ref):
    mix = _dot(ya_ref[...], wa_ref[...]) + _dot(yb_ref[...], wb_ref[...])
    xn = _layer_norm_rows(DEEPNORM_ALPHA * x_ref[...] + mix, g_ref[...], b_ref[...])
    xn_ref[...] = xn
    xh = xn.astype(BF16)
    xl = (xn - xh.astype(F32)).astype(BF16)
    lg_ref[...] = (_dot(xh, rwh_ref[...]) + _dot(xl, rwh_ref[...]) + _dot(xh, rwl_ref[...])) + rb_ref[...]


def _outproj_ln_router(ya, yb, time_major, w_bf16, x3, g, b, rw, rb):
    bsz, seq, d = x3.shape
    wa, wb = w_bf16[:ya.shape[-1]], w_bf16[ya.shape[-1]:]
    ne = rw.shape[1]
    rw_hi = rw.astype(BF16)
    rw_lo = (rw - rw_hi.astype(F32)).astype(BF16)
    tl = PROJ_TL
    if time_major:
        ya, yb = ya.reshape(seq, -1), yb.reshape(seq, -1)
        y_specs = [pl.BlockSpec((tl, wa.shape[0]), lambda bi, i: (i, bi)),
                   pl.BlockSpec((tl, wb.shape[0]), lambda bi, i: (i, bi))]
    else:
        y_specs = [pl.BlockSpec((None, tl, wa.shape[0]), lambda bi, i: (bi, i, 0)),
                   pl.BlockSpec((None, tl, wb.shape[0]), lambda bi, i: (bi, i, 0))]
    fixed = lambda bi, i: (0, 0)
    tok = lambda bi, i: (bi, i, 0)
    return pl.pallas_call(
        _outproj_ln_router_kernel,
        grid=(bsz, seq // tl),
        in_specs=y_specs + [pl.BlockSpec(wa.shape, fixed), pl.BlockSpec(wb.shape, fixed),
                            pl.BlockSpec((None, tl, d), tok),
                            pl.BlockSpec((1, d), fixed), pl.BlockSpec((1, d), fixed),
                            pl.BlockSpec((d, ne), fixed), pl.BlockSpec((d, ne), fixed),
                            pl.BlockSpec((1, ne), fixed)],
        out_specs=[pl.BlockSpec((None, tl, d), tok), pl.BlockSpec((None, tl, ne), tok)],
        out_shape=[jax.ShapeDtypeStruct((bsz, seq, d), F32), jax.ShapeDtypeStruct((bsz, seq, ne), F32)],
        compiler_params=_cparams("arbitrary", "arbitrary"),
        name="out_proj_ln_router",
    )(ya, yb, wa, wb, x3, g.reshape(1, d), b.reshape(1, d), rw_hi, rw_lo, rb.reshape(1, ne))


def _residual_ln_kernel(x_ref, f_ref, g_ref, b_ref, o_ref):
    o_ref[...] = _layer_norm_rows(DEEPNORM_ALPHA * x_ref[...] + f_ref[...], g_ref[...], b_ref[...])


def _residual_ln(x, f, g, b, tm=1024):
    m, d = x.shape
    row = lambda i: (i, 0)
    fixed = lambda i: (0, 0)
    return pl.pallas_call(
        _residual_ln_kernel,
        grid=(m // tm,),
        in_specs=[pl.BlockSpec((tm, d), row), pl.BlockSpec((tm, d), row),
                  pl.BlockSpec((1, d), fixed), pl.BlockSpec((1, d), fixed)],
        out_specs=pl.BlockSpec((tm, d), row),
        out_shape=jax.ShapeDtypeStruct((m, d), F32),
        compiler_params=_cparams("arbitrary"),
        name="residual_ln",
    )(x, f, g.reshape(1, d), b.reshape(1, d))


def _moe_kernel(be_ref, nreal_ref, xs_ref, win_ref, bin_ref, wout_ref, bout_ref, o_ref, win_bf, wout_bf):
    i = pl.program_id(0)
    e = be_ref[i]
    prev = be_ref[jnp.maximum(i - 1, 0)]

    @pl.when((i == 0) | (e != prev))
    def _load_expert():
        win_bf[...] = win_ref[...].astype(BF16)
        wout_bf[...] = wout_ref[...].astype(BF16)

    @pl.when(i < nreal_ref[0])
    def _compute():
        h = _dot(xs_ref[...], win_bf[...]) + bin_ref[...]
        g = jnp.minimum(h[:, :D_EXPERT], SWIGLU_LIMIT)
        lin = jnp.clip(h[:, D_EXPERT:], -SWIGLU_LIMIT, SWIGLU_LIMIT)
        y = g * jax.nn.sigmoid(SWIGLU_ALPHA * g) * (lin + 1.0)
        o_ref[...] = _dot(y.astype(BF16), wout_bf[...]) + bout_ref[...]

    @pl.when(i >= nreal_ref[0])
    def _unused_block():
        o_ref[...] = jnp.zeros_like(o_ref)


def _moe_experts(xs, block_expert, n_real, layer, w_in, b_in, w_out, b_out):
    n_rows, d = xs.shape
    tm = MOE_TM
    n_blocks = n_rows // tm
    nl, ne, _, dh2 = w_in.shape
    grid_spec = pltpu.PrefetchScalarGridSpec(
        num_scalar_prefetch=2,
        grid=(n_blocks,),
        in_specs=[
            pl.BlockSpec((tm, d), lambda i, be, nr: (i, 0)),
            pl.BlockSpec((None, None, d, dh2), lambda i, be, nr: (layer, be[i], 0, 0)),
            pl.BlockSpec((None, None, 1, dh2), lambda i, be, nr: (layer, be[i], 0, 0)),
            pl.BlockSpec((None, None, D_EXPERT, d), lambda i, be, nr: (layer, be[i], 0, 0)),
            pl.BlockSpec((None, None, 1, d), lambda i, be, nr: (layer, be[i], 0, 0)),
        ],
        out_specs=pl.BlockSpec((tm, d), lambda i, be, nr: (i, 0)),
        scratch_shapes=[pltpu.VMEM((d, dh2), BF16), pltpu.VMEM((D_EXPERT, d), BF16)],
    )
    return pl.pallas_call(
        _moe_kernel,
        grid_spec=grid_spec,
        out_shape=jax.ShapeDtypeStruct((n_rows, d), F32),
        compiler_params=_cparams("arbitrary"),
        name="moe_experts",
    )(block_expert, n_real, xs, w_in, b_in.reshape(nl, ne, 1, dh2), w_out, b_out.reshape(nl, ne, 1, d))


def _moe_ffn(xn, logits, layer, w_in, b_in, w_out, b_out):
    t, d = xn.shape
    tm = MOE_TM
    top_logit, top_idx = lax.top_k(logits, TOP_K)
    gate = jax.nn.softmax(top_logit, axis=-1)
    flat_e = top_idx.reshape(-1).astype(jnp.int32)
    onehot = (flat_e[:, None] == jnp.arange(N_EXPERTS, dtype=jnp.int32)[None, :]).astype(jnp.int32)
    csum = jnp.cumsum(onehot, axis=0)
    rank = jnp.sum(csum * onehot, axis=1) - 1
    counts = csum[-1]
    blocks_per_e = (counts + tm - 1) // tm
    blk_end = jnp.cumsum(blocks_per_e)
    blk_start = blk_end - blocks_per_e
    dest = blk_start[flat_e] * tm + rank
    n_blocks = (t * TOP_K) // tm + N_EXPERTS
    n_rows = n_blocks * tm
    row_tok = jnp.zeros((n_rows,), jnp.int32).at[dest].set(jnp.arange(t * TOP_K, dtype=jnp.int32) // TOP_K)
    block_expert = jnp.minimum(
        jnp.sum((blk_end[None, :] <= jnp.arange(n_blocks, dtype=jnp.int32)[:, None]).astype(jnp.int32), axis=1),
        N_EXPERTS - 1)
    n_real = blk_end[-1:].astype(jnp.int32)
    xs = xn.astype(BF16)[row_tok]
    yb = _moe_experts(xs, block_expert, n_real, layer, w_in, b_in, w_out, b_out)
    dest = dest.reshape(t, TOP_K)
    return sum(yb[dest[:, k]] * gate[:, k:k + 1] for k in range(TOP_K))


def _softplus(x):
    return jnp.maximum(x, 0.0) + jnp.log(1.0 + jnp.exp(-jnp.abs(x)))


def _gelu_tanh(x):
    return 0.5 * x * (1.0 + jnp.tanh(0.7978845608028654 * (x + 0.044715 * (x * x * x))))


def _lru_kernel(xa_ref, ga_ref, cw_ref, cb_ref, wr_ref, br_ref, wi_ref, bi_ref, lam_ref, o_ref,
                tail_ref, h_ref, a_ref, u_ref):
    tc, bsz, w = xa_ref.shape

    @pl.when(pl.program_id(0) == 0)
    def _init():
        tail_ref[...] = jnp.zeros_like(tail_ref)
        h_ref[...] = jnp.zeros_like(h_ref)

    xa = xa_ref[...]
    ext = jnp.concatenate([tail_ref[...], xa], axis=0)
    xc = cb_ref[...] + ext[0:tc] * cw_ref[0]
    for tap in range(1, CONV_WIDTH):
        xc = xc + ext[tap:tap + tc] * cw_ref[tap]
    tail_ref[...] = xa[tc - (CONV_WIDTH - 1):]
    x2 = xc.reshape(tc * bsz, w)
    xb = x2.astype(BF16)
    r = jax.nn.sigmoid(_dot(xb, wr_ref[...]) + br_ref[...])
    ig = jax.nn.sigmoid(_dot(xb, wi_ref[...]) + bi_ref[...])
    log_a = (-LRU_C) * r * _softplus(-lam_ref[...])
    a_ref[...] = jnp.exp(log_a).reshape(tc, bsz, w)
    u_ref[...] = (jnp.sqrt(1.0 - jnp.exp(2.0 * log_a)) * (ig * x2)).reshape(tc, bsz, w)

    def step(t, h):
        h = a_ref[t] * h + u_ref[t]
        u_ref[t] = h
        return h

    h_ref[...] = lax.fori_loop(0, tc, step, h_ref[...], unroll=8)
    o_ref[...] = (u_ref[...] * _gelu_tanh(ga_ref[...])).astype(o_ref.dtype)


def _block_diag(w):
    h, i, j = w.shape
    return jnp.einsum('hij,hk->hikj', w, jnp.eye(h, dtype=w.dtype)).reshape(h * i, h * j)


def _rg_lru(proj, conv_w, conv_b, gate_r_w, gate_r_b, gate_i_w, gate_i_b, lru_lambda):
    seq, bsz, _ = proj.shape
    w = A_WIDTH
    tc = LRU_TC
    fixed2 = lambda i: (0, 0)
    fixed3 = lambda i: (0, 0, 0)
    return pl.pallas_call(
        _lru_kernel,
        grid=(seq // tc,),
        in_specs=[pl.BlockSpec((tc, bsz, w), lambda i: (i, 0, 0)),
                  pl.BlockSpec((tc, bsz, w), lambda i: (i, 0, 1)),
                  pl.BlockSpec((CONV_WIDTH, 1, w), fixed3), pl.BlockSpec((1, w), fixed2),
                  pl.BlockSpec((w, w), fixed2), pl.BlockSpec((1, w), fixed2),
                  pl.BlockSpec((w, w), fixed2), pl.BlockSpec((1, w), fixed2),
                  pl.BlockSpec((1, w), fixed2)],
        out_specs=pl.BlockSpec((tc, bsz, w), lambda i: (i, 0, 0)),
        out_shape=jax.ShapeDtypeStruct((seq, bsz, w), BF16),
        scratch_shapes=[pltpu.VMEM((CONV_WIDTH - 1, bsz, w), F32), pltpu.VMEM((bsz, w), F32),
                        pltpu.VMEM((tc, bsz, w), F32), pltpu.VMEM((tc, bsz, w), F32)],
        compiler_params=_cparams("arbitrary"),
        name="rg_lru",
    )(proj, proj, conv_w.reshape(CONV_WIDTH, 1, w), conv_b.reshape(1, w),
      _block_diag(gate_r_w).astype(BF16), gate_r_b.reshape(1, w),
      _block_diag(gate_i_w).astype(BF16), gate_i_b.reshape(1, w), lru_lambda.reshape(1, w))


def _s5_kernel(u_ref, wbr_ref, wbi_ref, ar_ref, ai_ref, ccr_ref, cci_ref, d_ref, gw_ref, gb_ref, o_ref,
               xr_ref, xi_ref, sr_ref, si_ref):
    tc, bsz, w = u_ref.shape
    nstate = xr_ref.shape[-1]
    sh = nstate // S5_HALF
    wh = w // S5_HALF

    @pl.when(pl.program_id(0) == 0)
    def _init():
        sr_ref[...] = jnp.zeros_like(sr_ref)
        si_ref[...] = jnp.zeros_like(si_ref)

    u2 = u_ref[...].reshape(tc * bsz, w)
    ub = u2.astype(BF16)
    for hf in range(S5_HALF):
        uh = ub[:, hf * wh:(hf + 1) * wh]
        xr_ref[:, :, hf * sh:(hf + 1) * sh] = _dot(uh, wbr_ref[hf]).reshape(tc, bsz, sh)
        xi_ref[:, :, hf * sh:(hf + 1) * sh] = _dot(uh, wbi_ref[hf]).reshape(tc, bsz, sh)

    for hf in range(S5_HALF):
        lo, hi = hf * sh, (hf + 1) * sh
        ar = jnp.broadcast_to(ar_ref[:, lo:hi], (bsz, sh))
        ai = jnp.broadcast_to(ai_ref[:, lo:hi], (bsz, sh))

        def step(t, carry, lo=lo, hi=hi, ar=ar, ai=ai):
            xr, xi = carry
            nxr = ar * xr - ai * xi + xr_ref[t, :, lo:hi]
            nxi = ar * xi + ai * xr + xi_ref[t, :, lo:hi]
            xr_ref[t, :, lo:hi] = nxr
            xi_ref[t, :, lo:hi] = nxi
            return nxr, nxi

        xr, xi = lax.fori_loop(0, tc, step, (sr_ref[:, lo:hi], si_ref[:, lo:hi]), unroll=4)
        sr_ref[:, lo:hi] = xr
        si_ref[:, lo:hi] = xi

    ys = []
    for hf in range(S5_HALF):
        lo, hi = hf * sh, (hf + 1) * sh
        xrh = xr_ref[:, :, lo:hi].reshape(tc * bsz, sh).astype(BF16)
        xih = xi_ref[:, :, lo:hi].reshape(tc * bsz, sh).astype(BF16)
        ys.append(_dot(xrh, ccr_ref[hf]) - _dot(xih, cci_ref[hf]))
    y = jnp.concatenate(ys, axis=1) + d_ref[...] * u2
    s = _gelu_tanh(y)
    yb = s * jax.nn.sigmoid(_dot(s.astype(BF16), gw_ref[...]) + gb_ref[...])
    o_ref[...] = yb.reshape(tc, bsz, w).astype(o_ref.dtype)


def _s5_glu(proj, lam_re, lam_im, log_step, b_re, b_im, c_re, c_im, d_skip, glu_w, glu_b):
    seq, bsz, _ = proj.shape
    w = B_WIDTH
    tc = S5_TC
    nstate = S5_GROUPS * S5_STATE
    gh = S5_GROUPS // S5_HALF
    lr = jnp.minimum(lam_re, -1e-4)
    li = lam_im
    step = jnp.exp(log_step)[:, None]
    mag = jnp.exp(lr * step)
    ar = mag * jnp.cos(li * step)
    ai = mag * jnp.sin(li * step)
    inv = 1.0 / (lr * lr + li * li)
    zr = ((ar - 1.0) * lr + ai * li) * inv
    zi = (ai * lr - (ar - 1.0) * li) * inv
    bbr = zr[..., None] * b_re - zi[..., None] * b_im
    bbi = zr[..., None] * b_im + zi[..., None] * b_re
    eye = jnp.eye(gh, dtype=F32)

    def expand_in(bb):
        bb = bb.reshape(S5_HALF, gh, S5_STATE, S5_GROUP_CH)
        return jnp.einsum('fgph,gk->fghkp', bb, eye).reshape(
            S5_HALF, gh * S5_GROUP_CH, gh * S5_STATE).astype(BF16)

    def expand_out(c):
        c = c.reshape(S5_HALF, gh, S5_GROUP_CH, S5_STATE)
        return jnp.einsum('fghp,gk->fgpkh', c, eye).reshape(
            S5_HALF, gh * S5_STATE, gh * S5_GROUP_CH).astype(BF16)

    fixed2 = lambda i: (0, 0)
    fixed3 = lambda i: (0, 0, 0)
    return pl.pallas_call(
        _s5_kernel,
        grid=(seq // tc,),
        in_specs=[pl.BlockSpec((tc, bsz, w), lambda i: (i, 0, 2)),
                  pl.BlockSpec((S5_HALF, w // S5_HALF, nstate // S5_HALF), fixed3),
                  pl.BlockSpec((S5_HALF, w // S5_HALF, nstate // S5_HALF), fixed3),
                  pl.BlockSpec((1, nstate), fixed2), pl.BlockSpec((1, nstate), fixed2),
                  pl.BlockSpec((S5_HALF, nstate // S5_HALF, w // S5_HALF), fixed3),
                  pl.BlockSpec((S5_HALF, nstate // S5_HALF, w // S5_HALF), fixed3),
                  pl.BlockSpec((1, w), fixed2), pl.BlockSpec((w, w), fixed2), pl.BlockSpec((1, w), fixed2)],
        out_specs=pl.BlockSpec((tc, bsz, w), lambda i: (i, 0, 0)),
        out_shape=jax.ShapeDtypeStruct((seq, bsz, w), BF16),
        scratch_shapes=[pltpu.VMEM((tc, bsz, nstate), F32), pltpu.VMEM((tc, bsz, nstate), F32),
                        pltpu.VMEM((bsz, nstate), F32), pltpu.VMEM((bsz, nstate), F32)],
        compiler_params=_cparams("arbitrary"),
        name="s5_glu",
    )(proj, expand_in(bbr), expand_in(bbi), ar.reshape(1, nstate), ai.reshape(1, nstate),
      expand_out(c_re), expand_out(c_im), d_skip.reshape(1, w), glu_w.astype(BF16), glu_b.reshape(1, w))


def _cd_prep_kernel(xc_ref, q_ref, k_ref, v_ref, pos_ref, cw_ref, cb_ref, wq_ref, wk_ref, wv_ref, wif_ref,
                    bif_ref, invf_ref, mq_ref, mk_ref, mv_ref, gt_ref, qd_ref, kd_ref, vd_ref, tail_ref):
    tl, w = xc_ref.shape
    pad = tail_ref.shape[0]

    @pl.when(pl.program_id(1) == 0)
    def _init():
        tail_ref[...] = jnp.zeros_like(tail_ref)

    xc = xc_ref[...]
    ext = jnp.concatenate([tail_ref[...], xc], axis=0)
    conv = cb_ref[...]
    for tap in range(CONV_WIDTH):
        off = pad - (CONV_WIDTH - 1) + tap
        conv = conv + ext[off:off + tl] * cw_ref[tap]
    tail_ref[...] = xc[tl - pad:]
    xconv = (conv * jax.nn.sigmoid(conv)).astype(BF16)
    q = _dot(xconv, wq_ref[...]).astype(BF16)
    k = (_dot(xconv, wk_ref[...]) * (MLSTM_HEAD_DIM ** -0.5)).astype(BF16)
    v = _dot(xc.astype(BF16), wv_ref[...]).astype(BF16)
    mq_ref[...] = q
    mk_ref[...] = k
    mv_ref[...] = v
    gt_ref[...] = (_dot(q, wif_ref[0]) + _dot(k, wif_ref[1]) + _dot(v, wif_ref[2]) + bif_ref[...])

    ang = pos_ref[...].astype(F32) * invf_ref[...]
    cos = jnp.cos(ang)
    sin = jnp.sin(ang)
    lane = lax.broadcasted_iota(jnp.int32, (1, LANES), 1)
    half = ROPE_DIMS // 2
    sin_lo = jnp.where(lane < half, -sin, 0.0)
    sin_hi = jnp.where((lane >= half) & (lane < ROPE_DIMS), sin, 0.0)
    for src, dst in ((q_ref, qd_ref), (k_ref, kd_ref)):
        for h in range(DSW_HEADS):
            sl = slice(h * DSW_HEAD_DIM, (h + 1) * DSW_HEAD_DIM)
            t = src[:, sl]
            rot = (t * cos + pltpu.roll(t, LANES - half, axis=1) * sin_lo
                   + pltpu.roll(t, half, axis=1) * sin_hi)
            dst[:, sl] = rot.astype(BF16)
    vd_ref[...] = v_ref[...].astype(BF16)


def _cd_prep(proj, positions, conv_w, conv_b, w_q, w_k, w_v, w_if, b_if):
    bsz, seq, _ = proj.shape
    w = C_WIDTH
    tl = PROJ_TL
    half = ROPE_DIMS // 2
    inv_freq = ROPE_THETA ** (-jnp.arange(half, dtype=F32) / half)
    invf = jnp.zeros((1, LANES), F32).at[0, :half].set(inv_freq).at[0, half:ROPE_DIMS].set(inv_freq)
    wif = jnp.zeros((3, w, LANES), F32).at[:, :, :2 * MLSTM_HEADS].set(w_if.reshape(3, w, 2 * MLSTM_HEADS))
    bif = jnp.zeros((1, LANES), F32).at[0, :2 * MLSTM_HEADS].set(b_if)
    col = lambda c: (lambda b, i: (b, i, c))
    fixed2 = lambda b, i: (0, 0)
    fixed3 = lambda b, i: (0, 0, 0)
    tok_bf = jax.ShapeDtypeStruct((bsz, seq, w), BF16)
    return pl.pallas_call(
        _cd_prep_kernel,
        grid=(bsz, seq // tl),
        in_specs=[pl.BlockSpec((None, tl, w), col(0)), pl.BlockSpec((None, tl, w), col(2)),
                  pl.BlockSpec((None, tl, w), col(3)), pl.BlockSpec((None, tl, w), col(4)),
                  pl.BlockSpec((None, tl, 1), col(0)),
                  pl.BlockSpec((CONV_WIDTH, 1, w), fixed3), pl.BlockSpec((1, w), fixed2),
                  pl.BlockSpec((w, w), fixed2), pl.BlockSpec((w, w), fixed2), pl.BlockSpec((w, w), fixed2),
                  pl.BlockSpec((3, w, LANES), fixed3), pl.BlockSpec((1, LANES), fixed2),
                  pl.BlockSpec((1, LANES), fixed2)],
        out_specs=[pl.BlockSpec((None, tl, w), col(0))] * 3 + [pl.BlockSpec((None, tl, LANES), col(0))]
                  + [pl.BlockSpec((None, tl, w), col(0))] * 3,
        out_shape=[tok_bf, tok_bf, tok_bf, jax.ShapeDtypeStruct((bsz, seq, LANES), F32),
                   tok_bf, tok_bf, tok_bf],
        scratch_shapes=[pltpu.VMEM((8, w), F32)],
        compiler_params=_cparams("arbitrary", "arbitrary"),
        name="cd_prep",
    )(proj, proj, proj, proj, positions.reshape(bsz, seq, 1), conv_w.reshape(CONV_WIDTH, 1, w),
      conv_b.reshape(1, w), _block_diag(w_q).astype(BF16), _block_diag(w_k).astype(BF16),
      _block_diag(w_v).astype(BF16), wif.astype(BF16), bif, invf)


def _cumsum_rows(x):
    n = x.shape[0]
    row = lax.broadcasted_iota(jnp.int32, x.shape, 0)
    shift = 1
    while shift < n:
        x = x + jnp.where(row >= shift, pltpu.roll(x, shift, axis=0), 0.0)
        shift *= 2
    return x


def _mlstm_kernel(q_ref, k_ref, v_ref, g_ref, op_ref, o_ref, c_ref, n_ref, m_ref):
    cs = q_ref.shape[0]
    dh = MLSTM_HEAD_DIM

    @pl.when(pl.program_id(1) == 0)
    def _init():
        c_ref[...] = jnp.zeros_like(c_ref)
        n_ref[...] = jnp.zeros_like(n_ref)
        m_ref[...] = jnp.full_like(m_ref, NEG_INF)

    gates = g_ref[...]
    cum = _cumsum_rows(jax.nn.log_sigmoid(gates))
    gates_t = gates.T
    cum_t = cum.T
    causal = (lax.broadcasted_iota(jnp.int32, (cs, cs), 0) >= lax.broadcasted_iota(jnp.int32, (cs, cs), 1))
    for h in range(MLSTM_HEADS):
        sl = slice(h * dh, (h + 1) * dh)
        fcol = MLSTM_HEADS + h
        li_c, li_r = gates[:, h:h + 1], gates_t[h:h + 1, :]
        cum_c, cum_r = cum[:, fcol:fcol + 1], cum_t[fcol:fcol + 1, :]
        m_prev = m_ref[h][:, 0:1]
        q, k, v = q_ref[:, sl], k_ref[:, sl], v_ref[:, sl]
        log_d = jnp.where(causal, cum_c - cum_r + li_r, NEG_INF)
        log_inter = cum_c + m_prev
        m_s = jnp.maximum(jnp.max(log_d, axis=1, keepdims=True), log_inter)
        s = _dot_nt(q, k) * jnp.exp(log_d - m_s)
        inter = jnp.exp(log_inter - m_s)
        c_prev = c_ref[h]
        n_prev = n_ref[h]
        num = _dot(s.astype(BF16), v) + inter * _dot(q, c_prev.astype(BF16))
        den = (jnp.sum(s, axis=1, keepdims=True)
               + inter * jnp.sum(q.astype(F32) * n_prev, axis=1, keepdims=True))
        hh = num / jnp.maximum(jnp.abs(den), jnp.exp(-m_s))
        o_ref[:, sl] = (jax.nn.sigmoid(op_ref[:, sl]) * hh).astype(o_ref.dtype)
        chunk_f = cum_c[cs - 1:cs, :]
        to_end = chunk_f - cum_c + li_c
        m_c = jnp.max(to_end, axis=0, keepdims=True)
        m_new = jnp.maximum(chunk_f + m_prev, m_c)
        s_old = jnp.exp(chunk_f + m_prev - m_new)
        s_new = jnp.exp(m_c - m_new)
        kw = k.astype(F32) * jnp.exp(to_end - m_c)
        c_ref[h] = s_old * c_prev + s_new * _dot(kw.T.astype(BF16), v)
        n_ref[h] = s_old * n_prev + s_new * jnp.sum(kw, axis=0, keepdims=True)
        m_ref[h] = jnp.broadcast_to(m_new, (1, LANES))


def _mlstm(mq, mk, mv, gates, proj):
    bsz, seq, w = mq.shape
    cs = MLSTM_CS
    tok = lambda b, i: (b, i, 0)
    return pl.pallas_call(
        _mlstm_kernel,
        grid=(bsz, seq // cs),
        in_specs=[pl.BlockSpec((None, cs, w), tok)] * 3
                 + [pl.BlockSpec((None, cs, LANES), tok), pl.BlockSpec((None, cs, w), lambda b, i: (b, i, 1))],
        out_specs=pl.BlockSpec((None, cs, w), tok),
        out_shape=jax.ShapeDtypeStruct((bsz, seq, w), BF16),
        scratch_shapes=[pltpu.VMEM((MLSTM_HEADS, MLSTM_HEAD_DIM, MLSTM_HEAD_DIM), F32),
                        pltpu.VMEM((MLSTM_HEADS, 1, MLSTM_HEAD_DIM), F32),
                        pltpu.VMEM((MLSTM_HEADS, 1, LANES), F32)],
        compiler_params=_cparams("arbitrary", "arbitrary"),
        name="mlstm",
    )(mq, mk, mv, gates, proj)


def _dsw_kernel(q_ref, k_ref, v_ref, kp_ref, vp_ref, o_ref, l_ref, *, nres, nsub, span):
    blk = DSW_BLOCK
    dh = DSW_HEAD_DIM
    j = pl.program_id(2)
    qi = lax.broadcasted_iota(jnp.int32, (blk, 2 * blk), 0)
    ki = lax.broadcasted_iota(jnp.int32, (blk, 2 * blk), 1)
    dist = qi + blk - ki
    band = (dist >= 0) & (dist <= span)
    band_first = band & ((ki >= blk) | (j > 0))
    lane = lax.broadcasted_iota(jnp.int32, (blk, LANES), 1)
    scale = dh ** -0.5
    for r in range(nres):
        for sb in range(nsub):
            rows = slice(sb * blk, (sb + 1) * blk)
            lse_tile = jnp.zeros((blk, LANES), F32)
            for h in range(DSW_HEADS):
                cols = slice(r * D_WIDTH + h * dh, r * D_WIDTH + (h + 1) * dh)
                if sb == 0:
                    k_prev, v_prev = kp_ref[:, cols], vp_ref[:, cols]
                else:
                    prev_rows = slice((sb - 1) * blk, sb * blk)
                    k_prev, v_prev = k_ref[prev_rows, cols], v_ref[prev_rows, cols]
                kk = jnp.concatenate([k_prev, k_ref[rows, cols]], axis=0)
                vv = jnp.concatenate([v_prev, v_ref[rows, cols]], axis=0)
                s = _dot_nt(q_ref[rows, cols], kk) * scale
                s = jnp.where(band_first if sb == 0 else band, s, NEG_INF)
                m = jnp.max(s, axis=1, keepdims=True)
                p = jnp.exp(s - m)
                l = jnp.sum(p, axis=1, keepdims=True)
                o_ref[rows, cols] = _dot(p.astype(BF16), vv) / l
                lse_tile = jnp.where(lane == h, m + jnp.log(l), lse_tile)
            l_ref[rows, r * LANES:(r + 1) * LANES] = lse_tile


def _dsw_config(qd, kd, vd, window, dil):
    bsz, seq, w = qd.shape
    n = seq // dil
    tq = min(512, n)
    nres = 1 if n >= 512 else 512 // n
    nsub = tq // DSW_BLOCK
    view = lambda a: a.reshape(bsz, n, dil * w)
    cur = lambda b, r, j: (b, j, r)
    prev = lambda b, r, j: (b, jnp.maximum(j * nsub - 1, 0), r)
    o, lse = pl.pallas_call(
        functools.partial(_dsw_kernel, nres=nres, nsub=nsub, span=window // dil),
        grid=(bsz, dil // nres, n // tq),
        in_specs=[pl.BlockSpec((None, tq, nres * w), cur)] * 3
                 + [pl.BlockSpec((None, DSW_BLOCK, nres * w), prev)] * 2,
        out_specs=[pl.BlockSpec((None, tq, nres * w), cur), pl.BlockSpec((None, tq, nres * LANES), cur)],
        out_shape=[jax.ShapeDtypeStruct((bsz, n, dil * w), F32),
                   jax.ShapeDtypeStruct((bsz, n, dil * LANES), F32)],
        compiler_params=_cparams("arbitrary", "arbitrary", "arbitrary"),
        name=f"dsw_dil{dil}",
    )(view(qd), view(kd), view(vd), view(kd), view(vd))
    return o.reshape(bsz, seq, w), lse.reshape(bsz, seq, LANES)


def _dsw_combine_kernel(o0_ref, o1_ref, o2_ref, l0_ref, l1_ref, l2_ref, y_ref):
    dh = DSW_HEAD_DIM
    outs = (o0_ref, o1_ref, o2_ref)
    lses = (l0_ref[...], l1_ref[...], l2_ref[...])
    mx = jnp.maximum(jnp.maximum(lses[0], lses[1]), lses[2])
    ws = [jnp.exp(l - mx) for l in lses]
    tot = ws[0] + ws[1] + ws[2]
    for h in range(DSW_HEADS):
        sl = slice(h * dh, (h + 1) * dh)
        acc = sum(ws[g][:, h:h + 1] * outs[g][:, sl] for g in range(3))
        y_ref[:, sl] = (acc / tot[:, h:h + 1]).astype(y_ref.dtype)


def _dsw_attention(qd, kd, vd):
    bsz, seq, w = qd.shape
    outs, lses = [], []
    for window, dil in DSW_CONFIGS:
        o, lse = _dsw_config(qd, kd, vd, window, dil)
        outs.append(o)
        lses.append(lse)
    tl = PROJ_TL
    tok = lambda b, i: (b, i, 0)
    return pl.pallas_call(
        _dsw_combine_kernel,
        grid=(bsz, seq // tl),
        in_specs=[pl.BlockSpec((None, tl, w), tok)] * 3 + [pl.BlockSpec((None, tl, LANES), tok)] * 3,
        out_specs=pl.BlockSpec((None, tl, w), tok),
        out_shape=jax.ShapeDtypeStruct((bsz, seq, w), BF16),
        compiler_params=_cparams("arbitrary", "arbitrary"),
        name="dsw_combine",
    )(*outs, *lses)


def kernel(x, positions, ab_w_in, ab_conv_w, ab_conv_b, ab_gate_r_w, ab_gate_r_b, ab_gate_i_w, ab_gate_i_b, ab_lru_lambda, ab_s5_lambda_re, ab_s5_lambda_im, ab_s5_log_step, ab_s5_b_re, ab_s5_b_im, ab_s5_c_re, ab_s5_c_im, ab_s5_d, ab_glu_w, ab_glu_b, ab_w_out, cd_w_in, cd_conv_w, cd_conv_b, cd_w_q, cd_w_k, cd_w_v, cd_w_if, cd_b_if, cd_w_out, ln_mix_g, ln_mix_b, ln_ffn_g, ln_ffn_b, moe_router_w, moe_router_b, moe_w_in, moe_b_in, moe_w_out, moe_b_out):
    bsz, seq, d = x.shape
    t = bsz * seq
    for layer in range(DEPTH):
        j = layer // 2
        if layer % 2 == 0:
            proj = _inproj_time_major(x, ab_w_in[j].astype(BF16))
            ya = _rg_lru(proj, ab_conv_w[j], ab_conv_b[j], ab_gate_r_w[j], ab_gate_r_b[j],
                         ab_gate_i_w[j], ab_gate_i_b[j], ab_lru_lambda[j])
            yb = _s5_glu(proj, ab_s5_lambda_re[j], ab_s5_lambda_im[j], ab_s5_log_step[j], ab_s5_b_re[j],
                         ab_s5_b_im[j], ab_s5_c_re[j], ab_s5_c_im[j], ab_s5_d[j], ab_glu_w[j], ab_glu_b[j])
            time_major, w_out = True, ab_w_out[j]
        else:
            proj = _inproj(x, cd_w_in[j].astype(BF16))
            mq, mk, mv, gates, qd, kd, vd = _cd_prep(proj, positions, cd_conv_w[j], cd_conv_b[j], cd_w_q[j],
                                                     cd_w_k[j], cd_w_v[j], cd_w_if[j], cd_b_if[j])
            ya = _mlstm(mq, mk, mv, gates, proj)
            yb = _dsw_attention(qd, kd, vd)
            time_major, w_out = False, cd_w_out[j]
        xn, logits = _outproj_ln_router(ya, yb, time_major, w_out.astype(BF16), x, ln_mix_g[layer],
                                        ln_mix_b[layer], moe_router_w[layer], moe_router_b[layer])
        ffn = _moe_ffn(xn.reshape(t, d), logits.reshape(t, N_EXPERTS), layer, moe_w_in, moe_b_in,
                       moe_w_out, moe_b_out)
        x = _residual_ln(xn.reshape(t, d), ffn, ln_ffn_g[layer], ln_ffn_b[layer]).reshape(bsz, seq, d)
    return x
```

```python
import functools

import jax
import jax.numpy as jnp
from jax import lax
from jax.experimental import pallas as pl
from jax.experimental.pallas import tpu as pltpu

F32 = jnp.float32
BF16 = jnp.bfloat16

D_MODEL = 1024
DEPTH = 2
A_WIDTH = 512
B_WIDTH = 512
C_WIDTH = 512
D_WIDTH = 512
LRU_HEADS = 8
LRU_HEAD_DIM = A_WIDTH // LRU_HEADS
LRU_C = 8.0
CONV_WIDTH = 4
S5_GROUP_CH = 16
S5_GROUPS = B_WIDTH // S5_GROUP_CH
S5_STATE = 64
MLSTM_HEADS = 4
MLSTM_HEAD_DIM = C_WIDTH // MLSTM_HEADS
DSW_HEADS = 4
DSW_HEAD_DIM = D_WIDTH // DSW_HEADS
DSW_CONFIGS = ((128, 1), (512, 4), (2048, 16))
DSW_BLOCK = 128
ROPE_THETA = 500000.0
ROPE_DIMS = DSW_HEAD_DIM // 4
N_EXPERTS = 32
TOP_K = 4
D_EXPERT = D_MODEL
SWIGLU_LIMIT = 7.0
SWIGLU_ALPHA = 1.702
DEEPNORM_ALPHA = (2 * DEPTH) ** 0.25
LN_EPS = 1e-5

VMEM_LIMIT_BYTES = 56 * 1024 * 1024
LANES = 128
MOE_TM = 256
PROJ_TL = 512
LRU_TC = 128
S5_TC = 64
S5_HALF = 2
MLSTM_CS = 256
NEG_INF = float("-inf")


def _cparams(*sem):
    return pltpu.CompilerParams(dimension_semantics=sem, vmem_limit_bytes=VMEM_LIMIT_BYTES)


def _dot(a, b):
    return jnp.dot(a, b, preferred_element_type=F32)


def _dot_nt(a, b):
    return lax.dot_general(a, b, (((1,), (1,)), ((), ())), preferred_element_type=F32)


def _matmul_kernel(x_ref, w_ref, o_ref):
    o_ref[...] = _dot(x_ref[...].astype(BF16), w_ref[...])


def _inproj_time_major(x3, w_bf16):
    bsz, seq, k = x3.shape
    n = w_bf16.shape[1]
    out = pl.pallas_call(
        _matmul_kernel,
        grid=(bsz, seq // PROJ_TL),
        in_specs=[pl.BlockSpec((None, PROJ_TL, k), lambda b, i: (b, i, 0)),
                  pl.BlockSpec((k, n), lambda b, i: (0, 0))],
        out_specs=pl.BlockSpec((PROJ_TL, n), lambda b, i: (i, b)),
        out_shape=jax.ShapeDtypeStruct((seq, bsz * n), F32),
        compiler_params=_cparams("arbitrary", "arbitrary"),
        name="in_proj_time_major",
    )(x3, w_bf16)
    return out.reshape(seq, bsz, n)


def _inproj(x3, w_bf16):
    bsz, seq, k = x3.shape
    n = w_bf16.shape[1]
    return pl.pallas_call(
        _matmul_kernel,
        grid=(bsz, seq // PROJ_TL),
        in_specs=[pl.BlockSpec((None, PROJ_TL, k), lambda b, i: (b, i, 0)),
                  pl.BlockSpec((k, n), lambda b, i: (0, 0))],
        out_specs=pl.BlockSpec((None, PROJ_TL, n), lambda b, i: (b, i, 0)),
        out_shape=jax.ShapeDtypeStruct((bsz, seq, n), F32),
        compiler_params=_cparams("arbitrary", "arbitrary"),
        name="in_proj",
    )(x3, w_bf16)


def _layer_norm_rows(z, g, b):
    mu = jnp.mean(z, axis=-1, keepdims=True)
    zc = z - mu
    var = jnp.mean(zc * zc, axis=-1, keepdims=True)
    return zc * lax.rsqrt(var + LN_EPS) * g + b


def _outproj_ln_router_kernel(ya_ref, yb_ref, wa_ref, wb_ref, x_ref, g_ref, b_ref, rwh_ref, rwl_ref, rb_ref,
                              xn_ref, lg_ref):
    mix = _dot(ya_ref[...], wa_ref[...]) + _dot(yb_ref[...], wb_ref[...])
    xn = _layer_norm_rows(DEEPNORM_ALPHA * x_ref[...] + mix, g_ref[...], b_ref[...])
    xn_ref[...] = xn
    xh = xn.astype(BF16)
    xl = (xn - xh.astype(F32)).astype(BF16)
    lg_ref[...] = (_dot(xh, rwh_ref[...]) + _dot(xl, rwh_ref[...]) + _dot(xh, rwl_ref[...])) + rb_ref[...]


def _outproj_ln_router(ya, yb, time_major, w_bf16, x3, g, b, rw, rb):
    bsz, seq, d = x3.shape
    wa, wb = w_bf16[:ya.shape[-1]], w_bf16[ya.shape[-1]:]
    ne = rw.shape[1]
    rw_hi = rw.astype(BF16)
    rw_lo = (rw - rw_hi.astype(F32)).astype(BF16)
    tl = PROJ_TL
    if time_major:
        ya, yb = ya.reshape(seq, -1), yb.reshape(seq, -1)
        y_specs = [pl.BlockSpec((tl, wa.shape[0]), lambda bi, i: (i, bi)),
                   pl.BlockSpec((tl, wb.shape[0]), lambda bi, i: (i, bi))]
    else:
        y_specs = [pl.BlockSpec((None, tl, wa.shape[0]), lambda bi, i: (bi, i, 0)),
                   pl.BlockSpec((None, tl, wb.shape[0]), lambda bi, i: (bi, i, 0))]
    fixed = lambda bi, i: (0, 0)
    tok = lambda bi, i: (bi, i, 0)
    return pl.pallas_call(
        _outproj_ln_router_kernel,
        grid=(bsz, seq // tl),
        in_specs=y_specs + [pl.BlockSpec(wa.shape, fixed), pl.BlockSpec(wb.shape, fixed),
                            pl.BlockSpec((None, tl, d), tok),
                            pl.BlockSpec((1, d), fixed), pl.BlockSpec((1, d), fixed),
                            pl.BlockSpec((d, ne), fixed), pl.BlockSpec((d, ne), fixed),
                            pl.BlockSpec((1, ne), fixed)],
        out_specs=[pl.BlockSpec((None, tl, d), tok), pl.BlockSpec((None, tl, ne), tok)],
        out_shape=[jax.ShapeDtypeStruct((bsz, seq, d), F32), jax.ShapeDtypeStruct((bsz, seq, ne), F32)],
        compiler_params=_cparams("arbitrary", "arbitrary"),
        name="out_proj_ln_router",
    )(ya, yb, wa, wb, x3, g.reshape(1, d), b.reshape(1, d), rw_hi, rw_lo, rb.reshape(1, ne))


def _gather_rows(src_hbm, idx_ref, n, dst, sem):
    def body(r, carry):
        pltpu.make_async_copy(src_hbm.at[pl.ds(idx_ref[0, r], 1)], dst.at[pl.ds(r, 1)], sem).start()
        return carry
    lax.fori_loop(0, n, body, 0, unroll=8)


def _moe_kernel(be_ref, nreal_ref, rt_ref, rtn_ref, x_hbm, win_ref, bin_ref, wout_ref, bout_ref, o_ref,
                win_bf, wout_bf, xbuf, sem):
    i = pl.program_id(0)
    n_real = nreal_ref[0]
    e = be_ref[i]
    prev = be_ref[jnp.maximum(i - 1, 0)]
    tm = xbuf.shape[1]
    slot = lax.rem(i, 2)

    @pl.when(i == 0)
    def _first_gather():
        _gather_rows(x_hbm, rt_ref, tm, xbuf.at[0], sem.at[0])

    @pl.when(i + 1 < n_real)
    def _next_gather():
        _gather_rows(x_hbm, rtn_ref, tm, xbuf.at[1 - slot], sem.at[1 - slot])

    @pl.when((i == 0) | (e != prev))
    def _load_expert():
        win_bf[...] = win_ref[...].astype(BF16)
        wout_bf[...] = wout_ref[...].astype(BF16)

    @pl.when((i < n_real) | (i == 0))
    def _compute():
        pltpu.make_async_copy(x_hbm.at[pl.ds(0, tm)], xbuf.at[slot], sem.at[slot]).wait()
        h = _dot(xbuf[slot].astype(BF16), win_bf[...]) + bin_ref[...]
        g = jnp.minimum(h[:, :D_EXPERT], SWIGLU_LIMIT)
        lin = jnp.clip(h[:, D_EXPERT:], -SWIGLU_LIMIT, SWIGLU_LIMIT)
        y = g * jax.nn.sigmoid(SWIGLU_ALPHA * g) * (lin + 1.0)
        o_ref[...] = _dot(y.astype(BF16), wout_bf[...]) + bout_ref[...]

    @pl.when((i >= n_real) & (i > 0))
    def _unused_block():
        o_ref[...] = jnp.zeros_like(o_ref)


def _moe_experts(xn, row_tok, block_expert, n_real, layer, w_in, b_in, w_out, b_out):
    t, d = xn.shape
    tm = MOE_TM
    n_blocks = row_tok.shape[0] // tm
    nl, ne, _, dh2 = w_in.shape
    rt = row_tok.reshape(n_blocks, 1, tm)
    smem_rows = lambda imap: pl.BlockSpec((None, 1, tm), imap, memory_space=pltpu.SMEM)
    grid_spec = pltpu.PrefetchScalarGridSpec(
        num_scalar_prefetch=2,
        grid=(n_blocks,),
        in_specs=[
            smem_rows(lambda i, be, nr: (i, 0, 0)),
            smem_rows(lambda i, be, nr: (jnp.minimum(i + 1, n_blocks - 1), 0, 0)),
            pl.BlockSpec(memory_space=pl.ANY),
            pl.BlockSpec((None, None, d, dh2), lambda i, be, nr: (layer, be[i], 0, 0)),
            pl.BlockSpec((None, None, 1, dh2), lambda i, be, nr: (layer, be[i], 0, 0)),
            pl.BlockSpec((None, None, D_EXPERT, d), lambda i, be, nr: (layer, be[i], 0, 0)),
            pl.BlockSpec((None, None, 1, d), lambda i, be, nr: (layer, be[i], 0, 0)),
        ],
        out_specs=pl.BlockSpec((tm, d), lambda i, be, nr: (i, 0)),
        scratch_shapes=[pltpu.VMEM((d, dh2), BF16), pltpu.VMEM((D_EXPERT, d), BF16),
                        pltpu.VMEM((2, tm, d), F32), pltpu.SemaphoreType.DMA((2,))],
    )
    return pl.pallas_call(
        _moe_kernel,
        grid_spec=grid_spec,
        out_shape=jax.ShapeDtypeStruct((n_blocks * tm, d), F32),
        compiler_params=_cparams("arbitrary"),
        name="moe_experts",
    )(block_expert, n_real, rt, rt, xn, w_in, b_in.reshape(nl, ne, 1, dh2), w_out, b_out.reshape(nl, ne, 1, d))


def _moe_combine_kernel(dc_ref, dn_ref, x_ref, gate_ref, yb_hbm, g_ref, b_ref, o_ref, buf, sem):
    i = pl.program_id(0)
    n = pl.num_programs(0)
    tk = x_ref.shape[0]
    slot = lax.rem(i, 2)

    @pl.when(i == 0)
    def _first_gather():
        _gather_rows(yb_hbm, dc_ref, TOP_K * tk, buf.at[0], sem.at[0])

    @pl.when(i + 1 < n)
    def _next_gather():
        _gather_rows(yb_hbm, dn_ref, TOP_K * tk, buf.at[1 - slot], sem.at[1 - slot])

    pltpu.make_async_copy(buf.at[slot], buf.at[slot], sem.at[slot]).wait()
    gate = gate_ref[...]
    ffn = gate[:, 0:1] * buf[slot, 0:tk]
    for k in range(1, TOP_K):
        ffn = ffn + gate[:, k:k + 1] * buf[slot, k * tk:(k + 1) * tk]
    o_ref[...] = _layer_norm_rows(DEEPNORM_ALPHA * x_ref[...] + ffn, g_ref[...], b_ref[...])


def _moe_combine_ln(xn, gate, dest, yb, g, b):
    t, d = xn.shape
    tk = MOE_TM
    n = t // tk
    dest_tiles = dest.reshape(n, tk, TOP_K).transpose(0, 2, 1).reshape(n, 1, TOP_K * tk)
    smem_rows = lambda imap: pl.BlockSpec((None, 1, TOP_K * tk), imap, memory_space=pltpu.SMEM)
    row = lambda i: (i, 0)
    fixed = lambda i: (0, 0)
    return pl.pallas_call(
        _moe_combine_kernel,
        grid=(n,),
        in_specs=[smem_rows(lambda i: (i, 0, 0)), smem_rows(lambda i: (jnp.minimum(i + 1, n - 1), 0, 0)),
                  pl.BlockSpec((tk, d), row), pl.BlockSpec((tk, TOP_K), row),
                  pl.BlockSpec(memory_space=pl.ANY),
                  pl.BlockSpec((1, d), fixed), pl.BlockSpec((1, d), fixed)],
        out_specs=pl.BlockSpec((tk, d), row),
        out_shape=jax.ShapeDtypeStruct((t, d), F32),
        scratch_shapes=[pltpu.VMEM((2, TOP_K * tk, d), F32), pltpu.SemaphoreType.DMA((2,))],
        compiler_params=_cparams("arbitrary"),
        name="moe_combine_ln",
    )(dest_tiles, dest_tiles, xn, gate, yb, g.reshape(1, d), b.reshape(1, d))


def _moe_ffn_ln(xn, logits, layer, w_in, b_in, w_out, b_out, g, b):
    t, d = xn.shape
    tm = MOE_TM
    top_logit, top_idx = lax.top_k(logits, TOP_K)
    gate = jax.nn.softmax(top_logit, axis=-1)
    flat_e = top_idx.reshape(-1).astype(jnp.int32)
    onehot = (flat_e[:, None] == jnp.arange(N_EXPERTS, dtype=jnp.int32)[None, :]).astype(jnp.int32)
    csum = jnp.cumsum(onehot, axis=0)
    rank = jnp.sum(csum * onehot, axis=1) - 1
    counts = csum[-1]
    blocks_per_e = (counts + tm - 1) // tm
    blk_end = jnp.cumsum(blocks_per_e)
    blk_start = blk_end - blocks_per_e
    row_start = jnp.cumsum(counts) - counts
    dest = blk_start[flat_e] * tm + rank
    n_blocks = (t * TOP_K) // tm + N_EXPERTS
    block_expert = jnp.minimum(
        jnp.sum((blk_end[None, :] <= jnp.arange(n_blocks, dtype=jnp.int32)[:, None]).astype(jnp.int32), axis=1),
        N_EXPERTS - 1)
    n_real = blk_end[-1:].astype(jnp.int32)
    tok_sorted = (jnp.argsort(flat_e, stable=True) // TOP_K).astype(jnp.int32)
    e_row = jnp.repeat(block_expert, tm)
    rank_row = jnp.arange(n_blocks * tm, dtype=jnp.int32) - jnp.repeat(blk_start[block_expert] * tm, tm)
    valid = rank_row < counts[e_row]
    src = jnp.clip(row_start[e_row] + rank_row, 0, t * TOP_K - 1)
    row_tok = jnp.where(valid, tok_sorted[src], 0)
    yb = _moe_experts(xn, row_tok, block_expert, n_real, layer, w_in, b_in, w_out, b_out)
    return _moe_combine_ln(xn, gate, dest, yb, g, b)


def _softplus(x):
    return jnp.maximum(x, 0.0) + jnp.log(1.0 + jnp.exp(-jnp.abs(x)))


def _gelu_tanh(x):
    return 0.5 * x * (1.0 + jnp.tanh(0.7978845608028654 * (x + 0.044715 * (x * x * x))))


def _lru_kernel(xa_ref, ga_ref, cw_ref, cb_ref, wr_ref, br_ref, wi_ref, bi_ref, lam_ref, o_ref,
                tail_ref, h_ref, a_ref, u_ref):
    tc, bsz, w = xa_ref.shape

    @pl.when(pl.program_id(0) == 0)
    def _init():
        tail_ref[...] = jnp.zeros_like(tail_ref)
        h_ref[...] = jnp.zeros_like(h_ref)

    xa = xa_ref[...]
    ext = jnp.concatenate([tail_ref[...], xa], axis=0)
    xc = cb_ref[...] + ext[0:tc] * cw_ref[0]
    for tap in range(1, CONV_WIDTH):
        xc = xc + ext[tap:tap + tc] * cw_ref[tap]
    tail_ref[...] = xa[tc - (CONV_WIDTH - 1):]
    x2 = xc.reshape(tc * bsz, w)
    xb = x2.astype(BF16)
    r = jax.nn.sigmoid(_dot(xb, wr_ref[...]) + br_ref[...])
    ig = jax.nn.sigmoid(_dot(xb, wi_ref[...]) + bi_ref[...])
    log_a = (-LRU_C) * r * _softplus(-lam_ref[...])
    a_ref[...] = jnp.exp(log_a).reshape(tc, bsz, w)
    u_ref[...] = (jnp.sqrt(1.0 - jnp.exp(2.0 * log_a)) * (ig * x2)).reshape(tc, bsz, w)

    def step(t, h):
        h = a_ref[t] * h + u_ref[t]
        u_ref[t] = h
        return h

    h_ref[...] = lax.fori_loop(0, tc, step, h_ref[...], unroll=8)
    o_ref[...] = (u_ref[...] * _gelu_tanh(ga_ref[...])).astype(o_ref.dtype)


def _block_diag(w):
    h, i, j = w.shape
    return jnp.einsum('hij,hk->hikj', w, jnp.eye(h, dtype=w.dtype)).reshape(h * i, h * j)


def _rg_lru(proj, conv_w, conv_b, gate_r_w, gate_r_b, gate_i_w, gate_i_b, lru_lambda):
    seq, bsz, _ = proj.shape
    w = A_WIDTH
    tc = LRU_TC
    fixed2 = lambda i: (0, 0)
    fixed3 = lambda i: (0, 0, 0)
    return pl.pallas_call(
        _lru_kernel,
        grid=(seq // tc,),
        in_specs=[pl.BlockSpec((tc, bsz, w), lambda i: (i, 0, 0)),
                  pl.BlockSpec((tc, bsz, w), lambda i: (i, 0, 1)),
                  pl.BlockSpec((CONV_WIDTH, 1, w), fixed3), pl.BlockSpec((1, w), fixed2),
                  pl.BlockSpec((w, w), fixed2), pl.BlockSpec((1, w), fixed2),
                  pl.BlockSpec((w, w), fixed2), pl.BlockSpec((1, w), fixed2),
                  pl.BlockSpec((1, w), fixed2)],
        out_specs=pl.BlockSpec((tc, bsz, w), lambda i: (i, 0, 0)),
        out_shape=jax.ShapeDtypeStruct((seq, bsz, w), BF16),
        scratch_shapes=[pltpu.VMEM((CONV_WIDTH - 1, bsz, w), F32), pltpu.VMEM((bsz, w), F32),
                        pltpu.VMEM((tc, bsz, w), F32), pltpu.VMEM((tc, bsz, w), F32)],
        compiler_params=_cparams("arbitrary"),
        name="rg_lru",
    )(proj, proj, conv_w.reshape(CONV_WIDTH, 1, w), conv_b.reshape(1, w),
      _block_diag(gate_r_w).astype(BF16), gate_r_b.reshape(1, w),
      _block_diag(gate_i_w).astype(BF16), gate_i_b.reshape(1, w), lru_lambda.reshape(1, w))


def _s5_kernel(u_ref, wbr_ref, wbi_ref, ar_ref, ai_ref, ccr_ref, cci_ref, d_ref, gw_ref, gb_ref, o_ref,
               xr_ref, xi_ref, sr_ref, si_ref):
    tc, bsz, w = u_ref.shape
    nstate = xr_ref.shape[-1]
    sh = nstate // S5_HALF
    wh = w // S5_HALF

    @pl.when(pl.program_id(0) == 0)
    def _init():
        sr_ref[...] = jnp.zeros_like(sr_ref)
        si_ref[...] = jnp.zeros_like(si_ref)

    u2 = u_ref[...].reshape(tc * bsz, w)
    ub = u2.astype(BF16)
    for hf in range(S5_HALF):
        uh = ub[:, hf * wh:(hf + 1) * wh]
        xr_ref[:, :, hf * sh:(hf + 1) * sh] = _dot(uh, wbr_ref[hf]).reshape(tc, bsz, sh)
        xi_ref[:, :, hf * sh:(hf + 1) * sh] = _dot(uh, wbi_ref[hf]).reshape(tc, bsz, sh)

    for hf in range(S5_HALF):
        lo, hi = hf * sh, (hf + 1) * sh
        ar = jnp.broadcast_to(ar_ref[:, lo:hi], (bsz, sh))
        ai = jnp.broadcast_to(ai_ref[:, lo:hi], (bsz, sh))

        def step(t, carry, lo=lo, hi=hi, ar=ar, ai=ai):
            xr, xi = carry
            nxr = ar * xr - ai * xi + xr_ref[t, :, lo:hi]
            nxi = ar * xi + ai * xr + xi_ref[t, :, lo:hi]
            xr_ref[t, :, lo:hi] = nxr
            xi_ref[t, :, lo:hi] = nxi
            return nxr, nxi

        xr, xi = lax.fori_loop(0, tc, step, (sr_ref[:, lo:hi], si_ref[:, lo:hi]), unroll=4)
        sr_ref[:, lo:hi] = xr
        si_ref[:, lo:hi] = xi

    ys = []
    for hf in range(S5_HALF):
        lo, hi = hf * sh, (hf + 1) * sh
        xrh = xr_ref[:, :, lo:hi].reshape(tc * bsz, sh).astype(BF16)
        xih = xi_ref[:, :, lo:hi].reshape(tc * bsz, sh).astype(BF16)
        ys.append(_dot(xrh, ccr_ref[hf]) - _dot(xih, cci_ref[hf]))
    y = jnp.concatenate(ys, axis=1) + d_ref[...] * u2
    s = _gelu_tanh(y)
    yb = s * jax.nn.sigmoid(_dot(s.astype(BF16), gw_ref[...]) + gb_ref[...])
    o_ref[...] = yb.reshape(tc, bsz, w).astype(o_ref.dtype)


def _s5_glu(proj, lam_re, lam_im, log_step, b_re, b_im, c_re, c_im, d_skip, glu_w, glu_b):
    seq, bsz, _ = proj.shape
    w = B_WIDTH
    tc = S5_TC
    nstate = S5_GROUPS * S5_STATE
    gh = S5_GROUPS // S5_HALF
    lr = jnp.minimum(lam_re, -1e-4)
    li = lam_im
    step = jnp.exp(log_step)[:, None]
    mag = jnp.exp(lr * step)
    ar = mag * jnp.cos(li * step)
    ai = mag * jnp.sin(li * step)
    inv = 1.0 / (lr * lr + li * li)
    zr = ((ar - 1.0) * lr + ai * li) * inv
    zi = (ai * lr - (ar - 1.0) * li) * inv
    bbr = zr[..., None] * b_re - zi[..., None] * b_im
    bbi = zr[..., None] * b_im + zi[..., None] * b_re
    eye = jnp.eye(gh, dtype=F32)

    def expand_in(bb):
        bb = bb.reshape(S5_HALF, gh, S5_STATE, S5_GROUP_CH)
        return jnp.einsum('fgph,gk->fghkp', bb, eye).reshape(
            S5_HALF, gh * S5_GROUP_CH, gh * S5_STATE).astype(BF16)

    def expand_out(c):
        c = c.reshape(S5_HALF, gh, S5_GROUP_CH, S5_STATE)
        return jnp.einsum('fghp,gk->fgpkh', c, eye).reshape(
            S5_HALF, gh * S5_STATE, gh * S5_GROUP_CH).astype(BF16)

    fixed2 = lambda i: (0, 0)
    fixed3 = lambda i: (0, 0, 0)
    return pl.pallas_call(
        _s5_kernel,
        grid=(seq // tc,),
        in_specs=[pl.BlockSpec((tc, bsz, w), lambda i: (i, 0, 2)),
                  pl.BlockSpec((S5_HALF, w // S5_HALF, nstate // S5_HALF), fixed3),
                  pl.BlockSpec((S5_HALF, w // S5_HALF, nstate // S5_HALF), fixed3),
                  pl.BlockSpec((1, nstate), fixed2), pl.BlockSpec((1, nstate), fixed2),
                  pl.BlockSpec((S5_HALF, nstate // S5_HALF, w // S5_HALF), fixed3),
                  pl.BlockSpec((S5_HALF, nstate // S5_HALF, w // S5_HALF), fixed3),
                  pl.BlockSpec((1, w), fixed2), pl.BlockSpec((w, w), fixed2), pl.BlockSpec((1, w), fixed2)],
        out_specs=pl.BlockSpec((tc, bsz, w), lambda i: (i, 0, 0)),
        out_shape=jax.ShapeDtypeStruct((seq, bsz, w), BF16),
        scratch_shapes=[pltpu.VMEM((tc, bsz, nstate), F32), pltpu.VMEM((tc, bsz, nstate), F32),
                        pltpu.VMEM((bsz, nstate), F32), pltpu.VMEM((bsz, nstate), F32)],
        compiler_params=_cparams("arbitrary"),
        name="s5_glu",
    )(proj, expand_in(bbr), expand_in(bbi), ar.reshape(1, nstate), ai.reshape(1, nstate),
      expand_out(c_re), expand_out(c_im), d_skip.reshape(1, w), glu_w.astype(BF16), glu_b.reshape(1, w))


def _cd_prep_kernel(xc_ref, q_ref, k_ref, v_ref, pos_ref, cw_ref, cb_ref, wq_ref, wk_ref, wv_ref, wif_ref,
                    bif_ref, invf_ref, mq_ref, mk_ref, mv_ref, gt_ref, qd_ref, kd_ref, vd_ref, tail_ref):
    tl, w = xc_ref.shape
    pad = tail_ref.shape[0]

    @pl.when(pl.program_id(1) == 0)
    def _init():
        tail_ref[...] = jnp.zeros_like(tail_ref)

    xc = xc_ref[...]
    ext = jnp.concatenate([tail_ref[...], xc], axis=0)
    conv = cb_ref[...]
    for tap in range(CONV_WIDTH):
        off = pad - (CONV_WIDTH - 1) + tap
        conv = conv + ext[off:off + tl] * cw_ref[tap]
    tail_ref[...] = xc[tl - pad:]
    xconv = (conv * jax.nn.sigmoid(conv)).astype(BF16)
    q = _dot(xconv, wq_ref[...]).astype(BF16)
    k = (_dot(xconv, wk_ref[...]) * (MLSTM_HEAD_DIM ** -0.5)).astype(BF16)
    v = _dot(xc.astype(BF16), wv_ref[...]).astype(BF16)
    mq_ref[...] = q
    mk_ref[...] = k
    mv_ref[...] = v
    gt_ref[...] = (_dot(q, wif_ref[0]) + _dot(k, wif_ref[1]) + _dot(v, wif_ref[2]) + bif_ref[...])

    ang = pos_ref[...].astype(F32) * invf_ref[...]
    cos = jnp.cos(ang)
    sin = jnp.sin(ang)
    lane = lax.broadcasted_iota(jnp.int32, (1, LANES), 1)
    half = ROPE_DIMS // 2
    sin_lo = jnp.where(lane < half, -sin, 0.0)
    sin_hi = jnp.where((lane >= half) & (lane < ROPE_DIMS), sin, 0.0)
    for src, dst in ((q_ref, qd_ref), (k_ref, kd_ref)):
        for h in range(DSW_HEADS):
            sl = slice(h * DSW_HEAD_DIM, (h + 1) * DSW_HEAD_DIM)
            t = src[:, sl]
            rot = (t * cos + pltpu.roll(t, LANES - half, axis=1) * sin_lo
                   + pltpu.roll(t, half, axis=1) * sin_hi)
            dst[:, sl] = rot.astype(BF16)
    vd_ref[...] = v_ref[...].astype(BF16)


def _cd_prep(proj, positions, conv_w, conv_b, w_q, w_k, w_v, w_if, b_if):
    bsz, seq, _ = proj.shape
    w = C_WIDTH
    tl = PROJ_TL
    half = ROPE_DIMS // 2
    inv_freq = ROPE_THETA ** (-jnp.arange(half, dtype=F32) / half)
    invf = jnp.zeros((1, LANES), F32).at[0, :half].set(inv_freq).at[0, half:ROPE_DIMS].set(inv_freq)
    wif = jnp.zeros((3, w, LANES), F32).at[:, :, :2 * MLSTM_HEADS].set(w_if.reshape(3, w, 2 * MLSTM_HEADS))
    bif = jnp.zeros((1, LANES), F32).at[0, :2 * MLSTM_HEADS].set(b_if)
    col = lambda c: (lambda b, i: (b, i, c))
    fixed2 = lambda b, i: (0, 0)
    fixed3 = lambda b, i: (0, 0, 0)
    tok_bf = jax.ShapeDtypeStruct((bsz, seq, w), BF16)
    return pl.pallas_call(
        _cd_prep_kernel,
        grid=(bsz, seq // tl),
        in_specs=[pl.BlockSpec((None, tl, w), col(0)), pl.BlockSpec((None, tl, w), col(2)),
                  pl.BlockSpec((None, tl, w), col(3)), pl.BlockSpec((None, tl, w), col(4)),
                  pl.BlockSpec((None, tl, 1), col(0)),
                  pl.BlockSpec((CONV_WIDTH, 1, w), fixed3), pl.BlockSpec((1, w), fixed2),
                  pl.BlockSpec((w, w), fixed2), pl.BlockSpec((w, w), fixed2), pl.BlockSpec((w, w), fixed2),
                  pl.BlockSpec((3, w, LANES), fixed3), pl.BlockSpec((1, LANES), fixed2),
                  pl.BlockSpec((1, LANES), fixed2)],
        out_specs=[pl.BlockSpec((None, tl, w), col(0))] * 3 + [pl.BlockSpec((None, tl, LANES), col(0))]
                  + [pl.BlockSpec((None, tl, w), col(0))] * 3,
        out_shape=[tok_bf, tok_bf, tok_bf, jax.ShapeDtypeStruct((bsz, seq, LANES), F32),
                   tok_bf, tok_bf, tok_bf],
        scratch_shapes=[pltpu.VMEM((8, w), F32)],
        compiler_params=_cparams("arbitrary", "arbitrary"),
        name="cd_prep",
    )(proj, proj, proj, proj, positions.reshape(bsz, seq, 1), conv_w.reshape(CONV_WIDTH, 1, w),
      conv_b.reshape(1, w), _block_diag(w_q).astype(BF16), _block_diag(w_k).astype(BF16),
      _block_diag(w_v).astype(BF16), wif.astype(BF16), bif, invf)


def _cumsum_rows(x):
    n = x.shape[0]
    row = lax.broadcasted_iota(jnp.int32, x.shape, 0)
    shift = 1
    while shift < n:
        x = x + jnp.where(row >= shift, pltpu.roll(x, shift, axis=0), 0.0)
        shift *= 2
    return x


def _mlstm_kernel(q_ref, k_ref, v_ref, g_ref, op_ref, o_ref, c_ref, n_ref, m_ref):
    cs = q_ref.shape[0]
    dh = MLSTM_HEAD_DIM

    @pl.when(pl.program_id(1) == 0)
    def _init():
        c_ref[...] = jnp.zeros_like(c_ref)
        n_ref[...] = jnp.zeros_like(n_ref)
        m_ref[...] = jnp.full_like(m_ref, NEG_INF)

    gates = g_ref[...]
    cum = _cumsum_rows(jax.nn.log_sigmoid(gates))
    gates_t = gates.T
    cum_t = cum.T
    causal = (lax.broadcasted_iota(jnp.int32, (cs, cs), 0) >= lax.broadcasted_iota(jnp.int32, (cs, cs), 1))
    for h in range(MLSTM_HEADS):
        sl = slice(h * dh, (h + 1) * dh)
        fcol = MLSTM_HEADS + h
        li_c, li_r = gates[:, h:h + 1], gates_t[h:h + 1, :]
        cum_c, cum_r = cum[:, fcol:fcol + 1], cum_t[fcol:fcol + 1, :]
        m_prev = m_ref[h][:, 0:1]
        q, k, v = q_ref[:, sl], k_ref[:, sl], v_ref[:, sl]
        log_d = jnp.where(causal, cum_c - cum_r + li_r, NEG_INF)
        log_inter = cum_c + m_prev
        m_s = jnp.maximum(jnp.max(log_d, axis=1, keepdims=True), log_inter)
        s = _dot_nt(q, k) * jnp.exp(log_d - m_s)
        inter = jnp.exp(log_inter - m_s)
        c_prev = c_ref[h]
        n_prev = n_ref[h]
        num = _dot(s.astype(BF16), v) + inter * _dot(q, c_prev.astype(BF16))
        den = (jnp.sum(s, axis=1, keepdims=True)
               + inter * jnp.sum(q.astype(F32) * n_prev, axis=1, keepdims=True))
        hh = num / jnp.maximum(jnp.abs(den), jnp.exp(-m_s))
        o_ref[:, sl] = (jax.nn.sigmoid(op_ref[:, sl]) * hh).astype(o_ref.dtype)
        chunk_f = cum_c[cs - 1:cs, :]
        to_end = chunk_f - cum_c + li_c
        m_c = jnp.max(to_end, axis=0, keepdims=True)
        m_new = jnp.maximum(chunk_f + m_prev, m_c)
        s_old = jnp.exp(chunk_f + m_prev - m_new)
        s_new = jnp.exp(m_c - m_new)
        kw = k.astype(F32) * jnp.exp(to_end - m_c)
        c_ref[h] = s_old * c_prev + s_new * _dot(kw.T.astype(BF16), v)
        n_ref[h] = s_old * n_prev + s_new * jnp.sum(kw, axis=0, keepdims=True)
        m_ref[h] = jnp.broadcast_to(m_new, (1, LANES))


def _mlstm(mq, mk, mv, gates, proj):
    bsz, seq, w = mq.shape
    cs = MLSTM_CS
    tok = lambda b, i: (b, i, 0)
    return pl.pallas_call(
        _mlstm_kernel,
        grid=(bsz, seq // cs),
        in_specs=[pl.BlockSpec((None, cs, w), tok)] * 3
                 + [pl.BlockSpec((None, cs, LANES), tok), pl.BlockSpec((None, cs, w), lambda b, i: (b, i, 1))],
        out_specs=pl.BlockSpec((None, cs, w), tok),
        out_shape=jax.ShapeDtypeStruct((bsz, seq, w), BF16),
        scratch_shapes=[pltpu.VMEM((MLSTM_HEADS, MLSTM_HEAD_DIM, MLSTM_HEAD_DIM), F32),
                        pltpu.VMEM((MLSTM_HEADS, 1, MLSTM_HEAD_DIM), F32),
                        pltpu.VMEM((MLSTM_HEADS, 1, LANES), F32)],
        compiler_params=_cparams("arbitrary", "arbitrary"),
        name="mlstm",
    )(mq, mk, mv, gates, proj)


def _dsw_kernel(q_ref, k_ref, v_ref, kp_ref, vp_ref, o_ref, l_ref, *, nres, nsub, span):
    blk = DSW_BLOCK
    dh = DSW_HEAD_DIM
    j = pl.program_id(2)
    qi = lax.broadcasted_iota(jnp.int32, (blk, 2 * blk), 0)
    ki = lax.broadcasted_iota(jnp.int32, (blk, 2 * blk), 1)
    dist = qi + blk - ki
    band = (dist >= 0) & (dist <= span)
    band_first = band & ((ki >= blk) | (j > 0))
    lane = lax.broadcasted_iota(jnp.int32, (blk, LANES), 1)
    scale = dh ** -0.5
    for r in range(nres):
        for sb in range(nsub):
            rows = slice(sb * blk, (sb + 1) * blk)
            lse_tile = jnp.zeros((blk, LANES), F32)
            for h in range(DSW_HEADS):
                cols = slice(r * D_WIDTH + h * dh, r * D_WIDTH + (h + 1) * dh)
                if sb == 0:
                    k_prev, v_prev = kp_ref[:, cols], vp_ref[:, cols]
                else:
                    prev_rows = slice((sb - 1) * blk, sb * blk)
                    k_prev, v_prev = k_ref[prev_rows, cols], v_ref[prev_rows, cols]
                kk = jnp.concatenate([k_prev, k_ref[rows, cols]], axis=0)
                vv = jnp.concatenate([v_prev, v_ref[rows, cols]], axis=0)
                s = _dot_nt(q_ref[rows, cols], kk) * scale
                s = jnp.where(band_first if sb == 0 else band, s, NEG_INF)
                m = jnp.max(s, axis=1, keepdims=True)
                p = jnp.exp(s - m)
                l = jnp.sum(p, axis=1, keepdims=True)
                o_ref[rows, cols] = _dot(p.astype(BF16), vv) / l
                lse_tile = jnp.where(lane == h, m + jnp.log(l), lse_tile)
            l_ref[rows, r * LANES:(r + 1) * LANES] = lse_tile


def _dsw_config(qd, kd, vd, window, dil):
    bsz, seq, w = qd.shape
    n = seq // dil
    tq = min(512, n)
    nres = 1 if n >= 512 else 512 // n
    nsub = tq // DSW_BLOCK
    view = lambda a: a.reshape(bsz, n, dil * w)
    cur = lambda b, r, j: (b, j, r)
    prev = lambda b, r, j: (b, jnp.maximum(j * nsub - 1, 0), r)
    o, lse = pl.pallas_call(
        functools.partial(_dsw_kernel, nres=nres, nsub=nsub, span=window // dil),
        grid=(bsz, dil // nres, n // tq),
        in_specs=[pl.BlockSpec((None, tq, nres * w), cur)] * 3
                 + [pl.BlockSpec((None, DSW_BLOCK, nres * w), prev)] * 2,
        out_specs=[pl.BlockSpec((None, tq, nres * w), cur), pl.BlockSpec((None, tq, nres * LANES), cur)],
        out_shape=[jax.ShapeDtypeStruct((bsz, n, dil * w), F32),
                   jax.ShapeDtypeStruct((bsz, n, dil * LANES), F32)],
        compiler_params=_cparams("arbitrary", "arbitrary", "arbitrary"),
        name=f"dsw_dil{dil}",
    )(view(qd), view(kd), view(vd), view(kd), view(vd))
    return o.reshape(bsz, seq, w), lse.reshape(bsz, seq, LANES)


def _dsw_combine_kernel(o0_ref, o1_ref, o2_ref, l0_ref, l1_ref, l2_ref, y_ref):
    dh = DSW_HEAD_DIM
    outs = (o0_ref, o1_ref, o2_ref)
    lses = (l0_ref[...], l1_ref[...], l2_ref[...])
    mx = jnp.maximum(jnp.maximum(lses[0], lses[1]), lses[2])
    ws = [jnp.exp(l - mx) for l in lses]
    tot = ws[0] + ws[1] + ws[2]
    for h in range(DSW_HEADS):
        sl = slice(h * dh, (h + 1) * dh)
        acc = sum(ws[g][:, h:h + 1] * outs[g][:, sl] for g in range(3))
        y_ref[:, sl] = (acc / tot[:, h:h + 1]).astype(y_ref.dtype)


def _dsw_attention(qd, kd, vd):
    bsz, seq, w = qd.shape
    outs, lses = [], []
    for window, dil in DSW_CONFIGS:
        o, lse = _dsw_config(qd, kd, vd, window, dil)
        outs.append(o)
        lses.append(lse)
    tl = PROJ_TL
    tok = lambda b, i: (b, i, 0)
    return pl.pallas_call(
        _dsw_combine_kernel,
        grid=(bsz, seq // tl),
        in_specs=[pl.BlockSpec((None, tl, w), tok)] * 3 + [pl.BlockSpec((None, tl, LANES), tok)] * 3,
        out_specs=pl.BlockSpec((None, tl, w), tok),
        out_shape=jax.ShapeDtypeStruct((bsz, seq, w), BF16),
        compiler_params=_cparams("arbitrary", "arbitrary"),
        name="dsw_combine",
    )(*outs, *lses)


def kernel(x, positions, ab_w_in, ab_conv_w, ab_conv_b, ab_gate_r_w, ab_gate_r_b, ab_gate_i_w, ab_gate_i_b, ab_lru_lambda, ab_s5_lambda_re, ab_s5_lambda_im, ab_s5_log_step, ab_s5_b_re, ab_s5_b_im, ab_s5_c_re, ab_s5_c_im, ab_s5_d, ab_glu_w, ab_glu_b, ab_w_out, cd_w_in, cd_conv_w, cd_conv_b, cd_w_q, cd_w_k, cd_w_v, cd_w_if, cd_b_if, cd_w_out, ln_mix_g, ln_mix_b, ln_ffn_g, ln_ffn_b, moe_router_w, moe_router_b, moe_w_in, moe_b_in, moe_w_out, moe_b_out):
    bsz, seq, d = x.shape
    t = bsz * seq
    for layer in range(DEPTH):
        j = layer // 2
        if layer % 2 == 0:
            proj = _inproj_time_major(x, ab_w_in[j].astype(BF16))
            ya = _rg_lru(proj, ab_conv_w[j], ab_conv_b[j], ab_gate_r_w[j], ab_gate_r_b[j],
                         ab_gate_i_w[j], ab_gate_i_b[j], ab_lru_lambda[j])
            yb = _s5_glu(proj, ab_s5_lambda_re[j], ab_s5_lambda_im[j], ab_s5_log_step[j], ab_s5_b_re[j],
                         ab_s5_b_im[j], ab_s5_c_re[j], ab_s5_c_im[j], ab_s5_d[j], ab_glu_w[j], ab_glu_b[j])
            time_major, w_out = True, ab_w_out[j]
        else:
            proj = _inproj(x, cd_w_in[j].astype(BF16))
            mq, mk, mv, gates, qd, kd, vd = _cd_prep(proj, positions, cd_conv_w[j], cd_conv_b[j], cd_w_q[j],
                                                     cd_w_k[j], cd_w_v[j], cd_w_if[j], cd_b_if[j])
            ya = _mlstm(mq, mk, mv, gates, proj)
            yb = _dsw_attention(qd, kd, vd)
            time_major, w_out = False, cd_w_out[j]
        xn, logits = _outproj_ln_router(ya, yb, time_major, w_out.astype(BF16), x, ln_mix_g[layer],
                                        ln_mix_b[layer], moe_router_w[layer], moe_router_b[layer])
        x = _moe_ffn_ln(xn.reshape(t, d), logits.reshape(t, N_EXPERTS), layer, moe_w_in, moe_b_in,
                        moe_w_out, moe_b_out, ln_ffn_g[layer], ln_ffn_b[layer]).reshape(bsz, seq, d)
    return x
```

```python
import functools

import jax
import jax.numpy as jnp
from jax import lax
from jax.experimental import pallas as pl
from jax.experimental.pallas import tpu as pltpu

F32 = jnp.float32
BF16 = jnp.bfloat16

D_MODEL = 1024
DEPTH = 2
A_WIDTH = 512
B_WIDTH = 512
C_WIDTH = 512
D_WIDTH = 512
LRU_HEADS = 8
LRU_HEAD_DIM = A_WIDTH // LRU_HEADS
LRU_C = 8.0
CONV_WIDTH = 4
S5_GROUP_CH = 16
S5_GROUPS = B_WIDTH // S5_GROUP_CH
S5_STATE = 64
MLSTM_HEADS = 4
MLSTM_HEAD_DIM = C_WIDTH // MLSTM_HEADS
DSW_HEADS = 4
DSW_HEAD_DIM = D_WIDTH // DSW_HEADS
DSW_CONFIGS = ((128, 1), (512, 4), (2048, 16))
DSW_BLOCK = 128
ROPE_THETA = 500000.0
ROPE_DIMS = DSW_HEAD_DIM // 4
N_EXPERTS = 32
TOP_K = 4
D_EXPERT = D_MODEL
SWIGLU_LIMIT = 7.0
SWIGLU_ALPHA = 1.702
DEEPNORM_ALPHA = (2 * DEPTH) ** 0.25
LN_EPS = 1e-5

VMEM_LIMIT_BYTES = 56 * 1024 * 1024
LANES = 128
TILE_ROWS = D_MODEL // LANES
MOE_TM = 256
PROJ_TL = 512
LRU_TC = 128
S5_TC = 64
S5_HALF = 2
MLSTM_CS = 256
NEG_INF = float("-inf")


def _cparams(*sem):
    return pltpu.CompilerParams(dimension_semantics=sem, vmem_limit_bytes=VMEM_LIMIT_BYTES)


def _dot(a, b):
    return jnp.dot(a, b, preferred_element_type=F32)


def _dot_nt(a, b):
    return lax.dot_general(a, b, (((1,), (1,)), ((), ())), preferred_element_type=F32)


def _matmul_kernel(x_ref, w_ref, o_ref):
    o_ref[...] = _dot(x_ref[...].astype(BF16), w_ref[...])


def _inproj_time_major(x3, w_bf16):
    bsz, seq, k = x3.shape
    n = w_bf16.shape[1]
    out = pl.pallas_call(
        _matmul_kernel,
        grid=(bsz, seq // PROJ_TL),
        in_specs=[pl.BlockSpec((None, PROJ_TL, k), lambda b, i: (b, i, 0)),
                  pl.BlockSpec((k, n), lambda b, i: (0, 0))],
        out_specs=pl.BlockSpec((PROJ_TL, n), lambda b, i: (i, b)),
        out_shape=jax.ShapeDtypeStruct((seq, bsz * n), F32),
        compiler_params=_cparams("arbitrary", "arbitrary"),
        name="in_proj_time_major",
    )(x3, w_bf16)
    return out.reshape(seq, bsz, n)


def _inproj(x3, w_bf16):
    bsz, seq, k = x3.shape
    n = w_bf16.shape[1]
    return pl.pallas_call(
        _matmul_kernel,
        grid=(bsz, seq // PROJ_TL),
        in_specs=[pl.BlockSpec((None, PROJ_TL, k), lambda b, i: (b, i, 0)),
                  pl.BlockSpec((k, n), lambda b, i: (0, 0))],
        out_specs=pl.BlockSpec((None, PROJ_TL, n), lambda b, i: (b, i, 0)),
        out_shape=jax.ShapeDtypeStruct((bsz, seq, n), F32),
        compiler_params=_cparams("arbitrary", "arbitrary"),
        name="in_proj",
    )(x3, w_bf16)


def _layer_norm_rows(z, g, b):
    mu = jnp.mean(z, axis=-1, keepdims=True)
    zc = z - mu
    var = jnp.mean(zc * zc, axis=-1, keepdims=True)
    return zc * lax.rsqrt(var + LN_EPS) * g + b


def _outproj_ln_router_kernel(ya_ref, yb_ref, wa_ref, wb_ref, x_ref, g_ref, b_ref, rwh_ref, rwl_ref, rb_ref,
                              xn_ref, xt_ref, lg_ref):
    mix = _dot(ya_ref[...], wa_ref[...]) + _dot(yb_ref[...], wb_ref[...])
    xn = _layer_norm_rows(DEEPNORM_ALPHA * x_ref[...] + mix, g_ref[...], b_ref[...])
    xn_ref[...] = xn
    _store_token_tiles(xt_ref, xn)
    xh = xn.astype(BF16)
    xl = (xn - xh.astype(F32)).astype(BF16)
    lg_ref[...] = (_dot(xh, rwh_ref[...]) + _dot(xl, rwh_ref[...]) + _dot(xh, rwl_ref[...])) + rb_ref[...]


def _outproj_ln_router(ya, yb, time_major, w_bf16, x3, g, b, rw, rb):
    bsz, seq, d = x3.shape
    wa, wb = w_bf16[:ya.shape[-1]], w_bf16[ya.shape[-1]:]
    ne = rw.shape[1]
    rw_hi = rw.astype(BF16)
    rw_lo = (rw - rw_hi.astype(F32)).astype(BF16)
    tl = PROJ_TL
    if time_major:
        ya, yb = ya.reshape(seq, -1), yb.reshape(seq, -1)
        y_specs = [pl.BlockSpec((tl, wa.shape[0]), lambda bi, i: (i, bi)),
                   pl.BlockSpec((tl, wb.shape[0]), lambda bi, i: (i, bi))]
    else:
        y_specs = [pl.BlockSpec((None, tl, wa.shape[0]), lambda bi, i: (bi, i, 0)),
                   pl.BlockSpec((None, tl, wb.shape[0]), lambda bi, i: (bi, i, 0))]
    fixed = lambda bi, i: (0, 0)
    tok = lambda bi, i: (bi, i, 0)
    return pl.pallas_call(
        _outproj_ln_router_kernel,
        grid=(bsz, seq // tl),
        in_specs=y_specs + [pl.BlockSpec(wa.shape, fixed), pl.BlockSpec(wb.shape, fixed),
                            pl.BlockSpec((None, tl, d), tok),
                            pl.BlockSpec((1, d), fixed), pl.BlockSpec((1, d), fixed),
                            pl.BlockSpec((d, ne), fixed), pl.BlockSpec((d, ne), fixed),
                            pl.BlockSpec((1, ne), fixed)],
        out_specs=[pl.BlockSpec((None, tl, d), tok),
                   pl.BlockSpec((tl * TILE_ROWS, LANES), lambda bi, i: (bi * (seq // tl) + i, 0)),
                   pl.BlockSpec((None, tl, ne), tok)],
        out_shape=[jax.ShapeDtypeStruct((bsz, seq, d), F32),
                   jax.ShapeDtypeStruct((bsz * seq * TILE_ROWS, LANES), F32),
                   jax.ShapeDtypeStruct((bsz, seq, ne), F32)],
        compiler_params=_cparams("arbitrary", "arbitrary"),
        name="out_proj_ln_router",
    )(ya, yb, wa, wb, x3, g.reshape(1, d), b.reshape(1, d), rw_hi, rw_lo, rb.reshape(1, ne))


def _store_token_tiles(dst_ref, x):
    n = x.shape[0]
    for j in range(TILE_ROWS):
        dst_ref[pl.ds(j, n, stride=TILE_ROWS), :] = x[:, j * LANES:(j + 1) * LANES]


def _load_token_tiles(src_ref, n):
    return jnp.concatenate([src_ref[pl.ds(j, n, stride=TILE_ROWS), :] for j in range(TILE_ROWS)], axis=1)


def _gather_token_tiles(src_hbm, idx_ref, n, dst, sem):
    def body(g, carry):
        base = pl.multiple_of(g * (TILE_ROWS * TILE_ROWS), TILE_ROWS * TILE_ROWS)
        for u in range(TILE_ROWS):
            src_row = pl.multiple_of(idx_ref[0, g * TILE_ROWS + u] * TILE_ROWS, TILE_ROWS)
            pltpu.make_async_copy(src_hbm.at[pl.ds(src_row, TILE_ROWS)],
                                  dst.at[pl.ds(base + u * TILE_ROWS, TILE_ROWS)], sem).start()
        return carry
    lax.fori_loop(0, n // TILE_ROWS, body, 0)


def _moe_kernel(be_ref, nreal_ref, rt_ref, rtn_ref, x_hbm, win_ref, bin_ref, wout_ref, bout_ref, o_ref,
                win_bf, wout_bf, xbuf, sem):
    i = pl.program_id(0)
    n_real = nreal_ref[0]
    e = be_ref[i]
    prev = be_ref[jnp.maximum(i - 1, 0)]
    tm = xbuf.shape[1] // TILE_ROWS
    slot = lax.rem(i, 2)

    @pl.when(i == 0)
    def _first_gather():
        _gather_token_tiles(x_hbm, rt_ref, tm, xbuf.at[0], sem.at[0])

    @pl.when(i + 1 < n_real)
    def _next_gather():
        _gather_token_tiles(x_hbm, rtn_ref, tm, xbuf.at[1 - slot], sem.at[1 - slot])

    @pl.when((i == 0) | (e != prev))
    def _load_expert():
        win_bf[...] = win_ref[...].astype(BF16)
        wout_bf[...] = wout_ref[...].astype(BF16)

    @pl.when((i < n_real) | (i == 0))
    def _compute():
        pltpu.make_async_copy(xbuf.at[slot], xbuf.at[slot], sem.at[slot]).wait()
        x = _load_token_tiles(xbuf.at[slot], tm).astype(BF16)
        h = _dot(x, win_bf[...]) + bin_ref[...]
        g = jnp.minimum(h[:, :D_EXPERT], SWIGLU_LIMIT)
        lin = jnp.clip(h[:, D_EXPERT:], -SWIGLU_LIMIT, SWIGLU_LIMIT)
        y = g * jax.nn.sigmoid(SWIGLU_ALPHA * g) * (lin + 1.0)
        _store_token_tiles(o_ref, _dot(y.astype(BF16), wout_bf[...]) + bout_ref[...])

    @pl.when((i >= n_real) & (i > 0))
    def _unused_block():
        o_ref[...] = jnp.zeros_like(o_ref)


def _moe_experts(x_tiles, row_tok, block_expert, n_real, layer, w_in, b_in, w_out, b_out):
    d = D_MODEL
    tm = MOE_TM
    n_blocks = row_tok.shape[0] // tm
    nl, ne, _, dh2 = w_in.shape
    rt = row_tok.reshape(n_blocks, 1, tm)
    smem_rows = lambda imap: pl.BlockSpec((None, 1, tm), imap, memory_space=pltpu.SMEM)
    grid_spec = pltpu.PrefetchScalarGridSpec(
        num_scalar_prefetch=2,
        grid=(n_blocks,),
        in_specs=[
            smem_rows(lambda i, be, nr: (i, 0, 0)),
            smem_rows(lambda i, be, nr: (jnp.minimum(i + 1, n_blocks - 1), 0, 0)),
            pl.BlockSpec(memory_space=pl.ANY),
            pl.BlockSpec((None, None, d, dh2), lambda i, be, nr: (layer, be[i], 0, 0)),
            pl.BlockSpec((None, None, 1, dh2), lambda i, be, nr: (layer, be[i], 0, 0)),
            pl.BlockSpec((None, None, D_EXPERT, d), lambda i, be, nr: (layer, be[i], 0, 0)),
            pl.BlockSpec((None, None, 1, d), lambda i, be, nr: (layer, be[i], 0, 0)),
        ],
        out_specs=pl.BlockSpec((tm * TILE_ROWS, LANES), lambda i, be, nr: (i, 0)),
        scratch_shapes=[pltpu.VMEM((d, dh2), BF16), pltpu.VMEM((D_EXPERT, d), BF16),
                        pltpu.VMEM((2, tm * TILE_ROWS, LANES), F32), pltpu.SemaphoreType.DMA((2,))],
    )
    return pl.pallas_call(
        _moe_kernel,
        grid_spec=grid_spec,
        out_shape=jax.ShapeDtypeStruct((n_blocks * tm * TILE_ROWS, LANES), F32),
        compiler_params=_cparams("arbitrary"),
        name="moe_experts",
    )(block_expert, n_real, rt, rt, x_tiles, w_in, b_in.reshape(nl, ne, 1, dh2), w_out,
      b_out.reshape(nl, ne, 1, d))


def _moe_combine_kernel(dc_ref, dn_ref, x_ref, gate_ref, yb_hbm, g_ref, b_ref, o_ref, buf, ffn_ref, sem):
    i = pl.program_id(0)
    n = pl.num_programs(0)
    tk = x_ref.shape[0]
    rows = tk * TILE_ROWS
    slot = lax.rem(i, 2)

    @pl.when(i == 0)
    def _first_gather():
        _gather_token_tiles(yb_hbm, dc_ref, TOP_K * tk, buf.at[0], sem.at[0])

    @pl.when(i + 1 < n)
    def _next_gather():
        _gather_token_tiles(yb_hbm, dn_ref, TOP_K * tk, buf.at[1 - slot], sem.at[1 - slot])

    pltpu.make_async_copy(buf.at[slot], buf.at[slot], sem.at[slot]).wait()
    gate = gate_ref[...]
    ffn = gate[:, 0:1] * buf[slot, 0:rows]
    for k in range(1, TOP_K):
        ffn = ffn + gate[:, k:k + 1] * buf[slot, k * rows:(k + 1) * rows]
    ffn_ref[...] = ffn
    o_ref[...] = _layer_norm_rows(DEEPNORM_ALPHA * x_ref[...] + _load_token_tiles(ffn_ref, tk),
                                  g_ref[...], b_ref[...])


def _moe_combine_ln(xn, gate, dest, yb_tiles, g, b):
    t, d = xn.shape
    tk = MOE_TM
    n = t // tk
    dest_tiles = dest.reshape(n, tk, TOP_K).transpose(0, 2, 1).reshape(n, 1, TOP_K * tk)
    gate_rows = jnp.repeat(gate, TILE_ROWS, axis=0)
    smem_rows = lambda imap: pl.BlockSpec((None, 1, TOP_K * tk), imap, memory_space=pltpu.SMEM)
    row = lambda i: (i, 0)
    fixed = lambda i: (0, 0)
    return pl.pallas_call(
        _moe_combine_kernel,
        grid=(n,),
        in_specs=[smem_rows(lambda i: (i, 0, 0)), smem_rows(lambda i: (jnp.minimum(i + 1, n - 1), 0, 0)),
                  pl.BlockSpec((tk, d), row), pl.BlockSpec((tk * TILE_ROWS, TOP_K), row),
                  pl.BlockSpec(memory_space=pl.ANY),
                  pl.BlockSpec((1, d), fixed), pl.BlockSpec((1, d), fixed)],
        out_specs=pl.BlockSpec((tk, d), row),
        out_shape=jax.ShapeDtypeStruct((t, d), F32),
        scratch_shapes=[pltpu.VMEM((2, TOP_K * tk * TILE_ROWS, LANES), F32),
                        pltpu.VMEM((tk * TILE_ROWS, LANES), F32), pltpu.SemaphoreType.DMA((2,))],
        compiler_params=_cparams("arbitrary"),
        name="moe_combine_ln",
    )(dest_tiles, dest_tiles, xn, gate_rows, yb_tiles, g.reshape(1, d), b.reshape(1, d))


def _moe_ffn_ln(xn, x_tiles, logits, layer, w_in, b_in, w_out, b_out, g, b):
    t, d = xn.shape
    tm = MOE_TM
    top_logit, top_idx = lax.top_k(logits, TOP_K)
    gate = jax.nn.softmax(top_logit, axis=-1)
    flat_e = top_idx.reshape(-1).astype(jnp.int32)
    onehot = (flat_e[:, None] == jnp.arange(N_EXPERTS, dtype=jnp.int32)[None, :]).astype(jnp.int32)
    csum = jnp.cumsum(onehot, axis=0)
    rank = jnp.sum(csum * onehot, axis=1) - 1
    counts = csum[-1]
    blocks_per_e = (counts + tm - 1) // tm
    blk_end = jnp.cumsum(blocks_per_e)
    blk_start = blk_end - blocks_per_e
    row_start = jnp.cumsum(counts) - counts
    dest = blk_start[flat_e] * tm + rank
    n_blocks = (t * TOP_K) // tm + N_EXPERTS
    block_expert = jnp.minimum(
        jnp.sum((blk_end[None, :] <= jnp.arange(n_blocks, dtype=jnp.int32)[:, None]).astype(jnp.int32), axis=1),
        N_EXPERTS - 1)
    n_real = blk_end[-1:].astype(jnp.int32)
    tok_sorted = (jnp.argsort(flat_e, stable=True) // TOP_K).astype(jnp.int32)
    e_row = jnp.repeat(block_expert, tm)
    rank_row = jnp.arange(n_blocks * tm, dtype=jnp.int32) - jnp.repeat(blk_start[block_expert] * tm, tm)
    valid = rank_row < counts[e_row]
    src = jnp.clip(row_start[e_row] + rank_row, 0, t * TOP_K - 1)
    row_tok = jnp.where(valid, tok_sorted[src], 0)
    yb_tiles = _moe_experts(x_tiles, row_tok, block_expert, n_real, layer, w_in, b_in, w_out, b_out)
    return _moe_combine_ln(xn, gate, dest, yb_tiles, g, b)


def _softplus(x):
    return jnp.maximum(x, 0.0) + jnp.log(1.0 + jnp.exp(-jnp.abs(x)))


def _gelu_tanh(x):
    return 0.5 * x * (1.0 + jnp.tanh(0.7978845608028654 * (x + 0.044715 * (x * x * x))))


def _lru_kernel(xa_ref, ga_ref, cw_ref, cb_ref, wr_ref, br_ref, wi_ref, bi_ref, lam_ref, o_ref,
                tail_ref, h_ref, a_ref, u_ref):
    tc, bsz, w = xa_ref.shape

    @pl.when(pl.program_id(0) == 0)
    def _init():
        tail_ref[...] = jnp.zeros_like(tail_ref)
        h_ref[...] = jnp.zeros_like(h_ref)

    xa = xa_ref[...]
    ext = jnp.concatenate([tail_ref[...], xa], axis=0)
    xc = cb_ref[...] + ext[0:tc] * cw_ref[0]
    for tap in range(1, CONV_WIDTH):
        xc = xc + ext[tap:tap + tc] * cw_ref[tap]
    tail_ref[...] = xa[tc - (CONV_WIDTH - 1):]
    x2 = xc.reshape(tc * bsz, w)
    xb = x2.astype(BF16)
    r = jax.nn.sigmoid(_dot(xb, wr_ref[...]) + br_ref[...])
    ig = jax.nn.sigmoid(_dot(xb, wi_ref[...]) + bi_ref[...])
    log_a = (-LRU_C) * r * _softplus(-lam_ref[...])
    a_ref[...] = jnp.exp(log_a).reshape(tc, bsz, w)
    u_ref[...] = (jnp.sqrt(1.0 - jnp.exp(2.0 * log_a)) * (ig * x2)).reshape(tc, bsz, w)

    def step(t, h):
        h = a_ref[t] * h + u_ref[t]
        u_ref[t] = h
        return h

    h_ref[...] = lax.fori_loop(0, tc, step, h_ref[...], unroll=8)
    o_ref[...] = (u_ref[...] * _gelu_tanh(ga_ref[...])).astype(o_ref.dtype)


def _block_diag(w):
    h, i, j = w.shape
    return jnp.einsum('hij,hk->hikj', w, jnp.eye(h, dtype=w.dtype)).reshape(h * i, h * j)


def _rg_lru(proj, conv_w, conv_b, gate_r_w, gate_r_b, gate_i_w, gate_i_b, lru_lambda):
    seq, bsz, _ = proj.shape
    w = A_WIDTH
    tc = LRU_TC
    fixed2 = lambda i: (0, 0)
    fixed3 = lambda i: (0, 0, 0)
    return pl.pallas_call(
        _lru_kernel,
        grid=(seq // tc,),
        in_specs=[pl.BlockSpec((tc, bsz, w), lambda i: (i, 0, 0)),
                  pl.BlockSpec((tc, bsz, w), lambda i: (i, 0, 1)),
                  pl.BlockSpec((CONV_WIDTH, 1, w), fixed3), pl.BlockSpec((1, w), fixed2),
                  pl.BlockSpec((w, w), fixed2), pl.BlockSpec((1, w), fixed2),
                  pl.BlockSpec((w, w), fixed2), pl.BlockSpec((1, w), fixed2),
                  pl.BlockSpec((1, w), fixed2)],
        out_specs=pl.BlockSpec((tc, bsz, w), lambda i: (i, 0, 0)),
        out_shape=jax.ShapeDtypeStruct((seq, bsz, w), BF16),
        scratch_shapes=[pltpu.VMEM((CONV_WIDTH - 1, bsz, w), F32), pltpu.VMEM((bsz, w), F32),
                        pltpu.VMEM((tc, bsz, w), F32), pltpu.VMEM((tc, bsz, w), F32)],
        compiler_params=_cparams("arbitrary"),
        name="rg_lru",
    )(proj, proj, conv_w.reshape(CONV_WIDTH, 1, w), conv_b.reshape(1, w),
      _block_diag(gate_r_w).astype(BF16), gate_r_b.reshape(1, w),
      _block_diag(gate_i_w).astype(BF16), gate_i_b.reshape(1, w), lru_lambda.reshape(1, w))


def _s5_kernel(u_ref, wbr_ref, wbi_ref, ar_ref, ai_ref, ccr_ref, cci_ref, d_ref, gw_ref, gb_ref, o_ref,
               xr_ref, xi_ref, sr_ref, si_ref):
    tc, bsz, w = u_ref.shape
    nstate = xr_ref.shape[-1]
    sh = nstate // S5_HALF
    wh = w // S5_HALF

    @pl.when(pl.program_id(0) == 0)
    def _init():
        sr_ref[...] = jnp.zeros_like(sr_ref)
        si_ref[...] = jnp.zeros_like(si_ref)

    u2 = u_ref[...].reshape(tc * bsz, w)
    ub = u2.astype(BF16)
    for hf in range(S5_HALF):
        uh = ub[:, hf * wh:(hf + 1) * wh]
        xr_ref[:, :, hf * sh:(hf + 1) * sh] = _dot(uh, wbr_ref[hf]).reshape(tc, bsz, sh)
        xi_ref[:, :, hf * sh:(hf + 1) * sh] = _dot(uh, wbi_ref[hf]).reshape(tc, bsz, sh)

    for hf in range(S5_HALF):
        lo, hi = hf * sh, (hf + 1) * sh
        ar = jnp.broadcast_to(ar_ref[:, lo:hi], (bsz, sh))
        ai = jnp.broadcast_to(ai_ref[:, lo:hi], (bsz, sh))

        def step(t, carry, lo=lo, hi=hi, ar=ar, ai=ai):
            xr, xi = carry
            nxr = ar * xr - ai * xi + xr_ref[t, :, lo:hi]
            nxi = ar * xi + ai * xr + xi_ref[t, :, lo:hi]
            xr_ref[t, :, lo:hi] = nxr
            xi_ref[t, :, lo:hi] = nxi
            return nxr, nxi

        xr, xi = lax.fori_loop(0, tc, step, (sr_ref[:, lo:hi], si_ref[:, lo:hi]), unroll=4)
        sr_ref[:, lo:hi] = xr
        si_ref[:, lo:hi] = xi

    ys = []
    for hf in range(S5_HALF):
        lo, hi = hf * sh, (hf + 1) * sh
        xrh = xr_ref[:, :, lo:hi].reshape(tc * bsz, sh).astype(BF16)
        xih = xi_ref[:, :, lo:hi].reshape(tc * bsz, sh).astype(BF16)
        ys.append(_dot(xrh, ccr_ref[hf]) - _dot(xih, cci_ref[hf]))
    y = jnp.concatenate(ys, axis=1) + d_ref[...] * u2
    s = _gelu_tanh(y)
    yb = s * jax.nn.sigmoid(_dot(s.astype(BF16), gw_ref[...]) + gb_ref[...])
    o_ref[...] = yb.reshape(tc, bsz, w).astype(o_ref.dtype)


def _s5_glu(proj, lam_re, lam_im, log_step, b_re, b_im, c_re, c_im, d_skip, glu_w, glu_b):
    seq, bsz, _ = proj.shape
    w = B_WIDTH
    tc = S5_TC
    nstate = S5_GROUPS * S5_STATE
    gh = S5_GROUPS // S5_HALF
    lr = jnp.minimum(lam_re, -1e-4)
    li = lam_im
    step = jnp.exp(log_step)[:, None]
    mag = jnp.exp(lr * step)
    ar = mag * jnp.cos(li * step)
    ai = mag * jnp.sin(li * step)
    inv = 1.0 / (lr * lr + li * li)
    zr = ((ar - 1.0) * lr + ai * li) * inv
    zi = (ai * lr - (ar - 1.0) * li) * inv
    bbr = zr[..., None] * b_re - zi[..., None] * b_im
    bbi = zr[..., None] * b_im + zi[..., None] * b_re
    eye = jnp.eye(gh, dtype=F32)

    def expand_in(bb):
        bb = bb.reshape(S5_HALF, gh, S5_STATE, S5_GROUP_CH)
        return jnp.einsum('fgph,gk->fghkp', bb, eye).reshape(
            S5_HALF, gh * S5_GROUP_CH, gh * S5_STATE).astype(BF16)

    def expand_out(c):
        c = c.reshape(S5_HALF, gh, S5_GROUP_CH, S5_STATE)
        return jnp.einsum('fghp,gk->fgpkh', c, eye).reshape(
            S5_HALF, gh * S5_STATE, gh * S5_GROUP_CH).astype(BF16)

    fixed2 = lambda i: (0, 0)
    fixed3 = lambda i: (0, 0, 0)
    return pl.pallas_call(
        _s5_kernel,
        grid=(seq // tc,),
        in_specs=[pl.BlockSpec((tc, bsz, w), lambda i: (i, 0, 2)),
                  pl.BlockSpec((S5_HALF, w // S5_HALF, nstate // S5_HALF), fixed3),
                  pl.BlockSpec((S5_HALF, w // S5_HALF, nstate // S5_HALF), fixed3),
                  pl.BlockSpec((1, nstate), fixed2), pl.BlockSpec((1, nstate), fixed2),
                  pl.BlockSpec((S5_HALF, nstate // S5_HALF, w // S5_HALF), fixed3),
                  pl.BlockSpec((S5_HALF, nstate // S5_HALF, w // S5_HALF), fixed3),
                  pl.BlockSpec((1, w), fixed2), pl.BlockSpec((w, w), fixed2), pl.BlockSpec((1, w), fixed2)],
        out_specs=pl.BlockSpec((tc, bsz, w), lambda i: (i, 0, 0)),
        out_shape=jax.ShapeDtypeStruct((seq, bsz, w), BF16),
        scratch_shapes=[pltpu.VMEM((tc, bsz, nstate), F32), pltpu.VMEM((tc, bsz, nstate), F32),
                        pltpu.VMEM((bsz, nstate), F32), pltpu.VMEM((bsz, nstate), F32)],
        compiler_params=_cparams("arbitrary"),
        name="s5_glu",
    )(proj, expand_in(bbr), expand_in(bbi), ar.reshape(1, nstate), ai.reshape(1, nstate),
      expand_out(c_re), expand_out(c_im), d_skip.reshape(1, w), glu_w.astype(BF16), glu_b.reshape(1, w))


def _cd_prep_kernel(xc_ref, q_ref, k_ref, v_ref, pos_ref, cw_ref, cb_ref, wq_ref, wk_ref, wv_ref, wif_ref,
                    bif_ref, invf_ref, mq_ref, mk_ref, mv_ref, gt_ref, qd_ref, kd_ref, vd_ref, tail_ref):
    tl, w = xc_ref.shape
    pad = tail_ref.shape[0]

    @pl.when(pl.program_id(1) == 0)
    def _init():
        tail_ref[...] = jnp.zeros_like(tail_ref)

    xc = xc_ref[...]
    ext = jnp.concatenate([tail_ref[...], xc], axis=0)
    conv = cb_ref[...]
    for tap in range(CONV_WIDTH):
        off = pad - (CONV_WIDTH - 1) + tap
        conv = conv + ext[off:off + tl] * cw_ref[tap]
    tail_ref[...] = xc[tl - pad:]
    xconv = (conv * jax.nn.sigmoid(conv)).astype(BF16)
    q = _dot(xconv, wq_ref[...]).astype(BF16)
    k = (_dot(xconv, wk_ref[...]) * (MLSTM_HEAD_DIM ** -0.5)).astype(BF16)
    v = _dot(xc.astype(BF16), wv_ref[...]).astype(BF16)
    mq_ref[...] = q
    mk_ref[...] = k
    mv_ref[...] = v
    gt_ref[...] = (_dot(q, wif_ref[0]) + _dot(k, wif_ref[1]) + _dot(v, wif_ref[2]) + bif_ref[...])

    ang = pos_ref[...].astype(F32) * invf_ref[...]
    cos = jnp.cos(ang)
    sin = jnp.sin(ang)
    lane = lax.broadcasted_iota(jnp.int32, (1, LANES), 1)
    half = ROPE_DIMS // 2
    sin_lo = jnp.where(lane < half, -sin, 0.0)
    sin_hi = jnp.where((lane >= half) & (lane < ROPE_DIMS), sin, 0.0)
    for src, dst in ((q_ref, qd_ref), (k_ref, kd_ref)):
        for h in range(DSW_HEADS):
            sl = slice(h * DSW_HEAD_DIM, (h + 1) * DSW_HEAD_DIM)
            t = src[:, sl]
            rot = (t * cos + pltpu.roll(t, LANES - half, axis=1) * sin_lo
                   + pltpu.roll(t, half, axis=1) * sin_hi)
            dst[:, sl] = rot.astype(BF16)
    vd_ref[...] = v_ref[...].astype(BF16)


def _cd_prep(proj, positions, conv_w, conv_b, w_q, w_k, w_v, w_if, b_if):
    bsz, seq, _ = proj.shape
    w = C_WIDTH
    tl = PROJ_TL
    half = ROPE_DIMS // 2
    inv_freq = ROPE_THETA ** (-jnp.arange(half, dtype=F32) / half)
    invf = jnp.zeros((1, LANES), F32).at[0, :half].set(inv_freq).at[0, half:ROPE_DIMS].set(inv_freq)
    wif = jnp.zeros((3, w, LANES), F32).at[:, :, :2 * MLSTM_HEADS].set(w_if.reshape(3, w, 2 * MLSTM_HEADS))
    bif = jnp.zeros((1, LANES), F32).at[0, :2 * MLSTM_HEADS].set(b_if)
    col = lambda c: (lambda b, i: (b, i, c))
    fixed2 = lambda b, i: (0, 0)
    fixed3 = lambda b, i: (0, 0, 0)
    tok_bf = jax.ShapeDtypeStruct((bsz, seq, w), BF16)
    return pl.pallas_call(
        _cd_prep_kernel,
        grid=(bsz, seq // tl),
        in_specs=[pl.BlockSpec((None, tl, w), col(0)), pl.BlockSpec((None, tl, w), col(2)),
                  pl.BlockSpec((None, tl, w), col(3)), pl.BlockSpec((None, tl, w), col(4)),
                  pl.BlockSpec((None, tl, 1), col(0)),
                  pl.BlockSpec((CONV_WIDTH, 1, w), fixed3), pl.BlockSpec((1, w), fixed2),
                  pl.BlockSpec((w, w), fixed2), pl.BlockSpec((w, w), fixed2), pl.BlockSpec((w, w), fixed2),
                  pl.BlockSpec((3, w, LANES), fixed3), pl.BlockSpec((1, LANES), fixed2),
                  pl.BlockSpec((1, LANES), fixed2)],
        out_specs=[pl.BlockSpec((None, tl, w), col(0))] * 3 + [pl.BlockSpec((None, tl, LANES), col(0))]
                  + [pl.BlockSpec((None, tl, w), col(0))] * 3,
        out_shape=[tok_bf, tok_bf, tok_bf, jax.ShapeDtypeStruct((bsz, seq, LANES), F32),
                   tok_bf, tok_bf, tok_bf],
        scratch_shapes=[pltpu.VMEM((8, w), F32)],
        compiler_params=_cparams("arbitrary", "arbitrary"),
        name="cd_prep",
    )(proj, proj, proj, proj, positions.reshape(bsz, seq, 1), conv_w.reshape(CONV_WIDTH, 1, w),
      conv_b.reshape(1, w), _block_diag(w_q).astype(BF16), _block_diag(w_k).astype(BF16),
      _block_diag(w_v).astype(BF16), wif.astype(BF16), bif, invf)


def _cumsum_rows(x):
    n = x.shape[0]
    row = lax.broadcasted_iota(jnp.int32, x.shape, 0)
    shift = 1
    while shift < n:
        x = x + jnp.where(row >= shift, pltpu.roll(x, shift, axis=0), 0.0)
        shift *= 2
    return x


def _mlstm_kernel(q_ref, k_ref, v_ref, g_ref, op_ref, o_ref, c_ref, n_ref, m_ref):
    cs = q_ref.shape[0]
    dh = MLSTM_HEAD_DIM

    @pl.when(pl.program_id(1) == 0)
    def _init():
        c_ref[...] = jnp.zeros_like(c_ref)
        n_ref[...] = jnp.zeros_like(n_ref)
        m_ref[...] = jnp.full_like(m_ref, NEG_INF)

    gates = g_ref[...]
    cum = _cumsum_rows(jax.nn.log_sigmoid(gates))
    gates_t = gates.T
    cum_t = cum.T
    causal = (lax.broadcasted_iota(jnp.int32, (cs, cs), 0) >= lax.broadcasted_iota(jnp.int32, (cs, cs), 1))
    for h in range(MLSTM_HEADS):
        sl = slice(h * dh, (h + 1) * dh)
        fcol = MLSTM_HEADS + h
        li_c, li_r = gates[:, h:h + 1], gates_t[h:h + 1, :]
        cum_c, cum_r = cum[:, fcol:fcol + 1], cum_t[fcol:fcol + 1, :]
        m_prev = m_ref[h][:, 0:1]
        q, k, v = q_ref[:, sl], k_ref[:, sl], v_ref[:, sl]
        log_d = jnp.where(causal, cum_c - cum_r + li_r, NEG_INF)
        log_inter = cum_c + m_prev
        m_s = jnp.maximum(jnp.max(log_d, axis=1, keepdims=True), log_inter)
        s = _dot_nt(q, k) * jnp.exp(log_d - m_s)
        inter = jnp.exp(log_inter - m_s)
        c_prev = c_ref[h]
        n_prev = n_ref[h]
        num = _dot(s.astype(BF16), v) + inter * _dot(q, c_prev.astype(BF16))
        den = (jnp.sum(s, axis=1, keepdims=True)
               + inter * jnp.sum(q.astype(F32) * n_prev, axis=1, keepdims=True))
        hh = num / jnp.maximum(jnp.abs(den), jnp.exp(-m_s))
        o_ref[:, sl] = (jax.nn.sigmoid(op_ref[:, sl]) * hh).astype(o_ref.dtype)
        chunk_f = cum_c[cs - 1:cs, :]
        to_end = chunk_f - cum_c + li_c
        m_c = jnp.max(to_end, axis=0, keepdims=True)
        m_new = jnp.maximum(chunk_f + m_prev, m_c)
        s_old = jnp.exp(chunk_f + m_prev - m_new)
        s_new = jnp.exp(m_c - m_new)
        kw = k.astype(F32) * jnp.exp(to_end - m_c)
        c_ref[h] = s_old * c_prev + s_new * _dot(kw.T.astype(BF16), v)
        n_ref[h] = s_old * n_prev + s_new * jnp.sum(kw, axis=0, keepdims=True)
        m_ref[h] = jnp.broadcast_to(m_new, (1, LANES))


def _mlstm(mq, mk, mv, gates, proj):
    bsz, seq, w = mq.shape
    cs = MLSTM_CS
    tok = lambda b, i: (b, i, 0)
    return pl.pallas_call(
        _mlstm_kernel,
        grid=(bsz, seq // cs),
        in_specs=[pl.BlockSpec((None, cs, w), tok)] * 3
                 + [pl.BlockSpec((None, cs, LANES), tok), pl.BlockSpec((None, cs, w), lambda b, i: (b, i, 1))],
        out_specs=pl.BlockSpec((None, cs, w), tok),
        out_shape=jax.ShapeDtypeStruct((bsz, seq, w), BF16),
        scratch_shapes=[pltpu.VMEM((MLSTM_HEADS, MLSTM_HEAD_DIM, MLSTM_HEAD_DIM), F32),
                        pltpu.VMEM((MLSTM_HEADS, 1, MLSTM_HEAD_DIM), F32),
                        pltpu.VMEM((MLSTM_HEADS, 1, LANES), F32)],
        compiler_params=_cparams("arbitrary", "arbitrary"),
        name="mlstm",
    )(mq, mk, mv, gates, proj)


def _dsw_kernel(q_ref, k_ref, v_ref, kp_ref, vp_ref, o_ref, l_ref, *, nres, nsub, span):
    blk = DSW_BLOCK
    dh = DSW_HEAD_DIM
    j = pl.program_id(2)
    qi = lax.broadcasted_iota(jnp.int32, (blk, 2 * blk), 0)
    ki = lax.broadcasted_iota(jnp.int32, (blk, 2 * blk), 1)
    dist = qi + blk - ki
    band = (dist >= 0) & (dist <= span)
    band_first = band & ((ki >= blk) | (j > 0))
    lane = lax.broadcasted_iota(jnp.int32, (blk, LANES), 1)
    scale = dh ** -0.5
    for r in range(nres):
        for sb in range(nsub):
            rows = slice(sb * blk, (sb + 1) * blk)
            lse_tile = jnp.zeros((blk, LANES), F32)
            for h in range(DSW_HEADS):
                cols = slice(r * D_WIDTH + h * dh, r * D_WIDTH + (h + 1) * dh)
                if sb == 0:
                    k_prev, v_prev = kp_ref[:, cols], vp_ref[:, cols]
                else:
                    prev_rows = slice((sb - 1) * blk, sb * blk)
                    k_prev, v_prev = k_ref[prev_rows, cols], v_ref[prev_rows, cols]
                kk = jnp.concatenate([k_prev, k_ref[rows, cols]], axis=0)
                vv = jnp.concatenate([v_prev, v_ref[rows, cols]], axis=0)
                s = _dot_nt(q_ref[rows, cols], kk) * scale
                s = jnp.where(band_first if sb == 0 else band, s, NEG_INF)
                m = jnp.max(s, axis=1, keepdims=True)
                p = jnp.exp(s - m)
                l = jnp.sum(p, axis=1, keepdims=True)
                o_ref[rows, cols] = _dot(p.astype(BF16), vv) / l
                lse_tile = jnp.where(lane == h, m + jnp.log(l), lse_tile)
            l_ref[rows, r * LANES:(r + 1) * LANES] = lse_tile


def _dsw_config(qd, kd, vd, window, dil):
    bsz, seq, w = qd.shape
    n = seq // dil
    tq = min(512, n)
    nres = 1 if n >= 512 else 512 // n
    nsub = tq // DSW_BLOCK
    view = lambda a: a.reshape(bsz, n, dil * w)
    cur = lambda b, r, j: (b, j, r)
    prev = lambda b, r, j: (b, jnp.maximum(j * nsub - 1, 0), r)
    o, lse = pl.pallas_call(
        functools.partial(_dsw_kernel, nres=nres, nsub=nsub, span=window // dil),
        grid=(bsz, dil // nres, n // tq),
        in_specs=[pl.BlockSpec((None, tq, nres * w), cur)] * 3
                 + [pl.BlockSpec((None, DSW_BLOCK, nres * w), prev)] * 2,
        out_specs=[pl.BlockSpec((None, tq, nres * w), cur), pl.BlockSpec((None, tq, nres * LANES), cur)],
        out_shape=[jax.ShapeDtypeStruct((bsz, n, dil * w), F32),
                   jax.ShapeDtypeStruct((bsz, n, dil * LANES), F32)],
        compiler_params=_cparams("arbitrary", "arbitrary", "arbitrary"),
        name=f"dsw_dil{dil}",
    )(view(qd), view(kd), view(vd), view(kd), view(vd))
    return o.reshape(bsz, seq, w), lse.reshape(bsz, seq, LANES)


def _dsw_combine_kernel(o0_ref, o1_ref, o2_ref, l0_ref, l1_ref, l2_ref, y_ref):
    dh = DSW_HEAD_DIM
    outs = (o0_ref, o1_ref, o2_ref)
    lses = (l0_ref[...], l1_ref[...], l2_ref[...])
    mx = jnp.maximum(jnp.maximum(lses[0], lses[1]), lses[2])
    ws = [jnp.exp(l - mx) for l in lses]
    tot = ws[0] + ws[1] + ws[2]
    for h in range(DSW_HEADS):
        sl = slice(h * dh, (h + 1) * dh)
        acc = sum(ws[g][:, h:h + 1] * outs[g][:, sl] for g in range(3))
        y_ref[:, sl] = (acc / tot[:, h:h + 1]).astype(y_ref.dtype)


def _dsw_attention(qd, kd, vd):
    bsz, seq, w = qd.shape
    outs, lses = [], []
    for window, dil in DSW_CONFIGS:
        o, lse = _dsw_config(qd, kd, vd, window, dil)
        outs.append(o)
        lses.append(lse)
    tl = PROJ_TL
    tok = lambda b, i: (b, i, 0)
    return pl.pallas_call(
        _dsw_combine_kernel,
        grid=(bsz, seq // tl),
        in_specs=[pl.BlockSpec((None, tl, w), tok)] * 3 + [pl.BlockSpec((None, tl, LANES), tok)] * 3,
        out_specs=pl.BlockSpec((None, tl, w), tok),
        out_shape=jax.ShapeDtypeStruct((bsz, seq, w), BF16),
        compiler_params=_cparams("arbitrary", "arbitrary"),
        name="dsw_combine",
    )(*outs, *lses)


def kernel(x, positions, ab_w_in, ab_conv_w, ab_conv_b, ab_gate_r_w, ab_gate_r_b, ab_gate_i_w, ab_gate_i_b, ab_lru_lambda, ab_s5_lambda_re, ab_s5_lambda_im, ab_s5_log_step, ab_s5_b_re, ab_s5_b_im, ab_s5_c_re, ab_s5_c_im, ab_s5_d, ab_glu_w, ab_glu_b, ab_w_out, cd_w_in, cd_conv_w, cd_conv_b, cd_w_q, cd_w_k, cd_w_v, cd_w_if, cd_b_if, cd_w_out, ln_mix_g, ln_mix_b, ln_ffn_g, ln_ffn_b, moe_router_w, moe_router_b, moe_w_in, moe_b_in, moe_w_out, moe_b_out):
    bsz, seq, d = x.shape
    t = bsz * seq
    for layer in range(DEPTH):
        j = layer // 2
        if layer % 2 == 0:
            proj = _inproj_time_major(x, ab_w_in[j].astype(BF16))
            ya = _rg_lru(proj, ab_conv_w[j], ab_conv_b[j], ab_gate_r_w[j], ab_gate_r_b[j],
                         ab_gate_i_w[j], ab_gate_i_b[j], ab_lru_lambda[j])
            yb = _s5_glu(proj, ab_s5_lambda_re[j], ab_s5_lambda_im[j], ab_s5_log_step[j], ab_s5_b_re[j],
                         ab_s5_b_im[j], ab_s5_c_re[j], ab_s5_c_im[j], ab_s5_d[j], ab_glu_w[j], ab_glu_b[j])
            time_major, w_out = True, ab_w_out[j]
        else:
            proj = _inproj(x, cd_w_in[j].astype(BF16))
            mq, mk, mv, gates, qd, kd, vd = _cd_prep(proj, positions, cd_conv_w[j], cd_conv_b[j], cd_w_q[j],
                                                     cd_w_k[j], cd_w_v[j], cd_w_if[j], cd_b_if[j])
            ya = _mlstm(mq, mk, mv, gates, proj)
            yb = _dsw_attention(qd, kd, vd)
            time_major, w_out = False, cd_w_out[j]
        xn, x_tiles, logits = _outproj_ln_router(ya, yb, time_major, w_out.astype(BF16), x, ln_mix_g[layer],
                                                 ln_mix_b[layer], moe_router_w[layer], moe_router_b[layer])
        x = _moe_ffn_ln(xn.reshape(t, d), x_tiles, logits.reshape(t, N_EXPERTS), layer, moe_w_in, moe_b_in,
                        moe_w_out, moe_b_out, ln_ffn_g[layer], ln_ffn_b[layer]).reshape(bsz, seq, d)
    return x
```

```python
import functools

import jax
import jax.numpy as jnp
from jax import lax
from jax.experimental import pallas as pl
from jax.experimental.pallas import tpu as pltpu

F32 = jnp.float32
BF16 = jnp.bfloat16

D_MODEL = 1024
DEPTH = 2
A_WIDTH = 512
B_WIDTH = 512
C_WIDTH = 512
D_WIDTH = 512
LRU_HEADS = 8
LRU_HEAD_DIM = A_WIDTH // LRU_HEADS
LRU_C = 8.0
CONV_WIDTH = 4
S5_GROUP_CH = 16
S5_GROUPS = B_WIDTH // S5_GROUP_CH
S5_STATE = 64
MLSTM_HEADS = 4
MLSTM_HEAD_DIM = C_WIDTH // MLSTM_HEADS
DSW_HEADS = 4
DSW_HEAD_DIM = D_WIDTH // DSW_HEADS
DSW_CONFIGS = ((128, 1), (512, 4), (2048, 16))
DSW_BLOCK = 128
ROPE_THETA = 500000.0
ROPE_DIMS = DSW_HEAD_DIM // 4
N_EXPERTS = 32
TOP_K = 4
D_EXPERT = D_MODEL
SWIGLU_LIMIT = 7.0
SWIGLU_ALPHA = 1.702
DEEPNORM_ALPHA = (2 * DEPTH) ** 0.25
LN_EPS = 1e-5

VMEM_LIMIT_BYTES = 56 * 1024 * 1024
LANES = 128
TILE_ROWS = D_MODEL // LANES
MOE_TM = 512
MOE_TK = 256
PROJ_TL = 512
LRU_TC = 128
S5_TC = 64
S5_HALF = 2
MLSTM_CS = 256
NEG_INF = float("-inf")


def _cparams(*sem):
    return pltpu.CompilerParams(dimension_semantics=sem, vmem_limit_bytes=VMEM_LIMIT_BYTES)


def _dot(a, b):
    return jnp.dot(a, b, preferred_element_type=F32)


def _dot_nt(a, b):
    return lax.dot_general(a, b, (((1,), (1,)), ((), ())), preferred_element_type=F32)


def _matmul_kernel(x_ref, w_ref, o_ref):
    o_ref[...] = _dot(x_ref[...].astype(BF16), w_ref[...])


def _inproj_time_major(x3, w_bf16):
    bsz, seq, k = x3.shape
    n = w_bf16.shape[1]
    out = pl.pallas_call(
        _matmul_kernel,
        grid=(bsz, seq // PROJ_TL),
        in_specs=[pl.BlockSpec((None, PROJ_TL, k), lambda b, i: (b, i, 0)),
                  pl.BlockSpec((k, n), lambda b, i: (0, 0))],
        out_specs=pl.BlockSpec((PROJ_TL, n), lambda b, i: (i, b)),
        out_shape=jax.ShapeDtypeStruct((seq, bsz * n), F32),
        compiler_params=_cparams("arbitrary", "arbitrary"),
        name="in_proj_time_major",
    )(x3, w_bf16)
    return out.reshape(seq, bsz, n)


def _inproj(x3, w_bf16):
    bsz, seq, k = x3.shape
    n = w_bf16.shape[1]
    return pl.pallas_call(
        _matmul_kernel,
        grid=(bsz, seq // PROJ_TL),
        in_specs=[pl.BlockSpec((None, PROJ_TL, k), lambda b, i: (b, i, 0)),
                  pl.BlockSpec((k, n), lambda b, i: (0, 0))],
        out_specs=pl.BlockSpec((None, PROJ_TL, n), lambda b, i: (b, i, 0)),
        out_shape=jax.ShapeDtypeStruct((bsz, seq, n), F32),
        compiler_params=_cparams("arbitrary", "arbitrary"),
        name="in_proj",
    )(x3, w_bf16)


def _layer_norm_rows(z, g, b):
    mu = jnp.mean(z, axis=-1, keepdims=True)
    zc = z - mu
    var = jnp.mean(zc * zc, axis=-1, keepdims=True)
    return zc * lax.rsqrt(var + LN_EPS) * g + b


def _outproj_ln_router_kernel(ya_ref, yb_ref, wa_ref, wb_ref, x_ref, g_ref, b_ref, rwh_ref, rwl_ref, rb_ref,
                              xn_ref, xt_ref, lg_ref):
    mix = _dot(ya_ref[...], wa_ref[...]) + _dot(yb_ref[...], wb_ref[...])
    xn = _layer_norm_rows(DEEPNORM_ALPHA * x_ref[...] + mix, g_ref[...], b_ref[...])
    xn_ref[...] = xn
    _store_token_tiles(xt_ref, xn)
    xh = xn.astype(BF16)
    xl = (xn - xh.astype(F32)).astype(BF16)
    lg_ref[...] = (_dot(xh, rwh_ref[...]) + _dot(xl, rwh_ref[...]) + _dot(xh, rwl_ref[...])) + rb_ref[...]


def _outproj_ln_router(ya, yb, time_major, w_bf16, x3, g, b, rw, rb):
    bsz, seq, d = x3.shape
    wa, wb = w_bf16[:ya.shape[-1]], w_bf16[ya.shape[-1]:]
    ne = rw.shape[1]
    rw_hi = rw.astype(BF16)
    rw_lo = (rw - rw_hi.astype(F32)).astype(BF16)
    tl = PROJ_TL
    if time_major:
        ya, yb = ya.reshape(seq, -1), yb.reshape(seq, -1)
        y_specs = [pl.BlockSpec((tl, wa.shape[0]), lambda bi, i: (i, bi)),
                   pl.BlockSpec((tl, wb.shape[0]), lambda bi, i: (i, bi))]
    else:
        y_specs = [pl.BlockSpec((None, tl, wa.shape[0]), lambda bi, i: (bi, i, 0)),
                   pl.BlockSpec((None, tl, wb.shape[0]), lambda bi, i: (bi, i, 0))]
    fixed = lambda bi, i: (0, 0)
    tok = lambda bi, i: (bi, i, 0)
    return pl.pallas_call(
        _outproj_ln_router_kernel,
        grid=(bsz, seq // tl),
        in_specs=y_specs + [pl.BlockSpec(wa.shape, fixed), pl.BlockSpec(wb.shape, fixed),
                            pl.BlockSpec((None, tl, d), tok),
                            pl.BlockSpec((1, d), fixed), pl.BlockSpec((1, d), fixed),
                            pl.BlockSpec((d, ne), fixed), pl.BlockSpec((d, ne), fixed),
                            pl.BlockSpec((1, ne), fixed)],
        out_specs=[pl.BlockSpec((None, tl, d), tok),
                   pl.BlockSpec((tl * TILE_ROWS, LANES), lambda bi, i: (bi * (seq // tl) + i, 0)),
                   pl.BlockSpec((None, tl, ne), tok)],
        out_shape=[jax.ShapeDtypeStruct((bsz, seq, d), F32),
                   jax.ShapeDtypeStruct((bsz * seq * TILE_ROWS, LANES), F32),
                   jax.ShapeDtypeStruct((bsz, seq, ne), F32)],
        compiler_params=_cparams("arbitrary", "arbitrary"),
        name="out_proj_ln_router",
    )(ya, yb, wa, wb, x3, g.reshape(1, d), b.reshape(1, d), rw_hi, rw_lo, rb.reshape(1, ne))


def _store_token_tiles(dst_ref, x):
    n = x.shape[0]
    for j in range(TILE_ROWS):
        dst_ref[pl.ds(j, n, stride=TILE_ROWS), :] = x[:, j * LANES:(j + 1) * LANES]


def _load_token_tiles(src_ref, n):
    return jnp.concatenate([src_ref[pl.ds(j, n, stride=TILE_ROWS), :] for j in range(TILE_ROWS)], axis=1)


def _gather_token_tiles(src_hbm, idx_ref, n, dst, sem):
    def body(g, carry):
        base = pl.multiple_of(g * (TILE_ROWS * TILE_ROWS), TILE_ROWS * TILE_ROWS)
        for u in range(TILE_ROWS):
            src_row = pl.multiple_of(idx_ref[0, g * TILE_ROWS + u] * TILE_ROWS, TILE_ROWS)
            pltpu.make_async_copy(src_hbm.at[pl.ds(src_row, TILE_ROWS)],
                                  dst.at[pl.ds(base + u * TILE_ROWS, TILE_ROWS)], sem).start()
        return carry
    lax.fori_loop(0, n // TILE_ROWS, body, 0, unroll=4)


def _moe_kernel(be_ref, nreal_ref, rt_ref, rtn_ref, x_hbm, win_ref, bin_ref, wout_ref, bout_ref, o_ref,
                win_bf, wout_bf, xbuf, sem):
    i = pl.program_id(0)
    n_real = nreal_ref[0]
    e = be_ref[i]
    prev = be_ref[jnp.maximum(i - 1, 0)]
    tm = xbuf.shape[1] // TILE_ROWS
    slot = lax.rem(i, 2)

    @pl.when(i == 0)
    def _first_gather():
        _gather_token_tiles(x_hbm, rt_ref, tm, xbuf.at[0], sem.at[0])

    @pl.when(i + 1 < n_real)
    def _next_gather():
        _gather_token_tiles(x_hbm, rtn_ref, tm, xbuf.at[1 - slot], sem.at[1 - slot])

    @pl.when((i == 0) | (e != prev))
    def _load_expert():
        win_bf[...] = win_ref[...].astype(BF16)
        wout_bf[...] = wout_ref[...].astype(BF16)

    @pl.when((i < n_real) | (i == 0))
    def _compute():
        pltpu.make_async_copy(xbuf.at[slot], xbuf.at[slot], sem.at[slot]).wait()
        x = _load_token_tiles(xbuf.at[slot], tm).astype(BF16)
        h = _dot(x, win_bf[...]) + bin_ref[...]
        g = jnp.minimum(h[:, :D_EXPERT], SWIGLU_LIMIT)
        lin = jnp.clip(h[:, D_EXPERT:], -SWIGLU_LIMIT, SWIGLU_LIMIT)
        y = g * jax.nn.sigmoid(SWIGLU_ALPHA * g) * (lin + 1.0)
        _store_token_tiles(o_ref, _dot(y.astype(BF16), wout_bf[...]) + bout_ref[...])

    @pl.when((i >= n_real) & (i > 0))
    def _unused_block():
        o_ref[...] = jnp.zeros_like(o_ref)


def _moe_experts(x_tiles, row_tok, block_expert, n_real, layer, w_in, b_in, w_out, b_out):
    d = D_MODEL
    tm = MOE_TM
    n_blocks = row_tok.shape[0] // tm
    nl, ne, _, dh2 = w_in.shape
    rt = row_tok.reshape(n_blocks, 1, tm)
    smem_rows = lambda imap: pl.BlockSpec((None, 1, tm), imap, memory_space=pltpu.SMEM)
    grid_spec = pltpu.PrefetchScalarGridSpec(
        num_scalar_prefetch=2,
        grid=(n_blocks,),
        in_specs=[
            smem_rows(lambda i, be, nr: (i, 0, 0)),
            smem_rows(lambda i, be, nr: (jnp.minimum(i + 1, n_blocks - 1), 0, 0)),
            pl.BlockSpec(memory_space=pl.ANY),
            pl.BlockSpec((None, None, d, dh2), lambda i, be, nr: (layer, be[i], 0, 0)),
            pl.BlockSpec((None, None, 1, dh2), lambda i, be, nr: (layer, be[i], 0, 0)),
            pl.BlockSpec((None, None, D_EXPERT, d), lambda i, be, nr: (layer, be[i], 0, 0)),
            pl.BlockSpec((None, None, 1, d), lambda i, be, nr: (layer, be[i], 0, 0)),
        ],
        out_specs=pl.BlockSpec((tm * TILE_ROWS, LANES), lambda i, be, nr: (i, 0)),
        scratch_shapes=[pltpu.VMEM((d, dh2), BF16), pltpu.VMEM((D_EXPERT, d), BF16),
                        pltpu.VMEM((2, tm * TILE_ROWS, LANES), F32), pltpu.SemaphoreType.DMA((2,))],
    )
    return pl.pallas_call(
        _moe_kernel,
        grid_spec=grid_spec,
        out_shape=jax.ShapeDtypeStruct((n_blocks * tm * TILE_ROWS, LANES), F32),
        compiler_params=_cparams("arbitrary"),
        name="moe_experts",
    )(block_expert, n_real, rt, rt, x_tiles, w_in, b_in.reshape(nl, ne, 1, dh2), w_out,
      b_out.reshape(nl, ne, 1, d))


def _moe_combine_kernel(dc_ref, dn_ref, x_ref, gate_ref, yb_hbm, g_ref, b_ref, o_ref, buf, ffn_ref, sem):
    i = pl.program_id(0)
    n = pl.num_programs(0)
    tk = x_ref.shape[0]
    rows = tk * TILE_ROWS
    slot = lax.rem(i, 2)

    @pl.when(i == 0)
    def _first_gather():
        _gather_token_tiles(yb_hbm, dc_ref, TOP_K * tk, buf.at[0], sem.at[0])

    @pl.when(i + 1 < n)
    def _next_gather():
        _gather_token_tiles(yb_hbm, dn_ref, TOP_K * tk, buf.at[1 - slot], sem.at[1 - slot])

    pltpu.make_async_copy(buf.at[slot], buf.at[slot], sem.at[slot]).wait()
    gate = gate_ref[...]
    ffn = gate[:, 0:1] * buf[slot, 0:rows]
    for k in range(1, TOP_K):
        ffn = ffn + gate[:, k:k + 1] * buf[slot, k * rows:(k + 1) * rows]
    ffn_ref[...] = ffn
    o_ref[...] = _layer_norm_rows(DEEPNORM_ALPHA * x_ref[...] + _load_token_tiles(ffn_ref, tk),
                                  g_ref[...], b_ref[...])


def _moe_combine_ln(xn, gate, dest, yb_tiles, g, b):
    t, d = xn.shape
    tk = MOE_TK
    n = t // tk
    dest_tiles = dest.reshape(n, tk, TOP_K).transpose(0, 2, 1).reshape(n, 1, TOP_K * tk)
    gate_rows = jnp.repeat(gate, TILE_ROWS, axis=0)
    smem_rows = lambda imap: pl.BlockSpec((None, 1, TOP_K * tk), imap, memory_space=pltpu.SMEM)
    row = lambda i: (i, 0)
    fixed = lambda i: (0, 0)
    return pl.pallas_call(
        _moe_combine_kernel,
        grid=(n,),
        in_specs=[smem_rows(lambda i: (i, 0, 0)), smem_rows(lambda i: (jnp.minimum(i + 1, n - 1), 0, 0)),
                  pl.BlockSpec((tk, d), row), pl.BlockSpec((tk * TILE_ROWS, TOP_K), row),
                  pl.BlockSpec(memory_space=pl.ANY),
                  pl.BlockSpec((1, d), fixed), pl.BlockSpec((1, d), fixed)],
        out_specs=pl.BlockSpec((tk, d), row),
        out_shape=jax.ShapeDtypeStruct((t, d), F32),
        scratch_shapes=[pltpu.VMEM((2, TOP_K * tk * TILE_ROWS, LANES), F32),
                        pltpu.VMEM((tk * TILE_ROWS, LANES), F32), pltpu.SemaphoreType.DMA((2,))],
        compiler_params=_cparams("arbitrary"),
        name="moe_combine_ln",
    )(dest_tiles, dest_tiles, xn, gate_rows, yb_tiles, g.reshape(1, d), b.reshape(1, d))


def _moe_ffn_ln(xn, x_tiles, logits, layer, w_in, b_in, w_out, b_out, g, b):
    t, d = xn.shape
    tm = MOE_TM
    top_logit, top_idx = lax.top_k(logits, TOP_K)
    gate = jax.nn.softmax(top_logit, axis=-1)
    flat_e = top_idx.reshape(-1).astype(jnp.int32)
    onehot = (flat_e[:, None] == jnp.arange(N_EXPERTS, dtype=jnp.int32)[None, :]).astype(jnp.int32)
    csum = jnp.cumsum(onehot, axis=0)
    rank = jnp.sum(csum * onehot, axis=1) - 1
    counts = csum[-1]
    blocks_per_e = (counts + tm - 1) // tm
    blk_end = jnp.cumsum(blocks_per_e)
    blk_start = blk_end - blocks_per_e
    row_start = jnp.cumsum(counts) - counts
    dest = blk_start[flat_e] * tm + rank
    n_blocks = (t * TOP_K) // tm + N_EXPERTS
    block_expert = jnp.minimum(
        jnp.sum((blk_end[None, :] <= jnp.arange(n_blocks, dtype=jnp.int32)[:, None]).astype(jnp.int32), axis=1),
        N_EXPERTS - 1)
    n_real = blk_end[-1:].astype(jnp.int32)
    tok_sorted = (jnp.argsort(flat_e, stable=True) // TOP_K).astype(jnp.int32)
    e_row = jnp.repeat(block_expert, tm)
    rank_row = jnp.arange(n_blocks * tm, dtype=jnp.int32) - jnp.repeat(blk_start[block_expert] * tm, tm)
    valid = rank_row < counts[e_row]
    src = jnp.clip(row_start[e_row] + rank_row, 0, t * TOP_K - 1)
    row_tok = jnp.where(valid, tok_sorted[src], 0)
    yb_tiles = _moe_experts(x_tiles, row_tok, block_expert, n_real, layer, w_in, b_in, w_out, b_out)
    return _moe_combine_ln(xn, gate, dest, yb_tiles, g, b)


def _softplus(x):
    return jnp.maximum(x, 0.0) + jnp.log(1.0 + jnp.exp(-jnp.abs(x)))


def _gelu_tanh(x):
    return 0.5 * x * (1.0 + jnp.tanh(0.7978845608028654 * (x + 0.044715 * (x * x * x))))


def _lru_kernel(xa_ref, ga_ref, cw_ref, cb_ref, wr_ref, br_ref, wi_ref, bi_ref, lam_ref, o_ref,
                tail_ref, h_ref, a_ref, u_ref):
    tc, bsz, w = xa_ref.shape

    @pl.when(pl.program_id(0) == 0)
    def _init():
        tail_ref[...] = jnp.zeros_like(tail_ref)
        h_ref[...] = jnp.zeros_like(h_ref)

    xa = xa_ref[...]
    ext = jnp.concatenate([tail_ref[...], xa], axis=0)
    xc = cb_ref[...] + ext[0:tc] * cw_ref[0]
    for tap in range(1, CONV_WIDTH):
        xc = xc + ext[tap:tap + tc] * cw_ref[tap]
    tail_ref[...] = xa[tc - (CONV_WIDTH - 1):]
    x2 = xc.reshape(tc * bsz, w)
    xb = x2.astype(BF16)
    r = jax.nn.sigmoid(_dot(xb, wr_ref[...]) + br_ref[...])
    ig = jax.nn.sigmoid(_dot(xb, wi_ref[...]) + bi_ref[...])
    log_a = (-LRU_C) * r * _softplus(-lam_ref[...])
    a_ref[...] = jnp.exp(log_a).reshape(tc, bsz, w)
    u_ref[...] = (jnp.sqrt(1.0 - jnp.exp(2.0 * log_a)) * (ig * x2)).reshape(tc, bsz, w)

    def step(t, h):
        h = a_ref[t] * h + u_ref[t]
        u_ref[t] = h
        return h

    h_ref[...] = lax.fori_loop(0, tc, step, h_ref[...], unroll=8)
    o_ref[...] = (u_ref[...] * _gelu_tanh(ga_ref[...])).astype(o_ref.dtype)


def _block_diag(w):
    h, i, j = w.shape
    return jnp.einsum('hij,hk->hikj', w, jnp.eye(h, dtype=w.dtype)).reshape(h * i, h * j)


def _rg_lru(proj, conv_w, conv_b, gate_r_w, gate_r_b, gate_i_w, gate_i_b, lru_lambda):
    seq, bsz, _ = proj.shape
    w = A_WIDTH
    tc = LRU_TC
    fixed2 = lambda i: (0, 0)
    fixed3 = lambda i: (0, 0, 0)
    return pl.pallas_call(
        _lru_kernel,
        grid=(seq // tc,),
        in_specs=[pl.BlockSpec((tc, bsz, w), lambda i: (i, 0, 0)),
                  pl.BlockSpec((tc, bsz, w), lambda i: (i, 0, 1)),
                  pl.BlockSpec((CONV_WIDTH, 1, w), fixed3), pl.BlockSpec((1, w), fixed2),
                  pl.BlockSpec((w, w), fixed2), pl.BlockSpec((1, w), fixed2),
                  pl.BlockSpec((w, w), fixed2), pl.BlockSpec((1, w), fixed2),
                  pl.BlockSpec((1, w), fixed2)],
        out_specs=pl.BlockSpec((tc, bsz, w), lambda i: (i, 0, 0)),
        out_shape=jax.ShapeDtypeStruct((seq, bsz, w), BF16),
        scratch_shapes=[pltpu.VMEM((CONV_WIDTH - 1, bsz, w), F32), pltpu.VMEM((bsz, w), F32),
                        pltpu.VMEM((tc, bsz, w), F32), pltpu.VMEM((tc, bsz, w), F32)],
        compiler_params=_cparams("arbitrary"),
        name="rg_lru",
    )(proj, proj, conv_w.reshape(CONV_WIDTH, 1, w), conv_b.reshape(1, w),
      _block_diag(gate_r_w).astype(BF16), gate_r_b.reshape(1, w),
      _block_diag(gate_i_w).astype(BF16), gate_i_b.reshape(1, w), lru_lambda.reshape(1, w))


def _s5_kernel(u_ref, wbr_ref, wbi_ref, ar_ref, ai_ref, ccr_ref, cci_ref, d_ref, gw_ref, gb_ref, o_ref,
               xr_ref, xi_ref, sr_ref, si_ref):
    tc, bsz, w = u_ref.shape
    nstate = xr_ref.shape[-1]
    sh = nstate // S5_HALF
    wh = w // S5_HALF

    @pl.when(pl.program_id(0) == 0)
    def _init():
        sr_ref[...] = jnp.zeros_like(sr_ref)
        si_ref[...] = jnp.zeros_like(si_ref)

    u2 = u_ref[...].reshape(tc * bsz, w)
    ub = u2.astype(BF16)
    for hf in range(S5_HALF):
        uh = ub[:, hf * wh:(hf + 1) * wh]
        xr_ref[:, :, hf * sh:(hf + 1) * sh] = _dot(uh, wbr_ref[hf]).reshape(tc, bsz, sh)
        xi_ref[:, :, hf * sh:(hf + 1) * sh] = _dot(uh, wbi_ref[hf]).reshape(tc, bsz, sh)

    for hf in range(S5_HALF):
        lo, hi = hf * sh, (hf + 1) * sh
        ar = jnp.broadcast_to(ar_ref[:, lo:hi], (bsz, sh))
        ai = jnp.broadcast_to(ai_ref[:, lo:hi], (bsz, sh))

        def step(t, carry, lo=lo, hi=hi, ar=ar, ai=ai):
            xr, xi = carry
            nxr = ar * xr - ai * xi + xr_ref[t, :, lo:hi]
            nxi = ar * xi + ai * xr + xi_ref[t, :, lo:hi]
            xr_ref[t, :, lo:hi] = nxr
            xi_ref[t, :, lo:hi] = nxi
            return nxr, nxi

        xr, xi = lax.fori_loop(0, tc, step, (sr_ref[:, lo:hi], si_ref[:, lo:hi]), unroll=4)
        sr_ref[:, lo:hi] = xr
        si_ref[:, lo:hi] = xi

    ys = []
    for hf in range(S5_HALF):
        lo, hi = hf * sh, (hf + 1) * sh
        xrh = xr_ref[:, :, lo:hi].reshape(tc * bsz, sh).astype(BF16)
        xih = xi_ref[:, :, lo:hi].reshape(tc * bsz, sh).astype(BF16)
        ys.append(_dot(xrh, ccr_ref[hf]) - _dot(xih, cci_ref[hf]))
    y = jnp.concatenate(ys, axis=1) + d_ref[...] * u2
    s = _gelu_tanh(y)
    yb = s * jax.nn.sigmoid(_dot(s.astype(BF16), gw_ref[...]) + gb_ref[...])
    o_ref[...] = yb.reshape(tc, bsz, w).astype(o_ref.dtype)


def _s5_glu(proj, lam_re, lam_im, log_step, b_re, b_im, c_re, c_im, d_skip, glu_w, glu_b):
    seq, bsz, _ = proj.shape
    w = B_WIDTH
    tc = S5_TC
    nstate = S5_GROUPS * S5_STATE
    gh = S5_GROUPS // S5_HALF
    lr = jnp.minimum(lam_re, -1e-4)
    li = lam_im
    step = jnp.exp(log_step)[:, None]
    mag = jnp.exp(lr * step)
    ar = mag * jnp.cos(li * step)
    ai = mag * jnp.sin(li * step)
    inv = 1.0 / (lr * lr + li * li)
    zr = ((ar - 1.0) * lr + ai * li) * inv
    zi = (ai * lr - (ar - 1.0) * li) * inv
    bbr = zr[..., None] * b_re - zi[..., None] * b_im
    bbi = zr[..., None] * b_im + zi[..., None] * b_re
    eye = jnp.eye(gh, dtype=F32)

    def expand_in(bb):
        bb = bb.reshape(S5_HALF, gh, S5_STATE, S5_GROUP_CH)
        return jnp.einsum('fgph,gk->fghkp', bb, eye).reshape(
            S5_HALF, gh * S5_GROUP_CH, gh * S5_STATE).astype(BF16)

    def expand_out(c):
        c = c.reshape(S5_HALF, gh, S5_GROUP_CH, S5_STATE)
        return jnp.einsum('fghp,gk->fgpkh', c, eye).reshape(
            S5_HALF, gh * S5_STATE, gh * S5_GROUP_CH).astype(BF16)

    fixed2 = lambda i: (0, 0)
    fixed3 = lambda i: (0, 0, 0)
    return pl.pallas_call(
        _s5_kernel,
        grid=(seq // tc,),
        in_specs=[pl.BlockSpec((tc, bsz, w), lambda i: (i, 0, 2)),
                  pl.BlockSpec((S5_HALF, w // S5_HALF, nstate // S5_HALF), fixed3),
                  pl.BlockSpec((S5_HALF, w // S5_HALF, nstate // S5_HALF), fixed3),
                  pl.BlockSpec((1, nstate), fixed2), pl.BlockSpec((1, nstate), fixed2),
                  pl.BlockSpec((S5_HALF, nstate // S5_HALF, w // S5_HALF), fixed3),
                  pl.BlockSpec((S5_HALF, nstate // S5_HALF, w // S5_HALF), fixed3),
                  pl.BlockSpec((1, w), fixed2), pl.BlockSpec((w, w), fixed2), pl.BlockSpec((1, w), fixed2)],
        out_specs=pl.BlockSpec((tc, bsz, w), lambda i: (i, 0, 0)),
        out_shape=jax.ShapeDtypeStruct((seq, bsz, w), BF16),
        scratch_shapes=[pltpu.VMEM((tc, bsz, nstate), F32), pltpu.VMEM((tc, bsz, nstate), F32),
                        pltpu.VMEM((bsz, nstate), F32), pltpu.VMEM((bsz, nstate), F32)],
        compiler_params=_cparams("arbitrary"),
        name="s5_glu",
    )(proj, expand_in(bbr), expand_in(bbi), ar.reshape(1, nstate), ai.reshape(1, nstate),
      expand_out(c_re), expand_out(c_im), d_skip.reshape(1, w), glu_w.astype(BF16), glu_b.reshape(1, w))


def _cd_prep_kernel(xc_ref, q_ref, k_ref, pos_ref, cw_ref, cb_ref, wq_ref, wk_ref, wv_ref, wif_ref,
                    bif_ref, invf_ref, mq_ref, mk_ref, mv_ref, gt_ref, qd_ref, kd_ref, tail_ref):
    tl, w = xc_ref.shape
    pad = tail_ref.shape[0]

    @pl.when(pl.program_id(1) == 0)
    def _init():
        tail_ref[...] = jnp.zeros_like(tail_ref)

    xc = xc_ref[...]
    ext = jnp.concatenate([tail_ref[...], xc], axis=0)
    conv = cb_ref[...]
    for tap in range(CONV_WIDTH):
        off = pad - (CONV_WIDTH - 1) + tap
        conv = conv + ext[off:off + tl] * cw_ref[tap]
    tail_ref[...] = xc[tl - pad:]
    xconv = (conv * jax.nn.sigmoid(conv)).astype(BF16)
    q = _dot(xconv, wq_ref[...]).astype(BF16)
    k = (_dot(xconv, wk_ref[...]) * (MLSTM_HEAD_DIM ** -0.5)).astype(BF16)
    v = _dot(xc.astype(BF16), wv_ref[...]).astype(BF16)
    mq_ref[...] = q
    mk_ref[...] = k
    mv_ref[...] = v
    gt_ref[...] = (_dot(q, wif_ref[0]) + _dot(k, wif_ref[1]) + _dot(v, wif_ref[2]) + bif_ref[...])

    ang = pos_ref[...].astype(F32) * invf_ref[...]
    cos = jnp.cos(ang)
    sin = jnp.sin(ang)
    lane = lax.broadcasted_iota(jnp.int32, (1, LANES), 1)
    half = ROPE_DIMS // 2
    sin_lo = jnp.where(lane < half, -sin, 0.0)
    sin_hi = jnp.where((lane >= half) & (lane < ROPE_DIMS), sin, 0.0)
    for src, dst in ((q_ref, qd_ref), (k_ref, kd_ref)):
        for h in range(DSW_HEADS):
            sl = slice(h * DSW_HEAD_DIM, (h + 1) * DSW_HEAD_DIM)
            t = src[:, sl]
            rot = (t * cos + pltpu.roll(t, LANES - half, axis=1) * sin_lo
                   + pltpu.roll(t, half, axis=1) * sin_hi)
            dst[:, sl] = rot


def _cd_prep(proj, positions, conv_w, conv_b, w_q, w_k, w_v, w_if, b_if):
    bsz, seq, _ = proj.shape
    w = C_WIDTH
    tl = PROJ_TL
    half = ROPE_DIMS // 2
    inv_freq = ROPE_THETA ** (-jnp.arange(half, dtype=F32) / half)
    invf = jnp.zeros((1, LANES), F32).at[0, :half].set(inv_freq).at[0, half:ROPE_DIMS].set(inv_freq)
    wif = jnp.zeros((3, w, LANES), F32).at[:, :, :2 * MLSTM_HEADS].set(w_if.reshape(3, w, 2 * MLSTM_HEADS))
    bif = jnp.zeros((1, LANES), F32).at[0, :2 * MLSTM_HEADS].set(b_if)
    col = lambda c: (lambda b, i: (b, i, c))
    fixed2 = lambda b, i: (0, 0)
    fixed3 = lambda b, i: (0, 0, 0)
    tok_bf = jax.ShapeDtypeStruct((bsz, seq, w), BF16)
    return pl.pallas_call(
        _cd_prep_kernel,
        grid=(bsz, seq // tl),
        in_specs=[pl.BlockSpec((None, tl, w), col(0)), pl.BlockSpec((None, tl, w), col(2)),
                  pl.BlockSpec((None, tl, w), col(3)), pl.BlockSpec((None, tl, 1), col(0)),
                  pl.BlockSpec((CONV_WIDTH, 1, w), fixed3), pl.BlockSpec((1, w), fixed2),
                  pl.BlockSpec((w, w), fixed2), pl.BlockSpec((w, w), fixed2), pl.BlockSpec((w, w), fixed2),
                  pl.BlockSpec((3, w, LANES), fixed3), pl.BlockSpec((1, LANES), fixed2),
                  pl.BlockSpec((1, LANES), fixed2)],
        out_specs=[pl.BlockSpec((None, tl, w), col(0))] * 3 + [pl.BlockSpec((None, tl, LANES), col(0))]
                  + [pl.BlockSpec((None, tl, w), col(0))] * 2,
        out_shape=[tok_bf, tok_bf, tok_bf, jax.ShapeDtypeStruct((bsz, seq, LANES), F32),
                   jax.ShapeDtypeStruct((bsz, seq, w), F32), jax.ShapeDtypeStruct((bsz, seq, w), F32)],
        scratch_shapes=[pltpu.VMEM((8, w), F32)],
        compiler_params=_cparams("arbitrary", "arbitrary"),
        name="cd_prep",
    )(proj, proj, proj, positions.reshape(bsz, seq, 1), conv_w.reshape(CONV_WIDTH, 1, w),
      conv_b.reshape(1, w), _block_diag(w_q).astype(BF16), _block_diag(w_k).astype(BF16),
      _block_diag(w_v).astype(BF16), wif.astype(BF16), bif, invf)


def _cumsum_rows(x):
    n = x.shape[0]
    row = lax.broadcasted_iota(jnp.int32, x.shape, 0)
    shift = 1
    while shift < n:
        x = x + jnp.where(row >= shift, pltpu.roll(x, shift, axis=0), 0.0)
        shift *= 2
    return x


def _mlstm_kernel(q_ref, k_ref, v_ref, g_ref, op_ref, o_ref, c_ref, n_ref, m_ref):
    cs = q_ref.shape[0]
    dh = MLSTM_HEAD_DIM

    @pl.when(pl.program_id(1) == 0)
    def _init():
        c_ref[...] = jnp.zeros_like(c_ref)
        n_ref[...] = jnp.zeros_like(n_ref)
        m_ref[...] = jnp.full_like(m_ref, NEG_INF)

    gates = g_ref[...]
    cum = _cumsum_rows(jax.nn.log_sigmoid(gates))
    gates_t = gates.T
    cum_t = cum.T
    causal = (lax.broadcasted_iota(jnp.int32, (cs, cs), 0) >= lax.broadcasted_iota(jnp.int32, (cs, cs), 1))
    for h in range(MLSTM_HEADS):
        sl = slice(h * dh, (h + 1) * dh)
        fcol = MLSTM_HEADS + h
        li_c, li_r = gates[:, h:h + 1], gates_t[h:h + 1, :]
        cum_c, cum_r = cum[:, fcol:fcol + 1], cum_t[fcol:fcol + 1, :]
        m_prev = m_ref[h][:, 0:1]
        q, k, v = q_ref[:, sl], k_ref[:, sl], v_ref[:, sl]
        log_d = jnp.where(causal, cum_c - cum_r + li_r, NEG_INF)
        log_inter = cum_c + m_prev
        m_s = jnp.maximum(jnp.max(log_d, axis=1, keepdims=True), log_inter)
        s = _dot_nt(q, k) * jnp.exp(log_d - m_s)
        inter = jnp.exp(log_inter - m_s)
        c_prev = c_ref[h]
        n_prev = n_ref[h]
        num = _dot(s.astype(BF16), v) + inter * _dot(q, c_prev.astype(BF16))
        den = (jnp.sum(s, axis=1, keepdims=True)
               + inter * jnp.sum(q.astype(F32) * n_prev, axis=1, keepdims=True))
        hh = num / jnp.maximum(jnp.abs(den), jnp.exp(-m_s))
        o_ref[:, sl] = (jax.nn.sigmoid(op_ref[:, sl]) * hh).astype(o_ref.dtype)
        chunk_f = cum_c[cs - 1:cs, :]
        to_end = chunk_f - cum_c + li_c
        m_c = jnp.max(to_end, axis=0, keepdims=True)
        m_new = jnp.maximum(chunk_f + m_prev, m_c)
        s_old = jnp.exp(chunk_f + m_prev - m_new)
        s_new = jnp.exp(m_c - m_new)
        kw = k.astype(F32) * jnp.exp(to_end - m_c)
        c_ref[h] = s_old * c_prev + s_new * _dot(kw.T.astype(BF16), v)
        n_ref[h] = s_old * n_prev + s_new * jnp.sum(kw, axis=0, keepdims=True)
        m_ref[h] = jnp.broadcast_to(m_new, (1, LANES))


def _mlstm(mq, mk, mv, gates, proj):
    bsz, seq, w = mq.shape
    cs = MLSTM_CS
    tok = lambda b, i: (b, i, 0)
    return pl.pallas_call(
        _mlstm_kernel,
        grid=(bsz, seq // cs),
        in_specs=[pl.BlockSpec((None, cs, w), tok)] * 3
                 + [pl.BlockSpec((None, cs, LANES), tok), pl.BlockSpec((None, cs, w), lambda b, i: (b, i, 1))],
        out_specs=pl.BlockSpec((None, cs, w), tok),
        out_shape=jax.ShapeDtypeStruct((bsz, seq, w), BF16),
        scratch_shapes=[pltpu.VMEM((MLSTM_HEADS, MLSTM_HEAD_DIM, MLSTM_HEAD_DIM), F32),
                        pltpu.VMEM((MLSTM_HEADS, 1, MLSTM_HEAD_DIM), F32),
                        pltpu.VMEM((MLSTM_HEADS, 1, LANES), F32)],
        compiler_params=_cparams("arbitrary", "arbitrary"),
        name="mlstm",
    )(mq, mk, mv, gates, proj)


def _dsw_kernel(q_ref, k_ref, v_ref, y_ref, o_scr, lse_scr):
    seq, dh = q_ref.shape
    blk = DSW_BLOCK
    qi = lax.broadcasted_iota(jnp.int32, (blk, 2 * blk), 0)
    ki = lax.broadcasted_iota(jnp.int32, (blk, 2 * blk), 1)
    dist = qi + blk - ki
    scale = dh ** -0.5
    units = seq // blk
    for g, (window, dil) in enumerate(DSW_CONFIGS):
        nsub = seq // dil // blk
        band = (dist >= 0) & (dist <= window // dil)

        def unit(u, carry, g=g, dil=dil, nsub=nsub, band=band):
            r = u // nsub
            sb = u % nsub
            cur = pl.ds(r + sb * (blk * dil), blk, stride=dil)
            prev = pl.ds(r + jnp.maximum(sb - 1, 0) * (blk * dil), blk, stride=dil)
            kk = jnp.concatenate([k_ref[prev, :], k_ref[cur, :]], axis=0).astype(BF16)
            vv = jnp.concatenate([v_ref[prev, :], v_ref[cur, :]], axis=0).astype(BF16)
            s = _dot_nt(q_ref[cur, :].astype(BF16), kk) * scale
            s = jnp.where(band & ((ki >= blk) | (sb > 0)), s, NEG_INF)
            m = jnp.max(s, axis=1, keepdims=True)
            p = jnp.exp(s - m)
            l = jnp.sum(p, axis=1, keepdims=True)
            o_scr[g, cur, :] = _dot(p.astype(BF16), vv) / l
            lse_scr[g, cur, :] = m + jnp.log(l)
            return carry

        lax.fori_loop(0, units, unit, 0, unroll=4)

    lses = [lse_scr[g] for g in range(len(DSW_CONFIGS))]
    mx = functools.reduce(jnp.maximum, lses)
    ws = [jnp.exp(l - mx) for l in lses]
    acc = sum(ws[g] * o_scr[g] for g in range(len(DSW_CONFIGS)))
    y_ref[...] = (acc / sum(ws)).astype(y_ref.dtype)


def _dsw_attention(qd, kd, proj):
    bsz, seq, w = qd.shape
    dh = DSW_HEAD_DIM
    v_col = (proj.shape[-1] - w) // dh
    head = lambda b, h: (b, 0, h)
    return pl.pallas_call(
        _dsw_kernel,
        grid=(bsz, DSW_HEADS),
        in_specs=[pl.BlockSpec((None, seq, dh), head), pl.BlockSpec((None, seq, dh), head),
                  pl.BlockSpec((None, seq, dh), lambda b, h: (b, 0, v_col + h))],
        out_specs=pl.BlockSpec((None, seq, dh), head),
        out_shape=jax.ShapeDtypeStruct((bsz, seq, w), BF16),
        scratch_shapes=[pltpu.VMEM((len(DSW_CONFIGS), seq, dh), F32),
                        pltpu.VMEM((len(DSW_CONFIGS), seq, 1), F32)],
        compiler_params=_cparams("arbitrary", "arbitrary"),
        name="dsw_attention",
    )(qd, kd, proj)


def kernel(x, positions, ab_w_in, ab_conv_w, ab_conv_b, ab_gate_r_w, ab_gate_r_b, ab_gate_i_w, ab_gate_i_b, ab_lru_lambda, ab_s5_lambda_re, ab_s5_lambda_im, ab_s5_log_step, ab_s5_b_re, ab_s5_b_im, ab_s5_c_re, ab_s5_c_im, ab_s5_d, ab_glu_w, ab_glu_b, ab_w_out, cd_w_in, cd_conv_w, cd_conv_b, cd_w_q, cd_w_k, cd_w_v, cd_w_if, cd_b_if, cd_w_out, ln_mix_g, ln_mix_b, ln_ffn_g, ln_ffn_b, moe_router_w, moe_router_b, moe_w_in, moe_b_in, moe_w_out, moe_b_out):
    bsz, seq, d = x.shape
    t = bsz * seq
    for layer in range(DEPTH):
        j = layer // 2
        if layer % 2 == 0:
            proj = _inproj_time_major(x, ab_w_in[j].astype(BF16))
            ya = _rg_lru(proj, ab_conv_w[j], ab_conv_b[j], ab_gate_r_w[j], ab_gate_r_b[j],
                         ab_gate_i_w[j], ab_gate_i_b[j], ab_lru_lambda[j])
            yb = _s5_glu(proj, ab_s5_lambda_re[j], ab_s5_lambda_im[j], ab_s5_log_step[j], ab_s5_b_re[j],
                         ab_s5_b_im[j], ab_s5_c_re[j], ab_s5_c_im[j], ab_s5_d[j], ab_glu_w[j], ab_glu_b[j])
            time_major, w_out = True, ab_w_out[j]
        else:
            proj = _inproj(x, cd_w_in[j].astype(BF16))
            mq, mk, mv, gates, qd, kd = _cd_prep(proj, positions, cd_conv_w[j], cd_conv_b[j], cd_w_q[j],
                                                 cd_w_k[j], cd_w_v[j], cd_w_if[j], cd_b_if[j])
            ya = _mlstm(mq, mk, mv, gates, proj)
            yb = _dsw_attention(qd, kd, proj)
            time_major, w_out = False, cd_w_out[j]
        xn, x_tiles, logits = _outproj_ln_router(ya, yb, time_major, w_out.astype(BF16), x, ln_mix_g[layer],
                                                 ln_mix_b[layer], moe_router_w[layer], moe_router_b[layer])
        x = _moe_ffn_ln(xn.reshape(t, d), x_tiles, logits.reshape(t, N_EXPERTS), layer, moe_w_in, moe_b_in,
                        moe_w_out, moe_b_out, ln_ffn_g[layer], ln_ffn_b[layer]).reshape(bsz, seq, d)
    return x
```

```python
import functools

import jax
import jax.numpy as jnp
from jax import lax
from jax.experimental import pallas as pl
from jax.experimental.pallas import tpu as pltpu

F32 = jnp.float32
BF16 = jnp.bfloat16

D_MODEL = 1024
DEPTH = 2
A_WIDTH = 512
B_WIDTH = 512
C_WIDTH = 512
D_WIDTH = 512
LRU_HEADS = 8
LRU_HEAD_DIM = A_WIDTH // LRU_HEADS
LRU_C = 8.0
CONV_WIDTH = 4
S5_GROUP_CH = 16
S5_GROUPS = B_WIDTH // S5_GROUP_CH
S5_STATE = 64
MLSTM_HEADS = 4
MLSTM_HEAD_DIM = C_WIDTH // MLSTM_HEADS
DSW_HEADS = 4
DSW_HEAD_DIM = D_WIDTH // DSW_HEADS
DSW_CONFIGS = ((128, 1), (512, 4), (2048, 16))
DSW_BLOCK = 128
ROPE_THETA = 500000.0
ROPE_DIMS = DSW_HEAD_DIM // 4
N_EXPERTS = 32
TOP_K = 4
D_EXPERT = D_MODEL
SWIGLU_LIMIT = 7.0
SWIGLU_ALPHA = 1.702
DEEPNORM_ALPHA = (2 * DEPTH) ** 0.25
LN_EPS = 1e-5

VMEM_LIMIT_BYTES = 56 * 1024 * 1024
LANES = 128
TILE_ROWS = D_MODEL // LANES
MOE_TM = 512
MOE_TK = 256
PROJ_TL = 512
LRU_TC = 128
S5_TC = 64
S5_HALF = 2
MLSTM_CS = 256
NEG_INF = float("-inf")


def _cparams(*sem):
    return pltpu.CompilerParams(dimension_semantics=sem, vmem_limit_bytes=VMEM_LIMIT_BYTES)


def _dot(a, b):
    return jnp.dot(a, b, preferred_element_type=F32)


def _dot_nt(a, b):
    return lax.dot_general(a, b, (((1,), (1,)), ((), ())), preferred_element_type=F32)


def _matmul_kernel(x_ref, w_ref, o_ref):
    o_ref[...] = _dot(x_ref[...].astype(BF16), w_ref[...])


def _inproj_time_major(x3, w_bf16):
    bsz, seq, k = x3.shape
    n = w_bf16.shape[1]
    out = pl.pallas_call(
        _matmul_kernel,
        grid=(bsz, seq // PROJ_TL),
        in_specs=[pl.BlockSpec((None, PROJ_TL, k), lambda b, i: (b, i, 0)),
                  pl.BlockSpec((k, n), lambda b, i: (0, 0))],
        out_specs=pl.BlockSpec((PROJ_TL, n), lambda b, i: (i, b)),
        out_shape=jax.ShapeDtypeStruct((seq, bsz * n), F32),
        compiler_params=_cparams("arbitrary", "arbitrary"),
        name="in_proj_time_major",
    )(x3, w_bf16)
    return out.reshape(seq, bsz, n)


def _inproj(x3, w_bf16):
    bsz, seq, k = x3.shape
    n = w_bf16.shape[1]
    return pl.pallas_call(
        _matmul_kernel,
        grid=(bsz, seq // PROJ_TL),
        in_specs=[pl.BlockSpec((None, PROJ_TL, k), lambda b, i: (b, i, 0)),
                  pl.BlockSpec((k, n), lambda b, i: (0, 0))],
        out_specs=pl.BlockSpec((None, PROJ_TL, n), lambda b, i: (b, i, 0)),
        out_shape=jax.ShapeDtypeStruct((bsz, seq, n), F32),
        compiler_params=_cparams("arbitrary", "arbitrary"),
        name="in_proj",
    )(x3, w_bf16)


def _layer_norm_rows(z, g, b):
    mu = jnp.mean(z, axis=-1, keepdims=True)
    zc = z - mu
    var = jnp.mean(zc * zc, axis=-1, keepdims=True)
    return zc * lax.rsqrt(var + LN_EPS) * g + b


def _outproj_ln_router_kernel(ya_ref, yb_ref, wa_ref, wb_ref, x_ref, g_ref, b_ref, rwh_ref, rwl_ref, rb_ref,
                              tri_ref, xn_ref, xt_ref, rt_ref, cnt_ref):
    @pl.when((pl.program_id(0) == 0) & (pl.program_id(1) == 0))
    def _init():
        cnt_ref[...] = jnp.zeros_like(cnt_ref)

    mix = _dot(ya_ref[...], wa_ref[...]) + _dot(yb_ref[...], wb_ref[...])
    xn = _layer_norm_rows(DEEPNORM_ALPHA * x_ref[...] + mix, g_ref[...], b_ref[...])
    xn_ref[...] = xn
    _store_token_tiles(xt_ref, xn)
    xh = xn.astype(BF16)
    xl = (xn - xh.astype(F32)).astype(BF16)
    lg = (_dot(xh, rwh_ref[...]) + _dot(xl, rwh_ref[...]) + _dot(xh, rwl_ref[...])) + rb_ref[...]
    lane = lax.broadcasted_iota(jnp.int32, lg.shape, 1).astype(F32)
    tops, ids, hots = [], [], []
    for _ in range(TOP_K):
        m = jnp.max(lg, axis=1, keepdims=True)
        idx = jnp.min(jnp.where(lg == m, lane, float(LANES)), axis=1, keepdims=True)
        hot = lane == idx
        lg = jnp.where(hot, NEG_INF, lg)
        tops.append(m)
        ids.append(idx)
        hots.append(hot)
    exps = [jnp.exp(m - tops[0]) for m in tops]
    denom = functools.reduce(jnp.add, exps)
    chosen = functools.reduce(jnp.add, [h.astype(F32) for h in hots])
    before = _dot(tri_ref[...], chosen.astype(BF16)) + cnt_ref[...]
    route = jnp.zeros(lg.shape, F32)
    for k in range(TOP_K):
        rank = jnp.sum(jnp.where(hots[k], before, 0.0), axis=1, keepdims=True)
        route = jnp.where(lane == float(k), ids[k], route)
        route = jnp.where(lane == float(TOP_K + k), rank, route)
        route = jnp.where(lane == float(2 * TOP_K + k), exps[k] / denom, route)
    rt_ref[...] = route
    cnt_ref[...] = cnt_ref[...] + jnp.sum(chosen, axis=0, keepdims=True)


def _outproj_ln_router(ya, yb, time_major, w_bf16, x3, g, b, rw, rb):
    bsz, seq, d = x3.shape
    wa, wb = w_bf16[:ya.shape[-1]], w_bf16[ya.shape[-1]:]
    ne = LANES
    rw = jnp.zeros((d, ne), F32).at[:, :N_EXPERTS].set(rw)
    rb = jnp.full((ne,), NEG_INF, F32).at[:N_EXPERTS].set(rb)
    rw_hi = rw.astype(BF16)
    rw_lo = (rw - rw_hi.astype(F32)).astype(BF16)
    tl = PROJ_TL
    tri = jnp.tril(jnp.ones((tl, tl), BF16), -1)
    if time_major:
        ya, yb = ya.reshape(seq, -1), yb.reshape(seq, -1)
        y_specs = [pl.BlockSpec((tl, wa.shape[0]), lambda bi, i: (i, bi)),
                   pl.BlockSpec((tl, wb.shape[0]), lambda bi, i: (i, bi))]
    else:
        y_specs = [pl.BlockSpec((None, tl, wa.shape[0]), lambda bi, i: (bi, i, 0)),
                   pl.BlockSpec((None, tl, wb.shape[0]), lambda bi, i: (bi, i, 0))]
    fixed = lambda bi, i: (0, 0)
    tok = lambda bi, i: (bi, i, 0)
    return pl.pallas_call(
        _outproj_ln_router_kernel,
        grid=(bsz, seq // tl),
        in_specs=y_specs + [pl.BlockSpec(wa.shape, fixed), pl.BlockSpec(wb.shape, fixed),
                            pl.BlockSpec((None, tl, d), tok),
                            pl.BlockSpec((1, d), fixed), pl.BlockSpec((1, d), fixed),
                            pl.BlockSpec((d, ne), fixed), pl.BlockSpec((d, ne), fixed),
                            pl.BlockSpec((1, ne), fixed), pl.BlockSpec((tl, tl), fixed)],
        out_specs=[pl.BlockSpec((None, tl, d), tok),
                   pl.BlockSpec((tl * TILE_ROWS, LANES), lambda bi, i: (bi * (seq // tl) + i, 0)),
                   pl.BlockSpec((None, tl, ne), tok), pl.BlockSpec((1, ne), fixed)],
        out_shape=[jax.ShapeDtypeStruct((bsz, seq, d), F32),
                   jax.ShapeDtypeStruct((bsz * seq * TILE_ROWS, LANES), F32),
                   jax.ShapeDtypeStruct((bsz, seq, ne), F32), jax.ShapeDtypeStruct((1, ne), F32)],
        compiler_params=_cparams("arbitrary", "arbitrary"),
        name="out_proj_ln_router",
    )(ya, yb, wa, wb, x3, g.reshape(1, d), b.reshape(1, d), rw_hi, rw_lo, rb.reshape(1, ne), tri)


def _store_token_tiles(dst_ref, x):
    n = x.shape[0]
    for j in range(TILE_ROWS):
        dst_ref[pl.ds(j, n, stride=TILE_ROWS), :] = x[:, j * LANES:(j + 1) * LANES]


def _load_token_tiles(src_ref, n):
    return jnp.concatenate([src_ref[pl.ds(j, n, stride=TILE_ROWS), :] for j in range(TILE_ROWS)], axis=1)


def _gather_token_tiles(src_hbm, idx_ref, n, dst, sem):
    def body(g, carry):
        base = pl.multiple_of(g * (TILE_ROWS * TILE_ROWS), TILE_ROWS * TILE_ROWS)
        for u in range(TILE_ROWS):
            src_row = pl.multiple_of(idx_ref[0, g * TILE_ROWS + u] * TILE_ROWS, TILE_ROWS)
            pltpu.make_async_copy(src_hbm.at[pl.ds(src_row, TILE_ROWS)],
                                  dst.at[pl.ds(base + u * TILE_ROWS, TILE_ROWS)], sem).start()
        return carry
    lax.fori_loop(0, n // TILE_ROWS, body, 0, unroll=4)


def _moe_kernel(be_ref, nreal_ref, rt_ref, rtn_ref, x_hbm, rg_ref, win_ref, bin_ref, wout_ref, bout_ref, o_ref,
                win_bf, wout_bf, xbuf, sem):
    i = pl.program_id(0)
    n_real = nreal_ref[0]
    e = be_ref[i]
    prev = be_ref[jnp.maximum(i - 1, 0)]
    tm = xbuf.shape[1] // TILE_ROWS
    slot = lax.rem(i, 2)

    @pl.when(i == 0)
    def _first_gather():
        _gather_token_tiles(x_hbm, rt_ref, tm, xbuf.at[0], sem.at[0])

    @pl.when(i + 1 < n_real)
    def _next_gather():
        _gather_token_tiles(x_hbm, rtn_ref, tm, xbuf.at[1 - slot], sem.at[1 - slot])

    @pl.when((i == 0) | (e != prev))
    def _load_expert():
        win_bf[...] = win_ref[...].astype(BF16)
        wout_bf[...] = wout_ref[...].astype(BF16)

    @pl.when((i < n_real) | (i == 0))
    def _compute():
        pltpu.make_async_copy(xbuf.at[slot], xbuf.at[slot], sem.at[slot]).wait()
        x = _load_token_tiles(xbuf.at[slot], tm).astype(BF16)
        h = _dot(x, win_bf[...]) + bin_ref[...]
        g = jnp.minimum(h[:, :D_EXPERT], SWIGLU_LIMIT)
        lin = jnp.clip(h[:, D_EXPERT:], -SWIGLU_LIMIT, SWIGLU_LIMIT)
        y = g * jax.nn.sigmoid(SWIGLU_ALPHA * g) * (lin + 1.0)
        _store_token_tiles(o_ref, (_dot(y.astype(BF16), wout_bf[...]) + bout_ref[...]) * rg_ref[...])

    @pl.when((i >= n_real) & (i > 0))
    def _unused_block():
        o_ref[...] = jnp.zeros_like(o_ref)


def _moe_experts(x_tiles, row_tok, row_gate, block_expert, n_real, layer, w_in, b_in, w_out, b_out):
    d = D_MODEL
    tm = MOE_TM
    n_blocks = row_tok.shape[0] // tm
    nl, ne, _, dh2 = w_in.shape
    rt = row_tok.reshape(n_blocks, 1, tm)
    smem_rows = lambda imap: pl.BlockSpec((None, 1, tm), imap, memory_space=pltpu.SMEM)
    grid_spec = pltpu.PrefetchScalarGridSpec(
        num_scalar_prefetch=2,
        grid=(n_blocks,),
        in_specs=[
            smem_rows(lambda i, be, nr: (i, 0, 0)),
            smem_rows(lambda i, be, nr: (jnp.minimum(i + 1, n_blocks - 1), 0, 0)),
            pl.BlockSpec(memory_space=pl.ANY),
            pl.BlockSpec((tm, 1), lambda i, be, nr: (i, 0)),
            pl.BlockSpec((None, None, d, dh2), lambda i, be, nr: (layer, be[i], 0, 0)),
            pl.BlockSpec((None, None, 1, dh2), lambda i, be, nr: (layer, be[i], 0, 0)),
            pl.BlockSpec((None, None, D_EXPERT, d), lambda i, be, nr: (layer, be[i], 0, 0)),
            pl.BlockSpec((None, None, 1, d), lambda i, be, nr: (layer, be[i], 0, 0)),
        ],
        out_specs=pl.BlockSpec((tm * TILE_ROWS, LANES), lambda i, be, nr: (i, 0)),
        scratch_shapes=[pltpu.VMEM((d, dh2), BF16), pltpu.VMEM((D_EXPERT, d), BF16),
                        pltpu.VMEM((2, tm * TILE_ROWS, LANES), F32), pltpu.SemaphoreType.DMA((2,))],
    )
    return pl.pallas_call(
        _moe_kernel,
        grid_spec=grid_spec,
        out_shape=jax.ShapeDtypeStruct((n_blocks * tm * TILE_ROWS, LANES), F32),
        compiler_params=_cparams("arbitrary"),
        name="moe_experts",
    )(block_expert, n_real, rt, rt, x_tiles, row_gate.reshape(n_blocks * tm, 1), w_in,
      b_in.reshape(nl, ne, 1, dh2), w_out, b_out.reshape(nl, ne, 1, d))


def _moe_combine_kernel(dc_ref, dn_ref, x_ref, yb_hbm, g_ref, b_ref, o_ref, buf, ffn_ref, sem):
    i = pl.program_id(0)
    n = pl.num_programs(0)
    tk = x_ref.shape[0]
    rows = tk * TILE_ROWS
    slot = lax.rem(i, 2)

    @pl.when(i == 0)
    def _first_gather():
        _gather_token_tiles(yb_hbm, dc_ref, TOP_K * tk, buf.at[0], sem.at[0])

    @pl.when(i + 1 < n)
    def _next_gather():
        _gather_token_tiles(yb_hbm, dn_ref, TOP_K * tk, buf.at[1 - slot], sem.at[1 - slot])

    pltpu.make_async_copy(buf.at[slot], buf.at[slot], sem.at[slot]).wait()
    ffn = buf[slot, 0:rows]
    for k in range(1, TOP_K):
        ffn = ffn + buf[slot, k * rows:(k + 1) * rows]
    ffn_ref[...] = ffn
    o_ref[...] = _layer_norm_rows(DEEPNORM_ALPHA * x_ref[...] + _load_token_tiles(ffn_ref, tk),
                                  g_ref[...], b_ref[...])


def _moe_combine_ln(xn, dest, yb_tiles, g, b):
    t, d = xn.shape
    tk = MOE_TK
    n = t // tk
    dest_tiles = dest.reshape(n, tk, TOP_K).transpose(0, 2, 1).reshape(n, 1, TOP_K * tk)
    smem_rows = lambda imap: pl.BlockSpec((None, 1, TOP_K * tk), imap, memory_space=pltpu.SMEM)
    row = lambda i: (i, 0)
    fixed = lambda i: (0, 0)
    return pl.pallas_call(
        _moe_combine_kernel,
        grid=(n,),
        in_specs=[smem_rows(lambda i: (i, 0, 0)), smem_rows(lambda i: (jnp.minimum(i + 1, n - 1), 0, 0)),
                  pl.BlockSpec((tk, d), row), pl.BlockSpec(memory_space=pl.ANY),
                  pl.BlockSpec((1, d), fixed), pl.BlockSpec((1, d), fixed)],
        out_specs=pl.BlockSpec((tk, d), row),
        out_shape=jax.ShapeDtypeStruct((t, d), F32),
        scratch_shapes=[pltpu.VMEM((2, TOP_K * tk * TILE_ROWS, LANES), F32),
                        pltpu.VMEM((tk * TILE_ROWS, LANES), F32), pltpu.SemaphoreType.DMA((2,))],
        compiler_params=_cparams("arbitrary"),
        name="moe_combine_ln",
    )(dest_tiles, dest_tiles, xn, yb_tiles, g.reshape(1, d), b.reshape(1, d))


def _moe_ffn_ln(xn, x_tiles, route, counts, layer, w_in, b_in, w_out, b_out, g, b):
    t, d = xn.shape
    tm = MOE_TM
    flat_e = route[:, :TOP_K].astype(jnp.int32).reshape(-1)
    rank = route[:, TOP_K:2 * TOP_K].astype(jnp.int32).reshape(-1)
    gate = route[:, 2 * TOP_K:3 * TOP_K].reshape(-1)
    counts = counts[0, :N_EXPERTS].astype(jnp.int32)
    blocks_per_e = (counts + tm - 1) // tm
    blk_end = jnp.cumsum(blocks_per_e)
    blk_start = blk_end - blocks_per_e
    row_start = jnp.cumsum(counts) - counts
    dest = blk_start[flat_e] * tm + rank
    n_blocks = (t * TOP_K) // tm + N_EXPERTS
    block_expert = jnp.minimum(
        jnp.sum((blk_end[None, :] <= jnp.arange(n_blocks, dtype=jnp.int32)[:, None]).astype(jnp.int32), axis=1),
        N_EXPERTS - 1)
    n_real = blk_end[-1:].astype(jnp.int32)
    order = jnp.argsort(flat_e, stable=True).astype(jnp.int32)
    e_row = jnp.repeat(block_expert, tm)
    rank_row = jnp.arange(n_blocks * tm, dtype=jnp.int32) - jnp.repeat(blk_start[block_expert] * tm, tm)
    valid = rank_row < counts[e_row]
    pair = order[jnp.clip(row_start[e_row] + rank_row, 0, t * TOP_K - 1)]
    row_tok = jnp.where(valid, pair // TOP_K, 0)
    row_gate = jnp.where(valid, gate[pair], 0.0)
    yb_tiles = _moe_experts(x_tiles, row_tok, row_gate, block_expert, n_real, layer, w_in, b_in, w_out, b_out)
    return _moe_combine_ln(xn, dest, yb_tiles, g, b)


def _softplus(x):
    return jnp.maximum(x, 0.0) + jnp.log(1.0 + jnp.exp(-jnp.abs(x)))


def _gelu_tanh(x):
    return 0.5 * x * (1.0 + jnp.tanh(0.7978845608028654 * (x + 0.044715 * (x * x * x))))


def _lru_kernel(xa_ref, ga_ref, cw_ref, cb_ref, wr_ref, br_ref, wi_ref, bi_ref, lam_ref, o_ref,
                tail_ref, h_ref, a_ref, u_ref):
    tc, bsz, w = xa_ref.shape

    @pl.when(pl.program_id(0) == 0)
    def _init():
        tail_ref[...] = jnp.zeros_like(tail_ref)
        h_ref[...] = jnp.zeros_like(h_ref)

    xa = xa_ref[...]
    ext = jnp.concatenate([tail_ref[...], xa], axis=0)
    xc = cb_ref[...] + ext[0:tc] * cw_ref[0]
    for tap in range(1, CONV_WIDTH):
        xc = xc + ext[tap:tap + tc] * cw_ref[tap]
    tail_ref[...] = xa[tc - (CONV_WIDTH - 1):]
    x2 = xc.reshape(tc * bsz, w)
    xb = x2.astype(BF16)
    r = jax.nn.sigmoid(_dot(xb, wr_ref[...]) + br_ref[...])
    ig = jax.nn.sigmoid(_dot(xb, wi_ref[...]) + bi_ref[...])
    log_a = (-LRU_C) * r * _softplus(-lam_ref[...])
    a_ref[...] = jnp.exp(log_a).reshape(tc, bsz, w)
    u_ref[...] = (jnp.sqrt(1.0 - jnp.exp(2.0 * log_a)) * (ig * x2)).reshape(tc, bsz, w)

    def step(t, h):
        h = a_ref[t] * h + u_ref[t]
        u_ref[t] = h
        return h

    h_ref[...] = lax.fori_loop(0, tc, step, h_ref[...], unroll=8)
    o_ref[...] = (u_ref[...] * _gelu_tanh(ga_ref[...])).astype(o_ref.dtype)


def _block_diag(w):
    h, i, j = w.shape
    return jnp.einsum('hij,hk->hikj', w, jnp.eye(h, dtype=w.dtype)).reshape(h * i, h * j)


def _rg_lru(proj, conv_w, conv_b, gate_r_w, gate_r_b, gate_i_w, gate_i_b, lru_lambda):
    seq, bsz, _ = proj.shape
    w = A_WIDTH
    tc = LRU_TC
    fixed2 = lambda i: (0, 0)
    fixed3 = lambda i: (0, 0, 0)
    return pl.pallas_call(
        _lru_kernel,
        grid=(seq // tc,),
        in_specs=[pl.BlockSpec((tc, bsz, w), lambda i: (i, 0, 0)),
                  pl.BlockSpec((tc, bsz, w), lambda i: (i, 0, 1)),
                  pl.BlockSpec((CONV_WIDTH, 1, w), fixed3), pl.BlockSpec((1, w), fixed2),
                  pl.BlockSpec((w, w), fixed2), pl.BlockSpec((1, w), fixed2),
                  pl.BlockSpec((w, w), fixed2), pl.BlockSpec((1, w), fixed2),
                  pl.BlockSpec((1, w), fixed2)],
        out_specs=pl.BlockSpec((tc, bsz, w), lambda i: (i, 0, 0)),
        out_shape=jax.ShapeDtypeStruct((seq, bsz, w), BF16),
        scratch_shapes=[pltpu.VMEM((CONV_WIDTH - 1, bsz, w), F32), pltpu.VMEM((bsz, w), F32),
                        pltpu.VMEM((tc, bsz, w), F32), pltpu.VMEM((tc, bsz, w), F32)],
        compiler_params=_cparams("arbitrary"),
        name="rg_lru",
    )(proj, proj, conv_w.reshape(CONV_WIDTH, 1, w), conv_b.reshape(1, w),
      _block_diag(gate_r_w).astype(BF16), gate_r_b.reshape(1, w),
      _block_diag(gate_i_w).astype(BF16), gate_i_b.reshape(1, w), lru_lambda.reshape(1, w))


def _s5_kernel(u_ref, wbr_ref, wbi_ref, ar_ref, ai_ref, ccr_ref, cci_ref, d_ref, gw_ref, gb_ref, o_ref,
               xr_ref, xi_ref, sr_ref, si_ref):
    tc, bsz, w = u_ref.shape
    nstate = xr_ref.shape[-1]
    sh = nstate // S5_HALF
    wh = w // S5_HALF

    @pl.when(pl.program_id(0) == 0)
    def _init():
        sr_ref[...] = jnp.zeros_like(sr_ref)
        si_ref[...] = jnp.zeros_like(si_ref)

    u2 = u_ref[...].reshape(tc * bsz, w)
    ub = u2.astype(BF16)
    for hf in range(S5_HALF):
        uh = ub[:, hf * wh:(hf + 1) * wh]
        xr_ref[:, :, hf * sh:(hf + 1) * sh] = _dot(uh, wbr_ref[hf]).reshape(tc, bsz, sh)
        xi_ref[:, :, hf * sh:(hf + 1) * sh] = _dot(uh, wbi_ref[hf]).reshape(tc, bsz, sh)

    for hf in range(S5_HALF):
        lo, hi = hf * sh, (hf + 1) * sh
        ar = jnp.broadcast_to(ar_ref[:, lo:hi], (bsz, sh))
        ai = jnp.broadcast_to(ai_ref[:, lo:hi], (bsz, sh))

        def step(t, carry, lo=lo, hi=hi, ar=ar, ai=ai):
            xr, xi = carry
            nxr = ar * xr - ai * xi + xr_ref[t, :, lo:hi]
            nxi = ar * xi + ai * xr + xi_ref[t, :, lo:hi]
            xr_ref[t, :, lo:hi] = nxr
            xi_ref[t, :, lo:hi] = nxi
            return nxr, nxi

        xr, xi = lax.fori_loop(0, tc, step, (sr_ref[:, lo:hi], si_ref[:, lo:hi]), unroll=4)
        sr_ref[:, lo:hi] = xr
        si_ref[:, lo:hi] = xi

    ys = []
    for hf in range(S5_HALF):
        lo, hi = hf * sh, (hf + 1) * sh
        xrh = xr_ref[:, :, lo:hi].reshape(tc * bsz, sh).astype(BF16)
        xih = xi_ref[:, :, lo:hi].reshape(tc * bsz, sh).astype(BF16)
        ys.append(_dot(xrh, ccr_ref[hf]) - _dot(xih, cci_ref[hf]))
    y = jnp.concatenate(ys, axis=1) + d_ref[...] * u2
    s = _gelu_tanh(y)
    yb = s * jax.nn.sigmoid(_dot(s.astype(BF16), gw_ref[...]) + gb_ref[...])
    o_ref[...] = yb.reshape(tc, bsz, w).astype(o_ref.dtype)


def _s5_glu(proj, lam_re, lam_im, log_step, b_re, b_im, c_re, c_im, d_skip, glu_w, glu_b):
    seq, bsz, _ = proj.shape
    w = B_WIDTH
    tc = S5_TC
    nstate = S5_GROUPS * S5_STATE
    gh = S5_GROUPS // S5_HALF
    lr = jnp.minimum(lam_re, -1e-4)
    li = lam_im
    step = jnp.exp(log_step)[:, None]
    mag = jnp.exp(lr * step)
    ar = mag * jnp.cos(li * step)
    ai = mag * jnp.sin(li * step)
    inv = 1.0 / (lr * lr + li * li)
    zr = ((ar - 1.0) * lr + ai * li) * inv
    zi = (ai * lr - (ar - 1.0) * li) * inv
    bbr = zr[..., None] * b_re - zi[..., None] * b_im
    bbi = zr[..., None] * b_im + zi[..., None] * b_re
    eye = jnp.eye(gh, dtype=F32)

    def expand_in(bb):
        bb = bb.reshape(S5_HALF, gh, S5_STATE, S5_GROUP_CH)
        return jnp.einsum('fgph,gk->fghkp', bb, eye).reshape(
            S5_HALF, gh * S5_GROUP_CH, gh * S5_STATE).astype(BF16)

    def expand_out(c):
        c = c.reshape(S5_HALF, gh, S5_GROUP_CH, S5_STATE)
        return jnp.einsum('fghp,gk->fgpkh', c, eye).reshape(
            S5_HALF, gh * S5_STATE, gh * S5_GROUP_CH).astype(BF16)

    fixed2 = lambda i: (0, 0)
    fixed3 = lambda i: (0, 0, 0)
    return pl.pallas_call(
        _s5_kernel,
        grid=(seq // tc,),
        in_specs=[pl.BlockSpec((tc, bsz, w), lambda i: (i, 0, 2)),
                  pl.BlockSpec((S5_HALF, w // S5_HALF, nstate // S5_HALF), fixed3),
                  pl.BlockSpec((S5_HALF, w // S5_HALF, nstate // S5_HALF), fixed3),
                  pl.BlockSpec((1, nstate), fixed2), pl.BlockSpec((1, nstate), fixed2),
                  pl.BlockSpec((S5_HALF, nstate // S5_HALF, w // S5_HALF), fixed3),
                  pl.BlockSpec((S5_HALF, nstate // S5_HALF, w // S5_HALF), fixed3),
                  pl.BlockSpec((1, w), fixed2), pl.BlockSpec((w, w), fixed2), pl.BlockSpec((1, w), fixed2)],
        out_specs=pl.BlockSpec((tc, bsz, w), lambda i: (i, 0, 0)),
        out_shape=jax.ShapeDtypeStruct((seq, bsz, w), BF16),
        scratch_shapes=[pltpu.VMEM((tc, bsz, nstate), F32), pltpu.VMEM((tc, bsz, nstate), F32),
                        pltpu.VMEM((bsz, nstate), F32), pltpu.VMEM((bsz, nstate), F32)],
        compiler_params=_cparams("arbitrary"),
        name="s5_glu",
    )(proj, expand_in(bbr), expand_in(bbi), ar.reshape(1, nstate), ai.reshape(1, nstate),
      expand_out(c_re), expand_out(c_im), d_skip.reshape(1, w), glu_w.astype(BF16), glu_b.reshape(1, w))


def _cd_prep_kernel(xc_ref, q_ref, k_ref, pos_ref, cw_ref, cb_ref, wq_ref, wk_ref, wv_ref, wif_ref,
                    bif_ref, invf_ref, mq_ref, mk_ref, mv_ref, gt_ref, qd_ref, kd_ref, tail_ref):
    tl, w = xc_ref.shape
    pad = tail_ref.shape[0]

    @pl.when(pl.program_id(1) == 0)
    def _init():
        tail_ref[...] = jnp.zeros_like(tail_ref)

    xc = xc_ref[...]
    ext = jnp.concatenate([tail_ref[...], xc], axis=0)
    conv = cb_ref[...]
    for tap in range(CONV_WIDTH):
        off = pad - (CONV_WIDTH - 1) + tap
        conv = conv + ext[off:off + tl] * cw_ref[tap]
    tail_ref[...] = xc[tl - pad:]
    xconv = (conv * jax.nn.sigmoid(conv)).astype(BF16)
    q = _dot(xconv, wq_ref[...]).astype(BF16)
    k = (_dot(xconv, wk_ref[...]) * (MLSTM_HEAD_DIM ** -0.5)).astype(BF16)
    v = _dot(xc.astype(BF16), wv_ref[...]).astype(BF16)
    mq_ref[...] = q
    mk_ref[...] = k
    mv_ref[...] = v
    gt_ref[...] = (_dot(q, wif_ref[0]) + _dot(k, wif_ref[1]) + _dot(v, wif_ref[2]) + bif_ref[...])

    ang = pos_ref[...].astype(F32) * invf_ref[...]
    cos = jnp.cos(ang)
    sin = jnp.sin(ang)
    lane = lax.broadcasted_iota(jnp.int32, (1, LANES), 1)
    half = ROPE_DIMS // 2
    sin_lo = jnp.where(lane < half, -sin, 0.0)
    sin_hi = jnp.where((lane >= half) & (lane < ROPE_DIMS), sin, 0.0)
    for src, dst in ((q_ref, qd_ref), (k_ref, kd_ref)):
        for h in range(DSW_HEADS):
            sl = slice(h * DSW_HEAD_DIM, (h + 1) * DSW_HEAD_DIM)
            t = src[:, sl]
            rot = (t * cos + pltpu.roll(t, LANES - half, axis=1) * sin_lo
                   + pltpu.roll(t, half, axis=1) * sin_hi)
            dst[:, sl] = rot


def _cd_prep(proj, positions, conv_w, conv_b, w_q, w_k, w_v, w_if, b_if):
    bsz, seq, _ = proj.shape
    w = C_WIDTH
    tl = PROJ_TL
    half = ROPE_DIMS // 2
    inv_freq = ROPE_THETA ** (-jnp.arange(half, dtype=F32) / half)
    invf = jnp.zeros((1, LANES), F32).at[0, :half].set(inv_freq).at[0, half:ROPE_DIMS].set(inv_freq)
    wif = jnp.zeros((3, w, LANES), F32).at[:, :, :2 * MLSTM_HEADS].set(w_if.reshape(3, w, 2 * MLSTM_HEADS))
    bif = jnp.zeros((1, LANES), F32).at[0, :2 * MLSTM_HEADS].set(b_if)
    col = lambda c: (lambda b, i: (b, i, c))
    fixed2 = lambda b, i: (0, 0)
    fixed3 = lambda b, i: (0, 0, 0)
    tok_bf = jax.ShapeDtypeStruct((bsz, seq, w), BF16)
    return pl.pallas_call(
        _cd_prep_kernel,
        grid=(bsz, seq // tl),
        in_specs=[pl.BlockSpec((None, tl, w), col(0)), pl.BlockSpec((None, tl, w), col(2)),
                  pl.BlockSpec((None, tl, w), col(3)), pl.BlockSpec((None, tl, 1), col(0)),
                  pl.BlockSpec((CONV_WIDTH, 1, w), fixed3), pl.BlockSpec((1, w), fixed2),
                  pl.BlockSpec((w, w), fixed2), pl.BlockSpec((w, w), fixed2), pl.BlockSpec((w, w), fixed2),
                  pl.BlockSpec((3, w, LANES), fixed3), pl.BlockSpec((1, LANES), fixed2),
                  pl.BlockSpec((1, LANES), fixed2)],
        out_specs=[pl.BlockSpec((None, tl, w), col(0))] * 3 + [pl.BlockSpec((None, tl, LANES), col(0))]
                  + [pl.BlockSpec((None, tl, w), col(0))] * 2,
        out_shape=[tok_bf, tok_bf, tok_bf, jax.ShapeDtypeStruct((bsz, seq, LANES), F32),
                   jax.ShapeDtypeStruct((bsz, seq, w), F32), jax.ShapeDtypeStruct((bsz, seq, w), F32)],
        scratch_shapes=[pltpu.VMEM((8, w), F32)],
        compiler_params=_cparams("arbitrary", "arbitrary"),
        name="cd_prep",
    )(proj, proj, proj, positions.reshape(bsz, seq, 1), conv_w.reshape(CONV_WIDTH, 1, w),
      conv_b.reshape(1, w), _block_diag(w_q).astype(BF16), _block_diag(w_k).astype(BF16),
      _block_diag(w_v).astype(BF16), wif.astype(BF16), bif, invf)


def _cumsum_rows(x):
    n = x.shape[0]
    row = lax.broadcasted_iota(jnp.int32, x.shape, 0)
    shift = 1
    while shift < n:
        x = x + jnp.where(row >= shift, pltpu.roll(x, shift, axis=0), 0.0)
        shift *= 2
    return x


def _mlstm_kernel(q_ref, k_ref, v_ref, g_ref, op_ref, o_ref, c_ref, n_ref, m_ref):
    cs = q_ref.shape[0]
    dh = MLSTM_HEAD_DIM

    @pl.when(pl.program_id(1) == 0)
    def _init():
        c_ref[...] = jnp.zeros_like(c_ref)
        n_ref[...] = jnp.zeros_like(n_ref)
        m_ref[...] = jnp.full_like(m_ref, NEG_INF)

    gates = g_ref[...]
    cum = _cumsum_rows(jax.nn.log_sigmoid(gates))
    gates_t = gates.T
    cum_t = cum.T
    causal = (lax.broadcasted_iota(jnp.int32, (cs, cs), 0) >= lax.broadcasted_iota(jnp.int32, (cs, cs), 1))
    for h in range(MLSTM_HEADS):
        sl = slice(h * dh, (h + 1) * dh)
        fcol = MLSTM_HEADS + h
        li_c, li_r = gates[:, h:h + 1], gates_t[h:h + 1, :]
        cum_c, cum_r = cum[:, fcol:fcol + 1], cum_t[fcol:fcol + 1, :]
        m_prev = m_ref[h][:, 0:1]
        q, k, v = q_ref[:, sl], k_ref[:, sl], v_ref[:, sl]
        log_d = jnp.where(causal, cum_c - cum_r + li_r, NEG_INF)
        log_inter = cum_c + m_prev
        m_s = jnp.maximum(jnp.max(log_d, axis=1, keepdims=True), log_inter)
        s = _dot_nt(q, k) * jnp.exp(log_d - m_s)
        inter = jnp.exp(log_inter - m_s)
        c_prev = c_ref[h]
        n_prev = n_ref[h]
        num = _dot(s.astype(BF16), v) + inter * _dot(q, c_prev.astype(BF16))
        den = (jnp.sum(s, axis=1, keepdims=True)
               + inter * jnp.sum(q.astype(F32) * n_prev, axis=1, keepdims=True))
        hh = num / jnp.maximum(jnp.abs(den), jnp.exp(-m_s))
        o_ref[:, sl] = (jax.nn.sigmoid(op_ref[:, sl]) * hh).astype(o_ref.dtype)
        chunk_f = cum_c[cs - 1:cs, :]
        to_end = chunk_f - cum_c + li_c
        m_c = jnp.max(to_end, axis=0, keepdims=True)
        m_new = jnp.maximum(chunk_f + m_prev, m_c)
        s_old = jnp.exp(chunk_f + m_prev - m_new)
        s_new = jnp.exp(m_c - m_new)
        kw = k.astype(F32) * jnp.exp(to_end - m_c)
        c_ref[h] = s_old * c_prev + s_new * _dot(kw.T.astype(BF16), v)
        n_ref[h] = s_old * n_prev + s_new * jnp.sum(kw, axis=0, keepdims=True)
        m_ref[h] = jnp.broadcast_to(m_new, (1, LANES))


def _mlstm(mq, mk, mv, gates, proj):
    bsz, seq, w = mq.shape
    cs = MLSTM_CS
    tok = lambda b, i: (b, i, 0)
    return pl.pallas_call(
        _mlstm_kernel,
        grid=(bsz, seq // cs),
        in_specs=[pl.BlockSpec((None, cs, w), tok)] * 3
                 + [pl.BlockSpec((None, cs, LANES), tok), pl.BlockSpec((None, cs, w), lambda b, i: (b, i, 1))],
        out_specs=pl.BlockSpec((None, cs, w), tok),
        out_shape=jax.ShapeDtypeStruct((bsz, seq, w), BF16),
        scratch_shapes=[pltpu.VMEM((MLSTM_HEADS, MLSTM_HEAD_DIM, MLSTM_HEAD_DIM), F32),
                        pltpu.VMEM((MLSTM_HEADS, 1, MLSTM_HEAD_DIM), F32),
                        pltpu.VMEM((MLSTM_HEADS, 1, LANES), F32)],
        compiler_params=_cparams("arbitrary", "arbitrary"),
        name="mlstm",
    )(mq, mk, mv, gates, proj)


def _dsw_kernel(q_ref, k_ref, v_ref, y_ref, o_scr, lse_scr):
    seq, dh = q_ref.shape
    blk = DSW_BLOCK
    qi = lax.broadcasted_iota(jnp.int32, (blk, 2 * blk), 0)
    ki = lax.broadcasted_iota(jnp.int32, (blk, 2 * blk), 1)
    dist = qi + blk - ki
    scale = dh ** -0.5
    units = seq // blk
    for g, (window, dil) in enumerate(DSW_CONFIGS):
        nsub = seq // dil // blk
        band = (dist >= 0) & (dist <= window // dil)

        def unit(u, carry, g=g, dil=dil, nsub=nsub, band=band):
            r = u // nsub
            sb = u % nsub
            cur = pl.ds(r + sb * (blk * dil), blk, stride=dil)
            prev = pl.ds(r + jnp.maximum(sb - 1, 0) * (blk * dil), blk, stride=dil)
            kk = jnp.concatenate([k_ref[prev, :], k_ref[cur, :]], axis=0).astype(BF16)
            vv = jnp.concatenate([v_ref[prev, :], v_ref[cur, :]], axis=0).astype(BF16)
            s = _dot_nt(q_ref[cur, :].astype(BF16), kk) * scale
            s = jnp.where(band & ((ki >= blk) | (sb > 0)), s, NEG_INF)
            m = jnp.max(s, axis=1, keepdims=True)
            p = jnp.exp(s - m)
            l = jnp.sum(p, axis=1, keepdims=True)
            o_scr[g, cur, :] = _dot(p.astype(BF16), vv) / l
            lse_scr[g, cur, :] = m + jnp.log(l)
            return carry

        lax.fori_loop(0, units, unit, 0, unroll=4)

    lses = [lse_scr[g] for g in range(len(DSW_CONFIGS))]
    mx = functools.reduce(jnp.maximum, lses)
    ws = [jnp.exp(l - mx) for l in lses]
    acc = sum(ws[g] * o_scr[g] for g in range(len(DSW_CONFIGS)))
    y_ref[...] = (acc / sum(ws)).astype(y_ref.dtype)


def _dsw_attention(qd, kd, proj):
    bsz, seq, w = qd.shape
    dh = DSW_HEAD_DIM
    v_col = (proj.shape[-1] - w) // dh
    head = lambda b, h: (b, 0, h)
    return pl.pallas_call(
        _dsw_kernel,
        grid=(bsz, DSW_HEADS),
        in_specs=[pl.BlockSpec((None, seq, dh), head), pl.BlockSpec((None, seq, dh), head),
                  pl.BlockSpec((None, seq, dh), lambda b, h: (b, 0, v_col + h))],
        out_specs=pl.BlockSpec((None, seq, dh), head),
        out_shape=jax.ShapeDtypeStruct((bsz, seq, w), BF16),
        scratch_shapes=[pltpu.VMEM((len(DSW_CONFIGS), seq, dh), F32),
                        pltpu.VMEM((len(DSW_CONFIGS), seq, 1), F32)],
        compiler_params=_cparams("arbitrary", "arbitrary"),
        name="dsw_attention",
    )(qd, kd, proj)


def kernel(x, positions, ab_w_in, ab_conv_w, ab_conv_b, ab_gate_r_w, ab_gate_r_b, ab_gate_i_w, ab_gate_i_b, ab_lru_lambda, ab_s5_lambda_re, ab_s5_lambda_im, ab_s5_log_step, ab_s5_b_re, ab_s5_b_im, ab_s5_c_re, ab_s5_c_im, ab_s5_d, ab_glu_w, ab_glu_b, ab_w_out, cd_w_in, cd_conv_w, cd_conv_b, cd_w_q, cd_w_k, cd_w_v, cd_w_if, cd_b_if, cd_w_out, ln_mix_g, ln_mix_b, ln_ffn_g, ln_ffn_b, moe_router_w, moe_router_b, moe_w_in, moe_b_in, moe_w_out, moe_b_out):
    bsz, seq, d = x.shape
    t = bsz * seq
    for layer in range(DEPTH):
        j = layer // 2
        if layer % 2 == 0:
            proj = _inproj_time_major(x, ab_w_in[j].astype(BF16))
            ya = _rg_lru(proj, ab_conv_w[j], ab_conv_b[j], ab_gate_r_w[j], ab_gate_r_b[j],
                         ab_gate_i_w[j], ab_gate_i_b[j], ab_lru_lambda[j])
            yb = _s5_glu(proj, ab_s5_lambda_re[j], ab_s5_lambda_im[j], ab_s5_log_step[j], ab_s5_b_re[j],
                         ab_s5_b_im[j], ab_s5_c_re[j], ab_s5_c_im[j], ab_s5_d[j], ab_glu_w[j], ab_glu_b[j])
            time_major, w_out = True, ab_w_out[j]
        else:
            proj = _inproj(x, cd_w_in[j].astype(BF16))
            mq, mk, mv, gates, qd, kd = _cd_prep(proj, positions, cd_conv_w[j], cd_conv_b[j], cd_w_q[j],
                                                 cd_w_k[j], cd_w_v[j], cd_w_if[j], cd_b_if[j])
            ya = _mlstm(mq, mk, mv, gates, proj)
            yb = _dsw_attention(qd, kd, proj)
            time_major, w_out = False, cd_w_out[j]
        xn, x_tiles, route, counts = _outproj_ln_router(
            ya, yb, time_major, w_out.astype(BF16), x, ln_mix_g[layer], ln_mix_b[layer],
            moe_router_w[layer], moe_router_b[layer])
        x = _moe_ffn_ln(xn.reshape(t, d), x_tiles, route.reshape(t, LANES), counts, layer, moe_w_in, moe_b_in,
                        moe_w_out, moe_b_out, ln_ffn_g[layer], ln_ffn_b[layer]).reshape(bsz, seq, d)
    return x
```

```python
import functools

import jax
import jax.numpy as jnp
from jax import lax
from jax.experimental import pallas as pl
from jax.experimental.pallas import tpu as pltpu

F32 = jnp.float32
BF16 = jnp.bfloat16

D_MODEL = 1024
DEPTH = 2
A_WIDTH = 512
B_WIDTH = 512
C_WIDTH = 512
D_WIDTH = 512
LRU_HEADS = 8
LRU_HEAD_DIM = A_WIDTH // LRU_HEADS
LRU_C = 8.0
CONV_WIDTH = 4
S5_GROUP_CH = 16
S5_GROUPS = B_WIDTH // S5_GROUP_CH
S5_STATE = 64
MLSTM_HEADS = 4
MLSTM_HEAD_DIM = C_WIDTH // MLSTM_HEADS
DSW_HEADS = 4
DSW_HEAD_DIM = D_WIDTH // DSW_HEADS
DSW_CONFIGS = ((128, 1), (512, 4), (2048, 16))
DSW_BLOCK = 128
ROPE_THETA = 500000.0
ROPE_DIMS = DSW_HEAD_DIM // 4
N_EXPERTS = 32
TOP_K = 4
D_EXPERT = D_MODEL
SWIGLU_LIMIT = 7.0
SWIGLU_ALPHA = 1.702
DEEPNORM_ALPHA = (2 * DEPTH) ** 0.25
LN_EPS = 1e-5

VMEM_LIMIT_BYTES = 56 * 1024 * 1024
LANES = 128
TILE_ROWS = D_MODEL // LANES
MOE_TM = 512
MOE_TK = 256
PROJ_TL = 512
LRU_TC = 128
S5_TC = 64
S5_HALF = 2
MLSTM_CS = 256
NEG_INF = float("-inf")


def _cparams(*sem):
    return pltpu.CompilerParams(dimension_semantics=sem, vmem_limit_bytes=VMEM_LIMIT_BYTES)


def _dot(a, b):
    return jnp.dot(a, b, preferred_element_type=F32)


def _dot_nt(a, b):
    return lax.dot_general(a, b, (((1,), (1,)), ((), ())), preferred_element_type=F32)


def _matmul_kernel(x_ref, w_ref, o_ref):
    o_ref[...] = _dot(x_ref[...].astype(BF16), w_ref[...])


def _inproj_time_major(x3, w_bf16):
    bsz, seq, k = x3.shape
    n = w_bf16.shape[1]
    out = pl.pallas_call(
        _matmul_kernel,
        grid=(bsz, seq // PROJ_TL),
        in_specs=[pl.BlockSpec((None, PROJ_TL, k), lambda b, i: (b, i, 0)),
                  pl.BlockSpec((k, n), lambda b, i: (0, 0))],
        out_specs=pl.BlockSpec((PROJ_TL, n), lambda b, i: (i, b)),
        out_shape=jax.ShapeDtypeStruct((seq, bsz * n), F32),
        compiler_params=_cparams("arbitrary", "arbitrary"),
        name="in_proj_time_major",
    )(x3, w_bf16)
    return out.reshape(seq, bsz, n)


def _inproj(x3, w_bf16):
    bsz, seq, k = x3.shape
    n = w_bf16.shape[1]
    return pl.pallas_call(
        _matmul_kernel,
        grid=(bsz, seq // PROJ_TL),
        in_specs=[pl.BlockSpec((None, PROJ_TL, k), lambda b, i: (b, i, 0)),
                  pl.BlockSpec((k, n), lambda b, i: (0, 0))],
        out_specs=pl.BlockSpec((None, PROJ_TL, n), lambda b, i: (b, i, 0)),
        out_shape=jax.ShapeDtypeStruct((bsz, seq, n), F32),
        compiler_params=_cparams("arbitrary", "arbitrary"),
        name="in_proj",
    )(x3, w_bf16)


def _layer_norm_rows(z, g, b):
    mu = jnp.mean(z, axis=-1, keepdims=True)
    zc = z - mu
    var = jnp.mean(zc * zc, axis=-1, keepdims=True)
    return zc * lax.rsqrt(var + LN_EPS) * g + b


def _outproj_ln_router_kernel(ya_ref, yb_ref, wa_ref, wb_ref, x_ref, g_ref, b_ref, rwh_ref, rwl_ref, rb_ref,
                              tri_ref, xn_ref, xt_ref, rt_ref, cnt_ref):
    @pl.when((pl.program_id(0) == 0) & (pl.program_id(1) == 0))
    def _init():
        cnt_ref[...] = jnp.zeros_like(cnt_ref)

    mix = _dot(ya_ref[...], wa_ref[...]) + _dot(yb_ref[...], wb_ref[...])
    xn = _layer_norm_rows(DEEPNORM_ALPHA * x_ref[...] + mix, g_ref[...], b_ref[...])
    xn_ref[...] = xn
    _store_token_tiles(xt_ref, xn)
    xh = xn.astype(BF16)
    xl = (xn - xh.astype(F32)).astype(BF16)
    lg = (_dot(xh, rwh_ref[...]) + _dot(xl, rwh_ref[...]) + _dot(xh, rwl_ref[...])) + rb_ref[...]
    lane = lax.broadcasted_iota(jnp.int32, lg.shape, 1).astype(F32)
    tops, ids, hots = [], [], []
    for _ in range(TOP_K):
        m = jnp.max(lg, axis=1, keepdims=True)
        idx = jnp.min(jnp.where(lg == m, lane, float(LANES)), axis=1, keepdims=True)
        hot = lane == idx
        lg = jnp.where(hot, NEG_INF, lg)
        tops.append(m)
        ids.append(idx)
        hots.append(hot)
    exps = [jnp.exp(m - tops[0]) for m in tops]
    denom = functools.reduce(jnp.add, exps)
    chosen = functools.reduce(jnp.add, [h.astype(F32) for h in hots])
    before = _dot(tri_ref[...], chosen.astype(BF16)) + cnt_ref[...]
    route = jnp.zeros(lg.shape, F32)
    for k in range(TOP_K):
        rank = jnp.sum(jnp.where(hots[k], before, 0.0), axis=1, keepdims=True)
        route = jnp.where(lane == float(k), ids[k], route)
        route = jnp.where(lane == float(TOP_K + k), rank, route)
        route = jnp.where(lane == float(2 * TOP_K + k), exps[k] / denom, route)
    rt_ref[...] = route
    cnt_ref[...] = cnt_ref[...] + jnp.sum(chosen, axis=0, keepdims=True)


def _outproj_ln_router(ya, yb, time_major, w_bf16, x3, g, b, rw, rb):
    bsz, seq, d = x3.shape
    wa, wb = w_bf16[:ya.shape[-1]], w_bf16[ya.shape[-1]:]
    ne = LANES
    rw = jnp.zeros((d, ne), F32).at[:, :N_EXPERTS].set(rw)
    rb = jnp.full((ne,), NEG_INF, F32).at[:N_EXPERTS].set(rb)
    rw_hi = rw.astype(BF16)
    rw_lo = (rw - rw_hi.astype(F32)).astype(BF16)
    tl = PROJ_TL
    tri = jnp.tril(jnp.ones((tl, tl), BF16), -1)
    if time_major:
        ya, yb = ya.reshape(seq, -1), yb.reshape(seq, -1)
        y_specs = [pl.BlockSpec((tl, wa.shape[0]), lambda bi, i: (i, bi)),
                   pl.BlockSpec((tl, wb.shape[0]), lambda bi, i: (i, bi))]
    else:
        y_specs = [pl.BlockSpec((None, tl, wa.shape[0]), lambda bi, i: (bi, i, 0)),
                   pl.BlockSpec((None, tl, wb.shape[0]), lambda bi, i: (bi, i, 0))]
    fixed = lambda bi, i: (0, 0)
    tok = lambda bi, i: (bi, i, 0)
    return pl.pallas_call(
        _outproj_ln_router_kernel,
        grid=(bsz, seq // tl),
        in_specs=y_specs + [pl.BlockSpec(wa.shape, fixed), pl.BlockSpec(wb.shape, fixed),
                            pl.BlockSpec((None, tl, d), tok),
                            pl.BlockSpec((1, d), fixed), pl.BlockSpec((1, d), fixed),
                            pl.BlockSpec((d, ne), fixed), pl.BlockSpec((d, ne), fixed),
                            pl.BlockSpec((1, ne), fixed), pl.BlockSpec((tl, tl), fixed)],
        out_specs=[pl.BlockSpec((None, tl, d), tok),
                   pl.BlockSpec((tl * TILE_ROWS, LANES), lambda bi, i: (bi * (seq // tl) + i, 0)),
                   pl.BlockSpec((None, tl, ne), tok), pl.BlockSpec((1, ne), fixed)],
        out_shape=[jax.ShapeDtypeStruct((bsz, seq, d), F32),
                   jax.ShapeDtypeStruct((bsz * seq * TILE_ROWS, LANES), F32),
                   jax.ShapeDtypeStruct((bsz, seq, ne), F32), jax.ShapeDtypeStruct((1, ne), F32)],
        compiler_params=_cparams("arbitrary", "arbitrary"),
        name="out_proj_ln_router",
    )(ya, yb, wa, wb, x3, g.reshape(1, d), b.reshape(1, d), rw_hi, rw_lo, rb.reshape(1, ne), tri)


def _store_token_tiles(dst_ref, x):
    n = x.shape[0]
    for j in range(TILE_ROWS):
        dst_ref[pl.ds(j, n, stride=TILE_ROWS), :] = x[:, j * LANES:(j + 1) * LANES]


def _load_token_tiles(src_ref, n):
    return jnp.concatenate([src_ref[pl.ds(j, n, stride=TILE_ROWS), :] for j in range(TILE_ROWS)], axis=1)


def _gather_token_tiles(src_hbm, idx_ref, n, dst, sem):
    def body(g, carry):
        base = pl.multiple_of(g * (TILE_ROWS * TILE_ROWS), TILE_ROWS * TILE_ROWS)
        for u in range(TILE_ROWS):
            src_row = pl.multiple_of(idx_ref[0, g * TILE_ROWS + u] * TILE_ROWS, TILE_ROWS)
            pltpu.make_async_copy(src_hbm.at[pl.ds(src_row, TILE_ROWS)],
                                  dst.at[pl.ds(base + u * TILE_ROWS, TILE_ROWS)], sem).start()
        return carry
    lax.fori_loop(0, n // TILE_ROWS, body, 0, unroll=4)


def _moe_kernel(be_ref, nreal_ref, rt_ref, rtn_ref, x_hbm, rg_ref, win_ref, bin_ref, wout_ref, bout_ref, o_ref,
                win_bf, wout_bf, xbuf, sem):
    i = pl.program_id(0)
    n_real = nreal_ref[0]
    e = be_ref[i]
    prev = be_ref[jnp.maximum(i - 1, 0)]
    tm = xbuf.shape[1] // TILE_ROWS
    slot = lax.rem(i, 2)

    @pl.when(i == 0)
    def _first_gather():
        _gather_token_tiles(x_hbm, rt_ref, tm, xbuf.at[0], sem.at[0])

    @pl.when(i + 1 < n_real)
    def _next_gather():
        _gather_token_tiles(x_hbm, rtn_ref, tm, xbuf.at[1 - slot], sem.at[1 - slot])

    @pl.when((i == 0) | (e != prev))
    def _load_expert():
        win_bf[...] = win_ref[...].astype(BF16)
        wout_bf[...] = wout_ref[...].astype(BF16)

    @pl.when((i < n_real) | (i == 0))
    def _compute():
        pltpu.make_async_copy(xbuf.at[slot], xbuf.at[slot], sem.at[slot]).wait()
        x = _load_token_tiles(xbuf.at[slot], tm).astype(BF16)
        h = _dot(x, win_bf[...]) + bin_ref[...]
        g = jnp.minimum(h[:, :D_EXPERT], SWIGLU_LIMIT)
        lin = jnp.clip(h[:, D_EXPERT:], -SWIGLU_LIMIT, SWIGLU_LIMIT)
        y = g * jax.nn.sigmoid(SWIGLU_ALPHA * g) * (lin + 1.0)
        out = _dot(y.astype(BF16), wout_bf[...]) + bout_ref[...]
        gates = rg_ref[...].T
        out = jnp.concatenate([out[j * LANES:(j + 1) * LANES] * gates[:, j:j + 1] for j in range(tm // LANES)],
                              axis=0)
        _store_token_tiles(o_ref, out)

    @pl.when((i >= n_real) & (i > 0))
    def _unused_block():
        o_ref[...] = jnp.zeros_like(o_ref)


def _moe_experts(x_tiles, row_tok, row_gate, block_expert, n_real, layer, w_in, b_in, w_out, b_out):
    d = D_MODEL
    tm = MOE_TM
    n_blocks = row_tok.shape[0]
    nl, ne, _, dh2 = w_in.shape
    rt = row_tok.reshape(n_blocks, 1, tm)
    rg = jnp.pad(row_gate.reshape(n_blocks, tm // LANES, LANES), ((0, 0), (0, TILE_ROWS - tm // LANES), (0, 0)))
    smem_rows = lambda imap: pl.BlockSpec((None, 1, tm), imap, memory_space=pltpu.SMEM)
    grid_spec = pltpu.PrefetchScalarGridSpec(
        num_scalar_prefetch=2,
        grid=(n_blocks,),
        in_specs=[
            smem_rows(lambda i, be, nr: (i, 0, 0)),
            smem_rows(lambda i, be, nr: (jnp.minimum(i + 1, n_blocks - 1), 0, 0)),
            pl.BlockSpec(memory_space=pl.ANY),
            pl.BlockSpec((None, TILE_ROWS, LANES), lambda i, be, nr: (i, 0, 0)),
            pl.BlockSpec((None, None, d, dh2), lambda i, be, nr: (layer, be[i], 0, 0)),
            pl.BlockSpec((None, None, 1, dh2), lambda i, be, nr: (layer, be[i], 0, 0)),
            pl.BlockSpec((None, None, D_EXPERT, d), lambda i, be, nr: (layer, be[i], 0, 0)),
            pl.BlockSpec((None, None, 1, d), lambda i, be, nr: (layer, be[i], 0, 0)),
        ],
        out_specs=pl.BlockSpec((tm * TILE_ROWS, LANES), lambda i, be, nr: (i, 0)),
        scratch_shapes=[pltpu.VMEM((d, dh2), BF16), pltpu.VMEM((D_EXPERT, d), BF16),
                        pltpu.VMEM((2, tm * TILE_ROWS, LANES), F32), pltpu.SemaphoreType.DMA((2,))],
    )
    return pl.pallas_call(
        _moe_kernel,
        grid_spec=grid_spec,
        out_shape=jax.ShapeDtypeStruct((n_blocks * tm * TILE_ROWS, LANES), F32),
        compiler_params=_cparams("arbitrary"),
        name="moe_experts",
    )(block_expert, n_real, rt, rt, x_tiles, rg, w_in,
      b_in.reshape(nl, ne, 1, dh2), w_out, b_out.reshape(nl, ne, 1, d))


def _moe_combine_kernel(dc_ref, dn_ref, x_ref, yb_hbm, g_ref, b_ref, o_ref, buf, ffn_ref, sem):
    i = pl.program_id(0)
    n = pl.num_programs(0)
    tk = x_ref.shape[0]
    rows = tk * TILE_ROWS
    slot = lax.rem(i, 2)

    @pl.when(i == 0)
    def _first_gather():
        _gather_token_tiles(yb_hbm, dc_ref, TOP_K * tk, buf.at[0], sem.at[0])

    @pl.when(i + 1 < n)
    def _next_gather():
        _gather_token_tiles(yb_hbm, dn_ref, TOP_K * tk, buf.at[1 - slot], sem.at[1 - slot])

    pltpu.make_async_copy(buf.at[slot], buf.at[slot], sem.at[slot]).wait()
    ffn = buf[slot, 0:rows]
    for k in range(1, TOP_K):
        ffn = ffn + buf[slot, k * rows:(k + 1) * rows]
    ffn_ref[...] = ffn
    o_ref[...] = _layer_norm_rows(DEEPNORM_ALPHA * x_ref[...] + _load_token_tiles(ffn_ref, tk),
                                  g_ref[...], b_ref[...])


def _moe_combine_ln(xn, dest, yb_tiles, g, b):
    t, d = xn.shape
    tk = MOE_TK
    n = t // tk
    dest_tiles = dest.reshape(n, tk, TOP_K).transpose(0, 2, 1).reshape(n, 1, TOP_K * tk)
    smem_rows = lambda imap: pl.BlockSpec((None, 1, TOP_K * tk), imap, memory_space=pltpu.SMEM)
    row = lambda i: (i, 0)
    fixed = lambda i: (0, 0)
    return pl.pallas_call(
        _moe_combine_kernel,
        grid=(n,),
        in_specs=[smem_rows(lambda i: (i, 0, 0)), smem_rows(lambda i: (jnp.minimum(i + 1, n - 1), 0, 0)),
                  pl.BlockSpec((tk, d), row), pl.BlockSpec(memory_space=pl.ANY),
                  pl.BlockSpec((1, d), fixed), pl.BlockSpec((1, d), fixed)],
        out_specs=pl.BlockSpec((tk, d), row),
        out_shape=jax.ShapeDtypeStruct((t, d), F32),
        scratch_shapes=[pltpu.VMEM((2, TOP_K * tk * TILE_ROWS, LANES), F32),
                        pltpu.VMEM((tk * TILE_ROWS, LANES), F32), pltpu.SemaphoreType.DMA((2,))],
        compiler_params=_cparams("arbitrary"),
        name="moe_combine_ln",
    )(dest_tiles, dest_tiles, xn, yb_tiles, g.reshape(1, d), b.reshape(1, d))


def _moe_ffn_ln(xn, x_tiles, route, counts, layer, w_in, b_in, w_out, b_out, g, b):
    t, d = xn.shape
    tm = MOE_TM
    flat_e = route[:, :TOP_K].astype(jnp.int32).reshape(-1)
    rank = route[:, TOP_K:2 * TOP_K].astype(jnp.int32).reshape(-1)
    gate = route[:, 2 * TOP_K:3 * TOP_K].reshape(-1)
    counts = counts[0, :N_EXPERTS].astype(jnp.int32)
    blocks_per_e = (counts + tm - 1) // tm
    blk_end = jnp.cumsum(blocks_per_e)
    blk_start = blk_end - blocks_per_e
    row_start = jnp.cumsum(counts) - counts
    dest = blk_start[flat_e] * tm + rank
    n_blocks = (t * TOP_K) // tm + N_EXPERTS
    block_expert = jnp.minimum(
        jnp.sum((blk_end[None, :] <= jnp.arange(n_blocks, dtype=jnp.int32)[:, None]).astype(jnp.int32), axis=1),
        N_EXPERTS - 1)
    n_real = blk_end[-1:].astype(jnp.int32)
    order = jnp.argsort(flat_e, stable=True).astype(jnp.int32)
    blk = jnp.arange(n_blocks, dtype=jnp.int32)
    rank_row = ((blk - blk_start[block_expert]) * tm)[:, None] + jnp.arange(tm, dtype=jnp.int32)[None, :]
    valid = rank_row < counts[block_expert][:, None]
    pair = order[jnp.clip(row_start[block_expert][:, None] + rank_row, 0, t * TOP_K - 1)]
    row_tok = jnp.where(valid, pair // TOP_K, 0)
    row_gate = jnp.where(valid, gate[pair], 0.0)
    yb_tiles = _moe_experts(x_tiles, row_tok, row_gate, block_expert, n_real, layer, w_in, b_in, w_out, b_out)
    return _moe_combine_ln(xn, dest, yb_tiles, g, b)


def _softplus(x):
    return jnp.maximum(x, 0.0) + jnp.log(1.0 + jnp.exp(-jnp.abs(x)))


def _gelu_tanh(x):
    return 0.5 * x * (1.0 + jnp.tanh(0.7978845608028654 * (x + 0.044715 * (x * x * x))))


def _lru_kernel(xa_ref, ga_ref, cw_ref, cb_ref, wr_ref, br_ref, wi_ref, bi_ref, lam_ref, o_ref,
                tail_ref, h_ref, a_ref, u_ref):
    tc, bsz, w = xa_ref.shape

    @pl.when(pl.program_id(0) == 0)
    def _init():
        tail_ref[...] = jnp.zeros_like(tail_ref)
        h_ref[...] = jnp.zeros_like(h_ref)

    xa = xa_ref[...]
    ext = jnp.concatenate([tail_ref[...], xa], axis=0)
    xc = cb_ref[...] + ext[0:tc] * cw_ref[0]
    for tap in range(1, CONV_WIDTH):
        xc = xc + ext[tap:tap + tc] * cw_ref[tap]
    tail_ref[...] = xa[tc - (CONV_WIDTH - 1):]
    x2 = xc.reshape(tc * bsz, w)
    xb = x2.astype(BF16)
    r = jax.nn.sigmoid(_dot(xb, wr_ref[...]) + br_ref[...])
    ig = jax.nn.sigmoid(_dot(xb, wi_ref[...]) + bi_ref[...])
    log_a = (-LRU_C) * r * _softplus(-lam_ref[...])
    a_ref[...] = jnp.exp(log_a).reshape(tc, bsz, w)
    u_ref[...] = (jnp.sqrt(1.0 - jnp.exp(2.0 * log_a)) * (ig * x2)).reshape(tc, bsz, w)

    def step(t, h):
        h = a_ref[t] * h + u_ref[t]
        u_ref[t] = h
        return h

    h_ref[...] = lax.fori_loop(0, tc, step, h_ref[...], unroll=8)
    o_ref[...] = (u_ref[...] * _gelu_tanh(ga_ref[...])).astype(o_ref.dtype)


def _block_diag(w):
    h, i, j = w.shape
    return jnp.einsum('hij,hk->hikj', w, jnp.eye(h, dtype=w.dtype)).reshape(h * i, h * j)


def _rg_lru(proj, conv_w, conv_b, gate_r_w, gate_r_b, gate_i_w, gate_i_b, lru_lambda):
    seq, bsz, _ = proj.shape
    w = A_WIDTH
    tc = LRU_TC
    fixed2 = lambda i: (0, 0)
    fixed3 = lambda i: (0, 0, 0)
    return pl.pallas_call(
        _lru_kernel,
        grid=(seq // tc,),
        in_specs=[pl.BlockSpec((tc, bsz, w), lambda i: (i, 0, 0)),
                  pl.BlockSpec((tc, bsz, w), lambda i: (i, 0, 1)),
                  pl.BlockSpec((CONV_WIDTH, 1, w), fixed3), pl.BlockSpec((1, w), fixed2),
                  pl.BlockSpec((w, w), fixed2), pl.BlockSpec((1, w), fixed2),
                  pl.BlockSpec((w, w), fixed2), pl.BlockSpec((1, w), fixed2),
                  pl.BlockSpec((1, w), fixed2)],
        out_specs=pl.BlockSpec((tc, bsz, w), lambda i: (i, 0, 0)),
        out_shape=jax.ShapeDtypeStruct((seq, bsz, w), BF16),
        scratch_shapes=[pltpu.VMEM((CONV_WIDTH - 1, bsz, w), F32), pltpu.VMEM((bsz, w), F32),
                        pltpu.VMEM((tc, bsz, w), F32), pltpu.VMEM((tc, bsz, w), F32)],
        compiler_params=_cparams("arbitrary"),
        name="rg_lru",
    )(proj, proj, conv_w.reshape(CONV_WIDTH, 1, w), conv_b.reshape(1, w),
      _block_diag(gate_r_w).astype(BF16), gate_r_b.reshape(1, w),
      _block_diag(gate_i_w).astype(BF16), gate_i_b.reshape(1, w), lru_lambda.reshape(1, w))


def _s5_kernel(u_ref, wbr_ref, wbi_ref, ar_ref, ai_ref, ccr_ref, cci_ref, d_ref, gw_ref, gb_ref, o_ref,
               xr_ref, xi_ref, sr_ref, si_ref):
    tc, bsz, w = u_ref.shape
    nstate = xr_ref.shape[-1]
    sh = nstate // S5_HALF
    wh = w // S5_HALF

    @pl.when(pl.program_id(0) == 0)
    def _init():
        sr_ref[...] = jnp.zeros_like(sr_ref)
        si_ref[...] = jnp.zeros_like(si_ref)

    u2 = u_ref[...].reshape(tc * bsz, w)
    ub = u2.astype(BF16)
    for hf in range(S5_HALF):
        uh = ub[:, hf * wh:(hf + 1) * wh]
        xr_ref[:, :, hf * sh:(hf + 1) * sh] = _dot(uh, wbr_ref[hf]).reshape(tc, bsz, sh)
        xi_ref[:, :, hf * sh:(hf + 1) * sh] = _dot(uh, wbi_ref[hf]).reshape(tc, bsz, sh)

    for hf in range(S5_HALF):
        lo, hi = hf * sh, (hf + 1) * sh
        ar = jnp.broadcast_to(ar_ref[:, lo:hi], (bsz, sh))
        ai = jnp.broadcast_to(ai_ref[:, lo:hi], (bsz, sh))

        def step(t, carry, lo=lo, hi=hi, ar=ar, ai=ai):
            xr, xi = carry
            nxr = ar * xr - ai * xi + xr_ref[t, :, lo:hi]
            nxi = ar * xi + ai * xr + xi_ref[t, :, lo:hi]
            xr_ref[t, :, lo:hi] = nxr
            xi_ref[t, :, lo:hi] = nxi
            return nxr, nxi

        xr, xi = lax.fori_loop(0, tc, step, (sr_ref[:, lo:hi], si_ref[:, lo:hi]), unroll=4)
        sr_ref[:, lo:hi] = xr
        si_ref[:, lo:hi] = xi

    ys = []
    for hf in range(S5_HALF):
        lo, hi = hf * sh, (hf + 1) * sh
        xrh = xr_ref[:, :, lo:hi].reshape(tc * bsz, sh).astype(BF16)
        xih = xi_ref[:, :, lo:hi].reshape(tc * bsz, sh).astype(BF16)
        ys.append(_dot(xrh, ccr_ref[hf]) - _dot(xih, cci_ref[hf]))
    y = jnp.concatenate(ys, axis=1) + d_ref[...] * u2
    s = _gelu_tanh(y)
    yb = s * jax.nn.sigmoid(_dot(s.astype(BF16), gw_ref[...]) + gb_ref[...])
    o_ref[...] = yb.reshape(tc, bsz, w).astype(o_ref.dtype)


def _s5_glu(proj, lam_re, lam_im, log_step, b_re, b_im, c_re, c_im, d_skip, glu_w, glu_b):
    seq, bsz, _ = proj.shape
    w = B_WIDTH
    tc = S5_TC
    nstate = S5_GROUPS * S5_STATE
    gh = S5_GROUPS // S5_HALF
    lr = jnp.minimum(lam_re, -1e-4)
    li = lam_im
    step = jnp.exp(log_step)[:, None]
    mag = jnp.exp(lr * step)
    ar = mag * jnp.cos(li * step)
    ai = mag * jnp.sin(li * step)
    inv = 1.0 / (lr * lr + li * li)
    zr = ((ar - 1.0) * lr + ai * li) * inv
    zi = (ai * lr - (ar - 1.0) * li) * inv
    bbr = zr[..., None] * b_re - zi[..., None] * b_im
    bbi = zr[..., None] * b_im + zi[..., None] * b_re
    eye = jnp.eye(gh, dtype=F32)

    def expand_in(bb):
        bb = bb.reshape(S5_HALF, gh, S5_STATE, S5_GROUP_CH)
        return jnp.einsum('fgph,gk->fghkp', bb, eye).reshape(
            S5_HALF, gh * S5_GROUP_CH, gh * S5_STATE).astype(BF16)

    def expand_out(c):
        c = c.reshape(S5_HALF, gh, S5_GROUP_CH, S5_STATE)
        return jnp.einsum('fghp,gk->fgpkh', c, eye).reshape(
            S5_HALF, gh * S5_STATE, gh * S5_GROUP_CH).astype(BF16)

    fixed2 = lambda i: (0, 0)
    fixed3 = lambda i: (0, 0, 0)
    return pl.pallas_call(
        _s5_kernel,
        grid=(seq // tc,),
        in_specs=[pl.BlockSpec((tc, bsz, w), lambda i: (i, 0, 2)),
                  pl.BlockSpec((S5_HALF, w // S5_HALF, nstate // S5_HALF), fixed3),
                  pl.BlockSpec((S5_HALF, w // S5_HALF, nstate // S5_HALF), fixed3),
                  pl.BlockSpec((1, nstate), fixed2), pl.BlockSpec((1, nstate), fixed2),
                  pl.BlockSpec((S5_HALF, nstate // S5_HALF, w // S5_HALF), fixed3),
                  pl.BlockSpec((S5_HALF, nstate // S5_HALF, w // S5_HALF), fixed3),
                  pl.BlockSpec((1, w), fixed2), pl.BlockSpec((w, w), fixed2), pl.BlockSpec((1, w), fixed2)],
        out_specs=pl.BlockSpec((tc, bsz, w), lambda i: (i, 0, 0)),
        out_shape=jax.ShapeDtypeStruct((seq, bsz, w), BF16),
        scratch_shapes=[pltpu.VMEM((tc, bsz, nstate), F32), pltpu.VMEM((tc, bsz, nstate), F32),
                        pltpu.VMEM((bsz, nstate), F32), pltpu.VMEM((bsz, nstate), F32)],
        compiler_params=_cparams("arbitrary"),
        name="s5_glu",
    )(proj, expand_in(bbr), expand_in(bbi), ar.reshape(1, nstate), ai.reshape(1, nstate),
      expand_out(c_re), expand_out(c_im), d_skip.reshape(1, w), glu_w.astype(BF16), glu_b.reshape(1, w))


def _cd_prep_kernel(xc_ref, q_ref, k_ref, pos_ref, cw_ref, cb_ref, wq_ref, wk_ref, wv_ref, wif_ref,
                    bif_ref, invf_ref, mq_ref, mk_ref, mv_ref, gt_ref, qd_ref, kd_ref, tail_ref):
    tl, w = xc_ref.shape
    pad = tail_ref.shape[0]

    @pl.when(pl.program_id(1) == 0)
    def _init():
        tail_ref[...] = jnp.zeros_like(tail_ref)

    xc = xc_ref[...]
    ext = jnp.concatenate([tail_ref[...], xc], axis=0)
    conv = cb_ref[...]
    for tap in range(CONV_WIDTH):
        off = pad - (CONV_WIDTH - 1) + tap
        conv = conv + ext[off:off + tl] * cw_ref[tap]
    tail_ref[...] = xc[tl - pad:]
    xconv = (conv * jax.nn.sigmoid(conv)).astype(BF16)
    q = _dot(xconv, wq_ref[...]).astype(BF16)
    k = (_dot(xconv, wk_ref[...]) * (MLSTM_HEAD_DIM ** -0.5)).astype(BF16)
    v = _dot(xc.astype(BF16), wv_ref[...]).astype(BF16)
    mq_ref[...] = q
    mk_ref[...] = k
    mv_ref[...] = v
    gt_ref[...] = (_dot(q, wif_ref[0]) + _dot(k, wif_ref[1]) + _dot(v, wif_ref[2]) + bif_ref[...])

    ang = pos_ref[...].astype(F32) * invf_ref[...]
    cos = jnp.cos(ang)
    sin = jnp.sin(ang)
    lane = lax.broadcasted_iota(jnp.int32, (1, LANES), 1)
    half = ROPE_DIMS // 2
    sin_lo = jnp.where(lane < half, -sin, 0.0)
    sin_hi = jnp.where((lane >= half) & (lane < ROPE_DIMS), sin, 0.0)
    for src, dst in ((q_ref, qd_ref), (k_ref, kd_ref)):
        for h in range(DSW_HEADS):
            sl = slice(h * DSW_HEAD_DIM, (h + 1) * DSW_HEAD_DIM)
            t = src[:, sl]
            rot = (t * cos + pltpu.roll(t, LANES - half, axis=1) * sin_lo
                   + pltpu.roll(t, half, axis=1) * sin_hi)
            dst[:, sl] = rot


def _cd_prep(proj, positions, conv_w, conv_b, w_q, w_k, w_v, w_if, b_if):
    bsz, seq, _ = proj.shape
    w = C_WIDTH
    tl = PROJ_TL
    half = ROPE_DIMS // 2
    inv_freq = ROPE_THETA ** (-jnp.arange(half, dtype=F32) / half)
    invf = jnp.zeros((1, LANES), F32).at[0, :half].set(inv_freq).at[0, half:ROPE_DIMS].set(inv_freq)
    wif = jnp.zeros((3, w, LANES), F32).at[:, :, :2 * MLSTM_HEADS].set(w_if.reshape(3, w, 2 * MLSTM_HEADS))
    bif = jnp.zeros((1, LANES), F32).at[0, :2 * MLSTM_HEADS].set(b_if)
    col = lambda c: (lambda b, i: (b, i, c))
    fixed2 = lambda b, i: (0, 0)
    fixed3 = lambda b, i: (0, 0, 0)
    tok_bf = jax.ShapeDtypeStruct((bsz, seq, w), BF16)
    return pl.pallas_call(
        _cd_prep_kernel,
        grid=(bsz, seq // tl),
        in_specs=[pl.BlockSpec((None, tl, w), col(0)), pl.BlockSpec((None, tl, w), col(2)),
                  pl.BlockSpec((None, tl, w), col(3)), pl.BlockSpec((None, tl, 1), col(0)),
                  pl.BlockSpec((CONV_WIDTH, 1, w), fixed3), pl.BlockSpec((1, w), fixed2),
                  pl.BlockSpec((w, w), fixed2), pl.BlockSpec((w, w), fixed2), pl.BlockSpec((w, w), fixed2),
                  pl.BlockSpec((3, w, LANES), fixed3), pl.BlockSpec((1, LANES), fixed2),
                  pl.BlockSpec((1, LANES), fixed2)],
        out_specs=[pl.BlockSpec((None, tl, w), col(0))] * 3 + [pl.BlockSpec((None, tl, LANES), col(0))]
                  + [pl.BlockSpec((None, tl, w), col(0))] * 2,
        out_shape=[tok_bf, tok_bf, tok_bf, jax.ShapeDtypeStruct((bsz, seq, LANES), F32),
                   jax.ShapeDtypeStruct((bsz, seq, w), F32), jax.ShapeDtypeStruct((bsz, seq, w), F32)],
        scratch_shapes=[pltpu.VMEM((8, w), F32)],
        compiler_params=_cparams("arbitrary", "arbitrary"),
        name="cd_prep",
    )(proj, proj, proj, positions.reshape(bsz, seq, 1), conv_w.reshape(CONV_WIDTH, 1, w),
      conv_b.reshape(1, w), _block_diag(w_q).astype(BF16), _block_diag(w_k).astype(BF16),
      _block_diag(w_v).astype(BF16), wif.astype(BF16), bif, invf)


def _cumsum_rows(x):
    n = x.shape[0]
    row = lax.broadcasted_iota(jnp.int32, x.shape, 0)
    shift = 1
    while shift < n:
        x = x + jnp.where(row >= shift, pltpu.roll(x, shift, axis=0), 0.0)
        shift *= 2
    return x


def _mlstm_kernel(q_ref, k_ref, v_ref, g_ref, op_ref, o_ref, c_ref, n_ref, m_ref):
    cs = q_ref.shape[0]
    dh = MLSTM_HEAD_DIM

    @pl.when(pl.program_id(1) == 0)
    def _init():
        c_ref[...] = jnp.zeros_like(c_ref)
        n_ref[...] = jnp.zeros_like(n_ref)
        m_ref[...] = jnp.full_like(m_ref, NEG_INF)

    gates = g_ref[...]
    cum = _cumsum_rows(jax.nn.log_sigmoid(gates))
    gates_t = gates.T
    cum_t = cum.T
    causal = (lax.broadcasted_iota(jnp.int32, (cs, cs), 0) >= lax.broadcasted_iota(jnp.int32, (cs, cs), 1))
    for h in range(MLSTM_HEADS):
        sl = slice(h * dh, (h + 1) * dh)
        fcol = MLSTM_HEADS + h
        li_c, li_r = gates[:, h:h + 1], gates_t[h:h + 1, :]
        cum_c, cum_r = cum[:, fcol:fcol + 1], cum_t[fcol:fcol + 1, :]
        m_prev = m_ref[h][:, 0:1]
        q, k, v = q_ref[:, sl], k_ref[:, sl], v_ref[:, sl]
        log_d = jnp.where(causal, cum_c - cum_r + li_r, NEG_INF)
        log_inter = cum_c + m_prev
        m_s = jnp.maximum(jnp.max(log_d, axis=1, keepdims=True), log_inter)
        s = _dot_nt(q, k) * jnp.exp(log_d - m_s)
        inter = jnp.exp(log_inter - m_s)
        c_prev = c_ref[h]
        n_prev = n_ref[h]
        num = _dot(s.astype(BF16), v) + inter * _dot(q, c_prev.astype(BF16))
        den = (jnp.sum(s, axis=1, keepdims=True)
               + inter * jnp.sum(q.astype(F32) * n_prev, axis=1, keepdims=True))
        hh = num / jnp.maximum(jnp.abs(den), jnp.exp(-m_s))
        o_ref[:, sl] = (jax.nn.sigmoid(op_ref[:, sl]) * hh).astype(o_ref.dtype)
        chunk_f = cum_c[cs - 1:cs, :]
        to_end = chunk_f - cum_c + li_c
        m_c = jnp.max(to_end, axis=0, keepdims=True)
        m_new = jnp.maximum(chunk_f + m_prev, m_c)
        s_old = jnp.exp(chunk_f + m_prev - m_new)
        s_new = jnp.exp(m_c - m_new)
        kw = k.astype(F32) * jnp.exp(to_end - m_c)
        c_ref[h] = s_old * c_prev + s_new * _dot(kw.T.astype(BF16), v)
        n_ref[h] = s_old * n_prev + s_new * jnp.sum(kw, axis=0, keepdims=True)
        m_ref[h] = jnp.broadcast_to(m_new, (1, LANES))


def _mlstm(mq, mk, mv, gates, proj):
    bsz, seq, w = mq.shape
    cs = MLSTM_CS
    tok = lambda b, i: (b, i, 0)
    return pl.pallas_call(
        _mlstm_kernel,
        grid=(bsz, seq // cs),
        in_specs=[pl.BlockSpec((None, cs, w), tok)] * 3
                 + [pl.BlockSpec((None, cs, LANES), tok), pl.BlockSpec((None, cs, w), lambda b, i: (b, i, 1))],
        out_specs=pl.BlockSpec((None, cs, w), tok),
        out_shape=jax.ShapeDtypeStruct((bsz, seq, w), BF16),
        scratch_shapes=[pltpu.VMEM((MLSTM_HEADS, MLSTM_HEAD_DIM, MLSTM_HEAD_DIM), F32),
                        pltpu.VMEM((MLSTM_HEADS, 1, MLSTM_HEAD_DIM), F32),
                        pltpu.VMEM((MLSTM_HEADS, 1, LANES), F32)],
        compiler_params=_cparams("arbitrary", "arbitrary"),
        name="mlstm",
    )(mq, mk, mv, gates, proj)


def _dsw_kernel(q_ref, k_ref, v_ref, y_ref, o_scr, lse_scr):
    seq, dh = q_ref.shape
    blk = DSW_BLOCK
    qi = lax.broadcasted_iota(jnp.int32, (blk, 2 * blk), 0)
    ki = lax.broadcasted_iota(jnp.int32, (blk, 2 * blk), 1)
    dist = qi + blk - ki
    scale = dh ** -0.5
    units = seq // blk
    for g, (window, dil) in enumerate(DSW_CONFIGS):
        nsub = seq // dil // blk
        band = (dist >= 0) & (dist <= window // dil)

        def unit(u, carry, g=g, dil=dil, nsub=nsub, band=band):
            r = u // nsub
            sb = u % nsub
            cur = pl.ds(r + sb * (blk * dil), blk, stride=dil)
            prev = pl.ds(r + jnp.maximum(sb - 1, 0) * (blk * dil), blk, stride=dil)
            kk = jnp.concatenate([k_ref[prev, :], k_ref[cur, :]], axis=0).astype(BF16)
            vv = jnp.concatenate([v_ref[prev, :], v_ref[cur, :]], axis=0).astype(BF16)
            s = _dot_nt(q_ref[cur, :].astype(BF16), kk) * scale
            s = jnp.where(band & ((ki >= blk) | (sb > 0)), s, NEG_INF)
            m = jnp.max(s, axis=1, keepdims=True)
            p = jnp.exp(s - m)
            l = jnp.sum(p, axis=1, keepdims=True)
            o_scr[g, cur, :] = _dot(p.astype(BF16), vv) / l
            lse_scr[g, cur, :] = m + jnp.log(l)
            return carry

        lax.fori_loop(0, units, unit, 0, unroll=4)

    lses = [lse_scr[g] for g in range(len(DSW_CONFIGS))]
    mx = functools.reduce(jnp.maximum, lses)
    ws = [jnp.exp(l - mx) for l in lses]
    acc = sum(ws[g] * o_scr[g] for g in range(len(DSW_CONFIGS)))
    y_ref[...] = (acc / sum(ws)).astype(y_ref.dtype)


def _dsw_attention(qd, kd, proj):
    bsz, seq, w = qd.shape
    dh = DSW_HEAD_DIM
    v_col = (proj.shape[-1] - w) // dh
    head = lambda b, h: (b, 0, h)
    return pl.pallas_call(
        _dsw_kernel,
        grid=(bsz, DSW_HEADS),
        in_specs=[pl.BlockSpec((None, seq, dh), head), pl.BlockSpec((None, seq, dh), head),
                  pl.BlockSpec((None, seq, dh), lambda b, h: (b, 0, v_col + h))],
        out_specs=pl.BlockSpec((None, seq, dh), head),
        out_shape=jax.ShapeDtypeStruct((bsz, seq, w), BF16),
        scratch_shapes=[pltpu.VMEM((len(DSW_CONFIGS), seq, dh), F32),
                        pltpu.VMEM((len(DSW_CONFIGS), seq, 1), F32)],
        compiler_params=_cparams("arbitrary", "arbitrary"),
        name="dsw_attention",
    )(qd, kd, proj)


def kernel(x, positions, ab_w_in, ab_conv_w, ab_conv_b, ab_gate_r_w, ab_gate_r_b, ab_gate_i_w, ab_gate_i_b, ab_lru_lambda, ab_s5_lambda_re, ab_s5_lambda_im, ab_s5_log_step, ab_s5_b_re, ab_s5_b_im, ab_s5_c_re, ab_s5_c_im, ab_s5_d, ab_glu_w, ab_glu_b, ab_w_out, cd_w_in, cd_conv_w, cd_conv_b, cd_w_q, cd_w_k, cd_w_v, cd_w_if, cd_b_if, cd_w_out, ln_mix_g, ln_mix_b, ln_ffn_g, ln_ffn_b, moe_router_w, moe_router_b, moe_w_in, moe_b_in, moe_w_out, moe_b_out):
    bsz, seq, d = x.shape
    t = bsz * seq
    for layer in range(DEPTH):
        j = layer // 2
        if layer % 2 == 0:
            proj = _inproj_time_major(x, ab_w_in[j].astype(BF16))
            ya = _rg_lru(proj, ab_conv_w[j], ab_conv_b[j], ab_gate_r_w[j], ab_gate_r_b[j],
                         ab_gate_i_w[j], ab_gate_i_b[j], ab_lru_lambda[j])
            yb = _s5_glu(proj, ab_s5_lambda_re[j], ab_s5_lambda_im[j], ab_s5_log_step[j], ab_s5_b_re[j],
                         ab_s5_b_im[j], ab_s5_c_re[j], ab_s5_c_im[j], ab_s5_d[j], ab_glu_w[j], ab_glu_b[j])
            time_major, w_out = True, ab_w_out[j]
        else:
            proj = _inproj(x, cd_w_in[j].astype(BF16))
            mq, mk, mv, gates, qd, kd = _cd_prep(proj, positions, cd_conv_w[j], cd_conv_b[j], cd_w_q[j],
                                                 cd_w_k[j], cd_w_v[j], cd_w_if[j], cd_b_if[j])
            ya = _mlstm(mq, mk, mv, gates, proj)
            yb = _dsw_attention(qd, kd, proj)
            time_major, w_out = False, cd_w_out[j]
        xn, x_tiles, route, counts = _outproj_ln_router(
            ya, yb, time_major, w_out.astype(BF16), x, ln_mix_g[layer], ln_mix_b[layer],
            moe_router_w[layer], moe_router_b[layer])
        x = _moe_ffn_ln(xn.reshape(t, d), x_tiles, route.reshape(t, LANES), counts, layer, moe_w_in, moe_b_in,
                        moe_w_out, moe_b_out, ln_ffn_g[layer], ln_ffn_b[layer]).reshape(bsz, seq, d)
    return x
```

```python
import functools

import jax
import jax.numpy as jnp
from jax import lax
from jax.experimental import pallas as pl
from jax.experimental.pallas import tpu as pltpu

F32 = jnp.float32
BF16 = jnp.bfloat16

D_MODEL = 1024
DEPTH = 2
A_WIDTH = 512
B_WIDTH = 512
C_WIDTH = 512
D_WIDTH = 512
LRU_HEADS = 8
LRU_HEAD_DIM = A_WIDTH // LRU_HEADS
LRU_C = 8.0
CONV_WIDTH = 4
S5_GROUP_CH = 16
S5_GROUPS = B_WIDTH // S5_GROUP_CH
S5_STATE = 64
MLSTM_HEADS = 4
MLSTM_HEAD_DIM = C_WIDTH // MLSTM_HEADS
DSW_HEADS = 4
DSW_HEAD_DIM = D_WIDTH // DSW_HEADS
DSW_CONFIGS = ((128, 1), (512, 4), (2048, 16))
DSW_BLOCK = 128
ROPE_THETA = 500000.0
ROPE_DIMS = DSW_HEAD_DIM // 4
N_EXPERTS = 32
TOP_K = 4
D_EXPERT = D_MODEL
SWIGLU_LIMIT = 7.0
SWIGLU_ALPHA = 1.702
DEEPNORM_ALPHA = (2 * DEPTH) ** 0.25
LN_EPS = 1e-5

VMEM_LIMIT_BYTES = 56 * 1024 * 1024
LANES = 128
TILE_ROWS = D_MODEL // LANES
MOE_TM = 512
MOE_TK = 256
PROJ_TL = 512
ROUTE_TS = 128
LRU_TC = 128
S5_TC = 64
S5_HALF = 2
MLSTM_CS = 256
NEG_INF = float("-inf")


def _cparams(*sem):
    return pltpu.CompilerParams(dimension_semantics=sem, vmem_limit_bytes=VMEM_LIMIT_BYTES)


def _dot(a, b):
    return jnp.dot(a, b, preferred_element_type=F32)


def _dot_nt(a, b):
    return lax.dot_general(a, b, (((1,), (1,)), ((), ())), preferred_element_type=F32)


def _matmul_kernel(x_ref, w_ref, o_ref):
    o_ref[...] = _dot(x_ref[...].astype(BF16), w_ref[...])


def _inproj_time_major(x3, w_bf16):
    bsz, seq, k = x3.shape
    n = w_bf16.shape[1]
    out = pl.pallas_call(
        _matmul_kernel,
        grid=(bsz, seq // PROJ_TL),
        in_specs=[pl.BlockSpec((None, PROJ_TL, k), lambda b, i: (b, i, 0)),
                  pl.BlockSpec((k, n), lambda b, i: (0, 0))],
        out_specs=pl.BlockSpec((PROJ_TL, n), lambda b, i: (i, b)),
        out_shape=jax.ShapeDtypeStruct((seq, bsz * n), F32),
        compiler_params=_cparams("arbitrary", "arbitrary"),
        name="in_proj_time_major",
    )(x3, w_bf16)
    return out.reshape(seq, bsz, n)


def _inproj(x3, w_bf16):
    bsz, seq, k = x3.shape
    n = w_bf16.shape[1]
    return pl.pallas_call(
        _matmul_kernel,
        grid=(bsz, seq // PROJ_TL),
        in_specs=[pl.BlockSpec((None, PROJ_TL, k), lambda b, i: (b, i, 0)),
                  pl.BlockSpec((k, n), lambda b, i: (0, 0))],
        out_specs=pl.BlockSpec((None, PROJ_TL, n), lambda b, i: (b, i, 0)),
        out_shape=jax.ShapeDtypeStruct((bsz, seq, n), F32),
        compiler_params=_cparams("arbitrary", "arbitrary"),
        name="in_proj",
    )(x3, w_bf16)


def _layer_norm_rows(z, g, b):
    mu = jnp.mean(z, axis=-1, keepdims=True)
    zc = z - mu
    var = jnp.mean(zc * zc, axis=-1, keepdims=True)
    return zc * lax.rsqrt(var + LN_EPS) * g + b


def _outproj_ln_router_kernel(ya_ref, yb_ref, wa_ref, wb_ref, x_ref, g_ref, b_ref, rwh_ref, rwl_ref, rb_ref,
                              tri_ref, xn_ref, xt_ref, rt_ref, cnt_ref):
    @pl.when((pl.program_id(0) == 0) & (pl.program_id(1) == 0))
    def _init():
        cnt_ref[...] = jnp.zeros_like(cnt_ref)

    mix = _dot(ya_ref[...], wa_ref[...]) + _dot(yb_ref[...], wb_ref[...])
    xn = _layer_norm_rows(DEEPNORM_ALPHA * x_ref[...] + mix, g_ref[...], b_ref[...])
    xn_ref[...] = xn
    _store_token_tiles(xt_ref, xn)
    xh = xn.astype(BF16)
    xl = (xn - xh.astype(F32)).astype(BF16)
    logits = (_dot(xh, rwh_ref[...]) + _dot(xl, rwh_ref[...]) + _dot(xh, rwl_ref[...])) + rb_ref[...]
    ts = tri_ref.shape[0]
    lane = lax.broadcasted_iota(jnp.int32, (ts, LANES), 1).astype(F32)
    counts = cnt_ref[...]
    for s in range(logits.shape[0] // ts):
        lg = logits[s * ts:(s + 1) * ts]
        tops, ids, hots = [], [], []
        for _ in range(TOP_K):
            m = jnp.max(lg, axis=1, keepdims=True)
            idx = jnp.min(jnp.where(lg == m, lane, float(LANES)), axis=1, keepdims=True)
            hot = lane == idx
            lg = jnp.where(hot, NEG_INF, lg)
            tops.append(m)
            ids.append(idx)
            hots.append(hot)
        exps = [jnp.exp(m - tops[0]) for m in tops]
        denom = functools.reduce(jnp.add, exps)
        chosen = functools.reduce(jnp.add, [h.astype(F32) for h in hots])
        before = _dot(tri_ref[...], chosen.astype(BF16)) + counts
        route = jnp.zeros((ts, LANES), F32)
        for k in range(TOP_K):
            rank = jnp.sum(jnp.where(hots[k], before, 0.0), axis=1, keepdims=True)
            route = jnp.where(lane == float(k), ids[k], route)
            route = jnp.where(lane == float(TOP_K + k), rank, route)
            route = jnp.where(lane == float(2 * TOP_K + k), exps[k] / denom, route)
        rt_ref[s * ts:(s + 1) * ts, :] = route
        counts = counts + jnp.sum(chosen, axis=0, keepdims=True)
    cnt_ref[...] = counts


def _outproj_ln_router(ya, yb, time_major, w_bf16, x3, g, b, rw, rb):
    bsz, seq, d = x3.shape
    wa, wb = w_bf16[:ya.shape[-1]], w_bf16[ya.shape[-1]:]
    ne = LANES
    rw = jnp.zeros((d, ne), F32).at[:, :N_EXPERTS].set(rw)
    rb = jnp.full((ne,), NEG_INF, F32).at[:N_EXPERTS].set(rb)
    rw_hi = rw.astype(BF16)
    rw_lo = (rw - rw_hi.astype(F32)).astype(BF16)
    tl = PROJ_TL
    tri = jnp.tril(jnp.ones((ROUTE_TS, ROUTE_TS), BF16), -1)
    if time_major:
        ya, yb = ya.reshape(seq, -1), yb.reshape(seq, -1)
        y_specs = [pl.BlockSpec((tl, wa.shape[0]), lambda bi, i: (i, bi)),
                   pl.BlockSpec((tl, wb.shape[0]), lambda bi, i: (i, bi))]
    else:
        y_specs = [pl.BlockSpec((None, tl, wa.shape[0]), lambda bi, i: (bi, i, 0)),
                   pl.BlockSpec((None, tl, wb.shape[0]), lambda bi, i: (bi, i, 0))]
    fixed = lambda bi, i: (0, 0)
    tok = lambda bi, i: (bi, i, 0)
    return pl.pallas_call(
        _outproj_ln_router_kernel,
        grid=(bsz, seq // tl),
        in_specs=y_specs + [pl.BlockSpec(wa.shape, fixed), pl.BlockSpec(wb.shape, fixed),
                            pl.BlockSpec((None, tl, d), tok),
                            pl.BlockSpec((1, d), fixed), pl.BlockSpec((1, d), fixed),
                            pl.BlockSpec((d, ne), fixed), pl.BlockSpec((d, ne), fixed),
                            pl.BlockSpec((1, ne), fixed), pl.BlockSpec((ROUTE_TS, ROUTE_TS), fixed)],
        out_specs=[pl.BlockSpec((None, tl, d), tok),
                   pl.BlockSpec((tl * TILE_ROWS, LANES), lambda bi, i: (bi * (seq // tl) + i, 0)),
                   pl.BlockSpec((None, tl, ne), tok), pl.BlockSpec((1, ne), fixed)],
        out_shape=[jax.ShapeDtypeStruct((bsz, seq, d), F32),
                   jax.ShapeDtypeStruct((bsz * seq * TILE_ROWS, LANES), F32),
                   jax.ShapeDtypeStruct((bsz, seq, ne), F32), jax.ShapeDtypeStruct((1, ne), F32)],
        compiler_params=_cparams("arbitrary", "arbitrary"),
        name="out_proj_ln_router",
    )(ya, yb, wa, wb, x3, g.reshape(1, d), b.reshape(1, d), rw_hi, rw_lo, rb.reshape(1, ne), tri)


def _store_token_tiles(dst_ref, x):
    n = x.shape[0]
    for j in range(TILE_ROWS):
        dst_ref[pl.ds(j, n, stride=TILE_ROWS), :] = x[:, j * LANES:(j + 1) * LANES]


def _load_token_tiles(src_ref, n):
    return jnp.concatenate([src_ref[pl.ds(j, n, stride=TILE_ROWS), :] for j in range(TILE_ROWS)], axis=1)


def _gather_token_tiles(src_hbm, idx_ref, n, dst, sem):
    def body(g, carry):
        base = pl.multiple_of(g * (TILE_ROWS * TILE_ROWS), TILE_ROWS * TILE_ROWS)
        for u in range(TILE_ROWS):
            src_row = pl.multiple_of(idx_ref[0, g * TILE_ROWS + u], TILE_ROWS)
            pltpu.make_async_copy(src_hbm.at[pl.ds(src_row, TILE_ROWS)],
                                  dst.at[pl.ds(base + u * TILE_ROWS, TILE_ROWS)], sem).start()
        return carry
    lax.fori_loop(0, n // TILE_ROWS, body, 0, unroll=2)


def _moe_kernel(be_ref, nreal_ref, rt_ref, rtn_ref, x_hbm, rg_ref, win_ref, bin_ref, wout_ref, bout_ref, o_ref,
                win_bf, wout_bf, xbuf, sem):
    i = pl.program_id(0)
    n_real = nreal_ref[0]
    e = be_ref[i]
    prev = be_ref[jnp.maximum(i - 1, 0)]
    tm = xbuf.shape[1] // TILE_ROWS
    slot = lax.rem(i, 2)

    @pl.when(i == 0)
    def _first_gather():
        _gather_token_tiles(x_hbm, rt_ref, tm, xbuf.at[0], sem.at[0])

    @pl.when(i + 1 < n_real)
    def _next_gather():
        _gather_token_tiles(x_hbm, rtn_ref, tm, xbuf.at[1 - slot], sem.at[1 - slot])

    @pl.when((i == 0) | (e != prev))
    def _load_expert():
        win_bf[...] = win_ref[...].astype(BF16)
        wout_bf[...] = wout_ref[...].astype(BF16)

    @pl.when((i < n_real) | (i == 0))
    def _compute():
        pltpu.make_async_copy(xbuf.at[slot], xbuf.at[slot], sem.at[slot]).wait()
        x = _load_token_tiles(xbuf.at[slot], tm).astype(BF16)
        h = _dot(x, win_bf[...]) + bin_ref[...]
        g = jnp.minimum(h[:, :D_EXPERT], SWIGLU_LIMIT)
        lin = jnp.clip(h[:, D_EXPERT:], -SWIGLU_LIMIT, SWIGLU_LIMIT)
        y = g * jax.nn.sigmoid(SWIGLU_ALPHA * g) * (lin + 1.0)
        out = _dot(y.astype(BF16), wout_bf[...]) + bout_ref[...]
        gates = rg_ref[...].T
        out = jnp.concatenate([out[j * LANES:(j + 1) * LANES] * gates[:, j:j + 1] for j in range(tm // LANES)],
                              axis=0)
        _store_token_tiles(o_ref, out)

    @pl.when((i >= n_real) & (i > 0))
    def _unused_block():
        o_ref[...] = jnp.zeros_like(o_ref)


def _moe_experts(x_tiles, row_tok, row_gate, block_expert, n_real, layer, w_in, b_in, w_out, b_out):
    d = D_MODEL
    tm = MOE_TM
    n_blocks = row_tok.shape[0]
    nl, ne, _, dh2 = w_in.shape
    rt = (row_tok * TILE_ROWS).reshape(n_blocks, 1, tm)
    rg = jnp.pad(row_gate.reshape(n_blocks, tm // LANES, LANES), ((0, 0), (0, TILE_ROWS - tm // LANES), (0, 0)))
    smem_rows = lambda imap: pl.BlockSpec((None, 1, tm), imap, memory_space=pltpu.SMEM)
    grid_spec = pltpu.PrefetchScalarGridSpec(
        num_scalar_prefetch=2,
        grid=(n_blocks,),
        in_specs=[
            smem_rows(lambda i, be, nr: (i, 0, 0)),
            smem_rows(lambda i, be, nr: (jnp.minimum(i + 1, n_blocks - 1), 0, 0)),
            pl.BlockSpec(memory_space=pl.ANY),
            pl.BlockSpec((None, TILE_ROWS, LANES), lambda i, be, nr: (i, 0, 0)),
            pl.BlockSpec((None, None, d, dh2), lambda i, be, nr: (layer, be[i], 0, 0)),
            pl.BlockSpec((None, None, 1, dh2), lambda i, be, nr: (layer, be[i], 0, 0)),
            pl.BlockSpec((None, None, D_EXPERT, d), lambda i, be, nr: (layer, be[i], 0, 0)),
            pl.BlockSpec((None, None, 1, d), lambda i, be, nr: (layer, be[i], 0, 0)),
        ],
        out_specs=pl.BlockSpec((tm * TILE_ROWS, LANES), lambda i, be, nr: (i, 0)),
        scratch_shapes=[pltpu.VMEM((d, dh2), BF16), pltpu.VMEM((D_EXPERT, d), BF16),
                        pltpu.VMEM((2, tm * TILE_ROWS, LANES), F32), pltpu.SemaphoreType.DMA((2,))],
    )
    return pl.pallas_call(
        _moe_kernel,
        grid_spec=grid_spec,
        out_shape=jax.ShapeDtypeStruct((n_blocks * tm * TILE_ROWS, LANES), F32),
        compiler_params=_cparams("arbitrary"),
        name="moe_experts",
    )(block_expert, n_real, rt, rt, x_tiles, rg, w_in,
      b_in.reshape(nl, ne, 1, dh2), w_out, b_out.reshape(nl, ne, 1, d))


def _moe_combine_kernel(dc_ref, dn_ref, x_ref, yb_hbm, g_ref, b_ref, o_ref, buf, ffn_ref, sem):
    i = pl.program_id(0)
    n = pl.num_programs(0)
    tk = x_ref.shape[0]
    rows = tk * TILE_ROWS
    slot = lax.rem(i, 2)

    @pl.when(i == 0)
    def _first_gather():
        _gather_token_tiles(yb_hbm, dc_ref, TOP_K * tk, buf.at[0], sem.at[0])

    @pl.when(i + 1 < n)
    def _next_gather():
        _gather_token_tiles(yb_hbm, dn_ref, TOP_K * tk, buf.at[1 - slot], sem.at[1 - slot])

    pltpu.make_async_copy(buf.at[slot], buf.at[slot], sem.at[slot]).wait()
    ffn = buf[slot, 0:rows]
    for k in range(1, TOP_K):
        ffn = ffn + buf[slot, k * rows:(k + 1) * rows]
    ffn_ref[...] = ffn
    o_ref[...] = _layer_norm_rows(DEEPNORM_ALPHA * x_ref[...] + _load_token_tiles(ffn_ref, tk),
                                  g_ref[...], b_ref[...])


def _moe_combine_ln(xn, dest, yb_tiles, g, b):
    t, d = xn.shape
    tk = MOE_TK
    n = t // tk
    dest_tiles = (dest * TILE_ROWS).reshape(n, tk, TOP_K).transpose(0, 2, 1).reshape(n, 1, TOP_K * tk)
    smem_rows = lambda imap: pl.BlockSpec((None, 1, TOP_K * tk), imap, memory_space=pltpu.SMEM)
    row = lambda i: (i, 0)
    fixed = lambda i: (0, 0)
    return pl.pallas_call(
        _moe_combine_kernel,
        grid=(n,),
        in_specs=[smem_rows(lambda i: (i, 0, 0)), smem_rows(lambda i: (jnp.minimum(i + 1, n - 1), 0, 0)),
                  pl.BlockSpec((tk, d), row), pl.BlockSpec(memory_space=pl.ANY),
                  pl.BlockSpec((1, d), fixed), pl.BlockSpec((1, d), fixed)],
        out_specs=pl.BlockSpec((tk, d), row),
        out_shape=jax.ShapeDtypeStruct((t, d), F32),
        scratch_shapes=[pltpu.VMEM((2, TOP_K * tk * TILE_ROWS, LANES), F32),
                        pltpu.VMEM((tk * TILE_ROWS, LANES), F32), pltpu.SemaphoreType.DMA((2,))],
        compiler_params=_cparams("arbitrary"),
        name="moe_combine_ln",
    )(dest_tiles, dest_tiles, xn, yb_tiles, g.reshape(1, d), b.reshape(1, d))


def _moe_ffn_ln(xn, x_tiles, route, counts, layer, w_in, b_in, w_out, b_out, g, b):
    t, d = xn.shape
    tm = MOE_TM
    flat_e = route[:, :TOP_K].astype(jnp.int32).reshape(-1)
    rank = route[:, TOP_K:2 * TOP_K].astype(jnp.int32).reshape(-1)
    gate = route[:, 2 * TOP_K:3 * TOP_K].reshape(-1)
    counts = counts[0, :N_EXPERTS].astype(jnp.int32)
    blocks_per_e = (counts + tm - 1) // tm
    blk_end = jnp.cumsum(blocks_per_e)
    blk_start = blk_end - blocks_per_e
    row_start = jnp.cumsum(counts) - counts
    dest = blk_start[flat_e] * tm + rank
    n_blocks = (t * TOP_K) // tm + N_EXPERTS
    block_expert = jnp.minimum(
        jnp.sum((blk_end[None, :] <= jnp.arange(n_blocks, dtype=jnp.int32)[:, None]).astype(jnp.int32), axis=1),
        N_EXPERTS - 1)
    n_real = blk_end[-1:].astype(jnp.int32)
    order = jnp.argsort(flat_e, stable=True).astype(jnp.int32)
    blk = jnp.arange(n_blocks, dtype=jnp.int32)
    rank_row = ((blk - blk_start[block_expert]) * tm)[:, None] + jnp.arange(tm, dtype=jnp.int32)[None, :]
    valid = rank_row < counts[block_expert][:, None]
    pair = order[jnp.clip(row_start[block_expert][:, None] + rank_row, 0, t * TOP_K - 1)]
    row_tok = jnp.where(valid, pair // TOP_K, 0)
    row_gate = jnp.where(valid, gate[pair], 0.0)
    yb_tiles = _moe_experts(x_tiles, row_tok, row_gate, block_expert, n_real, layer, w_in, b_in, w_out, b_out)
    return _moe_combine_ln(xn, dest, yb_tiles, g, b)


def _softplus(x):
    return jnp.maximum(x, 0.0) + jnp.log(1.0 + jnp.exp(-jnp.abs(x)))


def _gelu_tanh(x):
    return 0.5 * x * (1.0 + jnp.tanh(0.7978845608028654 * (x + 0.044715 * (x * x * x))))


def _lru_kernel(xa_ref, ga_ref, cw_ref, cb_ref, wr_ref, br_ref, wi_ref, bi_ref, lam_ref, o_ref,
                tail_ref, h_ref, a_ref, u_ref):
    tc, bsz, w = xa_ref.shape

    @pl.when(pl.program_id(0) == 0)
    def _init():
        tail_ref[...] = jnp.zeros_like(tail_ref)
        h_ref[...] = jnp.zeros_like(h_ref)

    xa = xa_ref[...]
    ext = jnp.concatenate([tail_ref[...], xa], axis=0)
    xc = cb_ref[...] + ext[0:tc] * cw_ref[0]
    for tap in range(1, CONV_WIDTH):
        xc = xc + ext[tap:tap + tc] * cw_ref[tap]
    tail_ref[...] = xa[tc - (CONV_WIDTH - 1):]
    x2 = xc.reshape(tc * bsz, w)
    xb = x2.astype(BF16)
    r = jax.nn.sigmoid(_dot(xb, wr_ref[...]) + br_ref[...])
    ig = jax.nn.sigmoid(_dot(xb, wi_ref[...]) + bi_ref[...])
    log_a = (-LRU_C) * r * _softplus(-lam_ref[...])
    a_ref[...] = jnp.exp(log_a).reshape(tc, bsz, w)
    u_ref[...] = (jnp.sqrt(1.0 - jnp.exp(2.0 * log_a)) * (ig * x2)).reshape(tc, bsz, w)

    def step(t, h):
        h = a_ref[t] * h + u_ref[t]
        u_ref[t] = h
        return h

    h_ref[...] = lax.fori_loop(0, tc, step, h_ref[...], unroll=8)
    o_ref[...] = (u_ref[...] * _gelu_tanh(ga_ref[...])).astype(o_ref.dtype)


def _block_diag(w):
    h, i, j = w.shape
    return jnp.einsum('hij,hk->hikj', w, jnp.eye(h, dtype=w.dtype)).reshape(h * i, h * j)


def _rg_lru(proj, conv_w, conv_b, gate_r_w, gate_r_b, gate_i_w, gate_i_b, lru_lambda):
    seq, bsz, _ = proj.shape
    w = A_WIDTH
    tc = LRU_TC
    fixed2 = lambda i: (0, 0)
    fixed3 = lambda i: (0, 0, 0)
    return pl.pallas_call(
        _lru_kernel,
        grid=(seq // tc,),
        in_specs=[pl.BlockSpec((tc, bsz, w), lambda i: (i, 0, 0)),
                  pl.BlockSpec((tc, bsz, w), lambda i: (i, 0, 1)),
                  pl.BlockSpec((CONV_WIDTH, 1, w), fixed3), pl.BlockSpec((1, w), fixed2),
                  pl.BlockSpec((w, w), fixed2), pl.BlockSpec((1, w), fixed2),
                  pl.BlockSpec((w, w), fixed2), pl.BlockSpec((1, w), fixed2),
                  pl.BlockSpec((1, w), fixed2)],
        out_specs=pl.BlockSpec((tc, bsz, w), lambda i: (i, 0, 0)),
        out_shape=jax.ShapeDtypeStruct((seq, bsz, w), BF16),
        scratch_shapes=[pltpu.VMEM((CONV_WIDTH - 1, bsz, w), F32), pltpu.VMEM((bsz, w), F32),
                        pltpu.VMEM((tc, bsz, w), F32), pltpu.VMEM((tc, bsz, w), F32)],
        compiler_params=_cparams("arbitrary"),
        name="rg_lru",
    )(proj, proj, conv_w.reshape(CONV_WIDTH, 1, w), conv_b.reshape(1, w),
      _block_diag(gate_r_w).astype(BF16), gate_r_b.reshape(1, w),
      _block_diag(gate_i_w).astype(BF16), gate_i_b.reshape(1, w), lru_lambda.reshape(1, w))


def _s5_kernel(u_ref, wbr_ref, wbi_ref, ar_ref, ai_ref, ccr_ref, cci_ref, d_ref, gw_ref, gb_ref, o_ref,
               xr_ref, xi_ref, sr_ref, si_ref):
    tc, bsz, w = u_ref.shape
    nstate = xr_ref.shape[-1]
    sh = nstate // S5_HALF
    wh = w // S5_HALF

    @pl.when(pl.program_id(0) == 0)
    def _init():
        sr_ref[...] = jnp.zeros_like(sr_ref)
        si_ref[...] = jnp.zeros_like(si_ref)

    u2 = u_ref[...].reshape(tc * bsz, w)
    ub = u2.astype(BF16)
    for hf in range(S5_HALF):
        uh = ub[:, hf * wh:(hf + 1) * wh]
        xr_ref[:, :, hf * sh:(hf + 1) * sh] = _dot(uh, wbr_ref[hf]).reshape(tc, bsz, sh)
        xi_ref[:, :, hf * sh:(hf + 1) * sh] = _dot(uh, wbi_ref[hf]).reshape(tc, bsz, sh)

    for hf in range(S5_HALF):
        lo, hi = hf * sh, (hf + 1) * sh
        ar = jnp.broadcast_to(ar_ref[:, lo:hi], (bsz, sh))
        ai = jnp.broadcast_to(ai_ref[:, lo:hi], (bsz, sh))

        def step(t, carry, lo=lo, hi=hi, ar=ar, ai=ai):
            xr, xi = carry
            nxr = ar * xr - ai * xi + xr_ref[t, :, lo:hi]
            nxi = ar * xi + ai * xr + xi_ref[t, :, lo:hi]
            xr_ref[t, :, lo:hi] = nxr
            xi_ref[t, :, lo:hi] = nxi
            return nxr, nxi

        xr, xi = lax.fori_loop(0, tc, step, (sr_ref[:, lo:hi], si_ref[:, lo:hi]), unroll=4)
        sr_ref[:, lo:hi] = xr
        si_ref[:, lo:hi] = xi

    ys = []
    for hf in range(S5_HALF):
        lo, hi = hf * sh, (hf + 1) * sh
        xrh = xr_ref[:, :, lo:hi].reshape(tc * bsz, sh).astype(BF16)
        xih = xi_ref[:, :, lo:hi].reshape(tc * bsz, sh).astype(BF16)
        ys.append(_dot(xrh, ccr_ref[hf]) - _dot(xih, cci_ref[hf]))
    y = jnp.concatenate(ys, axis=1) + d_ref[...] * u2
    s = _gelu_tanh(y)
    yb = s * jax.nn.sigmoid(_dot(s.astype(BF16), gw_ref[...]) + gb_ref[...])
    o_ref[...] = yb.reshape(tc, bsz, w).astype(o_ref.dtype)


def _s5_glu(proj, lam_re, lam_im, log_step, b_re, b_im, c_re, c_im, d_skip, glu_w, glu_b):
    seq, bsz, _ = proj.shape
    w = B_WIDTH
    tc = S5_TC
    nstate = S5_GROUPS * S5_STATE
    gh = S5_GROUPS // S5_HALF
    lr = jnp.minimum(lam_re, -1e-4)
    li = lam_im
    step = jnp.exp(log_step)[:, None]
    mag = jnp.exp(lr * step)
    ar = mag * jnp.cos(li * step)
    ai = mag * jnp.sin(li * step)
    inv = 1.0 / (lr * lr + li * li)
    zr = ((ar - 1.0) * lr + ai * li) * inv
    zi = (ai * lr - (ar - 1.0) * li) * inv
    bbr = zr[..., None] * b_re - zi[..., None] * b_im
    bbi = zr[..., None] * b_im + zi[..., None] * b_re
    eye = jnp.eye(gh, dtype=F32)

    def expand_in(bb):
        bb = bb.reshape(S5_HALF, gh, S5_STATE, S5_GROUP_CH)
        return jnp.einsum('fgph,gk->fghkp', bb, eye).reshape(
            S5_HALF, gh * S5_GROUP_CH, gh * S5_STATE).astype(BF16)

    def expand_out(c):
        c = c.reshape(S5_HALF, gh, S5_GROUP_CH, S5_STATE)
        return jnp.einsum('fghp,gk->fgpkh', c, eye).reshape(
            S5_HALF, gh * S5_STATE, gh * S5_GROUP_CH).astype(BF16)

    fixed2 = lambda i: (0, 0)
    fixed3 = lambda i: (0, 0, 0)
    return pl.pallas_call(
        _s5_kernel,
        grid=(seq // tc,),
        in_specs=[pl.BlockSpec((tc, bsz, w), lambda i: (i, 0, 2)),
                  pl.BlockSpec((S5_HALF, w // S5_HALF, nstate // S5_HALF), fixed3),
                  pl.BlockSpec((S5_HALF, w // S5_HALF, nstate // S5_HALF), fixed3),
                  pl.BlockSpec((1, nstate), fixed2), pl.BlockSpec((1, nstate), fixed2),
                  pl.BlockSpec((S5_HALF, nstate // S5_HALF, w // S5_HALF), fixed3),
                  pl.BlockSpec((S5_HALF, nstate // S5_HALF, w // S5_HALF), fixed3),
                  pl.BlockSpec((1, w), fixed2), pl.BlockSpec((w, w), fixed2), pl.BlockSpec((1, w), fixed2)],
        out_specs=pl.BlockSpec((tc, bsz, w), lambda i: (i, 0, 0)),
        out_shape=jax.ShapeDtypeStruct((seq, bsz, w), BF16),
        scratch_shapes=[pltpu.VMEM((tc, bsz, nstate), F32), pltpu.VMEM((tc, bsz, nstate), F32),
                        pltpu.VMEM((bsz, nstate), F32), pltpu.VMEM((bsz, nstate), F32)],
        compiler_params=_cparams("arbitrary"),
        name="s5_glu",
    )(proj, expand_in(bbr), expand_in(bbi), ar.reshape(1, nstate), ai.reshape(1, nstate),
      expand_out(c_re), expand_out(c_im), d_skip.reshape(1, w), glu_w.astype(BF16), glu_b.reshape(1, w))


def _cd_prep_kernel(xc_ref, q_ref, k_ref, pos_ref, cw_ref, cb_ref, wq_ref, wk_ref, wv_ref, wif_ref,
                    bif_ref, invf_ref, mq_ref, mk_ref, mv_ref, gt_ref, qd_ref, kd_ref, tail_ref):
    tl, w = xc_ref.shape
    pad = tail_ref.shape[0]

    @pl.when(pl.program_id(1) == 0)
    def _init():
        tail_ref[...] = jnp.zeros_like(tail_ref)

    xc = xc_ref[...]
    ext = jnp.concatenate([tail_ref[...], xc], axis=0)
    conv = cb_ref[...]
    for tap in range(CONV_WIDTH):
        off = pad - (CONV_WIDTH - 1) + tap
        conv = conv + ext[off:off + tl] * cw_ref[tap]
    tail_ref[...] = xc[tl - pad:]
    xconv = (conv * jax.nn.sigmoid(conv)).astype(BF16)
    q = _dot(xconv, wq_ref[...]).astype(BF16)
    k = (_dot(xconv, wk_ref[...]) * (MLSTM_HEAD_DIM ** -0.5)).astype(BF16)
    v = _dot(xc.astype(BF16), wv_ref[...]).astype(BF16)
    mq_ref[...] = q
    mk_ref[...] = k
    mv_ref[...] = v
    gt_ref[...] = (_dot(q, wif_ref[0]) + _dot(k, wif_ref[1]) + _dot(v, wif_ref[2]) + bif_ref[...])

    ang = pos_ref[...].astype(F32) * invf_ref[...]
    cos = jnp.cos(ang)
    sin = jnp.sin(ang)
    lane = lax.broadcasted_iota(jnp.int32, (1, LANES), 1)
    half = ROPE_DIMS // 2
    sin_lo = jnp.where(lane < half, -sin, 0.0)
    sin_hi = jnp.where((lane >= half) & (lane < ROPE_DIMS), sin, 0.0)
    for src, dst in ((q_ref, qd_ref), (k_ref, kd_ref)):
        for h in range(DSW_HEADS):
            sl = slice(h * DSW_HEAD_DIM, (h + 1) * DSW_HEAD_DIM)
            t = src[:, sl]
            rot = (t * cos + pltpu.roll(t, LANES - half, axis=1) * sin_lo
                   + pltpu.roll(t, half, axis=1) * sin_hi)
            dst[:, sl] = rot


def _cd_prep(proj, positions, conv_w, conv_b, w_q, w_k, w_v, w_if, b_if):
    bsz, seq, _ = proj.shape
    w = C_WIDTH
    tl = PROJ_TL
    half = ROPE_DIMS // 2
    inv_freq = ROPE_THETA ** (-jnp.arange(half, dtype=F32) / half)
    invf = jnp.zeros((1, LANES), F32).at[0, :half].set(inv_freq).at[0, half:ROPE_DIMS].set(inv_freq)
    wif = jnp.zeros((3, w, LANES), F32).at[:, :, :2 * MLSTM_HEADS].set(w_if.reshape(3, w, 2 * MLSTM_HEADS))
    bif = jnp.zeros((1, LANES), F32).at[0, :2 * MLSTM_HEADS].set(b_if)
    col = lambda c: (lambda b, i: (b, i, c))
    fixed2 = lambda b, i: (0, 0)
    fixed3 = lambda b, i: (0, 0, 0)
    tok_bf = jax.ShapeDtypeStruct((bsz, seq, w), BF16)
    return pl.pallas_call(
        _cd_prep_kernel,
        grid=(bsz, seq // tl),
        in_specs=[pl.BlockSpec((None, tl, w), col(0)), pl.BlockSpec((None, tl, w), col(2)),
                  pl.BlockSpec((None, tl, w), col(3)), pl.BlockSpec((None, tl, 1), col(0)),
                  pl.BlockSpec((CONV_WIDTH, 1, w), fixed3), pl.BlockSpec((1, w), fixed2),
                  pl.BlockSpec((w, w), fixed2), pl.BlockSpec((w, w), fixed2), pl.BlockSpec((w, w), fixed2),
                  pl.BlockSpec((3, w, LANES), fixed3), pl.BlockSpec((1, LANES), fixed2),
                  pl.BlockSpec((1, LANES), fixed2)],
        out_specs=[pl.BlockSpec((None, tl, w), col(0))] * 3 + [pl.BlockSpec((None, tl, LANES), col(0))]
                  + [pl.BlockSpec((None, tl, w), col(0))] * 2,
        out_shape=[tok_bf, tok_bf, tok_bf, jax.ShapeDtypeStruct((bsz, seq, LANES), F32),
                   jax.ShapeDtypeStruct((bsz, seq, w), F32), jax.ShapeDtypeStruct((bsz, seq, w), F32)],
        scratch_shapes=[pltpu.VMEM((8, w), F32)],
        compiler_params=_cparams("arbitrary", "arbitrary"),
        name="cd_prep",
    )(proj, proj, proj, positions.reshape(bsz, seq, 1), conv_w.reshape(CONV_WIDTH, 1, w),
      conv_b.reshape(1, w), _block_diag(w_q).astype(BF16), _block_diag(w_k).astype(BF16),
      _block_diag(w_v).astype(BF16), wif.astype(BF16), bif, invf)


def _cumsum_rows(x):
    n = x.shape[0]
    row = lax.broadcasted_iota(jnp.int32, x.shape, 0)
    shift = 1
    while shift < n:
        x = x + jnp.where(row >= shift, pltpu.roll(x, shift, axis=0), 0.0)
        shift *= 2
    return x


def _mlstm_kernel(q_ref, k_ref, v_ref, g_ref, op_ref, o_ref, c_ref, n_ref, m_ref):
    cs = q_ref.shape[0]
    dh = MLSTM_HEAD_DIM

    @pl.when(pl.program_id(1) == 0)
    def _init():
        c_ref[...] = jnp.zeros_like(c_ref)
        n_ref[...] = jnp.zeros_like(n_ref)
        m_ref[...] = jnp.full_like(m_ref, NEG_INF)

    gates = g_ref[...]
    cum = _cumsum_rows(jax.nn.log_sigmoid(gates))
    gates_t = gates.T
    cum_t = cum.T
    causal = (lax.broadcasted_iota(jnp.int32, (cs, cs), 0) >= lax.broadcasted_iota(jnp.int32, (cs, cs), 1))
    for h in range(MLSTM_HEADS):
        sl = slice(h * dh, (h + 1) * dh)
        fcol = MLSTM_HEADS + h
        li_c, li_r = gates[:, h:h + 1], gates_t[h:h + 1, :]
        cum_c, cum_r = cum[:, fcol:fcol + 1], cum_t[fcol:fcol + 1, :]
        m_prev = m_ref[h][:, 0:1]
        q, k, v = q_ref[:, sl], k_ref[:, sl], v_ref[:, sl]
        log_d = jnp.where(causal, cum_c - cum_r + li_r, NEG_INF)
        log_inter = cum_c + m_prev
        m_s = jnp.maximum(jnp.max(log_d, axis=1, keepdims=True), log_inter)
        s = _dot_nt(q, k) * jnp.exp(log_d - m_s)
        inter = jnp.exp(log_inter - m_s)
        c_prev = c_ref[h]
        n_prev = n_ref[h]
        num = _dot(s.astype(BF16), v) + inter * _dot(q, c_prev.astype(BF16))
        den = (jnp.sum(s, axis=1, keepdims=True)
               + inter * jnp.sum(q.astype(F32) * n_prev, axis=1, keepdims=True))
        hh = num / jnp.maximum(jnp.abs(den), jnp.exp(-m_s))
        o_ref[:, sl] = (jax.nn.sigmoid(op_ref[:, sl]) * hh).astype(o_ref.dtype)
        chunk_f = cum_c[cs - 1:cs, :]
        to_end = chunk_f - cum_c + li_c
        m_c = jnp.max(to_end, axis=0, keepdims=True)
        m_new = jnp.maximum(chunk_f + m_prev, m_c)
        s_old = jnp.exp(chunk_f + m_prev - m_new)
        s_new = jnp.exp(m_c - m_new)
        kw = k.astype(F32) * jnp.exp(to_end - m_c)
        c_ref[h] = s_old * c_prev + s_new * _dot(kw.T.astype(BF16), v)
        n_ref[h] = s_old * n_prev + s_new * jnp.sum(kw, axis=0, keepdims=True)
        m_ref[h] = jnp.broadcast_to(m_new, (1, LANES))


def _mlstm(mq, mk, mv, gates, proj):
    bsz, seq, w = mq.shape
    cs = MLSTM_CS
    tok = lambda b, i: (b, i, 0)
    return pl.pallas_call(
        _mlstm_kernel,
        grid=(bsz, seq // cs),
        in_specs=[pl.BlockSpec((None, cs, w), tok)] * 3
                 + [pl.BlockSpec((None, cs, LANES), tok), pl.BlockSpec((None, cs, w), lambda b, i: (b, i, 1))],
        out_specs=pl.BlockSpec((None, cs, w), tok),
        out_shape=jax.ShapeDtypeStruct((bsz, seq, w), BF16),
        scratch_shapes=[pltpu.VMEM((MLSTM_HEADS, MLSTM_HEAD_DIM, MLSTM_HEAD_DIM), F32),
                        pltpu.VMEM((MLSTM_HEADS, 1, MLSTM_HEAD_DIM), F32),
                        pltpu.VMEM((MLSTM_HEADS, 1, LANES), F32)],
        compiler_params=_cparams("arbitrary", "arbitrary"),
        name="mlstm",
    )(mq, mk, mv, gates, proj)


def _dsw_kernel(q_ref, k_ref, v_ref, y_ref, o_scr, lse_scr):
    seq, dh = q_ref.shape
    blk = DSW_BLOCK
    qi = lax.broadcasted_iota(jnp.int32, (blk, 2 * blk), 0)
    ki = lax.broadcasted_iota(jnp.int32, (blk, 2 * blk), 1)
    dist = qi + blk - ki
    scale = dh ** -0.5
    units = seq // blk
    for g, (window, dil) in enumerate(DSW_CONFIGS):
        nsub = seq // dil // blk
        band = (dist >= 0) & (dist <= window // dil)

        def unit(u, carry, g=g, dil=dil, nsub=nsub, band=band):
            r = u // nsub
            sb = u % nsub
            cur = pl.ds(r + sb * (blk * dil), blk, stride=dil)
            prev = pl.ds(r + jnp.maximum(sb - 1, 0) * (blk * dil), blk, stride=dil)
            kk = jnp.concatenate([k_ref[prev, :], k_ref[cur, :]], axis=0).astype(BF16)
            vv = jnp.concatenate([v_ref[prev, :], v_ref[cur, :]], axis=0).astype(BF16)
            s = _dot_nt(q_ref[cur, :].astype(BF16), kk) * scale
            s = jnp.where(band & ((ki >= blk) | (sb > 0)), s, NEG_INF)
            m = jnp.max(s, axis=1, keepdims=True)
            p = jnp.exp(s - m)
            l = jnp.sum(p, axis=1, keepdims=True)
            o_scr[g, cur, :] = _dot(p.astype(BF16), vv) / l
            lse_scr[g, cur, :] = jnp.broadcast_to(m + jnp.log(l), (blk, dh))
            return carry

        lax.fori_loop(0, units, unit, 0, unroll=8)

    lses = [lse_scr[g] for g in range(len(DSW_CONFIGS))]
    mx = functools.reduce(jnp.maximum, lses)
    ws = [jnp.exp(l - mx) for l in lses]
    acc = sum(ws[g] * o_scr[g] for g in range(len(DSW_CONFIGS)))
    y_ref[...] = (acc / sum(ws)).astype(y_ref.dtype)


def _dsw_attention(qd, kd, proj):
    bsz, seq, w = qd.shape
    dh = DSW_HEAD_DIM
    v_col = (proj.shape[-1] - w) // dh
    head = lambda b, h: (b, 0, h)
    return pl.pallas_call(
        _dsw_kernel,
        grid=(bsz, DSW_HEADS),
        in_specs=[pl.BlockSpec((None, seq, dh), head), pl.BlockSpec((None, seq, dh), head),
                  pl.BlockSpec((None, seq, dh), lambda b, h: (b, 0, v_col + h))],
        out_specs=pl.BlockSpec((None, seq, dh), head),
        out_shape=jax.ShapeDtypeStruct((bsz, seq, w), BF16),
        scratch_shapes=[pltpu.VMEM((len(DSW_CONFIGS), seq, dh), F32),
                        pltpu.VMEM((len(DSW_CONFIGS), seq, dh), F32)],
        compiler_params=_cparams("arbitrary", "arbitrary"),
        name="dsw_attention",
    )(qd, kd, proj)


def kernel(x, positions, ab_w_in, ab_conv_w, ab_conv_b, ab_gate_r_w, ab_gate_r_b, ab_gate_i_w, ab_gate_i_b, ab_lru_lambda, ab_s5_lambda_re, ab_s5_lambda_im, ab_s5_log_step, ab_s5_b_re, ab_s5_b_im, ab_s5_c_re, ab_s5_c_im, ab_s5_d, ab_glu_w, ab_glu_b, ab_w_out, cd_w_in, cd_conv_w, cd_conv_b, cd_w_q, cd_w_k, cd_w_v, cd_w_if, cd_b_if, cd_w_out, ln_mix_g, ln_mix_b, ln_ffn_g, ln_ffn_b, moe_router_w, moe_router_b, moe_w_in, moe_b_in, moe_w_out, moe_b_out):
    bsz, seq, d = x.shape
    t = bsz * seq
    for layer in range(DEPTH):
        j = layer // 2
        if layer % 2 == 0:
            proj = _inproj_time_major(x, ab_w_in[j].astype(BF16))
            ya = _rg_lru(proj, ab_conv_w[j], ab_conv_b[j], ab_gate_r_w[j], ab_gate_r_b[j],
                         ab_gate_i_w[j], ab_gate_i_b[j], ab_lru_lambda[j])
            yb = _s5_glu(proj, ab_s5_lambda_re[j], ab_s5_lambda_im[j], ab_s5_log_step[j], ab_s5_b_re[j],
                         ab_s5_b_im[j], ab_s5_c_re[j], ab_s5_c_im[j], ab_s5_d[j], ab_glu_w[j], ab_glu_b[j])
            time_major, w_out = True, ab_w_out[j]
        else:
            proj = _inproj(x, cd_w_in[j].astype(BF16))
            mq, mk, mv, gates, qd, kd = _cd_prep(proj, positions, cd_conv_w[j], cd_conv_b[j], cd_w_q[j],
                                                 cd_w_k[j], cd_w_v[j], cd_w_if[j], cd_b_if[j])
            ya = _mlstm(mq, mk, mv, gates, proj)
            yb = _dsw_attention(qd, kd, proj)
            time_major, w_out = False, cd_w_out[j]
        xn, x_tiles, route, counts = _outproj_ln_router(
            ya, yb, time_major, w_out.astype(BF16), x, ln_mix_g[layer], ln_mix_b[layer],
            moe_router_w[layer], moe_router_b[layer])
        x = _moe_ffn_ln(xn.reshape(t, d), x_tiles, route.reshape(t, LANES), counts, layer, moe_w_in, moe_b_in,
                        moe_w_out, moe_b_out, ln_ffn_g[layer], ln_ffn_b[layer]).reshape(bsz, seq, d)
    return x
```

```python
import functools

import jax
import jax.numpy as jnp
from jax import lax
from jax.experimental import pallas as pl
from jax.experimental.pallas import tpu as pltpu

F32 = jnp.float32
BF16 = jnp.bfloat16

D_MODEL = 1024
DEPTH = 2
A_WIDTH = 512
B_WIDTH = 512
C_WIDTH = 512
D_WIDTH = 512
LRU_HEADS = 8
LRU_HEAD_DIM = A_WIDTH // LRU_HEADS
LRU_C = 8.0
CONV_WIDTH = 4
S5_GROUP_CH = 16
S5_GROUPS = B_WIDTH // S5_GROUP_CH
S5_STATE = 64
MLSTM_HEADS = 4
MLSTM_HEAD_DIM = C_WIDTH // MLSTM_HEADS
DSW_HEADS = 4
DSW_HEAD_DIM = D_WIDTH // DSW_HEADS
DSW_CONFIGS = ((128, 1), (512, 4), (2048, 16))
DSW_BLOCK = 128
ROPE_THETA = 500000.0
ROPE_DIMS = DSW_HEAD_DIM // 4
N_EXPERTS = 32
TOP_K = 4
D_EXPERT = D_MODEL
SWIGLU_LIMIT = 7.0
SWIGLU_ALPHA = 1.702
DEEPNORM_ALPHA = (2 * DEPTH) ** 0.25
LN_EPS = 1e-5

VMEM_LIMIT_BYTES = 56 * 1024 * 1024
LANES = 128
TILE_ROWS = D_MODEL // LANES
MOE_TM = 512
MOE_TK = 256
PROJ_TL = 512
ROUTE_TS = 128
LRU_TC = 128
S5_TC = 64
S5_HALF = 2
MLSTM_CS = 256
NEG_INF = float("-inf")


def _cparams(*sem):
    return pltpu.CompilerParams(dimension_semantics=sem, vmem_limit_bytes=VMEM_LIMIT_BYTES)


def _dot(a, b):
    return jnp.dot(a, b, preferred_element_type=F32)


def _dot_nt(a, b):
    return lax.dot_general(a, b, (((1,), (1,)), ((), ())), preferred_element_type=F32)


def _matmul_kernel(x_ref, w_ref, o_ref):
    o_ref[...] = _dot(x_ref[...].astype(BF16), w_ref[...])


def _inproj_time_major(x3, w_bf16):
    bsz, seq, k = x3.shape
    n = w_bf16.shape[1]
    out = pl.pallas_call(
        _matmul_kernel,
        grid=(bsz, seq // PROJ_TL),
        in_specs=[pl.BlockSpec((None, PROJ_TL, k), lambda b, i: (b, i, 0)),
                  pl.BlockSpec((k, n), lambda b, i: (0, 0))],
        out_specs=pl.BlockSpec((PROJ_TL, n), lambda b, i: (i, b)),
        out_shape=jax.ShapeDtypeStruct((seq, bsz * n), F32),
        compiler_params=_cparams("arbitrary", "arbitrary"),
        name="in_proj_time_major",
    )(x3, w_bf16)
    return out.reshape(seq, bsz, n)


def _inproj(x3, w_bf16):
    bsz, seq, k = x3.shape
    n = w_bf16.shape[1]
    return pl.pallas_call(
        _matmul_kernel,
        grid=(bsz, seq // PROJ_TL),
        in_specs=[pl.BlockSpec((None, PROJ_TL, k), lambda b, i: (b, i, 0)),
                  pl.BlockSpec((k, n), lambda b, i: (0, 0))],
        out_specs=pl.BlockSpec((None, PROJ_TL, n), lambda b, i: (b, i, 0)),
        out_shape=jax.ShapeDtypeStruct((bsz, seq, n), F32),
        compiler_params=_cparams("arbitrary", "arbitrary"),
        name="in_proj",
    )(x3, w_bf16)


def _layer_norm_rows(z, g, b):
    mu = jnp.mean(z, axis=-1, keepdims=True)
    zc = z - mu
    var = jnp.mean(zc * zc, axis=-1, keepdims=True)
    return zc * lax.rsqrt(var + LN_EPS) * g + b


def _outproj_ln_router_kernel(ya_ref, yb_ref, wa_ref, wb_ref, x_ref, g_ref, b_ref, rwh_ref, rwl_ref, rb_ref,
                              tri_ref, xn_ref, xt_ref, rt_ref, cnt_ref):
    @pl.when((pl.program_id(0) == 0) & (pl.program_id(1) == 0))
    def _init():
        cnt_ref[...] = jnp.zeros_like(cnt_ref)

    mix = _dot(ya_ref[...], wa_ref[...]) + _dot(yb_ref[...], wb_ref[...])
    xn = _layer_norm_rows(DEEPNORM_ALPHA * x_ref[...] + mix, g_ref[...], b_ref[...])
    xn_ref[...] = xn
    _store_token_tiles(xt_ref, xn)
    xh = xn.astype(BF16)
    xl = (xn - xh.astype(F32)).astype(BF16)
    logits = (_dot(xh, rwh_ref[...]) + _dot(xl, rwh_ref[...]) + _dot(xh, rwl_ref[...])) + rb_ref[...]
    ts = tri_ref.shape[0]
    lane = lax.broadcasted_iota(jnp.int32, (ts, LANES), 1).astype(F32)
    counts = cnt_ref[...]
    for s in range(logits.shape[0] // ts):
        lg = logits[s * ts:(s + 1) * ts]
        tops, ids, hots = [], [], []
        for _ in range(TOP_K):
            m = jnp.max(lg, axis=1, keepdims=True)
            idx = jnp.min(jnp.where(lg == m, lane, float(LANES)), axis=1, keepdims=True)
            hot = lane == idx
            lg = jnp.where(hot, NEG_INF, lg)
            tops.append(m)
            ids.append(idx)
            hots.append(hot)
        exps = [jnp.exp(m - tops[0]) for m in tops]
        denom = functools.reduce(jnp.add, exps)
        chosen = functools.reduce(jnp.add, [h.astype(F32) for h in hots])
        before = _dot(tri_ref[...], chosen.astype(BF16)) + counts
        route = jnp.zeros((ts, LANES), F32)
        for k in range(TOP_K):
            rank = jnp.sum(jnp.where(hots[k], before, 0.0), axis=1, keepdims=True)
            route = jnp.where(lane == float(k), ids[k], route)
            route = jnp.where(lane == float(TOP_K + k), rank, route)
            route = jnp.where(lane == float(2 * TOP_K + k), exps[k] / denom, route)
        rt_ref[s * ts:(s + 1) * ts, :] = route
        counts = counts + jnp.sum(chosen, axis=0, keepdims=True)
    cnt_ref[...] = counts


def _outproj_ln_router(ya, yb, time_major, w_bf16, x3, g, b, rw, rb):
    bsz, seq, d = x3.shape
    wa, wb = w_bf16[:ya.shape[-1]], w_bf16[ya.shape[-1]:]
    ne = LANES
    rw = jnp.zeros((d, ne), F32).at[:, :N_EXPERTS].set(rw)
    rb = jnp.full((ne,), NEG_INF, F32).at[:N_EXPERTS].set(rb)
    rw_hi = rw.astype(BF16)
    rw_lo = (rw - rw_hi.astype(F32)).astype(BF16)
    tl = PROJ_TL
    tri = jnp.tril(jnp.ones((ROUTE_TS, ROUTE_TS), BF16), -1)
    if time_major:
        ya, yb = ya.reshape(seq, -1), yb.reshape(seq, -1)
        y_specs = [pl.BlockSpec((tl, wa.shape[0]), lambda bi, i: (i, bi)),
                   pl.BlockSpec((tl, wb.shape[0]), lambda bi, i: (i, bi))]
    else:
        y_specs = [pl.BlockSpec((None, tl, wa.shape[0]), lambda bi, i: (bi, i, 0)),
                   pl.BlockSpec((None, tl, wb.shape[0]), lambda bi, i: (bi, i, 0))]
    fixed = lambda bi, i: (0, 0)
    tok = lambda bi, i: (bi, i, 0)
    return pl.pallas_call(
        _outproj_ln_router_kernel,
        grid=(bsz, seq // tl),
        in_specs=y_specs + [pl.BlockSpec(wa.shape, fixed), pl.BlockSpec(wb.shape, fixed),
                            pl.BlockSpec((None, tl, d), tok),
                            pl.BlockSpec((1, d), fixed), pl.BlockSpec((1, d), fixed),
                            pl.BlockSpec((d, ne), fixed), pl.BlockSpec((d, ne), fixed),
                            pl.BlockSpec((1, ne), fixed), pl.BlockSpec((ROUTE_TS, ROUTE_TS), fixed)],
        out_specs=[pl.BlockSpec((None, tl, d), tok),
                   pl.BlockSpec((tl * TILE_ROWS, LANES), lambda bi, i: (bi * (seq // tl) + i, 0)),
                   pl.BlockSpec((None, tl, ne), tok), pl.BlockSpec((1, ne), fixed)],
        out_shape=[jax.ShapeDtypeStruct((bsz, seq, d), F32),
                   jax.ShapeDtypeStruct((bsz * seq * TILE_ROWS, LANES), F32),
                   jax.ShapeDtypeStruct((bsz, seq, ne), F32), jax.ShapeDtypeStruct((1, ne), F32)],
        compiler_params=_cparams("arbitrary", "arbitrary"),
        name="out_proj_ln_router",
    )(ya, yb, wa, wb, x3, g.reshape(1, d), b.reshape(1, d), rw_hi, rw_lo, rb.reshape(1, ne), tri)


def _store_token_tiles(dst_ref, x):
    n = x.shape[0]
    for j in range(TILE_ROWS):
        dst_ref[pl.ds(j, n, stride=TILE_ROWS), :] = x[:, j * LANES:(j + 1) * LANES]


def _load_token_tiles(src_ref, n):
    return jnp.concatenate([src_ref[pl.ds(j, n, stride=TILE_ROWS), :] for j in range(TILE_ROWS)], axis=1)


def _gather_token_tiles(src_hbm, idx_ref, n, dst, sem):
    def body(g, carry):
        base = pl.multiple_of(g * (TILE_ROWS * TILE_ROWS), TILE_ROWS * TILE_ROWS)
        for u in range(TILE_ROWS):
            src_row = pl.multiple_of(idx_ref[0, g * TILE_ROWS + u], TILE_ROWS)
            pltpu.make_async_copy(src_hbm.at[pl.ds(src_row, TILE_ROWS)],
                                  dst.at[pl.ds(base + u * TILE_ROWS, TILE_ROWS)], sem).start()
        return carry
    lax.fori_loop(0, n // TILE_ROWS, body, 0, unroll=2)


def _moe_kernel(be_ref, nreal_ref, rt_ref, rtn_ref, x_hbm, rg_ref, win_ref, bin_ref, wout_ref, bout_ref, o_ref,
                win_bf, wout_bf, xbuf, sem):
    i = pl.program_id(0)
    n_real = nreal_ref[0]
    e = be_ref[i]
    prev = be_ref[jnp.maximum(i - 1, 0)]
    tm = xbuf.shape[1] // TILE_ROWS
    slot = lax.rem(i, 2)

    @pl.when(i == 0)
    def _first_gather():
        _gather_token_tiles(x_hbm, rt_ref, tm, xbuf.at[0], sem.at[0])

    @pl.when(i + 1 < n_real)
    def _next_gather():
        _gather_token_tiles(x_hbm, rtn_ref, tm, xbuf.at[1 - slot], sem.at[1 - slot])

    @pl.when((i == 0) | (e != prev))
    def _load_expert():
        win_bf[...] = win_ref[...].astype(BF16)
        wout_bf[...] = wout_ref[...].astype(BF16)

    @pl.when((i < n_real) | (i == 0))
    def _compute():
        pltpu.make_async_copy(xbuf.at[slot], xbuf.at[slot], sem.at[slot]).wait()
        x = _load_token_tiles(xbuf.at[slot], tm).astype(BF16)
        h = _dot(x, win_bf[...]) + bin_ref[...]
        g = jnp.minimum(h[:, :D_EXPERT], SWIGLU_LIMIT)
        lin = jnp.clip(h[:, D_EXPERT:], -SWIGLU_LIMIT, SWIGLU_LIMIT)
        y = g * jax.nn.sigmoid(SWIGLU_ALPHA * g) * (lin + 1.0)
        out = _dot(y.astype(BF16), wout_bf[...]) + bout_ref[...]
        gates = rg_ref[...].T
        out = jnp.concatenate([out[j * LANES:(j + 1) * LANES] * gates[:, j:j + 1] for j in range(tm // LANES)],
                              axis=0)
        _store_token_tiles(o_ref, out)

    @pl.when((i >= n_real) & (i > 0))
    def _unused_block():
        o_ref[...] = jnp.zeros_like(o_ref)


def _moe_experts(x_tiles, row_tok, row_gate, block_expert, n_real, layer, w_in, b_in, w_out, b_out):
    d = D_MODEL
    tm = MOE_TM
    n_blocks = row_tok.shape[0]
    nl, ne, _, dh2 = w_in.shape
    rt = (row_tok * TILE_ROWS).reshape(n_blocks, 1, tm)
    rg = jnp.pad(row_gate.reshape(n_blocks, tm // LANES, LANES), ((0, 0), (0, TILE_ROWS - tm // LANES), (0, 0)))
    smem_rows = lambda imap: pl.BlockSpec((None, 1, tm), imap, memory_space=pltpu.SMEM)
    grid_spec = pltpu.PrefetchScalarGridSpec(
        num_scalar_prefetch=2,
        grid=(n_blocks,),
        in_specs=[
            smem_rows(lambda i, be, nr: (i, 0, 0)),
            smem_rows(lambda i, be, nr: (jnp.minimum(i + 1, n_blocks - 1), 0, 0)),
            pl.BlockSpec(memory_space=pl.ANY),
            pl.BlockSpec((None, TILE_ROWS, LANES), lambda i, be, nr: (i, 0, 0)),
            pl.BlockSpec((None, None, d, dh2), lambda i, be, nr: (layer, be[i], 0, 0)),
            pl.BlockSpec((None, None, 1, dh2), lambda i, be, nr: (layer, be[i], 0, 0)),
            pl.BlockSpec((None, None, D_EXPERT, d), lambda i, be, nr: (layer, be[i], 0, 0)),
            pl.BlockSpec((None, None, 1, d), lambda i, be, nr: (layer, be[i], 0, 0)),
        ],
        out_specs=pl.BlockSpec((tm * TILE_ROWS, LANES), lambda i, be, nr: (i, 0)),
        scratch_shapes=[pltpu.VMEM((d, dh2), BF16), pltpu.VMEM((D_EXPERT, d), BF16),
                        pltpu.VMEM((2, tm * TILE_ROWS, LANES), F32), pltpu.SemaphoreType.DMA((2,))],
    )
    return pl.pallas_call(
        _moe_kernel,
        grid_spec=grid_spec,
        out_shape=jax.ShapeDtypeStruct((n_blocks * tm * TILE_ROWS, LANES), F32),
        compiler_params=_cparams("arbitrary"),
        name="moe_experts",
    )(block_expert, n_real, rt, rt, x_tiles, rg, w_in,
      b_in.reshape(nl, ne, 1, dh2), w_out, b_out.reshape(nl, ne, 1, d))


def _moe_part_kernel(be_ref, nreal_ref, xs_ref, rg_ref, win_ref, bin_ref, wout_ref, bout_ref, prev_hbm, o_ref,
                     win_bf, wout_bf, *, first):
    i = pl.program_id(0)
    blk = i + first
    e = be_ref[blk]
    prev = be_ref[jnp.maximum(blk - 1, 0)]
    tm = xs_ref.shape[0]

    @pl.when((i == 0) | (e != prev))
    def _load_expert():
        win_bf[...] = win_ref[...].astype(BF16)
        wout_bf[...] = wout_ref[...].astype(BF16)

    @pl.when(blk < nreal_ref[0])
    def _compute():
        h = _dot(xs_ref[...].astype(BF16), win_bf[...]) + bin_ref[...]
        g = jnp.minimum(h[:, :D_EXPERT], SWIGLU_LIMIT)
        lin = jnp.clip(h[:, D_EXPERT:], -SWIGLU_LIMIT, SWIGLU_LIMIT)
        y = g * jax.nn.sigmoid(SWIGLU_ALPHA * g) * (lin + 1.0)
        out = _dot(y.astype(BF16), wout_bf[...]) + bout_ref[...]
        gates = rg_ref[...].T
        out = jnp.concatenate([out[j * LANES:(j + 1) * LANES] * gates[:, j:j + 1] for j in range(tm // LANES)],
                              axis=0)
        _store_token_tiles(o_ref, out)

    @pl.when(blk >= nreal_ref[0])
    def _unused_block():
        o_ref[...] = jnp.zeros_like(o_ref)


def _moe_experts_split(xn, row_tok, row_gate, block_expert, n_real, layer, w_in, b_in, w_out, b_out, parts=2):
    d = D_MODEL
    tm = MOE_TM
    n_blocks = row_tok.shape[0]
    nl, ne, _, dh2 = w_in.shape
    rg = jnp.pad(row_gate.reshape(n_blocks, tm // LANES, LANES), ((0, 0), (0, TILE_ROWS - tm // LANES), (0, 0)))
    per = n_blocks // parts
    yb = jnp.zeros((n_blocks * tm * TILE_ROWS, LANES), F32)
    for p in range(parts):
        first = p * per
        xs = jnp.take(xn, row_tok[first:first + per].reshape(-1), axis=0)
        grid_spec = pltpu.PrefetchScalarGridSpec(
            num_scalar_prefetch=2,
            grid=(per,),
            in_specs=[
                pl.BlockSpec((tm, d), lambda i, be, nr: (i, 0)),
                pl.BlockSpec((None, TILE_ROWS, LANES), lambda i, be, nr, first=first: (i + first, 0, 0)),
                pl.BlockSpec((None, None, d, dh2), lambda i, be, nr, first=first: (layer, be[i + first], 0, 0)),
                pl.BlockSpec((None, None, 1, dh2), lambda i, be, nr, first=first: (layer, be[i + first], 0, 0)),
                pl.BlockSpec((None, None, D_EXPERT, d), lambda i, be, nr, first=first: (layer, be[i + first], 0, 0)),
                pl.BlockSpec((None, None, 1, d), lambda i, be, nr, first=first: (layer, be[i + first], 0, 0)),
                pl.BlockSpec(memory_space=pl.ANY),
            ],
            out_specs=pl.BlockSpec((tm * TILE_ROWS, LANES), lambda i, be, nr, first=first: (i + first, 0)),
            scratch_shapes=[pltpu.VMEM((d, dh2), BF16), pltpu.VMEM((D_EXPERT, d), BF16)],
        )
        yb = pl.pallas_call(
            functools.partial(_moe_part_kernel, first=first),
            grid_spec=grid_spec,
            out_shape=jax.ShapeDtypeStruct(yb.shape, F32),
            input_output_aliases={8: 0},
            compiler_params=_cparams("arbitrary"),
            name=f"moe_experts_part{p}",
        )(block_expert, n_real, xs, rg, w_in, b_in.reshape(nl, ne, 1, dh2), w_out, b_out.reshape(nl, ne, 1, d), yb)
    return yb


def _moe_combine_kernel(dc_ref, dn_ref, x_ref, yb_hbm, g_ref, b_ref, o_ref, buf, ffn_ref, sem):
    i = pl.program_id(0)
    n = pl.num_programs(0)
    tk = x_ref.shape[0]
    rows = tk * TILE_ROWS
    slot = lax.rem(i, 2)

    @pl.when(i == 0)
    def _first_gather():
        _gather_token_tiles(yb_hbm, dc_ref, TOP_K * tk, buf.at[0], sem.at[0])

    @pl.when(i + 1 < n)
    def _next_gather():
        _gather_token_tiles(yb_hbm, dn_ref, TOP_K * tk, buf.at[1 - slot], sem.at[1 - slot])

    pltpu.make_async_copy(buf.at[slot], buf.at[slot], sem.at[slot]).wait()
    ffn = buf[slot, 0:rows]
    for k in range(1, TOP_K):
        ffn = ffn + buf[slot, k * rows:(k + 1) * rows]
    ffn_ref[...] = ffn
    o_ref[...] = _layer_norm_rows(DEEPNORM_ALPHA * x_ref[...] + _load_token_tiles(ffn_ref, tk),
                                  g_ref[...], b_ref[...])


def _moe_combine_ln(xn, dest, yb_tiles, g, b):
    t, d = xn.shape
    tk = MOE_TK
    n = t // tk
    dest_tiles = (dest * TILE_ROWS).reshape(n, tk, TOP_K).transpose(0, 2, 1).reshape(n, 1, TOP_K * tk)
    smem_rows = lambda imap: pl.BlockSpec((None, 1, TOP_K * tk), imap, memory_space=pltpu.SMEM)
    row = lambda i: (i, 0)
    fixed = lambda i: (0, 0)
    return pl.pallas_call(
        _moe_combine_kernel,
        grid=(n,),
        in_specs=[smem_rows(lambda i: (i, 0, 0)), smem_rows(lambda i: (jnp.minimum(i + 1, n - 1), 0, 0)),
                  pl.BlockSpec((tk, d), row), pl.BlockSpec(memory_space=pl.ANY),
                  pl.BlockSpec((1, d), fixed), pl.BlockSpec((1, d), fixed)],
        out_specs=pl.BlockSpec((tk, d), row),
        out_shape=jax.ShapeDtypeStruct((t, d), F32),
        scratch_shapes=[pltpu.VMEM((2, TOP_K * tk * TILE_ROWS, LANES), F32),
                        pltpu.VMEM((tk * TILE_ROWS, LANES), F32), pltpu.SemaphoreType.DMA((2,))],
        compiler_params=_cparams("arbitrary"),
        name="moe_combine_ln",
    )(dest_tiles, dest_tiles, xn, yb_tiles, g.reshape(1, d), b.reshape(1, d))


def _moe_ffn_ln(xn, x_tiles, route, counts, layer, w_in, b_in, w_out, b_out, g, b):
    t, d = xn.shape
    tm = MOE_TM
    flat_e = route[:, :TOP_K].astype(jnp.int32).reshape(-1)
    rank = route[:, TOP_K:2 * TOP_K].astype(jnp.int32).reshape(-1)
    gate = route[:, 2 * TOP_K:3 * TOP_K].reshape(-1)
    counts = counts[0, :N_EXPERTS].astype(jnp.int32)
    blocks_per_e = (counts + tm - 1) // tm
    blk_end = jnp.cumsum(blocks_per_e)
    blk_start = blk_end - blocks_per_e
    row_start = jnp.cumsum(counts) - counts
    dest = blk_start[flat_e] * tm + rank
    n_blocks = (t * TOP_K) // tm + N_EXPERTS
    block_expert = jnp.minimum(
        jnp.sum((blk_end[None, :] <= jnp.arange(n_blocks, dtype=jnp.int32)[:, None]).astype(jnp.int32), axis=1),
        N_EXPERTS - 1)
    n_real = blk_end[-1:].astype(jnp.int32)
    order = jnp.argsort(flat_e, stable=True).astype(jnp.int32)
    blk = jnp.arange(n_blocks, dtype=jnp.int32)
    rank_row = ((blk - blk_start[block_expert]) * tm)[:, None] + jnp.arange(tm, dtype=jnp.int32)[None, :]
    valid = rank_row < counts[block_expert][:, None]
    pair = order[jnp.clip(row_start[block_expert][:, None] + rank_row, 0, t * TOP_K - 1)]
    row_tok = jnp.where(valid, pair // TOP_K, 0)
    row_gate = jnp.where(valid, gate[pair], 0.0)
    yb_tiles = _moe_experts_split(xn, row_tok, row_gate, block_expert, n_real, layer, w_in, b_in, w_out, b_out)
    return _moe_combine_ln(xn, dest, yb_tiles, g, b)


def _softplus(x):
    return jnp.maximum(x, 0.0) + jnp.log(1.0 + jnp.exp(-jnp.abs(x)))


def _gelu_tanh(x):
    return 0.5 * x * (1.0 + jnp.tanh(0.7978845608028654 * (x + 0.044715 * (x * x * x))))


def _lru_kernel(xa_ref, ga_ref, cw_ref, cb_ref, wr_ref, br_ref, wi_ref, bi_ref, lam_ref, o_ref,
                tail_ref, h_ref, a_ref, u_ref):
    tc, bsz, w = xa_ref.shape

    @pl.when(pl.program_id(0) == 0)
    def _init():
        tail_ref[...] = jnp.zeros_like(tail_ref)
        h_ref[...] = jnp.zeros_like(h_ref)

    xa = xa_ref[...]
    ext = jnp.concatenate([tail_ref[...], xa], axis=0)
    xc = cb_ref[...] + ext[0:tc] * cw_ref[0]
    for tap in range(1, CONV_WIDTH):
        xc = xc + ext[tap:tap + tc] * cw_ref[tap]
    tail_ref[...] = xa[tc - (CONV_WIDTH - 1):]
    x2 = xc.reshape(tc * bsz, w)
    xb = x2.astype(BF16)
    r = jax.nn.sigmoid(_dot(xb, wr_ref[...]) + br_ref[...])
    ig = jax.nn.sigmoid(_dot(xb, wi_ref[...]) + bi_ref[...])
    log_a = (-LRU_C) * r * _softplus(-lam_ref[...])
    a_ref[...] = jnp.exp(log_a).reshape(tc, bsz, w)
    u_ref[...] = (jnp.sqrt(1.0 - jnp.exp(2.0 * log_a)) * (ig * x2)).reshape(tc, bsz, w)

    def step(t, h):
        h = a_ref[t] * h + u_ref[t]
        u_ref[t] = h
        return h

    h_ref[...] = lax.fori_loop(0, tc, step, h_ref[...], unroll=8)
    o_ref[...] = (u_ref[...] * _gelu_tanh(ga_ref[...])).astype(o_ref.dtype)


def _block_diag(w):
    h, i, j = w.shape
    return jnp.einsum('hij,hk->hikj', w, jnp.eye(h, dtype=w.dtype)).reshape(h * i, h * j)


def _rg_lru(proj, conv_w, conv_b, gate_r_w, gate_r_b, gate_i_w, gate_i_b, lru_lambda):
    seq, bsz, _ = proj.shape
    w = A_WIDTH
    tc = LRU_TC
    fixed2 = lambda i: (0, 0)
    fixed3 = lambda i: (0, 0, 0)
    return pl.pallas_call(
        _lru_kernel,
        grid=(seq // tc,),
        in_specs=[pl.BlockSpec((tc, bsz, w), lambda i: (i, 0, 0)),
                  pl.BlockSpec((tc, bsz, w), lambda i: (i, 0, 1)),
                  pl.BlockSpec((CONV_WIDTH, 1, w), fixed3), pl.BlockSpec((1, w), fixed2),
                  pl.BlockSpec((w, w), fixed2), pl.BlockSpec((1, w), fixed2),
                  pl.BlockSpec((w, w), fixed2), pl.BlockSpec((1, w), fixed2),
                  pl.BlockSpec((1, w), fixed2)],
        out_specs=pl.BlockSpec((tc, bsz, w), lambda i: (i, 0, 0)),
        out_shape=jax.ShapeDtypeStruct((seq, bsz, w), BF16),
        scratch_shapes=[pltpu.VMEM((CONV_WIDTH - 1, bsz, w), F32), pltpu.VMEM((bsz, w), F32),
                        pltpu.VMEM((tc, bsz, w), F32), pltpu.VMEM((tc, bsz, w), F32)],
        compiler_params=_cparams("arbitrary"),
        name="rg_lru",
    )(proj, proj, conv_w.reshape(CONV_WIDTH, 1, w), conv_b.reshape(1, w),
      _block_diag(gate_r_w).astype(BF16), gate_r_b.reshape(1, w),
      _block_diag(gate_i_w).astype(BF16), gate_i_b.reshape(1, w), lru_lambda.reshape(1, w))


def _s5_kernel(u_ref, wbr_ref, wbi_ref, ar_ref, ai_ref, ccr_ref, cci_ref, d_ref, gw_ref, gb_ref, o_ref,
               xr_ref, xi_ref, sr_ref, si_ref):
    tc, bsz, w = u_ref.shape
    nstate = xr_ref.shape[-1]
    sh = nstate // S5_HALF
    wh = w // S5_HALF

    @pl.when(pl.program_id(0) == 0)
    def _init():
        sr_ref[...] = jnp.zeros_like(sr_ref)
        si_ref[...] = jnp.zeros_like(si_ref)

    u2 = u_ref[...].reshape(tc * bsz, w)
    ub = u2.astype(BF16)
    for hf in range(S5_HALF):
        uh = ub[:, hf * wh:(hf + 1) * wh]
        xr_ref[:, :, hf * sh:(hf + 1) * sh] = _dot(uh, wbr_ref[hf]).reshape(tc, bsz, sh)
        xi_ref[:, :, hf * sh:(hf + 1) * sh] = _dot(uh, wbi_ref[hf]).reshape(tc, bsz, sh)

    for hf in range(S5_HALF):
        lo, hi = hf * sh, (hf + 1) * sh
        ar = jnp.broadcast_to(ar_ref[:, lo:hi], (bsz, sh))
        ai = jnp.broadcast_to(ai_ref[:, lo:hi], (bsz, sh))

        def step(t, carry, lo=lo, hi=hi, ar=ar, ai=ai):
            xr, xi = carry
            nxr = ar * xr - ai * xi + xr_ref[t, :, lo:hi]
            nxi = ar * xi + ai * xr + xi_ref[t, :, lo:hi]
            xr_ref[t, :, lo:hi] = nxr
            xi_ref[t, :, lo:hi] = nxi
            return nxr, nxi

        xr, xi = lax.fori_loop(0, tc, step, (sr_ref[:, lo:hi], si_ref[:, lo:hi]), unroll=4)
        sr_ref[:, lo:hi] = xr
        si_ref[:, lo:hi] = xi

    ys = []
    for hf in range(S5_HALF):
        lo, hi = hf * sh, (hf + 1) * sh
        xrh = xr_ref[:, :, lo:hi].reshape(tc * bsz, sh).astype(BF16)
        xih = xi_ref[:, :, lo:hi].reshape(tc * bsz, sh).astype(BF16)
        ys.append(_dot(xrh, ccr_ref[hf]) - _dot(xih, cci_ref[hf]))
    y = jnp.concatenate(ys, axis=1) + d_ref[...] * u2
    s = _gelu_tanh(y)
    yb = s * jax.nn.sigmoid(_dot(s.astype(BF16), gw_ref[...]) + gb_ref[...])
    o_ref[...] = yb.reshape(tc, bsz, w).astype(o_ref.dtype)


def _s5_glu(proj, lam_re, lam_im, log_step, b_re, b_im, c_re, c_im, d_skip, glu_w, glu_b):
    seq, bsz, _ = proj.shape
    w = B_WIDTH
    tc = S5_TC
    nstate = S5_GROUPS * S5_STATE
    gh = S5_GROUPS // S5_HALF
    lr = jnp.minimum(lam_re, -1e-4)
    li = lam_im
    step = jnp.exp(log_step)[:, None]
    mag = jnp.exp(lr * step)
    ar = mag * jnp.cos(li * step)
    ai = mag * jnp.sin(li * step)
    inv = 1.0 / (lr * lr + li * li)
    zr = ((ar - 1.0) * lr + ai * li) * inv
    zi = (ai * lr - (ar - 1.0) * li) * inv
    bbr = zr[..., None] * b_re - zi[..., None] * b_im
    bbi = zr[..., None] * b_im + zi[..., None] * b_re
    eye = jnp.eye(gh, dtype=F32)

    def expand_in(bb):
        bb = bb.reshape(S5_HALF, gh, S5_STATE, S5_GROUP_CH)
        return jnp.einsum('fgph,gk->fghkp', bb, eye).reshape(
            S5_HALF, gh * S5_GROUP_CH, gh * S5_STATE).astype(BF16)

    def expand_out(c):
        c = c.reshape(S5_HALF, gh, S5_GROUP_CH, S5_STATE)
        return jnp.einsum('fghp,gk->fgpkh', c, eye).reshape(
            S5_HALF, gh * S5_STATE, gh * S5_GROUP_CH).astype(BF16)

    fixed2 = lambda i: (0, 0)
    fixed3 = lambda i: (0, 0, 0)
    return pl.pallas_call(
        _s5_kernel,
        grid=(seq // tc,),
        in_specs=[pl.BlockSpec((tc, bsz, w), lambda i: (i, 0, 2)),
                  pl.BlockSpec((S5_HALF, w // S5_HALF, nstate // S5_HALF), fixed3),
                  pl.BlockSpec((S5_HALF, w // S5_HALF, nstate // S5_HALF), fixed3),
                  pl.BlockSpec((1, nstate), fixed2), pl.BlockSpec((1, nstate), fixed2),
                  pl.BlockSpec((S5_HALF, nstate // S5_HALF, w // S5_HALF), fixed3),
                  pl.BlockSpec((S5_HALF, nstate // S5_HALF, w // S5_HALF), fixed3),
                  pl.BlockSpec((1, w), fixed2), pl.BlockSpec((w, w), fixed2), pl.BlockSpec((1, w), fixed2)],
        out_specs=pl.BlockSpec((tc, bsz, w), lambda i: (i, 0, 0)),
        out_shape=jax.ShapeDtypeStruct((seq, bsz, w), BF16),
        scratch_shapes=[pltpu.VMEM((tc, bsz, nstate), F32), pltpu.VMEM((tc, bsz, nstate), F32),
                        pltpu.VMEM((bsz, nstate), F32), pltpu.VMEM((bsz, nstate), F32)],
        compiler_params=_cparams("arbitrary"),
        name="s5_glu",
    )(proj, expand_in(bbr), expand_in(bbi), ar.reshape(1, nstate), ai.reshape(1, nstate),
      expand_out(c_re), expand_out(c_im), d_skip.reshape(1, w), glu_w.astype(BF16), glu_b.reshape(1, w))


def _cd_prep_kernel(xc_ref, q_ref, k_ref, pos_ref, cw_ref, cb_ref, wq_ref, wk_ref, wv_ref, wif_ref,
                    bif_ref, invf_ref, mq_ref, mk_ref, mv_ref, gt_ref, qd_ref, kd_ref, tail_ref):
    tl, w = xc_ref.shape
    pad = tail_ref.shape[0]

    @pl.when(pl.program_id(1) == 0)
    def _init():
        tail_ref[...] = jnp.zeros_like(tail_ref)

    xc = xc_ref[...]
    ext = jnp.concatenate([tail_ref[...], xc], axis=0)
    conv = cb_ref[...]
    for tap in range(CONV_WIDTH):
        off = pad - (CONV_WIDTH - 1) + tap
        conv = conv + ext[off:off + tl] * cw_ref[tap]
    tail_ref[...] = xc[tl - pad:]
    xconv = (conv * jax.nn.sigmoid(conv)).astype(BF16)
    q = _dot(xconv, wq_ref[...]).astype(BF16)
    k = (_dot(xconv, wk_ref[...]) * (MLSTM_HEAD_DIM ** -0.5)).astype(BF16)
    v = _dot(xc.astype(BF16), wv_ref[...]).astype(BF16)
    mq_ref[...] = q
    mk_ref[...] = k
    mv_ref[...] = v
    gt_ref[...] = (_dot(q, wif_ref[0]) + _dot(k, wif_ref[1]) + _dot(v, wif_ref[2]) + bif_ref[...])

    ang = pos_ref[...].astype(F32) * invf_ref[...]
    cos = jnp.cos(ang)
    sin = jnp.sin(ang)
    lane = lax.broadcasted_iota(jnp.int32, (1, LANES), 1)
    half = ROPE_DIMS // 2
    sin_lo = jnp.where(lane < half, -sin, 0.0)
    sin_hi = jnp.where((lane >= half) & (lane < ROPE_DIMS), sin, 0.0)
    for src, dst in ((q_ref, qd_ref), (k_ref, kd_ref)):
        for h in range(DSW_HEADS):
            sl = slice(h * DSW_HEAD_DIM, (h + 1) * DSW_HEAD_DIM)
            t = src[:, sl]
            rot = (t * cos + pltpu.roll(t, LANES - half, axis=1) * sin_lo
                   + pltpu.roll(t, half, axis=1) * sin_hi)
            dst[:, sl] = rot


def _cd_prep(proj, positions, conv_w, conv_b, w_q, w_k, w_v, w_if, b_if):
    bsz, seq, _ = proj.shape
    w = C_WIDTH
    tl = PROJ_TL
    half = ROPE_DIMS // 2
    inv_freq = ROPE_THETA ** (-jnp.arange(half, dtype=F32) / half)
    invf = jnp.zeros((1, LANES), F32).at[0, :half].set(inv_freq).at[0, half:ROPE_DIMS].set(inv_freq)
    wif = jnp.zeros((3, w, LANES), F32).at[:, :, :2 * MLSTM_HEADS].set(w_if.reshape(3, w, 2 * MLSTM_HEADS))
    bif = jnp.zeros((1, LANES), F32).at[0, :2 * MLSTM_HEADS].set(b_if)
    col = lambda c: (lambda b, i: (b, i, c))
    fixed2 = lambda b, i: (0, 0)
    fixed3 = lambda b, i: (0, 0, 0)
    tok_bf = jax.ShapeDtypeStruct((bsz, seq, w), BF16)
    return pl.pallas_call(
        _cd_prep_kernel,
        grid=(bsz, seq // tl),
        in_specs=[pl.BlockSpec((None, tl, w), col(0)), pl.BlockSpec((None, tl, w), col(2)),
                  pl.BlockSpec((None, tl, w), col(3)), pl.BlockSpec((None, tl, 1), col(0)),
                  pl.BlockSpec((CONV_WIDTH, 1, w), fixed3), pl.BlockSpec((1, w), fixed2),
                  pl.BlockSpec((w, w), fixed2), pl.BlockSpec((w, w), fixed2), pl.BlockSpec((w, w), fixed2),
                  pl.BlockSpec((3, w, LANES), fixed3), pl.BlockSpec((1, LANES), fixed2),
                  pl.BlockSpec((1, LANES), fixed2)],
        out_specs=[pl.BlockSpec((None, tl, w), col(0))] * 3 + [pl.BlockSpec((None, tl, LANES), col(0))]
                  + [pl.BlockSpec((None, tl, w), col(0))] * 2,
        out_shape=[tok_bf, tok_bf, tok_bf, jax.ShapeDtypeStruct((bsz, seq, LANES), F32),
                   jax.ShapeDtypeStruct((bsz, seq, w), F32), jax.ShapeDtypeStruct((bsz, seq, w), F32)],
        scratch_shapes=[pltpu.VMEM((8, w), F32)],
        compiler_params=_cparams("arbitrary", "arbitrary"),
        name="cd_prep",
    )(proj, proj, proj, positions.reshape(bsz, seq, 1), conv_w.reshape(CONV_WIDTH, 1, w),
      conv_b.reshape(1, w), _block_diag(w_q).astype(BF16), _block_diag(w_k).astype(BF16),
      _block_diag(w_v).astype(BF16), wif.astype(BF16), bif, invf)


def _cumsum_rows(x):
    n = x.shape[0]
    row = lax.broadcasted_iota(jnp.int32, x.shape, 0)
    shift = 1
    while shift < n:
        x = x + jnp.where(row >= shift, pltpu.roll(x, shift, axis=0), 0.0)
        shift *= 2
    return x


def _mlstm_kernel(q_ref, k_ref, v_ref, g_ref, op_ref, o_ref, c_ref, n_ref, m_ref):
    cs = q_ref.shape[0]
    dh = MLSTM_HEAD_DIM

    @pl.when(pl.program_id(1) == 0)
    def _init():
        c_ref[...] = jnp.zeros_like(c_ref)
        n_ref[...] = jnp.zeros_like(n_ref)
        m_ref[...] = jnp.full_like(m_ref, NEG_INF)

    gates = g_ref[...]
    cum = _cumsum_rows(jax.nn.log_sigmoid(gates))
    gates_t = gates.T
    cum_t = cum.T
    causal = (lax.broadcasted_iota(jnp.int32, (cs, cs), 0) >= lax.broadcasted_iota(jnp.int32, (cs, cs), 1))
    for h in range(MLSTM_HEADS):
        sl = slice(h * dh, (h + 1) * dh)
        fcol = MLSTM_HEADS + h
        li_c, li_r = gates[:, h:h + 1], gates_t[h:h + 1, :]
        cum_c, cum_r = cum[:, fcol:fcol + 1], cum_t[fcol:fcol + 1, :]
        m_prev = m_ref[h][:, 0:1]
        q, k, v = q_ref[:, sl], k_ref[:, sl], v_ref[:, sl]
        log_d = jnp.where(causal, cum_c - cum_r + li_r, NEG_INF)
        log_inter = cum_c + m_prev
        m_s = jnp.maximum(jnp.max(log_d, axis=1, keepdims=True), log_inter)
        s = _dot_nt(q, k) * jnp.exp(log_d - m_s)
        inter = jnp.exp(log_inter - m_s)
        c_prev = c_ref[h]
        n_prev = n_ref[h]
        num = _dot(s.astype(BF16), v) + inter * _dot(q, c_prev.astype(BF16))
        den = (jnp.sum(s, axis=1, keepdims=True)
               + inter * jnp.sum(q.astype(F32) * n_prev, axis=1, keepdims=True))
        hh = num / jnp.maximum(jnp.abs(den), jnp.exp(-m_s))
        o_ref[:, sl] = (jax.nn.sigmoid(op_ref[:, sl]) * hh).astype(o_ref.dtype)
        chunk_f = cum_c[cs - 1:cs, :]
        to_end = chunk_f - cum_c + li_c
        m_c = jnp.max(to_end, axis=0, keepdims=True)
        m_new = jnp.maximum(chunk_f + m_prev, m_c)
        s_old = jnp.exp(chunk_f + m_prev - m_new)
        s_new = jnp.exp(m_c - m_new)
        kw = k.astype(F32) * jnp.exp(to_end - m_c)
        c_ref[h] = s_old * c_prev + s_new * _dot(kw.T.astype(BF16), v)
        n_ref[h] = s_old * n_prev + s_new * jnp.sum(kw, axis=0, keepdims=True)
        m_ref[h] = jnp.broadcast_to(m_new, (1, LANES))


def _mlstm(mq, mk, mv, gates, proj):
    bsz, seq, w = mq.shape
    cs = MLSTM_CS
    tok = lambda b, i: (b, i, 0)
    return pl.pallas_call(
        _mlstm_kernel,
        grid=(bsz, seq // cs),
        in_specs=[pl.BlockSpec((None, cs, w), tok)] * 3
                 + [pl.BlockSpec((None, cs, LANES), tok), pl.BlockSpec((None, cs, w), lambda b, i: (b, i, 1))],
        out_specs=pl.BlockSpec((None, cs, w), tok),
        out_shape=jax.ShapeDtypeStruct((bsz, seq, w), BF16),
        scratch_shapes=[pltpu.VMEM((MLSTM_HEADS, MLSTM_HEAD_DIM, MLSTM_HEAD_DIM), F32),
                        pltpu.VMEM((MLSTM_HEADS, 1, MLSTM_HEAD_DIM), F32),
                        pltpu.VMEM((MLSTM_HEADS, 1, LANES), F32)],
        compiler_params=_cparams("arbitrary", "arbitrary"),
        name="mlstm",
    )(mq, mk, mv, gates, proj)


def _dsw_kernel(q_ref, k_ref, v_ref, y_ref, o_scr, lse_scr):
    seq, dh = q_ref.shape
    blk = DSW_BLOCK
    qi = lax.broadcasted_iota(jnp.int32, (blk, 2 * blk), 0)
    ki = lax.broadcasted_iota(jnp.int32, (blk, 2 * blk), 1)
    dist = qi + blk - ki
    scale = dh ** -0.5
    units = seq // blk
    for g, (window, dil) in enumerate(DSW_CONFIGS):
        nsub = seq // dil // blk
        band = (dist >= 0) & (dist <= window // dil)

        def unit(u, carry, g=g, dil=dil, nsub=nsub, band=band):
            r = u // nsub
            sb = u % nsub
            cur = pl.ds(r + sb * (blk * dil), blk, stride=dil)
            prev = pl.ds(r + jnp.maximum(sb - 1, 0) * (blk * dil), blk, stride=dil)
            kk = jnp.concatenate([k_ref[prev, :], k_ref[cur, :]], axis=0).astype(BF16)
            vv = jnp.concatenate([v_ref[prev, :], v_ref[cur, :]], axis=0).astype(BF16)
            s = _dot_nt(q_ref[cur, :].astype(BF16), kk) * scale
            s = jnp.where(band & ((ki >= blk) | (sb > 0)), s, NEG_INF)
            m = jnp.max(s, axis=1, keepdims=True)
            p = jnp.exp(s - m)
            l = jnp.sum(p, axis=1, keepdims=True)
            o_scr[g, cur, :] = _dot(p.astype(BF16), vv) / l
            lse_scr[g, cur, :] = jnp.broadcast_to(m + jnp.log(l), (blk, dh))
            return carry

        lax.fori_loop(0, units, unit, 0, unroll=8)

    lses = [lse_scr[g] for g in range(len(DSW_CONFIGS))]
    mx = functools.reduce(jnp.maximum, lses)
    ws = [jnp.exp(l - mx) for l in lses]
    acc = sum(ws[g] * o_scr[g] for g in range(len(DSW_CONFIGS)))
    y_ref[...] = (acc / sum(ws)).astype(y_ref.dtype)


def _dsw_attention(qd, kd, proj):
    bsz, seq, w = qd.shape
    dh = DSW_HEAD_DIM
    v_col = (proj.shape[-1] - w) // dh
    head = lambda b, h: (b, 0, h)
    return pl.pallas_call(
        _dsw_kernel,
        grid=(bsz, DSW_HEADS),
        in_specs=[pl.BlockSpec((None, seq, dh), head), pl.BlockSpec((None, seq, dh), head),
                  pl.BlockSpec((None, seq, dh), lambda b, h: (b, 0, v_col + h))],
        out_specs=pl.BlockSpec((None, seq, dh), head),
        out_shape=jax.ShapeDtypeStruct((bsz, seq, w), BF16),
        scratch_shapes=[pltpu.VMEM((len(DSW_CONFIGS), seq, dh), F32),
                        pltpu.VMEM((len(DSW_CONFIGS), seq, dh), F32)],
        compiler_params=_cparams("arbitrary", "arbitrary"),
        name="dsw_attention",
    )(qd, kd, proj)


def kernel(x, positions, ab_w_in, ab_conv_w, ab_conv_b, ab_gate_r_w, ab_gate_r_b, ab_gate_i_w, ab_gate_i_b, ab_lru_lambda, ab_s5_lambda_re, ab_s5_lambda_im, ab_s5_log_step, ab_s5_b_re, ab_s5_b_im, ab_s5_c_re, ab_s5_c_im, ab_s5_d, ab_glu_w, ab_glu_b, ab_w_out, cd_w_in, cd_conv_w, cd_conv_b, cd_w_q, cd_w_k, cd_w_v, cd_w_if, cd_b_if, cd_w_out, ln_mix_g, ln_mix_b, ln_ffn_g, ln_ffn_b, moe_router_w, moe_router_b, moe_w_in, moe_b_in, moe_w_out, moe_b_out):
    bsz, seq, d = x.shape
    t = bsz * seq
    for layer in range(DEPTH):
        j = layer // 2
        if layer % 2 == 0:
            proj = _inproj_time_major(x, ab_w_in[j].astype(BF16))
            ya = _rg_lru(proj, ab_conv_w[j], ab_conv_b[j], ab_gate_r_w[j], ab_gate_r_b[j],
                         ab_gate_i_w[j], ab_gate_i_b[j], ab_lru_lambda[j])
            yb = _s5_glu(proj, ab_s5_lambda_re[j], ab_s5_lambda_im[j], ab_s5_log_step[j], ab_s5_b_re[j],
                         ab_s5_b_im[j], ab_s5_c_re[j], ab_s5_c_im[j], ab_s5_d[j], ab_glu_w[j], ab_glu_b[j])
            time_major, w_out = True, ab_w_out[j]
        else:
            proj = _inproj(x, cd_w_in[j].astype(BF16))
            mq, mk, mv, gates, qd, kd = _cd_prep(proj, positions, cd_conv_w[j], cd_conv_b[j], cd_w_q[j],
                                                 cd_w_k[j], cd_w_v[j], cd_w_if[j], cd_b_if[j])
            ya = _mlstm(mq, mk, mv, gates, proj)
            yb = _dsw_attention(qd, kd, proj)
            time_major, w_out = False, cd_w_out[j]
        xn, x_tiles, route, counts = _outproj_ln_router(
            ya, yb, time_major, w_out.astype(BF16), x, ln_mix_g[layer], ln_mix_b[layer],
            moe_router_w[layer], moe_router_b[layer])
        x = _moe_ffn_ln(xn.reshape(t, d), x_tiles, route.reshape(t, LANES), counts, layer, moe_w_in, moe_b_in,
                        moe_w_out, moe_b_out, ln_ffn_g[layer], ln_ffn_b[layer]).reshape(bsz, seq, d)
    return x
```

```python
import functools

import jax
import jax.numpy as jnp
from jax import lax
from jax.experimental import pallas as pl
from jax.experimental.pallas import tpu as pltpu

F32 = jnp.float32
BF16 = jnp.bfloat16

D_MODEL = 1024
DEPTH = 2
A_WIDTH = 512
B_WIDTH = 512
C_WIDTH = 512
D_WIDTH = 512
LRU_HEADS = 8
LRU_HEAD_DIM = A_WIDTH // LRU_HEADS
LRU_C = 8.0
CONV_WIDTH = 4
S5_GROUP_CH = 16
S5_GROUPS = B_WIDTH // S5_GROUP_CH
S5_STATE = 64
MLSTM_HEADS = 4
MLSTM_HEAD_DIM = C_WIDTH // MLSTM_HEADS
DSW_HEADS = 4
DSW_HEAD_DIM = D_WIDTH // DSW_HEADS
DSW_CONFIGS = ((128, 1), (512, 4), (2048, 16))
DSW_BLOCK = 128
ROPE_THETA = 500000.0
ROPE_DIMS = DSW_HEAD_DIM // 4
N_EXPERTS = 32
TOP_K = 4
D_EXPERT = D_MODEL
SWIGLU_LIMIT = 7.0
SWIGLU_ALPHA = 1.702
DEEPNORM_ALPHA = (2 * DEPTH) ** 0.25
LN_EPS = 1e-5

VMEM_LIMIT_BYTES = 56 * 1024 * 1024
LANES = 128
TILE_ROWS = D_MODEL // LANES
MOE_TM = 512
MOE_TK = 256
PROJ_TL = 512
ROUTE_TS = 128
LRU_TC = 128
S5_TC = 64
S5_HALF = 2
MLSTM_CS = 256
REGROUP_MIN_DIL = 16
NEG_INF = float("-inf")


def _cparams(*sem):
    return pltpu.CompilerParams(dimension_semantics=sem, vmem_limit_bytes=VMEM_LIMIT_BYTES)


def _dot(a, b):
    return jnp.dot(a, b, preferred_element_type=F32)


def _dot_nt(a, b):
    return lax.dot_general(a, b, (((1,), (1,)), ((), ())), preferred_element_type=F32)


def _matmul_kernel(x_ref, w_ref, o_ref):
    o_ref[...] = _dot(x_ref[...].astype(BF16), w_ref[...])


def _inproj_time_major(x3, w_bf16):
    bsz, seq, k = x3.shape
    n = w_bf16.shape[1]
    out = pl.pallas_call(
        _matmul_kernel,
        grid=(bsz, seq // PROJ_TL),
        in_specs=[pl.BlockSpec((None, PROJ_TL, k), lambda b, i: (b, i, 0)),
                  pl.BlockSpec((k, n), lambda b, i: (0, 0))],
        out_specs=pl.BlockSpec((PROJ_TL, n), lambda b, i: (i, b)),
        out_shape=jax.ShapeDtypeStruct((seq, bsz * n), F32),
        compiler_params=_cparams("arbitrary", "arbitrary"),
        name="in_proj_time_major",
    )(x3, w_bf16)
    return out.reshape(seq, bsz, n)


def _inproj(x3, w_bf16):
    bsz, seq, k = x3.shape
    n = w_bf16.shape[1]
    return pl.pallas_call(
        _matmul_kernel,
        grid=(bsz, seq // PROJ_TL),
        in_specs=[pl.BlockSpec((None, PROJ_TL, k), lambda b, i: (b, i, 0)),
                  pl.BlockSpec((k, n), lambda b, i: (0, 0))],
        out_specs=pl.BlockSpec((None, PROJ_TL, n), lambda b, i: (b, i, 0)),
        out_shape=jax.ShapeDtypeStruct((bsz, seq, n), F32),
        compiler_params=_cparams("arbitrary", "arbitrary"),
        name="in_proj",
    )(x3, w_bf16)


def _layer_norm_rows(z, g, b):
    mu = jnp.mean(z, axis=-1, keepdims=True)
    zc = z - mu
    var = jnp.mean(zc * zc, axis=-1, keepdims=True)
    return zc * lax.rsqrt(var + LN_EPS) * g + b


def _outproj_ln_router_kernel(ya_ref, yb_ref, wa_ref, wb_ref, x_ref, g_ref, b_ref, rwh_ref, rwl_ref, rb_ref,
                              tri_ref, xn_ref, xt_ref, rt_ref, cnt_ref):
    @pl.when((pl.program_id(0) == 0) & (pl.program_id(1) == 0))
    def _init():
        cnt_ref[...] = jnp.zeros_like(cnt_ref)

    mix = _dot(ya_ref[...], wa_ref[...]) + _dot(yb_ref[...], wb_ref[...])
    xn = _layer_norm_rows(DEEPNORM_ALPHA * x_ref[...] + mix, g_ref[...], b_ref[...])
    xn_ref[...] = xn
    _store_token_tiles(xt_ref, xn)
    xh = xn.astype(BF16)
    xl = (xn - xh.astype(F32)).astype(BF16)
    logits = (_dot(xh, rwh_ref[...]) + _dot(xl, rwh_ref[...]) + _dot(xh, rwl_ref[...])) + rb_ref[...]
    ts = tri_ref.shape[0]
    lane = lax.broadcasted_iota(jnp.int32, (ts, LANES), 1).astype(F32)
    counts = cnt_ref[...]
    for s in range(logits.shape[0] // ts):
        lg = logits[s * ts:(s + 1) * ts]
        tops, ids, hots = [], [], []
        for _ in range(TOP_K):
            m = jnp.max(lg, axis=1, keepdims=True)
            idx = jnp.min(jnp.where(lg == m, lane, float(LANES)), axis=1, keepdims=True)
            hot = lane == idx
            lg = jnp.where(hot, NEG_INF, lg)
            tops.append(m)
            ids.append(idx)
            hots.append(hot)
        exps = [jnp.exp(m - tops[0]) for m in tops]
        denom = functools.reduce(jnp.add, exps)
        chosen = functools.reduce(jnp.add, [h.astype(F32) for h in hots])
        before = _dot(tri_ref[...], chosen.astype(BF16)) + counts
        route = jnp.zeros((ts, LANES), F32)
        for k in range(TOP_K):
            rank = jnp.sum(jnp.where(hots[k], before, 0.0), axis=1, keepdims=True)
            route = jnp.where(lane == float(k), ids[k], route)
            route = jnp.where(lane == float(TOP_K + k), rank, route)
            route = jnp.where(lane == float(2 * TOP_K + k), exps[k] / denom, route)
        rt_ref[s * ts:(s + 1) * ts, :] = route
        counts = counts + jnp.sum(chosen, axis=0, keepdims=True)
    cnt_ref[...] = counts


def _outproj_ln_router(ya, yb, time_major, w_bf16, x3, g, b, rw, rb):
    bsz, seq, d = x3.shape
    wa, wb = w_bf16[:ya.shape[-1]], w_bf16[ya.shape[-1]:]
    ne = LANES
    rw = jnp.zeros((d, ne), F32).at[:, :N_EXPERTS].set(rw)
    rb = jnp.full((ne,), NEG_INF, F32).at[:N_EXPERTS].set(rb)
    rw_hi = rw.astype(BF16)
    rw_lo = (rw - rw_hi.astype(F32)).astype(BF16)
    tl = PROJ_TL
    tri = jnp.tril(jnp.ones((ROUTE_TS, ROUTE_TS), BF16), -1)
    if time_major:
        ya, yb = ya.reshape(seq, -1), yb.reshape(seq, -1)
        y_specs = [pl.BlockSpec((tl, wa.shape[0]), lambda bi, i: (i, bi)),
                   pl.BlockSpec((tl, wb.shape[0]), lambda bi, i: (i, bi))]
    else:
        y_specs = [pl.BlockSpec((None, tl, wa.shape[0]), lambda bi, i: (bi, i, 0)),
                   pl.BlockSpec((None, tl, wb.shape[0]), lambda bi, i: (bi, i, 0))]
    fixed = lambda bi, i: (0, 0)
    tok = lambda bi, i: (bi, i, 0)
    return pl.pallas_call(
        _outproj_ln_router_kernel,
        grid=(bsz, seq // tl),
        in_specs=y_specs + [pl.BlockSpec(wa.shape, fixed), pl.BlockSpec(wb.shape, fixed),
                            pl.BlockSpec((None, tl, d), tok),
                            pl.BlockSpec((1, d), fixed), pl.BlockSpec((1, d), fixed),
                            pl.BlockSpec((d, ne), fixed), pl.BlockSpec((d, ne), fixed),
                            pl.BlockSpec((1, ne), fixed), pl.BlockSpec((ROUTE_TS, ROUTE_TS), fixed)],
        out_specs=[pl.BlockSpec((None, tl, d), tok),
                   pl.BlockSpec((tl * TILE_ROWS, LANES), lambda bi, i: (bi * (seq // tl) + i, 0)),
                   pl.BlockSpec((None, tl, ne), tok), pl.BlockSpec((1, ne), fixed)],
        out_shape=[jax.ShapeDtypeStruct((bsz, seq, d), F32),
                   jax.ShapeDtypeStruct((bsz * seq * TILE_ROWS, LANES), F32),
                   jax.ShapeDtypeStruct((bsz, seq, ne), F32), jax.ShapeDtypeStruct((1, ne), F32)],
        compiler_params=_cparams("arbitrary", "arbitrary"),
        name="out_proj_ln_router",
    )(ya, yb, wa, wb, x3, g.reshape(1, d), b.reshape(1, d), rw_hi, rw_lo, rb.reshape(1, ne), tri)


def _store_token_tiles(dst_ref, x):
    n = x.shape[0]
    for j in range(TILE_ROWS):
        dst_ref[pl.ds(j, n, stride=TILE_ROWS), :] = x[:, j * LANES:(j + 1) * LANES]


def _load_token_tiles(src_ref, n):
    return jnp.concatenate([src_ref[pl.ds(j, n, stride=TILE_ROWS), :] for j in range(TILE_ROWS)], axis=1)


def _gather_token_tiles(src_hbm, idx_ref, n, dst, sem):
    def body(g, carry):
        base = pl.multiple_of(g * (TILE_ROWS * TILE_ROWS), TILE_ROWS * TILE_ROWS)
        for u in range(TILE_ROWS):
            src_row = pl.multiple_of(idx_ref[0, g * TILE_ROWS + u], TILE_ROWS)
            pltpu.make_async_copy(src_hbm.at[pl.ds(src_row, TILE_ROWS)],
                                  dst.at[pl.ds(base + u * TILE_ROWS, TILE_ROWS)], sem).start(priority=u % 2)
        return carry
    lax.fori_loop(0, n // TILE_ROWS, body, 0, unroll=2)


def _moe_kernel(be_ref, nreal_ref, rt_ref, rtn_ref, x_hbm, rg_ref, win_ref, bin_ref, wout_ref, bout_ref, o_ref,
                win_bf, wout_bf, xbuf, sem):
    i = pl.program_id(0)
    n_real = nreal_ref[0]
    e = be_ref[i]
    prev = be_ref[jnp.maximum(i - 1, 0)]
    tm = xbuf.shape[1] // TILE_ROWS
    slot = lax.rem(i, 2)

    @pl.when(i == 0)
    def _first_gather():
        _gather_token_tiles(x_hbm, rt_ref, tm, xbuf.at[0], sem.at[0])

    @pl.when(i + 1 < n_real)
    def _next_gather():
        _gather_token_tiles(x_hbm, rtn_ref, tm, xbuf.at[1 - slot], sem.at[1 - slot])

    @pl.when((i == 0) | (e != prev))
    def _load_expert():
        win_bf[...] = win_ref[...].astype(BF16)
        wout_bf[...] = wout_ref[...].astype(BF16)

    @pl.when((i < n_real) | (i == 0))
    def _compute():
        pltpu.make_async_copy(xbuf.at[slot], xbuf.at[slot], sem.at[slot]).wait()
        x = _load_token_tiles(xbuf.at[slot], tm).astype(BF16)
        h = _dot(x, win_bf[...]) + bin_ref[...]
        g = jnp.minimum(h[:, :D_EXPERT], SWIGLU_LIMIT)
        lin = jnp.clip(h[:, D_EXPERT:], -SWIGLU_LIMIT, SWIGLU_LIMIT)
        y = g * jax.nn.sigmoid(SWIGLU_ALPHA * g) * (lin + 1.0)
        out = _dot(y.astype(BF16), wout_bf[...]) + bout_ref[...]
        gates = rg_ref[...].T
        out = jnp.concatenate([out[j * LANES:(j + 1) * LANES] * gates[:, j:j + 1] for j in range(tm // LANES)],
                              axis=0)
        _store_token_tiles(o_ref, out)

    @pl.when((i >= n_real) & (i > 0))
    def _unused_block():
        o_ref[...] = jnp.zeros_like(o_ref)


def _moe_experts(x_tiles, row_tok, row_gate, block_expert, n_real, layer, w_in, b_in, w_out, b_out):
    d = D_MODEL
    tm = MOE_TM
    n_blocks = row_tok.shape[0]
    nl, ne, _, dh2 = w_in.shape
    rt = (row_tok * TILE_ROWS).reshape(n_blocks, 1, tm)
    rg = jnp.pad(row_gate.reshape(n_blocks, tm // LANES, LANES), ((0, 0), (0, TILE_ROWS - tm // LANES), (0, 0)))
    smem_rows = lambda imap: pl.BlockSpec((None, 1, tm), imap, memory_space=pltpu.SMEM)
    grid_spec = pltpu.PrefetchScalarGridSpec(
        num_scalar_prefetch=2,
        grid=(n_blocks,),
        in_specs=[
            smem_rows(lambda i, be, nr: (i, 0, 0)),
            smem_rows(lambda i, be, nr: (jnp.minimum(i + 1, n_blocks - 1), 0, 0)),
            pl.BlockSpec(memory_space=pl.ANY),
            pl.BlockSpec((None, TILE_ROWS, LANES), lambda i, be, nr: (i, 0, 0)),
            pl.BlockSpec((None, None, d, dh2), lambda i, be, nr: (layer, be[i], 0, 0)),
            pl.BlockSpec((None, None, 1, dh2), lambda i, be, nr: (layer, be[i], 0, 0)),
            pl.BlockSpec((None, None, D_EXPERT, d), lambda i, be, nr: (layer, be[i], 0, 0)),
            pl.BlockSpec((None, None, 1, d), lambda i, be, nr: (layer, be[i], 0, 0)),
        ],
        out_specs=pl.BlockSpec((tm * TILE_ROWS, LANES), lambda i, be, nr: (i, 0)),
        scratch_shapes=[pltpu.VMEM((d, dh2), BF16), pltpu.VMEM((D_EXPERT, d), BF16),
                        pltpu.VMEM((2, tm * TILE_ROWS, LANES), F32), pltpu.SemaphoreType.DMA((2,))],
    )
    return pl.pallas_call(
        _moe_kernel,
        grid_spec=grid_spec,
        out_shape=jax.ShapeDtypeStruct((n_blocks * tm * TILE_ROWS, LANES), F32),
        compiler_params=_cparams("arbitrary"),
        name="moe_experts",
    )(block_expert, n_real, rt, rt, x_tiles, rg, w_in,
      b_in.reshape(nl, ne, 1, dh2), w_out, b_out.reshape(nl, ne, 1, d))


def _moe_combine_kernel(dc_ref, dn_ref, x_ref, yb_hbm, g_ref, b_ref, o_ref, buf, ffn_ref, sem):
    i = pl.program_id(0)
    n = pl.num_programs(0)
    tk = x_ref.shape[0]
    rows = tk * TILE_ROWS
    slot = lax.rem(i, 2)

    @pl.when(i == 0)
    def _first_gather():
        _gather_token_tiles(yb_hbm, dc_ref, TOP_K * tk, buf.at[0], sem.at[0])

    @pl.when(i + 1 < n)
    def _next_gather():
        _gather_token_tiles(yb_hbm, dn_ref, TOP_K * tk, buf.at[1 - slot], sem.at[1 - slot])

    pltpu.make_async_copy(buf.at[slot], buf.at[slot], sem.at[slot]).wait()
    ffn = buf[slot, 0:rows]
    for k in range(1, TOP_K):
        ffn = ffn + buf[slot, k * rows:(k + 1) * rows]
    ffn_ref[...] = ffn
    o_ref[...] = _layer_norm_rows(DEEPNORM_ALPHA * x_ref[...] + _load_token_tiles(ffn_ref, tk),
                                  g_ref[...], b_ref[...])


def _moe_combine_ln(xn, dest, yb_tiles, g, b):
    t, d = xn.shape
    tk = MOE_TK
    n = t // tk
    dest_tiles = (dest * TILE_ROWS).reshape(n, tk, TOP_K).transpose(0, 2, 1).reshape(n, 1, TOP_K * tk)
    smem_rows = lambda imap: pl.BlockSpec((None, 1, TOP_K * tk), imap, memory_space=pltpu.SMEM)
    row = lambda i: (i, 0)
    fixed = lambda i: (0, 0)
    return pl.pallas_call(
        _moe_combine_kernel,
        grid=(n,),
        in_specs=[smem_rows(lambda i: (i, 0, 0)), smem_rows(lambda i: (jnp.minimum(i + 1, n - 1), 0, 0)),
                  pl.BlockSpec((tk, d), row), pl.BlockSpec(memory_space=pl.ANY),
                  pl.BlockSpec((1, d), fixed), pl.BlockSpec((1, d), fixed)],
        out_specs=pl.BlockSpec((tk, d), row),
        out_shape=jax.ShapeDtypeStruct((t, d), F32),
        scratch_shapes=[pltpu.VMEM((2, TOP_K * tk * TILE_ROWS, LANES), F32),
                        pltpu.VMEM((tk * TILE_ROWS, LANES), F32), pltpu.SemaphoreType.DMA((2,))],
        compiler_params=_cparams("arbitrary"),
        name="moe_combine_ln",
    )(dest_tiles, dest_tiles, xn, yb_tiles, g.reshape(1, d), b.reshape(1, d))


def _moe_ffn_ln(xn, x_tiles, route, counts, layer, w_in, b_in, w_out, b_out, g, b):
    t, d = xn.shape
    tm = MOE_TM
    flat_e = route[:, :TOP_K].astype(jnp.int32).reshape(-1)
    rank = route[:, TOP_K:2 * TOP_K].astype(jnp.int32).reshape(-1)
    gate = route[:, 2 * TOP_K:3 * TOP_K].reshape(-1)
    counts = counts[0, :N_EXPERTS].astype(jnp.int32)
    blocks_per_e = (counts + tm - 1) // tm
    blk_end = jnp.cumsum(blocks_per_e)
    blk_start = blk_end - blocks_per_e
    row_start = jnp.cumsum(counts) - counts
    dest = blk_start[flat_e] * tm + rank
    n_blocks = (t * TOP_K) // tm + N_EXPERTS
    block_expert = jnp.minimum(
        jnp.sum((blk_end[None, :] <= jnp.arange(n_blocks, dtype=jnp.int32)[:, None]).astype(jnp.int32), axis=1),
        N_EXPERTS - 1)
    n_real = blk_end[-1:].astype(jnp.int32)
    order = jnp.argsort(flat_e, stable=True).astype(jnp.int32)
    blk = jnp.arange(n_blocks, dtype=jnp.int32)
    rank_row = ((blk - blk_start[block_expert]) * tm)[:, None] + jnp.arange(tm, dtype=jnp.int32)[None, :]
    valid = rank_row < counts[block_expert][:, None]
    pair = order[jnp.clip(row_start[block_expert][:, None] + rank_row, 0, t * TOP_K - 1)]
    row_tok = jnp.where(valid, pair // TOP_K, 0)
    row_gate = jnp.where(valid, gate[pair], 0.0)
    yb_tiles = _moe_experts(x_tiles, row_tok, row_gate, block_expert, n_real, layer, w_in, b_in, w_out, b_out)
    return _moe_combine_ln(xn, dest, yb_tiles, g, b)


def _softplus(x):
    return jnp.maximum(x, 0.0) + jnp.log(1.0 + jnp.exp(-jnp.abs(x)))


def _gelu_tanh(x):
    return 0.5 * x * (1.0 + jnp.tanh(0.7978845608028654 * (x + 0.044715 * (x * x * x))))


def _lru_kernel(xa_ref, ga_ref, cw_ref, cb_ref, wr_ref, br_ref, wi_ref, bi_ref, lam_ref, o_ref,
                tail_ref, h_ref, a_ref, u_ref):
    tc, bsz, w = xa_ref.shape

    @pl.when(pl.program_id(0) == 0)
    def _init():
        tail_ref[...] = jnp.zeros_like(tail_ref)
        h_ref[...] = jnp.zeros_like(h_ref)

    xa = xa_ref[...]
    ext = jnp.concatenate([tail_ref[...], xa], axis=0)
    xc = cb_ref[...] + ext[0:tc] * cw_ref[0]
    for tap in range(1, CONV_WIDTH):
        xc = xc + ext[tap:tap + tc] * cw_ref[tap]
    tail_ref[...] = xa[tc - (CONV_WIDTH - 1):]
    x2 = xc.reshape(tc * bsz, w)
    xb = x2.astype(BF16)
    r = jax.nn.sigmoid(_dot(xb, wr_ref[...]) + br_ref[...])
    ig = jax.nn.sigmoid(_dot(xb, wi_ref[...]) + bi_ref[...])
    log_a = (-LRU_C) * r * _softplus(-lam_ref[...])
    a_ref[...] = jnp.exp(log_a).reshape(tc, bsz, w)
    u_ref[...] = (jnp.sqrt(1.0 - jnp.exp(2.0 * log_a)) * (ig * x2)).reshape(tc, bsz, w)

    def step(t, h):
        h = a_ref[t] * h + u_ref[t]
        u_ref[t] = h
        return h

    h_ref[...] = lax.fori_loop(0, tc, step, h_ref[...], unroll=8)
    o_ref[...] = (u_ref[...] * _gelu_tanh(ga_ref[...])).astype(o_ref.dtype)


def _block_diag(w):
    h, i, j = w.shape
    return jnp.einsum('hij,hk->hikj', w, jnp.eye(h, dtype=w.dtype)).reshape(h * i, h * j)


def _rg_lru(proj, conv_w, conv_b, gate_r_w, gate_r_b, gate_i_w, gate_i_b, lru_lambda):
    seq, bsz, _ = proj.shape
    w = A_WIDTH
    tc = LRU_TC
    fixed2 = lambda i: (0, 0)
    fixed3 = lambda i: (0, 0, 0)
    return pl.pallas_call(
        _lru_kernel,
        grid=(seq // tc,),
        in_specs=[pl.BlockSpec((tc, bsz, w), lambda i: (i, 0, 0)),
                  pl.BlockSpec((tc, bsz, w), lambda i: (i, 0, 1)),
                  pl.BlockSpec((CONV_WIDTH, 1, w), fixed3), pl.BlockSpec((1, w), fixed2),
                  pl.BlockSpec((w, w), fixed2), pl.BlockSpec((1, w), fixed2),
                  pl.BlockSpec((w, w), fixed2), pl.BlockSpec((1, w), fixed2),
                  pl.BlockSpec((1, w), fixed2)],
        out_specs=pl.BlockSpec((tc, bsz, w), lambda i: (i, 0, 0)),
        out_shape=jax.ShapeDtypeStruct((seq, bsz, w), BF16),
        scratch_shapes=[pltpu.VMEM((CONV_WIDTH - 1, bsz, w), F32), pltpu.VMEM((bsz, w), F32),
                        pltpu.VMEM((tc, bsz, w), F32), pltpu.VMEM((tc, bsz, w), F32)],
        compiler_params=_cparams("arbitrary"),
        name="rg_lru",
    )(proj, proj, conv_w.reshape(CONV_WIDTH, 1, w), conv_b.reshape(1, w),
      _block_diag(gate_r_w).astype(BF16), gate_r_b.reshape(1, w),
      _block_diag(gate_i_w).astype(BF16), gate_i_b.reshape(1, w), lru_lambda.reshape(1, w))


def _s5_kernel(u_ref, wbr_ref, wbi_ref, ar_ref, ai_ref, ccr_ref, cci_ref, d_ref, gw_ref, gb_ref, o_ref,
               xr_ref, xi_ref, sr_ref, si_ref):
    tc, bsz, w = u_ref.shape
    nstate = xr_ref.shape[-1]
    sh = nstate // S5_HALF
    wh = w // S5_HALF

    @pl.when(pl.program_id(0) == 0)
    def _init():
        sr_ref[...] = jnp.zeros_like(sr_ref)
        si_ref[...] = jnp.zeros_like(si_ref)

    u2 = u_ref[...].reshape(tc * bsz, w)
    ub = u2.astype(BF16)
    for hf in range(S5_HALF):
        uh = ub[:, hf * wh:(hf + 1) * wh]
        xr_ref[:, :, hf * sh:(hf + 1) * sh] = _dot(uh, wbr_ref[hf]).reshape(tc, bsz, sh)
        xi_ref[:, :, hf * sh:(hf + 1) * sh] = _dot(uh, wbi_ref[hf]).reshape(tc, bsz, sh)

    for hf in range(S5_HALF):
        lo, hi = hf * sh, (hf + 1) * sh
        ar = jnp.broadcast_to(ar_ref[:, lo:hi], (bsz, sh))
        ai = jnp.broadcast_to(ai_ref[:, lo:hi], (bsz, sh))

        def step(t, carry, lo=lo, hi=hi, ar=ar, ai=ai):
            xr, xi = carry
            nxr = ar * xr - ai * xi + xr_ref[t, :, lo:hi]
            nxi = ar * xi + ai * xr + xi_ref[t, :, lo:hi]
            xr_ref[t, :, lo:hi] = nxr
            xi_ref[t, :, lo:hi] = nxi
            return nxr, nxi

        xr, xi = lax.fori_loop(0, tc, step, (sr_ref[:, lo:hi], si_ref[:, lo:hi]), unroll=4)
        sr_ref[:, lo:hi] = xr
        si_ref[:, lo:hi] = xi

    ys = []
    for hf in range(S5_HALF):
        lo, hi = hf * sh, (hf + 1) * sh
        xrh = xr_ref[:, :, lo:hi].reshape(tc * bsz, sh).astype(BF16)
        xih = xi_ref[:, :, lo:hi].reshape(tc * bsz, sh).astype(BF16)
        ys.append(_dot(xrh, ccr_ref[hf]) - _dot(xih, cci_ref[hf]))
    y = jnp.concatenate(ys, axis=1) + d_ref[...] * u2
    s = _gelu_tanh(y)
    yb = s * jax.nn.sigmoid(_dot(s.astype(BF16), gw_ref[...]) + gb_ref[...])
    o_ref[...] = yb.reshape(tc, bsz, w).astype(o_ref.dtype)


def _s5_glu(proj, lam_re, lam_im, log_step, b_re, b_im, c_re, c_im, d_skip, glu_w, glu_b):
    seq, bsz, _ = proj.shape
    w = B_WIDTH
    tc = S5_TC
    nstate = S5_GROUPS * S5_STATE
    gh = S5_GROUPS // S5_HALF
    lr = jnp.minimum(lam_re, -1e-4)
    li = lam_im
    step = jnp.exp(log_step)[:, None]
    mag = jnp.exp(lr * step)
    ar = mag * jnp.cos(li * step)
    ai = mag * jnp.sin(li * step)
    inv = 1.0 / (lr * lr + li * li)
    zr = ((ar - 1.0) * lr + ai * li) * inv
    zi = (ai * lr - (ar - 1.0) * li) * inv
    bbr = zr[..., None] * b_re - zi[..., None] * b_im
    bbi = zr[..., None] * b_im + zi[..., None] * b_re
    eye = jnp.eye(gh, dtype=F32)

    def expand_in(bb):
        bb = bb.reshape(S5_HALF, gh, S5_STATE, S5_GROUP_CH)
        return jnp.einsum('fgph,gk->fghkp', bb, eye).reshape(
            S5_HALF, gh * S5_GROUP_CH, gh * S5_STATE).astype(BF16)

    def expand_out(c):
        c = c.reshape(S5_HALF, gh, S5_GROUP_CH, S5_STATE)
        return jnp.einsum('fghp,gk->fgpkh', c, eye).reshape(
            S5_HALF, gh * S5_STATE, gh * S5_GROUP_CH).astype(BF16)

    fixed2 = lambda i: (0, 0)
    fixed3 = lambda i: (0, 0, 0)
    return pl.pallas_call(
        _s5_kernel,
        grid=(seq // tc,),
        in_specs=[pl.BlockSpec((tc, bsz, w), lambda i: (i, 0, 2)),
                  pl.BlockSpec((S5_HALF, w // S5_HALF, nstate // S5_HALF), fixed3),
                  pl.BlockSpec((S5_HALF, w // S5_HALF, nstate // S5_HALF), fixed3),
                  pl.BlockSpec((1, nstate), fixed2), pl.BlockSpec((1, nstate), fixed2),
                  pl.BlockSpec((S5_HALF, nstate // S5_HALF, w // S5_HALF), fixed3),
                  pl.BlockSpec((S5_HALF, nstate // S5_HALF, w // S5_HALF), fixed3),
                  pl.BlockSpec((1, w), fixed2), pl.BlockSpec((w, w), fixed2), pl.BlockSpec((1, w), fixed2)],
        out_specs=pl.BlockSpec((tc, bsz, w), lambda i: (i, 0, 0)),
        out_shape=jax.ShapeDtypeStruct((seq, bsz, w), BF16),
        scratch_shapes=[pltpu.VMEM((tc, bsz, nstate), F32), pltpu.VMEM((tc, bsz, nstate), F32),
                        pltpu.VMEM((bsz, nstate), F32), pltpu.VMEM((bsz, nstate), F32)],
        compiler_params=_cparams("arbitrary"),
        name="s5_glu",
    )(proj, expand_in(bbr), expand_in(bbi), ar.reshape(1, nstate), ai.reshape(1, nstate),
      expand_out(c_re), expand_out(c_im), d_skip.reshape(1, w), glu_w.astype(BF16), glu_b.reshape(1, w))


def _cd_prep_kernel(xc_ref, q_ref, k_ref, pos_ref, cw_ref, cb_ref, wq_ref, wk_ref, wv_ref, wif_ref,
                    bif_ref, invf_ref, mq_ref, mk_ref, mv_ref, gt_ref, qd_ref, kd_ref, tail_ref):
    tl, w = xc_ref.shape
    pad = tail_ref.shape[0]

    @pl.when(pl.program_id(1) == 0)
    def _init():
        tail_ref[...] = jnp.zeros_like(tail_ref)

    xc = xc_ref[...]
    ext = jnp.concatenate([tail_ref[...], xc], axis=0)
    conv = cb_ref[...]
    for tap in range(CONV_WIDTH):
        off = pad - (CONV_WIDTH - 1) + tap
        conv = conv + ext[off:off + tl] * cw_ref[tap]
    tail_ref[...] = xc[tl - pad:]
    xconv = (conv * jax.nn.sigmoid(conv)).astype(BF16)
    q = _dot(xconv, wq_ref[...]).astype(BF16)
    k = (_dot(xconv, wk_ref[...]) * (MLSTM_HEAD_DIM ** -0.5)).astype(BF16)
    v = _dot(xc.astype(BF16), wv_ref[...]).astype(BF16)
    mq_ref[...] = q
    mk_ref[...] = k
    mv_ref[...] = v
    gt_ref[...] = (_dot(q, wif_ref[0]) + _dot(k, wif_ref[1]) + _dot(v, wif_ref[2]) + bif_ref[...])

    ang = pos_ref[...].astype(F32) * invf_ref[...]
    cos = jnp.cos(ang)
    sin = jnp.sin(ang)
    lane = lax.broadcasted_iota(jnp.int32, (1, LANES), 1)
    half = ROPE_DIMS // 2
    sin_lo = jnp.where(lane < half, -sin, 0.0)
    sin_hi = jnp.where((lane >= half) & (lane < ROPE_DIMS), sin, 0.0)
    for src, dst in ((q_ref, qd_ref), (k_ref, kd_ref)):
        for h in range(DSW_HEADS):
            sl = slice(h * DSW_HEAD_DIM, (h + 1) * DSW_HEAD_DIM)
            t = src[:, sl]
            rot = (t * cos + pltpu.roll(t, LANES - half, axis=1) * sin_lo
                   + pltpu.roll(t, half, axis=1) * sin_hi)
            dst[:, sl] = rot


def _cd_prep(proj, positions, conv_w, conv_b, w_q, w_k, w_v, w_if, b_if):
    bsz, seq, _ = proj.shape
    w = C_WIDTH
    tl = PROJ_TL
    half = ROPE_DIMS // 2
    inv_freq = ROPE_THETA ** (-jnp.arange(half, dtype=F32) / half)
    invf = jnp.zeros((1, LANES), F32).at[0, :half].set(inv_freq).at[0, half:ROPE_DIMS].set(inv_freq)
    wif = jnp.zeros((3, w, LANES), F32).at[:, :, :2 * MLSTM_HEADS].set(w_if.reshape(3, w, 2 * MLSTM_HEADS))
    bif = jnp.zeros((1, LANES), F32).at[0, :2 * MLSTM_HEADS].set(b_if)
    col = lambda c: (lambda b, i: (b, i, c))
    fixed2 = lambda b, i: (0, 0)
    fixed3 = lambda b, i: (0, 0, 0)
    tok_bf = jax.ShapeDtypeStruct((bsz, seq, w), BF16)
    return pl.pallas_call(
        _cd_prep_kernel,
        grid=(bsz, seq // tl),
        in_specs=[pl.BlockSpec((None, tl, w), col(0)), pl.BlockSpec((None, tl, w), col(2)),
                  pl.BlockSpec((None, tl, w), col(3)), pl.BlockSpec((None, tl, 1), col(0)),
                  pl.BlockSpec((CONV_WIDTH, 1, w), fixed3), pl.BlockSpec((1, w), fixed2),
                  pl.BlockSpec((w, w), fixed2), pl.BlockSpec((w, w), fixed2), pl.BlockSpec((w, w), fixed2),
                  pl.BlockSpec((3, w, LANES), fixed3), pl.BlockSpec((1, LANES), fixed2),
                  pl.BlockSpec((1, LANES), fixed2)],
        out_specs=[pl.BlockSpec((None, tl, w), col(0))] * 3 + [pl.BlockSpec((None, tl, LANES), col(0))]
                  + [pl.BlockSpec((None, tl, w), col(0))] * 2,
        out_shape=[tok_bf, tok_bf, tok_bf, jax.ShapeDtypeStruct((bsz, seq, LANES), F32),
                   jax.ShapeDtypeStruct((bsz, seq, w), F32), jax.ShapeDtypeStruct((bsz, seq, w), F32)],
        scratch_shapes=[pltpu.VMEM((8, w), F32)],
        compiler_params=_cparams("arbitrary", "arbitrary"),
        name="cd_prep",
    )(proj, proj, proj, positions.reshape(bsz, seq, 1), conv_w.reshape(CONV_WIDTH, 1, w),
      conv_b.reshape(1, w), _block_diag(w_q).astype(BF16), _block_diag(w_k).astype(BF16),
      _block_diag(w_v).astype(BF16), wif.astype(BF16), bif, invf)


def _cumsum_rows(x):
    n = x.shape[0]
    row = lax.broadcasted_iota(jnp.int32, x.shape, 0)
    shift = 1
    while shift < n:
        x = x + jnp.where(row >= shift, pltpu.roll(x, shift, axis=0), 0.0)
        shift *= 2
    return x


def _mlstm_kernel(q_ref, k_ref, v_ref, g_ref, op_ref, o_ref, c_ref, n_ref, m_ref):
    cs = q_ref.shape[0]
    dh = MLSTM_HEAD_DIM

    @pl.when(pl.program_id(1) == 0)
    def _init():
        c_ref[...] = jnp.zeros_like(c_ref)
        n_ref[...] = jnp.zeros_like(n_ref)
        m_ref[...] = jnp.full_like(m_ref, NEG_INF)

    gates = g_ref[...]
    cum = _cumsum_rows(jax.nn.log_sigmoid(gates))
    gates_t = gates.T
    cum_t = cum.T
    causal = (lax.broadcasted_iota(jnp.int32, (cs, cs), 0) >= lax.broadcasted_iota(jnp.int32, (cs, cs), 1))
    for h in range(MLSTM_HEADS):
        sl = slice(h * dh, (h + 1) * dh)
        fcol = MLSTM_HEADS + h
        li_c, li_r = gates[:, h:h + 1], gates_t[h:h + 1, :]
        cum_c, cum_r = cum[:, fcol:fcol + 1], cum_t[fcol:fcol + 1, :]
        m_prev = m_ref[h][:, 0:1]
        q, k, v = q_ref[:, sl], k_ref[:, sl], v_ref[:, sl]
        log_d = jnp.where(causal, cum_c - cum_r + li_r, NEG_INF)
        log_inter = cum_c + m_prev
        m_s = jnp.maximum(jnp.max(log_d, axis=1, keepdims=True), log_inter)
        s = _dot_nt(q, k) * jnp.exp(log_d - m_s)
        inter = jnp.exp(log_inter - m_s)
        c_prev = c_ref[h]
        n_prev = n_ref[h]
        num = _dot(s.astype(BF16), v) + inter * _dot(q, c_prev.astype(BF16))
        den = (jnp.sum(s, axis=1, keepdims=True)
               + inter * jnp.sum(q.astype(F32) * n_prev, axis=1, keepdims=True))
        hh = num / jnp.maximum(jnp.abs(den), jnp.exp(-m_s))
        o_ref[:, sl] = (jax.nn.sigmoid(op_ref[:, sl]) * hh).astype(o_ref.dtype)
        chunk_f = cum_c[cs - 1:cs, :]
        to_end = chunk_f - cum_c + li_c
        m_c = jnp.max(to_end, axis=0, keepdims=True)
        m_new = jnp.maximum(chunk_f + m_prev, m_c)
        s_old = jnp.exp(chunk_f + m_prev - m_new)
        s_new = jnp.exp(m_c - m_new)
        kw = k.astype(F32) * jnp.exp(to_end - m_c)
        c_ref[h] = s_old * c_prev + s_new * _dot(kw.T.astype(BF16), v)
        n_ref[h] = s_old * n_prev + s_new * jnp.sum(kw, axis=0, keepdims=True)
        m_ref[h] = jnp.broadcast_to(m_new, (1, LANES))


def _mlstm(mq, mk, mv, gates, proj):
    bsz, seq, w = mq.shape
    cs = MLSTM_CS
    tok = lambda b, i: (b, i, 0)
    return pl.pallas_call(
        _mlstm_kernel,
        grid=(bsz, seq // cs),
        in_specs=[pl.BlockSpec((None, cs, w), tok)] * 3
                 + [pl.BlockSpec((None, cs, LANES), tok), pl.BlockSpec((None, cs, w), lambda b, i: (b, i, 1))],
        out_specs=pl.BlockSpec((None, cs, w), tok),
        out_shape=jax.ShapeDtypeStruct((bsz, seq, w), BF16),
        scratch_shapes=[pltpu.VMEM((MLSTM_HEADS, MLSTM_HEAD_DIM, MLSTM_HEAD_DIM), F32),
                        pltpu.VMEM((MLSTM_HEADS, 1, MLSTM_HEAD_DIM), F32),
                        pltpu.VMEM((MLSTM_HEADS, 1, LANES), F32)],
        compiler_params=_cparams("arbitrary", "arbitrary"),
        name="mlstm",
    )(mq, mk, mv, gates, proj)


def _dsw_kernel(q_ref, k_ref, v_ref, y_ref, o_scr, lse_scr, rg_scr):
    seq, dh = q_ref.shape
    blk = DSW_BLOCK
    qi = lax.broadcasted_iota(jnp.int32, (blk, 2 * blk), 0)
    ki = lax.broadcasted_iota(jnp.int32, (blk, 2 * blk), 1)
    dist = qi + blk - ki
    scale = dh ** -0.5
    units = seq // blk
    for g, (window, dil) in enumerate(DSW_CONFIGS):
        nsub = seq // dil // blk
        band = (dist >= 0) & (dist <= window // dil)
        regroup = dil >= REGROUP_MIN_DIL
        if regroup:
            per = seq // dil
            for j, src in enumerate((q_ref, k_ref, v_ref)):
                for r in range(dil):
                    rg_scr[j, r * per:(r + 1) * per, :] = src[pl.ds(r, per, stride=dil), :]
            q_src, k_src, v_src = rg_scr.at[0], rg_scr.at[1], rg_scr.at[2]
            step = 1
        else:
            q_src, k_src, v_src = q_ref, k_ref, v_ref
            step = dil

        def unit(u, carry, g=g, dil=dil, nsub=nsub, band=band, step=step, regroup=regroup,
                 q_src=q_src, k_src=k_src, v_src=v_src):
            r = u // nsub
            sb = u % nsub
            first = r * (seq // dil) if regroup else r
            cur = pl.ds(first + sb * (blk * step), blk, stride=step)
            prev = pl.ds(first + jnp.maximum(sb - 1, 0) * (blk * step), blk, stride=step)
            kk = jnp.concatenate([k_src[prev, :], k_src[cur, :]], axis=0).astype(BF16)
            vv = jnp.concatenate([v_src[prev, :], v_src[cur, :]], axis=0).astype(BF16)
            s = _dot_nt(q_src[cur, :].astype(BF16), kk) * scale
            s = jnp.where(band & ((ki >= blk) | (sb > 0)), s, NEG_INF)
            m = jnp.max(s, axis=1, keepdims=True)
            p = jnp.exp(s - m)
            l = jnp.sum(p, axis=1, keepdims=True)
            o_scr[g, cur, :] = _dot(p.astype(BF16), vv) / l
            lse_scr[g, cur, :] = jnp.broadcast_to(m + jnp.log(l), (blk, dh))
            return carry

        lax.fori_loop(0, units, unit, 0, unroll=8)
        if regroup:
            for j, scr in enumerate((o_scr, lse_scr)):
                rg_scr[j] = scr[g]
                for r in range(dil):
                    scr[g, pl.ds(r, per, stride=dil), :] = rg_scr[j, r * per:(r + 1) * per, :]

    lses = [lse_scr[g] for g in range(len(DSW_CONFIGS))]
    mx = functools.reduce(jnp.maximum, lses)
    ws = [jnp.exp(l - mx) for l in lses]
    acc = sum(ws[g] * o_scr[g] for g in range(len(DSW_CONFIGS)))
    y_ref[...] = (acc / sum(ws)).astype(y_ref.dtype)


def _dsw_attention(qd, kd, proj):
    bsz, seq, w = qd.shape
    dh = DSW_HEAD_DIM
    v_col = (proj.shape[-1] - w) // dh
    head = lambda b, h: (b, 0, h)
    return pl.pallas_call(
        _dsw_kernel,
        grid=(bsz, DSW_HEADS),
        in_specs=[pl.BlockSpec((None, seq, dh), head), pl.BlockSpec((None, seq, dh), head),
                  pl.BlockSpec((None, seq, dh), lambda b, h: (b, 0, v_col + h))],
        out_specs=pl.BlockSpec((None, seq, dh), head),
        out_shape=jax.ShapeDtypeStruct((bsz, seq, w), BF16),
        scratch_shapes=[pltpu.VMEM((len(DSW_CONFIGS), seq, dh), F32),
                        pltpu.VMEM((len(DSW_CONFIGS), seq, dh), F32),
                        pltpu.VMEM((3, seq, dh), F32)],
        compiler_params=_cparams("arbitrary", "arbitrary"),
        name="dsw_attention",
    )(qd, kd, proj)


def kernel(x, positions, ab_w_in, ab_conv_w, ab_conv_b, ab_gate_r_w, ab_gate_r_b, ab_gate_i_w, ab_gate_i_b, ab_lru_lambda, ab_s5_lambda_re, ab_s5_lambda_im, ab_s5_log_step, ab_s5_b_re, ab_s5_b_im, ab_s5_c_re, ab_s5_c_im, ab_s5_d, ab_glu_w, ab_glu_b, ab_w_out, cd_w_in, cd_conv_w, cd_conv_b, cd_w_q, cd_w_k, cd_w_v, cd_w_if, cd_b_if, cd_w_out, ln_mix_g, ln_mix_b, ln_ffn_g, ln_ffn_b, moe_router_w, moe_router_b, moe_w_in, moe_b_in, moe_w_out, moe_b_out):
    bsz, seq, d = x.shape
    t = bsz * seq
    for layer in range(DEPTH):
        j = layer // 2
        if layer % 2 == 0:
            proj = _inproj_time_major(x, ab_w_in[j].astype(BF16))
            ya = _rg_lru(proj, ab_conv_w[j], ab_conv_b[j], ab_gate_r_w[j], ab_gate_r_b[j],
                         ab_gate_i_w[j], ab_gate_i_b[j], ab_lru_lambda[j])
            yb = _s5_glu(proj, ab_s5_lambda_re[j], ab_s5_lambda_im[j], ab_s5_log_step[j], ab_s5_b_re[j],
                         ab_s5_b_im[j], ab_s5_c_re[j], ab_s5_c_im[j], ab_s5_d[j], ab_glu_w[j], ab_glu_b[j])
            time_major, w_out = True, ab_w_out[j]
        else:
            proj = _inproj(x, cd_w_in[j].astype(BF16))
            mq, mk, mv, gates, qd, kd = _cd_prep(proj, positions, cd_conv_w[j], cd_conv_b[j], cd_w_q[j],
                                                 cd_w_k[j], cd_w_v[j], cd_w_if[j], cd_b_if[j])
            ya = _mlstm(mq, mk, mv, gates, proj)
            yb = _dsw_attention(qd, kd, proj)
            time_major, w_out = False, cd_w_out[j]
        xn, x_tiles, route, counts = _outproj_ln_router(
            ya, yb, time_major, w_out.astype(BF16), x, ln_mix_g[layer], ln_mix_b[layer],
            moe_router_w[layer], moe_router_b[layer])
        x = _moe_ffn_ln(xn.reshape(t, d), x_tiles, route.reshape(t, LANES), counts, layer, moe_w_in, moe_b_in,
                        moe_w_out, moe_b_out, ln_ffn_g[layer], ln_ffn_b[layer]).reshape(bsz, seq, d)
    return x
```

```python
import functools

import jax
import jax.numpy as jnp
from jax import lax
from jax.experimental import pallas as pl
from jax.experimental.pallas import tpu as pltpu

F32 = jnp.float32
BF16 = jnp.bfloat16

D_MODEL = 1024
DEPTH = 2
A_WIDTH = 512
B_WIDTH = 512
C_WIDTH = 512
D_WIDTH = 512
LRU_HEADS = 8
LRU_HEAD_DIM = A_WIDTH // LRU_HEADS
LRU_C = 8.0
CONV_WIDTH = 4
S5_GROUP_CH = 16
S5_GROUPS = B_WIDTH // S5_GROUP_CH
S5_STATE = 64
MLSTM_HEADS = 4
MLSTM_HEAD_DIM = C_WIDTH // MLSTM_HEADS
DSW_HEADS = 4
DSW_HEAD_DIM = D_WIDTH // DSW_HEADS
DSW_CONFIGS = ((128, 1), (512, 4), (2048, 16))
DSW_BLOCK = 128
ROPE_THETA = 500000.0
ROPE_DIMS = DSW_HEAD_DIM // 4
N_EXPERTS = 32
TOP_K = 4
D_EXPERT = D_MODEL
SWIGLU_LIMIT = 7.0
SWIGLU_ALPHA = 1.702
DEEPNORM_ALPHA = (2 * DEPTH) ** 0.25
LN_EPS = 1e-5

VMEM_LIMIT_BYTES = 56 * 1024 * 1024
LANES = 128
TILE_ROWS = D_MODEL // LANES
MOE_TM = 512
MOE_TK = 256
PROJ_TL = 512
ROUTE_TS = 128
LRU_TC = 128
S5_TC = 64
S5_HALF = 2
MLSTM_CS = 256
NEG_INF = float("-inf")


def _cparams(*sem):
    return pltpu.CompilerParams(dimension_semantics=sem, vmem_limit_bytes=VMEM_LIMIT_BYTES)


def _dot(a, b):
    return jnp.dot(a, b, preferred_element_type=F32)


def _dot_nt(a, b):
    return lax.dot_general(a, b, (((1,), (1,)), ((), ())), preferred_element_type=F32)


def _matmul_kernel(x_ref, w_ref, o_ref):
    o_ref[...] = _dot(x_ref[...].astype(BF16), w_ref[...])


def _matmul_time_major_kernel(x_ref, w_ref, o_ref, scr):
    bsz, tl, _ = x_ref.shape
    nchunk = scr.shape[0]
    for b in range(bsz):
        p = _dot(x_ref[b].astype(BF16), w_ref[...])
        for c in range(nchunk):
            scr[c, pl.ds(b, tl, stride=bsz), :] = p[:, c * LANES:(c + 1) * LANES]
    for c in range(nchunk):
        o_ref[:, c * LANES:(c + 1) * LANES] = scr[c]


def _inproj_time_major(x3, w_bf16):
    bsz, seq, k = x3.shape
    n = w_bf16.shape[1]
    tl = LRU_TC
    out = pl.pallas_call(
        _matmul_time_major_kernel,
        grid=(seq // tl,),
        in_specs=[pl.BlockSpec((bsz, tl, k), lambda i: (0, i, 0)),
                  pl.BlockSpec((k, n), lambda i: (0, 0))],
        out_specs=pl.BlockSpec((tl * bsz, n), lambda i: (i, 0)),
        out_shape=jax.ShapeDtypeStruct((seq * bsz, n), F32),
        scratch_shapes=[pltpu.VMEM((n // LANES, tl * bsz, LANES), F32)],
        compiler_params=_cparams("arbitrary"),
        name="in_proj_time_major",
    )(x3, w_bf16)
    return out.reshape(seq, bsz, n)


def _inproj(x3, w_bf16):
    bsz, seq, k = x3.shape
    n = w_bf16.shape[1]
    return pl.pallas_call(
        _matmul_kernel,
        grid=(bsz, seq // PROJ_TL),
        in_specs=[pl.BlockSpec((None, PROJ_TL, k), lambda b, i: (b, i, 0)),
                  pl.BlockSpec((k, n), lambda b, i: (0, 0))],
        out_specs=pl.BlockSpec((None, PROJ_TL, n), lambda b, i: (b, i, 0)),
        out_shape=jax.ShapeDtypeStruct((bsz, seq, n), F32),
        compiler_params=_cparams("arbitrary", "arbitrary"),
        name="in_proj",
    )(x3, w_bf16)


def _layer_norm_rows(z, g, b):
    mu = jnp.mean(z, axis=-1, keepdims=True)
    zc = z - mu
    var = jnp.mean(zc * zc, axis=-1, keepdims=True)
    return zc * lax.rsqrt(var + LN_EPS) * g + b


def _outproj_ln_router_kernel(ya_ref, yb_ref, wa_ref, wb_ref, x_ref, g_ref, b_ref, rwh_ref, rwl_ref, rb_ref,
                              tri_ref, xn_ref, xt_ref, rt_ref, cnt_ref):
    @pl.when((pl.program_id(0) == 0) & (pl.program_id(1) == 0))
    def _init():
        cnt_ref[...] = jnp.zeros_like(cnt_ref)

    mix = _dot(ya_ref[...], wa_ref[...]) + _dot(yb_ref[...], wb_ref[...])
    xn = _layer_norm_rows(DEEPNORM_ALPHA * x_ref[...] + mix, g_ref[...], b_ref[...])
    xn_ref[...] = xn
    _store_token_tiles(xt_ref, xn)
    xh = xn.astype(BF16)
    xl = (xn - xh.astype(F32)).astype(BF16)
    logits = (_dot(xh, rwh_ref[...]) + _dot(xl, rwh_ref[...]) + _dot(xh, rwl_ref[...])) + rb_ref[...]
    ts = tri_ref.shape[0]
    lane = lax.broadcasted_iota(jnp.int32, (ts, LANES), 1).astype(F32)
    counts = cnt_ref[...]
    for s in range(logits.shape[0] // ts):
        lg = logits[s * ts:(s + 1) * ts]
        tops, ids, hots = [], [], []
        for _ in range(TOP_K):
            m = jnp.max(lg, axis=1, keepdims=True)
            idx = jnp.min(jnp.where(lg == m, lane, float(LANES)), axis=1, keepdims=True)
            hot = lane == idx
            lg = jnp.where(hot, NEG_INF, lg)
            tops.append(m)
            ids.append(idx)
            hots.append(hot)
        exps = [jnp.exp(m - tops[0]) for m in tops]
        denom = functools.reduce(jnp.add, exps)
        chosen = functools.reduce(jnp.add, [h.astype(F32) for h in hots])
        before = _dot(tri_ref[...], chosen.astype(BF16)) + counts
        route = jnp.zeros((ts, LANES), F32)
        for k in range(TOP_K):
            rank = jnp.sum(jnp.where(hots[k], before, 0.0), axis=1, keepdims=True)
            route = jnp.where(lane == float(k), ids[k], route)
            route = jnp.where(lane == float(TOP_K + k), rank, route)
            route = jnp.where(lane == float(2 * TOP_K + k), exps[k] / denom, route)
        rt_ref[s * ts:(s + 1) * ts, :] = route
        counts = counts + jnp.sum(chosen, axis=0, keepdims=True)
    cnt_ref[...] = counts


def _outproj_ln_router(ya, yb, time_major, w_bf16, x3, g, b, rw, rb):
    bsz, seq, d = x3.shape
    wa, wb = w_bf16[:ya.shape[-1]], w_bf16[ya.shape[-1]:]
    ne = LANES
    rw = jnp.zeros((d, ne), F32).at[:, :N_EXPERTS].set(rw)
    rb = jnp.full((ne,), NEG_INF, F32).at[:N_EXPERTS].set(rb)
    rw_hi = rw.astype(BF16)
    rw_lo = (rw - rw_hi.astype(F32)).astype(BF16)
    tl = PROJ_TL
    tri = jnp.tril(jnp.ones((ROUTE_TS, ROUTE_TS), BF16), -1)
    if time_major:
        ya, yb = ya.reshape(seq, -1), yb.reshape(seq, -1)
        y_specs = [pl.BlockSpec((tl, wa.shape[0]), lambda bi, i: (i, bi)),
                   pl.BlockSpec((tl, wb.shape[0]), lambda bi, i: (i, bi))]
    else:
        y_specs = [pl.BlockSpec((None, tl, wa.shape[0]), lambda bi, i: (bi, i, 0)),
                   pl.BlockSpec((None, tl, wb.shape[0]), lambda bi, i: (bi, i, 0))]
    fixed = lambda bi, i: (0, 0)
    tok = lambda bi, i: (bi, i, 0)
    return pl.pallas_call(
        _outproj_ln_router_kernel,
        grid=(bsz, seq // tl),
        in_specs=y_specs + [pl.BlockSpec(wa.shape, fixed), pl.BlockSpec(wb.shape, fixed),
                            pl.BlockSpec((None, tl, d), tok),
                            pl.BlockSpec((1, d), fixed), pl.BlockSpec((1, d), fixed),
                            pl.BlockSpec((d, ne), fixed), pl.BlockSpec((d, ne), fixed),
                            pl.BlockSpec((1, ne), fixed), pl.BlockSpec((ROUTE_TS, ROUTE_TS), fixed)],
        out_specs=[pl.BlockSpec((None, tl, d), tok),
                   pl.BlockSpec((tl * TILE_ROWS, LANES), lambda bi, i: (bi * (seq // tl) + i, 0)),
                   pl.BlockSpec((None, tl, ne), tok), pl.BlockSpec((1, ne), fixed)],
        out_shape=[jax.ShapeDtypeStruct((bsz, seq, d), F32),
                   jax.ShapeDtypeStruct((bsz * seq * TILE_ROWS, LANES), F32),
                   jax.ShapeDtypeStruct((bsz, seq, ne), F32), jax.ShapeDtypeStruct((1, ne), F32)],
        compiler_params=_cparams("arbitrary", "arbitrary"),
        name="out_proj_ln_router",
    )(ya, yb, wa, wb, x3, g.reshape(1, d), b.reshape(1, d), rw_hi, rw_lo, rb.reshape(1, ne), tri)


def _store_token_tiles(dst_ref, x):
    n = x.shape[0]
    for j in range(TILE_ROWS):
        dst_ref[pl.ds(j, n, stride=TILE_ROWS), :] = x[:, j * LANES:(j + 1) * LANES]


def _load_token_tiles(src_ref, n):
    return jnp.concatenate([src_ref[pl.ds(j, n, stride=TILE_ROWS), :] for j in range(TILE_ROWS)], axis=1)


def _gather_token_tiles(src_hbm, idx_ref, n, dst, sem, priorities):
    def body(g, carry):
        base = pl.multiple_of(g * (TILE_ROWS * TILE_ROWS), TILE_ROWS * TILE_ROWS)
        for u in range(TILE_ROWS):
            src_row = pl.multiple_of(idx_ref[0, g * TILE_ROWS + u], TILE_ROWS)
            pltpu.make_async_copy(src_hbm.at[pl.ds(src_row, TILE_ROWS)],
                                  dst.at[pl.ds(base + u * TILE_ROWS, TILE_ROWS)], sem).start(priority=u % priorities)
        return carry
    lax.fori_loop(0, n // TILE_ROWS, body, 0, unroll=2)


def _moe_kernel(be_ref, nreal_ref, rt_ref, rtn_ref, x_hbm, rg_ref, win_ref, bin_ref, wout_ref, bout_ref, o_ref,
                win_bf, wout_bf, xbuf, sem):
    i = pl.program_id(0)
    n_real = nreal_ref[0]
    e = be_ref[i]
    prev = be_ref[jnp.maximum(i - 1, 0)]
    tm = xbuf.shape[1] // TILE_ROWS
    slot = lax.rem(i, 2)

    @pl.when(i == 0)
    def _first_gather():
        _gather_token_tiles(x_hbm, rt_ref, tm, xbuf.at[0], sem.at[0], priorities=1)

    @pl.when(i + 1 < n_real)
    def _next_gather():
        _gather_token_tiles(x_hbm, rtn_ref, tm, xbuf.at[1 - slot], sem.at[1 - slot], priorities=1)

    @pl.when((i == 0) | (e != prev))
    def _load_expert():
        win_bf[...] = win_ref[...].astype(BF16)
        wout_bf[...] = wout_ref[...].astype(BF16)

    @pl.when((i < n_real) | (i == 0))
    def _compute():
        pltpu.make_async_copy(xbuf.at[slot], xbuf.at[slot], sem.at[slot]).wait()
        x = _load_token_tiles(xbuf.at[slot], tm).astype(BF16)
        h = _dot(x, win_bf[...]) + bin_ref[...]
        g = jnp.minimum(h[:, :D_EXPERT], SWIGLU_LIMIT)
        lin = jnp.clip(h[:, D_EXPERT:], -SWIGLU_LIMIT, SWIGLU_LIMIT)
        y = g * jax.nn.sigmoid(SWIGLU_ALPHA * g) * (lin + 1.0)
        out = _dot(y.astype(BF16), wout_bf[...]) + bout_ref[...]
        gates = rg_ref[...].T
        out = jnp.concatenate([out[j * LANES:(j + 1) * LANES] * gates[:, j:j + 1] for j in range(tm // LANES)],
                              axis=0)
        _store_token_tiles(o_ref, out)

    @pl.when((i >= n_real) & (i > 0))
    def _unused_block():
        o_ref[...] = jnp.zeros_like(o_ref)


def _moe_experts(x_tiles, row_tok, row_gate, block_expert, n_real, layer, w_in, b_in, w_out, b_out):
    d = D_MODEL
    tm = MOE_TM
    n_blocks = row_tok.shape[0]
    nl, ne, _, dh2 = w_in.shape
    rt = (row_tok * TILE_ROWS).reshape(n_blocks, 1, tm)
    rg = jnp.pad(row_gate.reshape(n_blocks, tm // LANES, LANES), ((0, 0), (0, TILE_ROWS - tm // LANES), (0, 0)))
    smem_rows = lambda imap: pl.BlockSpec((None, 1, tm), imap, memory_space=pltpu.SMEM)
    grid_spec = pltpu.PrefetchScalarGridSpec(
        num_scalar_prefetch=2,
        grid=(n_blocks,),
        in_specs=[
            smem_rows(lambda i, be, nr: (i, 0, 0)),
            smem_rows(lambda i, be, nr: (jnp.minimum(i + 1, n_blocks - 1), 0, 0)),
            pl.BlockSpec(memory_space=pl.ANY),
            pl.BlockSpec((None, TILE_ROWS, LANES), lambda i, be, nr: (i, 0, 0)),
            pl.BlockSpec((None, None, d, dh2), lambda i, be, nr: (layer, be[i], 0, 0)),
            pl.BlockSpec((None, None, 1, dh2), lambda i, be, nr: (layer, be[i], 0, 0)),
            pl.BlockSpec((None, None, D_EXPERT, d), lambda i, be, nr: (layer, be[i], 0, 0)),
            pl.BlockSpec((None, None, 1, d), lambda i, be, nr: (layer, be[i], 0, 0)),
        ],
        out_specs=pl.BlockSpec((tm * TILE_ROWS, LANES), lambda i, be, nr: (i, 0)),
        scratch_shapes=[pltpu.VMEM((d, dh2), BF16), pltpu.VMEM((D_EXPERT, d), BF16),
                        pltpu.VMEM((2, tm * TILE_ROWS, LANES), F32), pltpu.SemaphoreType.DMA((2,))],
    )
    return pl.pallas_call(
        _moe_kernel,
        grid_spec=grid_spec,
        out_shape=jax.ShapeDtypeStruct((n_blocks * tm * TILE_ROWS, LANES), F32),
        compiler_params=_cparams("arbitrary"),
        name="moe_experts",
    )(block_expert, n_real, rt, rt, x_tiles, rg, w_in,
      b_in.reshape(nl, ne, 1, dh2), w_out, b_out.reshape(nl, ne, 1, d))


def _moe_combine_kernel(dc_ref, dn_ref, x_ref, yb_hbm, g_ref, b_ref, o_ref, buf, ffn_ref, sem):
    i = pl.program_id(0)
    n = pl.num_programs(0)
    tk = x_ref.shape[0]
    rows = tk * TILE_ROWS
    slot = lax.rem(i, 2)

    @pl.when(i == 0)
    def _first_gather():
        _gather_token_tiles(yb_hbm, dc_ref, TOP_K * tk, buf.at[0], sem.at[0], priorities=2)

    @pl.when(i + 1 < n)
    def _next_gather():
        _gather_token_tiles(yb_hbm, dn_ref, TOP_K * tk, buf.at[1 - slot], sem.at[1 - slot], priorities=2)

    pltpu.make_async_copy(buf.at[slot], buf.at[slot], sem.at[slot]).wait()
    ffn = buf[slot, 0:rows]
    for k in range(1, TOP_K):
        ffn = ffn + buf[slot, k * rows:(k + 1) * rows]
    ffn_ref[...] = ffn
    o_ref[...] = _layer_norm_rows(DEEPNORM_ALPHA * x_ref[...] + _load_token_tiles(ffn_ref, tk),
                                  g_ref[...], b_ref[...])


def _moe_combine_ln(xn, dest, yb_tiles, g, b):
    t, d = xn.shape
    tk = MOE_TK
    n = t // tk
    dest_tiles = (dest * TILE_ROWS).reshape(n, tk, TOP_K).transpose(0, 2, 1).reshape(n, 1, TOP_K * tk)
    smem_rows = lambda imap: pl.BlockSpec((None, 1, TOP_K * tk), imap, memory_space=pltpu.SMEM)
    row = lambda i: (i, 0)
    fixed = lambda i: (0, 0)
    return pl.pallas_call(
        _moe_combine_kernel,
        grid=(n,),
        in_specs=[smem_rows(lambda i: (i, 0, 0)), smem_rows(lambda i: (jnp.minimum(i + 1, n - 1), 0, 0)),
                  pl.BlockSpec((tk, d), row), pl.BlockSpec(memory_space=pl.ANY),
                  pl.BlockSpec((1, d), fixed), pl.BlockSpec((1, d), fixed)],
        out_specs=pl.BlockSpec((tk, d), row),
        out_shape=jax.ShapeDtypeStruct((t, d), F32),
        scratch_shapes=[pltpu.VMEM((2, TOP_K * tk * TILE_ROWS, LANES), F32),
                        pltpu.VMEM((tk * TILE_ROWS, LANES), F32), pltpu.SemaphoreType.DMA((2,))],
        compiler_params=_cparams("arbitrary"),
        name="moe_combine_ln",
    )(dest_tiles, dest_tiles, xn, yb_tiles, g.reshape(1, d), b.reshape(1, d))


def _moe_ffn_ln(xn, x_tiles, route, counts, layer, w_in, b_in, w_out, b_out, g, b):
    t, d = xn.shape
    tm = MOE_TM
    flat_e = route[:, :TOP_K].astype(jnp.int32).reshape(-1)
    rank = route[:, TOP_K:2 * TOP_K].astype(jnp.int32).reshape(-1)
    gate = route[:, 2 * TOP_K:3 * TOP_K].reshape(-1)
    counts = counts[0, :N_EXPERTS].astype(jnp.int32)
    blocks_per_e = (counts + tm - 1) // tm
    blk_end = jnp.cumsum(blocks_per_e)
    blk_start = blk_end - blocks_per_e
    row_start = jnp.cumsum(counts) - counts
    dest = blk_start[flat_e] * tm + rank
    n_blocks = (t * TOP_K) // tm + N_EXPERTS
    block_expert = jnp.minimum(
        jnp.sum((blk_end[None, :] <= jnp.arange(n_blocks, dtype=jnp.int32)[:, None]).astype(jnp.int32), axis=1),
        N_EXPERTS - 1)
    n_real = blk_end[-1:].astype(jnp.int32)
    order = jnp.argsort(flat_e, stable=True).astype(jnp.int32)
    blk = jnp.arange(n_blocks, dtype=jnp.int32)
    rank_row = ((blk - blk_start[block_expert]) * tm)[:, None] + jnp.arange(tm, dtype=jnp.int32)[None, :]
    valid = rank_row < counts[block_expert][:, None]
    pair = order[jnp.clip(row_start[block_expert][:, None] + rank_row, 0, t * TOP_K - 1)]
    row_tok = jnp.where(valid, pair // TOP_K, 0)
    row_gate = jnp.where(valid, gate[pair], 0.0)
    yb_tiles = _moe_experts(x_tiles, row_tok, row_gate, block_expert, n_real, layer, w_in, b_in, w_out, b_out)
    return _moe_combine_ln(xn, dest, yb_tiles, g, b)


def _softplus(x):
    return jnp.maximum(x, 0.0) + jnp.log(1.0 + jnp.exp(-jnp.abs(x)))


def _gelu_tanh(x):
    return 0.5 * x * (1.0 + jnp.tanh(0.7978845608028654 * (x + 0.044715 * (x * x * x))))


def _lru_kernel(xa_ref, ga_ref, cw_ref, cb_ref, wr_ref, br_ref, wi_ref, bi_ref, lam_ref, o_ref,
                tail_ref, h_ref, a_ref, u_ref):
    tc, bsz, w = xa_ref.shape

    @pl.when(pl.program_id(0) == 0)
    def _init():
        tail_ref[...] = jnp.zeros_like(tail_ref)
        h_ref[...] = jnp.zeros_like(h_ref)

    xa = xa_ref[...]
    ext = jnp.concatenate([tail_ref[...], xa], axis=0)
    xc = cb_ref[...] + ext[0:tc] * cw_ref[0]
    for tap in range(1, CONV_WIDTH):
        xc = xc + ext[tap:tap + tc] * cw_ref[tap]
    tail_ref[...] = xa[tc - (CONV_WIDTH - 1):]
    x2 = xc.reshape(tc * bsz, w)
    xb = x2.astype(BF16)
    r = jax.nn.sigmoid(_dot(xb, wr_ref[...]) + br_ref[...])
    ig = jax.nn.sigmoid(_dot(xb, wi_ref[...]) + bi_ref[...])
    log_a = (-LRU_C) * r * _softplus(-lam_ref[...])
    a_ref[...] = jnp.exp(log_a).reshape(tc, bsz, w)
    u_ref[...] = (jnp.sqrt(1.0 - jnp.exp(2.0 * log_a)) * (ig * x2)).reshape(tc, bsz, w)

    def step(t, h):
        h = a_ref[t] * h + u_ref[t]
        u_ref[t] = h
        return h

    h_ref[...] = lax.fori_loop(0, tc, step, h_ref[...], unroll=8)
    o_ref[...] = (u_ref[...] * _gelu_tanh(ga_ref[...])).astype(o_ref.dtype)


def _block_diag(w):
    h, i, j = w.shape
    return jnp.einsum('hij,hk->hikj', w, jnp.eye(h, dtype=w.dtype)).reshape(h * i, h * j)


def _rg_lru(proj, conv_w, conv_b, gate_r_w, gate_r_b, gate_i_w, gate_i_b, lru_lambda):
    seq, bsz, _ = proj.shape
    w = A_WIDTH
    tc = LRU_TC
    fixed2 = lambda i: (0, 0)
    fixed3 = lambda i: (0, 0, 0)
    return pl.pallas_call(
        _lru_kernel,
        grid=(seq // tc,),
        in_specs=[pl.BlockSpec((tc, bsz, w), lambda i: (i, 0, 0)),
                  pl.BlockSpec((tc, bsz, w), lambda i: (i, 0, 1)),
                  pl.BlockSpec((CONV_WIDTH, 1, w), fixed3), pl.BlockSpec((1, w), fixed2),
                  pl.BlockSpec((w, w), fixed2), pl.BlockSpec((1, w), fixed2),
                  pl.BlockSpec((w, w), fixed2), pl.BlockSpec((1, w), fixed2),
                  pl.BlockSpec((1, w), fixed2)],
        out_specs=pl.BlockSpec((tc, bsz, w), lambda i: (i, 0, 0)),
        out_shape=jax.ShapeDtypeStruct((seq, bsz, w), BF16),
        scratch_shapes=[pltpu.VMEM((CONV_WIDTH - 1, bsz, w), F32), pltpu.VMEM((bsz, w), F32),
                        pltpu.VMEM((tc, bsz, w), F32), pltpu.VMEM((tc, bsz, w), F32)],
        compiler_params=_cparams("arbitrary"),
        name="rg_lru",
    )(proj, proj, conv_w.reshape(CONV_WIDTH, 1, w), conv_b.reshape(1, w),
      _block_diag(gate_r_w).astype(BF16), gate_r_b.reshape(1, w),
      _block_diag(gate_i_w).astype(BF16), gate_i_b.reshape(1, w), lru_lambda.reshape(1, w))


def _s5_kernel(u_ref, wbr_ref, wbi_ref, ar_ref, ai_ref, ccr_ref, cci_ref, d_ref, gw_ref, gb_ref, o_ref,
               xr_ref, xi_ref, sr_ref, si_ref):
    tc, bsz, w = u_ref.shape
    nstate = xr_ref.shape[-1]
    sh = nstate // S5_HALF
    wh = w // S5_HALF

    @pl.when(pl.program_id(0) == 0)
    def _init():
        sr_ref[...] = jnp.zeros_like(sr_ref)
        si_ref[...] = jnp.zeros_like(si_ref)

    u2 = u_ref[...].reshape(tc * bsz, w)
    ub = u2.astype(BF16)
    for hf in range(S5_HALF):
        uh = ub[:, hf * wh:(hf + 1) * wh]
        xr_ref[:, :, hf * sh:(hf + 1) * sh] = _dot(uh, wbr_ref[hf]).reshape(tc, bsz, sh)
        xi_ref[:, :, hf * sh:(hf + 1) * sh] = _dot(uh, wbi_ref[hf]).reshape(tc, bsz, sh)

    for hf in range(S5_HALF):
        lo, hi = hf * sh, (hf + 1) * sh
        ar = jnp.broadcast_to(ar_ref[:, lo:hi], (bsz, sh))
        ai = jnp.broadcast_to(ai_ref[:, lo:hi], (bsz, sh))

        def step(t, carry, lo=lo, hi=hi, ar=ar, ai=ai):
            xr, xi = carry
            nxr = ar * xr - ai * xi + xr_ref[t, :, lo:hi]
            nxi = ar * xi + ai * xr + xi_ref[t, :, lo:hi]
            xr_ref[t, :, lo:hi] = nxr
            xi_ref[t, :, lo:hi] = nxi
            return nxr, nxi

        xr, xi = lax.fori_loop(0, tc, step, (sr_ref[:, lo:hi], si_ref[:, lo:hi]), unroll=4)
        sr_ref[:, lo:hi] = xr
        si_ref[:, lo:hi] = xi

    ys = []
    for hf in range(S5_HALF):
        lo, hi = hf * sh, (hf + 1) * sh
        xrh = xr_ref[:, :, lo:hi].reshape(tc * bsz, sh).astype(BF16)
        xih = xi_ref[:, :, lo:hi].reshape(tc * bsz, sh).astype(BF16)
        ys.append(_dot(xrh, ccr_ref[hf]) - _dot(xih, cci_ref[hf]))
    y = jnp.concatenate(ys, axis=1) + d_ref[...] * u2
    s = _gelu_tanh(y)
    yb = s * jax.nn.sigmoid(_dot(s.astype(BF16), gw_ref[...]) + gb_ref[...])
    o_ref[...] = yb.reshape(tc, bsz, w).astype(o_ref.dtype)


def _s5_glu(proj, lam_re, lam_im, log_step, b_re, b_im, c_re, c_im, d_skip, glu_w, glu_b):
    seq, bsz, _ = proj.shape
    w = B_WIDTH
    tc = S5_TC
    nstate = S5_GROUPS * S5_STATE
    gh = S5_GROUPS // S5_HALF
    lr = jnp.minimum(lam_re, -1e-4)
    li = lam_im
    step = jnp.exp(log_step)[:, None]
    mag = jnp.exp(lr * step)
    ar = mag * jnp.cos(li * step)
    ai = mag * jnp.sin(li * step)
    inv = 1.0 / (lr * lr + li * li)
    zr = ((ar - 1.0) * lr + ai * li) * inv
    zi = (ai * lr - (ar - 1.0) * li) * inv
    bbr = zr[..., None] * b_re - zi[..., None] * b_im
    bbi = zr[..., None] * b_im + zi[..., None] * b_re
    eye = jnp.eye(gh, dtype=F32)

    def expand_in(bb):
        bb = bb.reshape(S5_HALF, gh, S5_STATE, S5_GROUP_CH)
        return jnp.einsum('fgph,gk->fghkp', bb, eye).reshape(
            S5_HALF, gh * S5_GROUP_CH, gh * S5_STATE).astype(BF16)

    def expand_out(c):
        c = c.reshape(S5_HALF, gh, S5_GROUP_CH, S5_STATE)
        return jnp.einsum('fghp,gk->fgpkh', c, eye).reshape(
            S5_HALF, gh * S5_STATE, gh * S5_GROUP_CH).astype(BF16)

    fixed2 = lambda i: (0, 0)
    fixed3 = lambda i: (0, 0, 0)
    return pl.pallas_call(
        _s5_kernel,
        grid=(seq // tc,),
        in_specs=[pl.BlockSpec((tc, bsz, w), lambda i: (i, 0, 2)),
                  pl.BlockSpec((S5_HALF, w // S5_HALF, nstate // S5_HALF), fixed3),
                  pl.BlockSpec((S5_HALF, w // S5_HALF, nstate // S5_HALF), fixed3),
                  pl.BlockSpec((1, nstate), fixed2), pl.BlockSpec((1, nstate), fixed2),
                  pl.BlockSpec((S5_HALF, nstate // S5_HALF, w // S5_HALF), fixed3),
                  pl.BlockSpec((S5_HALF, nstate // S5_HALF, w // S5_HALF), fixed3),
                  pl.BlockSpec((1, w), fixed2), pl.BlockSpec((w, w), fixed2), pl.BlockSpec((1, w), fixed2)],
        out_specs=pl.BlockSpec((tc, bsz, w), lambda i: (i, 0, 0)),
        out_shape=jax.ShapeDtypeStruct((seq, bsz, w), BF16),
        scratch_shapes=[pltpu.VMEM((tc, bsz, nstate), F32), pltpu.VMEM((tc, bsz, nstate), F32),
                        pltpu.VMEM((bsz, nstate), F32), pltpu.VMEM((bsz, nstate), F32)],
        compiler_params=_cparams("arbitrary"),
        name="s5_glu",
    )(proj, expand_in(bbr), expand_in(bbi), ar.reshape(1, nstate), ai.reshape(1, nstate),
      expand_out(c_re), expand_out(c_im), d_skip.reshape(1, w), glu_w.astype(BF16), glu_b.reshape(1, w))


def _cd_prep_kernel(xc_ref, q_ref, k_ref, pos_ref, cw_ref, cb_ref, wq_ref, wk_ref, wv_ref, wif_ref,
                    bif_ref, invf_ref, mq_ref, mk_ref, mv_ref, gt_ref, qd_ref, kd_ref, tail_ref):
    tl, w = xc_ref.shape
    pad = tail_ref.shape[0]

    @pl.when(pl.program_id(1) == 0)
    def _init():
        tail_ref[...] = jnp.zeros_like(tail_ref)

    xc = xc_ref[...]
    ext = jnp.concatenate([tail_ref[...], xc], axis=0)
    conv = cb_ref[...]
    for tap in range(CONV_WIDTH):
        off = pad - (CONV_WIDTH - 1) + tap
        conv = conv + ext[off:off + tl] * cw_ref[tap]
    tail_ref[...] = xc[tl - pad:]
    xconv = (conv * jax.nn.sigmoid(conv)).astype(BF16)
    q = _dot(xconv, wq_ref[...]).astype(BF16)
    k = (_dot(xconv, wk_ref[...]) * (MLSTM_HEAD_DIM ** -0.5)).astype(BF16)
    v = _dot(xc.astype(BF16), wv_ref[...]).astype(BF16)
    mq_ref[...] = q
    mk_ref[...] = k
    mv_ref[...] = v
    gt_ref[...] = (_dot(q, wif_ref[0]) + _dot(k, wif_ref[1]) + _dot(v, wif_ref[2]) + bif_ref[...])

    ang = pos_ref[...].astype(F32) * invf_ref[...]
    cos = jnp.cos(ang)
    sin = jnp.sin(ang)
    lane = lax.broadcasted_iota(jnp.int32, (1, LANES), 1)
    half = ROPE_DIMS // 2
    sin_lo = jnp.where(lane < half, -sin, 0.0)
    sin_hi = jnp.where((lane >= half) & (lane < ROPE_DIMS), sin, 0.0)
    for src, dst in ((q_ref, qd_ref), (k_ref, kd_ref)):
        for h in range(DSW_HEADS):
            sl = slice(h * DSW_HEAD_DIM, (h + 1) * DSW_HEAD_DIM)
            t = src[:, sl]
            rot = (t * cos + pltpu.roll(t, LANES - half, axis=1) * sin_lo
                   + pltpu.roll(t, half, axis=1) * sin_hi)
            dst[:, sl] = rot


def _cd_prep(proj, positions, conv_w, conv_b, w_q, w_k, w_v, w_if, b_if):
    bsz, seq, _ = proj.shape
    w = C_WIDTH
    tl = PROJ_TL
    half = ROPE_DIMS // 2
    inv_freq = ROPE_THETA ** (-jnp.arange(half, dtype=F32) / half)
    invf = jnp.zeros((1, LANES), F32).at[0, :half].set(inv_freq).at[0, half:ROPE_DIMS].set(inv_freq)
    wif = jnp.zeros((3, w, LANES), F32).at[:, :, :2 * MLSTM_HEADS].set(w_if.reshape(3, w, 2 * MLSTM_HEADS))
    bif = jnp.zeros((1, LANES), F32).at[0, :2 * MLSTM_HEADS].set(b_if)
    col = lambda c: (lambda b, i: (b, i, c))
    fixed2 = lambda b, i: (0, 0)
    fixed3 = lambda b, i: (0, 0, 0)
    tok_bf = jax.ShapeDtypeStruct((bsz, seq, w), BF16)
    return pl.pallas_call(
        _cd_prep_kernel,
        grid=(bsz, seq // tl),
        in_specs=[pl.BlockSpec((None, tl, w), col(0)), pl.BlockSpec((None, tl, w), col(2)),
                  pl.BlockSpec((None, tl, w), col(3)), pl.BlockSpec((None, tl, 1), col(0)),
                  pl.BlockSpec((CONV_WIDTH, 1, w), fixed3), pl.BlockSpec((1, w), fixed2),
                  pl.BlockSpec((w, w), fixed2), pl.BlockSpec((w, w), fixed2), pl.BlockSpec((w, w), fixed2),
                  pl.BlockSpec((3, w, LANES), fixed3), pl.BlockSpec((1, LANES), fixed2),
                  pl.BlockSpec((1, LANES), fixed2)],
        out_specs=[pl.BlockSpec((None, tl, w), col(0))] * 3 + [pl.BlockSpec((None, tl, LANES), col(0))]
                  + [pl.BlockSpec((None, tl, w), col(0))] * 2,
        out_shape=[tok_bf, tok_bf, tok_bf, jax.ShapeDtypeStruct((bsz, seq, LANES), F32),
                   jax.ShapeDtypeStruct((bsz, seq, w), F32), jax.ShapeDtypeStruct((bsz, seq, w), F32)],
        scratch_shapes=[pltpu.VMEM((8, w), F32)],
        compiler_params=_cparams("arbitrary", "arbitrary"),
        name="cd_prep",
    )(proj, proj, proj, positions.reshape(bsz, seq, 1), conv_w.reshape(CONV_WIDTH, 1, w),
      conv_b.reshape(1, w), _block_diag(w_q).astype(BF16), _block_diag(w_k).astype(BF16),
      _block_diag(w_v).astype(BF16), wif.astype(BF16), bif, invf)


def _cumsum_rows(x):
    n = x.shape[0]
    row = lax.broadcasted_iota(jnp.int32, x.shape, 0)
    shift = 1
    while shift < n:
        x = x + jnp.where(row >= shift, pltpu.roll(x, shift, axis=0), 0.0)
        shift *= 2
    return x


def _mlstm_kernel(q_ref, k_ref, v_ref, g_ref, op_ref, o_ref, c_ref, n_ref, m_ref):
    cs = q_ref.shape[0]
    dh = MLSTM_HEAD_DIM

    @pl.when(pl.program_id(1) == 0)
    def _init():
        c_ref[...] = jnp.zeros_like(c_ref)
        n_ref[...] = jnp.zeros_like(n_ref)
        m_ref[...] = jnp.full_like(m_ref, NEG_INF)

    gates = g_ref[...]
    cum = _cumsum_rows(jax.nn.log_sigmoid(gates))
    gates_t = gates.T
    cum_t = cum.T
    causal = (lax.broadcasted_iota(jnp.int32, (cs, cs), 0) >= lax.broadcasted_iota(jnp.int32, (cs, cs), 1))
    for h in range(MLSTM_HEADS):
        sl = slice(h * dh, (h + 1) * dh)
        fcol = MLSTM_HEADS + h
        li_c, li_r = gates[:, h:h + 1], gates_t[h:h + 1, :]
        cum_c, cum_r = cum[:, fcol:fcol + 1], cum_t[fcol:fcol + 1, :]
        m_prev = m_ref[h][:, 0:1]
        q, k, v = q_ref[:, sl], k_ref[:, sl], v_ref[:, sl]
        log_d = jnp.where(causal, cum_c - cum_r + li_r, NEG_INF)
        log_inter = cum_c + m_prev
        m_s = jnp.maximum(jnp.max(log_d, axis=1, keepdims=True), log_inter)
        s = _dot_nt(q, k) * jnp.exp(log_d - m_s)
        inter = jnp.exp(log_inter - m_s)
        c_prev = c_ref[h]
        n_prev = n_ref[h]
        num = _dot(s.astype(BF16), v) + inter * _dot(q, c_prev.astype(BF16))
        den = (jnp.sum(s, axis=1, keepdims=True)
               + inter * jnp.sum(q.astype(F32) * n_prev, axis=1, keepdims=True))
        hh = num / jnp.maximum(jnp.abs(den), jnp.exp(-m_s))
        o_ref[:, sl] = (jax.nn.sigmoid(op_ref[:, sl]) * hh).astype(o_ref.dtype)
        chunk_f = cum_c[cs - 1:cs, :]
        to_end = chunk_f - cum_c + li_c
        m_c = jnp.max(to_end, axis=0, keepdims=True)
        m_new = jnp.maximum(chunk_f + m_prev, m_c)
        s_old = jnp.exp(chunk_f + m_prev - m_new)
        s_new = jnp.exp(m_c - m_new)
        kw = k.astype(F32) * jnp.exp(to_end - m_c)
        c_ref[h] = s_old * c_prev + s_new * _dot(kw.T.astype(BF16), v)
        n_ref[h] = s_old * n_prev + s_new * jnp.sum(kw, axis=0, keepdims=True)
        m_ref[h] = jnp.broadcast_to(m_new, (1, LANES))


def _mlstm(mq, mk, mv, gates, proj):
    bsz, seq, w = mq.shape
    cs = MLSTM_CS
    tok = lambda b, i: (b, i, 0)
    return pl.pallas_call(
        _mlstm_kernel,
        grid=(bsz, seq // cs),
        in_specs=[pl.BlockSpec((None, cs, w), tok)] * 3
                 + [pl.BlockSpec((None, cs, LANES), tok), pl.BlockSpec((None, cs, w), lambda b, i: (b, i, 1))],
        out_specs=pl.BlockSpec((None, cs, w), tok),
        out_shape=jax.ShapeDtypeStruct((bsz, seq, w), BF16),
        scratch_shapes=[pltpu.VMEM((MLSTM_HEADS, MLSTM_HEAD_DIM, MLSTM_HEAD_DIM), F32),
                        pltpu.VMEM((MLSTM_HEADS, 1, MLSTM_HEAD_DIM), F32),
                        pltpu.VMEM((MLSTM_HEADS, 1, LANES), F32)],
        compiler_params=_cparams("arbitrary", "arbitrary"),
        name="mlstm",
    )(mq, mk, mv, gates, proj)


def _dsw_kernel(q_ref, k_ref, v_ref, y_ref, o_scr, lse_scr):
    seq, dh = q_ref.shape
    blk = DSW_BLOCK
    qi = lax.broadcasted_iota(jnp.int32, (blk, 2 * blk), 0)
    ki = lax.broadcasted_iota(jnp.int32, (blk, 2 * blk), 1)
    dist = qi + blk - ki
    scale = dh ** -0.5
    units = seq // blk
    for g, (window, dil) in enumerate(DSW_CONFIGS):
        nsub = seq // dil // blk
        band = (dist >= 0) & (dist <= window // dil)

        def unit(u, carry, g=g, dil=dil, nsub=nsub, band=band):
            r = u // nsub
            sb = u % nsub
            cur = pl.ds(r + sb * (blk * dil), blk, stride=dil)
            prev = pl.ds(r + jnp.maximum(sb - 1, 0) * (blk * dil), blk, stride=dil)
            kk = jnp.concatenate([k_ref[prev, :], k_ref[cur, :]], axis=0).astype(BF16)
            vv = jnp.concatenate([v_ref[prev, :], v_ref[cur, :]], axis=0).astype(BF16)
            s = _dot_nt(q_ref[cur, :].astype(BF16), kk) * scale
            s = jnp.where(band & ((ki >= blk) | (sb > 0)), s, NEG_INF)
            m = jnp.max(s, axis=1, keepdims=True)
            p = jnp.exp(s - m)
            l = jnp.sum(p, axis=1, keepdims=True)
            o_scr[g, cur, :] = _dot(p.astype(BF16), vv) / l
            lse_scr[g, cur, :] = jnp.broadcast_to(m + jnp.log(l), (blk, dh))
            return carry

        lax.fori_loop(0, units, unit, 0, unroll=8)

    lses = [lse_scr[g] for g in range(len(DSW_CONFIGS))]
    mx = functools.reduce(jnp.maximum, lses)
    ws = [jnp.exp(l - mx) for l in lses]
    acc = sum(ws[g] * o_scr[g] for g in range(len(DSW_CONFIGS)))
    y_ref[...] = (acc / sum(ws)).astype(y_ref.dtype)


def _dsw_attention(qd, kd, proj):
    bsz, seq, w = qd.shape
    dh = DSW_HEAD_DIM
    v_col = (proj.shape[-1] - w) // dh
    head = lambda b, h: (b, 0, h)
    return pl.pallas_call(
        _dsw_kernel,
        grid=(bsz, DSW_HEADS),
        in_specs=[pl.BlockSpec((None, seq, dh), head), pl.BlockSpec((None, seq, dh), head),
                  pl.BlockSpec((None, seq, dh), lambda b, h: (b, 0, v_col + h))],
        out_specs=pl.BlockSpec((None, seq, dh), head),
        out_shape=jax.ShapeDtypeStruct((bsz, seq, w), BF16),
        scratch_shapes=[pltpu.VMEM((len(DSW_CONFIGS), seq, dh), F32),
                        pltpu.VMEM((len(DSW_CONFIGS), seq, dh), F32)],
        compiler_params=_cparams("arbitrary", "arbitrary"),
        name="dsw_attention",
    )(qd, kd, proj)


def kernel(x, positions, ab_w_in, ab_conv_w, ab_conv_b, ab_gate_r_w, ab_gate_r_b, ab_gate_i_w, ab_gate_i_b, ab_lru_lambda, ab_s5_lambda_re, ab_s5_lambda_im, ab_s5_log_step, ab_s5_b_re, ab_s5_b_im, ab_s5_c_re, ab_s5_c_im, ab_s5_d, ab_glu_w, ab_glu_b, ab_w_out, cd_w_in, cd_conv_w, cd_conv_b, cd_w_q, cd_w_k, cd_w_v, cd_w_if, cd_b_if, cd_w_out, ln_mix_g, ln_mix_b, ln_ffn_g, ln_ffn_b, moe_router_w, moe_router_b, moe_w_in, moe_b_in, moe_w_out, moe_b_out):
    bsz, seq, d = x.shape
    t = bsz * seq
    for layer in range(DEPTH):
        j = layer // 2
        if layer % 2 == 0:
            proj = _inproj_time_major(x, ab_w_in[j].astype(BF16))
            ya = _rg_lru(proj, ab_conv_w[j], ab_conv_b[j], ab_gate_r_w[j], ab_gate_r_b[j],
                         ab_gate_i_w[j], ab_gate_i_b[j], ab_lru_lambda[j])
            yb = _s5_glu(proj, ab_s5_lambda_re[j], ab_s5_lambda_im[j], ab_s5_log_step[j], ab_s5_b_re[j],
                         ab_s5_b_im[j], ab_s5_c_re[j], ab_s5_c_im[j], ab_s5_d[j], ab_glu_w[j], ab_glu_b[j])
            time_major, w_out = True, ab_w_out[j]
        else:
            proj = _inproj(x, cd_w_in[j].astype(BF16))
            mq, mk, mv, gates, qd, kd = _cd_prep(proj, positions, cd_conv_w[j], cd_conv_b[j], cd_w_q[j],
                                                 cd_w_k[j], cd_w_v[j], cd_w_if[j], cd_b_if[j])
            ya = _mlstm(mq, mk, mv, gates, proj)
            yb = _dsw_attention(qd, kd, proj)
            time_major, w_out = False, cd_w_out[j]
        xn, x_tiles, route, counts = _outproj_ln_router(
            ya, yb, time_major, w_out.astype(BF16), x, ln_mix_g[layer], ln_mix_b[layer],
            moe_router_w[layer], moe_router_b[layer])
        x = _moe_ffn_ln(xn.reshape(t, d), x_tiles, route.reshape(t, LANES), counts, layer, moe_w_in, moe_b_in,
                        moe_w_out, moe_b_out, ln_ffn_g[layer], ln_ffn_b[layer]).reshape(bsz, seq, d)
    return x
```

```python
import functools

import jax
import jax.numpy as jnp
from jax import lax
from jax.experimental import pallas as pl
from jax.experimental.pallas import tpu as pltpu

F32 = jnp.float32
BF16 = jnp.bfloat16

D_MODEL = 1024
DEPTH = 2
A_WIDTH = 512
B_WIDTH = 512
C_WIDTH = 512
D_WIDTH = 512
LRU_HEADS = 8
LRU_HEAD_DIM = A_WIDTH // LRU_HEADS
LRU_C = 8.0
CONV_WIDTH = 4
S5_GROUP_CH = 16
S5_GROUPS = B_WIDTH // S5_GROUP_CH
S5_STATE = 64
MLSTM_HEADS = 4
MLSTM_HEAD_DIM = C_WIDTH // MLSTM_HEADS
DSW_HEADS = 4
DSW_HEAD_DIM = D_WIDTH // DSW_HEADS
DSW_CONFIGS = ((128, 1), (512, 4), (2048, 16))
DSW_BLOCK = 128
ROPE_THETA = 500000.0
ROPE_DIMS = DSW_HEAD_DIM // 4
N_EXPERTS = 32
TOP_K = 4
D_EXPERT = D_MODEL
SWIGLU_LIMIT = 7.0
SWIGLU_ALPHA = 1.702
DEEPNORM_ALPHA = (2 * DEPTH) ** 0.25
LN_EPS = 1e-5

VMEM_LIMIT_BYTES = 56 * 1024 * 1024
LANES = 128
TILE_ROWS = D_MODEL // LANES
MOE_TM = 512
MOE_TK = 256
MOE_NT = 256
PROJ_TL = 512
ROUTE_TS = 128
LRU_TC = 128
S5_TC = 64
S5_HALF = 2
MLSTM_CS = 256
NEG_INF = float("-inf")


def _cparams(*sem):
    return pltpu.CompilerParams(dimension_semantics=sem, vmem_limit_bytes=VMEM_LIMIT_BYTES)


def _dot(a, b):
    return jnp.dot(a, b, preferred_element_type=F32)


def _dot_nt(a, b):
    return lax.dot_general(a, b, (((1,), (1,)), ((), ())), preferred_element_type=F32)


def _matmul_kernel(x_ref, w_ref, o_ref):
    o_ref[...] = _dot(x_ref[...].astype(BF16), w_ref[...])


def _matmul_time_major_kernel(x_ref, w_ref, o_ref, scr):
    bsz, tl, _ = x_ref.shape
    nchunk = scr.shape[0]
    for b in range(bsz):
        p = _dot(x_ref[b].astype(BF16), w_ref[...])
        for c in range(nchunk):
            scr[c, pl.ds(b, tl, stride=bsz), :] = p[:, c * LANES:(c + 1) * LANES]
    for c in range(nchunk):
        o_ref[:, c * LANES:(c + 1) * LANES] = scr[c]


def _inproj_time_major(x3, w_bf16):
    bsz, seq, k = x3.shape
    n = w_bf16.shape[1]
    tl = LRU_TC
    out = pl.pallas_call(
        _matmul_time_major_kernel,
        grid=(seq // tl,),
        in_specs=[pl.BlockSpec((bsz, tl, k), lambda i: (0, i, 0)),
                  pl.BlockSpec((k, n), lambda i: (0, 0))],
        out_specs=pl.BlockSpec((tl * bsz, n), lambda i: (i, 0)),
        out_shape=jax.ShapeDtypeStruct((seq * bsz, n), F32),
        scratch_shapes=[pltpu.VMEM((n // LANES, tl * bsz, LANES), F32)],
        compiler_params=_cparams("arbitrary"),
        name="in_proj_time_major",
    )(x3, w_bf16)
    return out.reshape(seq, bsz, n)


def _inproj(x3, w_bf16):
    bsz, seq, k = x3.shape
    n = w_bf16.shape[1]
    return pl.pallas_call(
        _matmul_kernel,
        grid=(bsz, seq // PROJ_TL),
        in_specs=[pl.BlockSpec((None, PROJ_TL, k), lambda b, i: (b, i, 0)),
                  pl.BlockSpec((k, n), lambda b, i: (0, 0))],
        out_specs=pl.BlockSpec((None, PROJ_TL, n), lambda b, i: (b, i, 0)),
        out_shape=jax.ShapeDtypeStruct((bsz, seq, n), F32),
        compiler_params=_cparams("arbitrary", "arbitrary"),
        name="in_proj",
    )(x3, w_bf16)


def _layer_norm_rows(z, g, b):
    mu = jnp.mean(z, axis=-1, keepdims=True)
    zc = z - mu
    var = jnp.mean(zc * zc, axis=-1, keepdims=True)
    return zc * lax.rsqrt(var + LN_EPS) * g + b


def _outproj_ln_router_kernel(ya_ref, yb_ref, wa_ref, wb_ref, x_ref, g_ref, b_ref, rwh_ref, rwl_ref, rb_ref,
                              tri_ref, xn_ref, xt_ref, rt_ref, cnt_ref):
    @pl.when((pl.program_id(0) == 0) & (pl.program_id(1) == 0))
    def _init():
        cnt_ref[...] = jnp.zeros_like(cnt_ref)

    mix = _dot(ya_ref[...], wa_ref[...]) + _dot(yb_ref[...], wb_ref[...])
    xn = _layer_norm_rows(DEEPNORM_ALPHA * x_ref[...] + mix, g_ref[...], b_ref[...])
    xn_ref[...] = xn
    _store_token_tiles(xt_ref, xn)
    xh = xn.astype(BF16)
    xl = (xn - xh.astype(F32)).astype(BF16)
    logits = (_dot(xh, rwh_ref[...]) + _dot(xl, rwh_ref[...]) + _dot(xh, rwl_ref[...])) + rb_ref[...]
    ts = tri_ref.shape[0]
    lane = lax.broadcasted_iota(jnp.int32, (ts, LANES), 1).astype(F32)
    counts = cnt_ref[...]
    for s in range(logits.shape[0] // ts):
        lg = logits[s * ts:(s + 1) * ts]
        tops, ids, hots = [], [], []
        for _ in range(TOP_K):
            m = jnp.max(lg, axis=1, keepdims=True)
            idx = jnp.min(jnp.where(lg == m, lane, float(LANES)), axis=1, keepdims=True)
            hot = lane == idx
            lg = jnp.where(hot, NEG_INF, lg)
            tops.append(m)
            ids.append(idx)
            hots.append(hot)
        exps = [jnp.exp(m - tops[0]) for m in tops]
        denom = functools.reduce(jnp.add, exps)
        chosen = functools.reduce(jnp.add, [h.astype(F32) for h in hots])
        before = _dot(tri_ref[...], chosen.astype(BF16)) + counts
        route = jnp.zeros((ts, LANES), F32)
        for k in range(TOP_K):
            rank = jnp.sum(jnp.where(hots[k], before, 0.0), axis=1, keepdims=True)
            route = jnp.where(lane == float(k), ids[k], route)
            route = jnp.where(lane == float(TOP_K + k), rank, route)
            route = jnp.where(lane == float(2 * TOP_K + k), exps[k] / denom, route)
        rt_ref[s * ts:(s + 1) * ts, :] = route
        counts = counts + jnp.sum(chosen, axis=0, keepdims=True)
    cnt_ref[...] = counts


def _outproj_ln_router(ya, yb, time_major, w_bf16, x3, g, b, rw, rb):
    bsz, seq, d = x3.shape
    wa, wb = w_bf16[:ya.shape[-1]], w_bf16[ya.shape[-1]:]
    ne = LANES
    rw = jnp.zeros((d, ne), F32).at[:, :N_EXPERTS].set(rw)
    rb = jnp.full((ne,), NEG_INF, F32).at[:N_EXPERTS].set(rb)
    rw_hi = rw.astype(BF16)
    rw_lo = (rw - rw_hi.astype(F32)).astype(BF16)
    tl = PROJ_TL
    tri = jnp.tril(jnp.ones((ROUTE_TS, ROUTE_TS), BF16), -1)
    if time_major:
        ya, yb = ya.reshape(seq, -1), yb.reshape(seq, -1)
        y_specs = [pl.BlockSpec((tl, wa.shape[0]), lambda bi, i: (i, bi)),
                   pl.BlockSpec((tl, wb.shape[0]), lambda bi, i: (i, bi))]
    else:
        y_specs = [pl.BlockSpec((None, tl, wa.shape[0]), lambda bi, i: (bi, i, 0)),
                   pl.BlockSpec((None, tl, wb.shape[0]), lambda bi, i: (bi, i, 0))]
    fixed = lambda bi, i: (0, 0)
    tok = lambda bi, i: (bi, i, 0)
    return pl.pallas_call(
        _outproj_ln_router_kernel,
        grid=(bsz, seq // tl),
        in_specs=y_specs + [pl.BlockSpec(wa.shape, fixed), pl.BlockSpec(wb.shape, fixed),
                            pl.BlockSpec((None, tl, d), tok),
                            pl.BlockSpec((1, d), fixed), pl.BlockSpec((1, d), fixed),
                            pl.BlockSpec((d, ne), fixed), pl.BlockSpec((d, ne), fixed),
                            pl.BlockSpec((1, ne), fixed), pl.BlockSpec((ROUTE_TS, ROUTE_TS), fixed)],
        out_specs=[pl.BlockSpec((None, tl, d), tok),
                   pl.BlockSpec((tl * TILE_ROWS, LANES), lambda bi, i: (bi * (seq // tl) + i, 0)),
                   pl.BlockSpec((None, tl, ne), tok), pl.BlockSpec((1, ne), fixed)],
        out_shape=[jax.ShapeDtypeStruct((bsz, seq, d), F32),
                   jax.ShapeDtypeStruct((bsz * seq * TILE_ROWS, LANES), F32),
                   jax.ShapeDtypeStruct((bsz, seq, ne), F32), jax.ShapeDtypeStruct((1, ne), F32)],
        compiler_params=_cparams("arbitrary", "arbitrary"),
        name="out_proj_ln_router",
    )(ya, yb, wa, wb, x3, g.reshape(1, d), b.reshape(1, d), rw_hi, rw_lo, rb.reshape(1, ne), tri)


def _store_token_tiles(dst_ref, x):
    n = x.shape[0]
    for j in range(TILE_ROWS):
        dst_ref[pl.ds(j, n, stride=TILE_ROWS), :] = x[:, j * LANES:(j + 1) * LANES]


def _load_token_tiles(src_ref, n):
    return jnp.concatenate([src_ref[pl.ds(j, n, stride=TILE_ROWS), :] for j in range(TILE_ROWS)], axis=1)


def _gather_token_tiles(src_hbm, idx_ref, n, dst, sem, priorities):
    def body(g, carry):
        base = pl.multiple_of(g * (TILE_ROWS * TILE_ROWS), TILE_ROWS * TILE_ROWS)
        for u in range(TILE_ROWS):
            src_row = pl.multiple_of(idx_ref[0, g * TILE_ROWS + u], TILE_ROWS)
            pltpu.make_async_copy(src_hbm.at[pl.ds(src_row, TILE_ROWS)],
                                  dst.at[pl.ds(base + u * TILE_ROWS, TILE_ROWS)], sem).start(priority=u % priorities)
        return carry
    lax.fori_loop(0, n // TILE_ROWS, body, 0, unroll=2)


def _moe_kernel(be_ref, nreal_ref, rt_ref, rtn_ref, x_hbm, rg_ref, win_ref, bin_ref, wout_ref, bout_ref, o_ref,
                win_bf, wout_bf, xbuf, sem):
    i = pl.program_id(0)
    last = pl.num_programs(0) - 1
    n_real = nreal_ref[0]
    e = be_ref[i]
    prev = be_ref[jnp.maximum(i - 1, 0)]
    tm = xbuf.shape[1] // TILE_ROWS
    slot = lax.rem(i, 2)
    nt = MOE_NT
    n_groups = 3 * (D_EXPERT // nt)
    bounds = [tm * k // n_groups for k in range(n_groups + 1)]

    def wait(s):
        pltpu.make_async_copy(xbuf.at[s], xbuf.at[s], sem.at[s]).wait()

    def start_group(k, s):
        for r in range(bounds[k], bounds[k + 1]):
            src_row = pl.multiple_of(rtn_ref[0, r], TILE_ROWS)
            pltpu.make_async_copy(x_hbm.at[pl.ds(src_row, TILE_ROWS)],
                                  xbuf.at[s, pl.ds(r * TILE_ROWS, TILE_ROWS)], sem.at[s]).start()

    @pl.when(i == 0)
    def _first_gather():
        _gather_token_tiles(x_hbm, rt_ref, tm, xbuf.at[0], sem.at[0], priorities=1)

    @pl.when((i == 0) | (e != prev))
    def _load_expert():
        win_bf[...] = win_ref[...].astype(BF16)
        wout_bf[...] = wout_ref[...].astype(BF16)

    computes = (i < n_real) | (i == 0)

    @pl.when(computes)
    def _compute():
        wait(slot)
        x = _load_token_tiles(xbuf.at[slot], tm).astype(BF16)
        ys = []
        for t in range(D_EXPERT // nt):
            lo, hi = t * nt, (t + 1) * nt
            hg = _dot(x, win_bf[:, lo:hi]) + bin_ref[:, lo:hi]
            start_group(2 * t, 1 - slot)
            hl = _dot(x, win_bf[:, D_EXPERT + lo:D_EXPERT + hi]) + bin_ref[:, D_EXPERT + lo:D_EXPERT + hi]
            start_group(2 * t + 1, 1 - slot)
            g = jnp.minimum(hg, SWIGLU_LIMIT)
            lin = jnp.clip(hl, -SWIGLU_LIMIT, SWIGLU_LIMIT)
            ys.append((g * jax.nn.sigmoid(SWIGLU_ALPHA * g) * (lin + 1.0)).astype(BF16))
        y = jnp.concatenate(ys, axis=1)
        gates = rg_ref[...].T
        for t in range(D_MODEL // nt):
            lo, hi = t * nt, (t + 1) * nt
            out = _dot(y, wout_bf[:, lo:hi]) + bout_ref[:, lo:hi]
            start_group(2 * (D_EXPERT // nt) + t, 1 - slot)
            out = jnp.concatenate(
                [out[j * LANES:(j + 1) * LANES] * gates[:, j:j + 1] for j in range(tm // LANES)], axis=0)
            for c in range(nt // LANES):
                o_ref[pl.ds(t * (nt // LANES) + c, tm, stride=TILE_ROWS), :] = out[:, c * LANES:(c + 1) * LANES]

        @pl.when(i == last)
        def _drain():
            wait(1 - slot)

    @pl.when(jnp.logical_not(computes))
    def _unused_block():
        @pl.when((i <= n_real) | (i == 1))
        def _drain():
            wait(slot)
        o_ref[...] = jnp.zeros_like(o_ref)


def _moe_experts(x_tiles, row_tok, row_gate, block_expert, n_real, layer, w_in, b_in, w_out, b_out):
    d = D_MODEL
    tm = MOE_TM
    n_blocks = row_tok.shape[0]
    nl, ne, _, dh2 = w_in.shape
    rt = (row_tok * TILE_ROWS).reshape(n_blocks, 1, tm)
    rg = jnp.pad(row_gate.reshape(n_blocks, tm // LANES, LANES), ((0, 0), (0, TILE_ROWS - tm // LANES), (0, 0)))
    smem_rows = lambda imap: pl.BlockSpec((None, 1, tm), imap, memory_space=pltpu.SMEM)
    grid_spec = pltpu.PrefetchScalarGridSpec(
        num_scalar_prefetch=2,
        grid=(n_blocks,),
        in_specs=[
            smem_rows(lambda i, be, nr: (i, 0, 0)),
            smem_rows(lambda i, be, nr: (jnp.minimum(i + 1, n_blocks - 1), 0, 0)),
            pl.BlockSpec(memory_space=pl.ANY),
            pl.BlockSpec((None, TILE_ROWS, LANES), lambda i, be, nr: (i, 0, 0)),
            pl.BlockSpec((None, None, d, dh2), lambda i, be, nr: (layer, be[i], 0, 0)),
            pl.BlockSpec((None, None, 1, dh2), lambda i, be, nr: (layer, be[i], 0, 0)),
            pl.BlockSpec((None, None, D_EXPERT, d), lambda i, be, nr: (layer, be[i], 0, 0)),
            pl.BlockSpec((None, None, 1, d), lambda i, be, nr: (layer, be[i], 0, 0)),
        ],
        out_specs=pl.BlockSpec((tm * TILE_ROWS, LANES), lambda i, be, nr: (i, 0)),
        scratch_shapes=[pltpu.VMEM((d, dh2), BF16), pltpu.VMEM((D_EXPERT, d), BF16),
                        pltpu.VMEM((2, tm * TILE_ROWS, LANES), F32), pltpu.SemaphoreType.DMA((2,))],
    )
    return pl.pallas_call(
        _moe_kernel,
        grid_spec=grid_spec,
        out_shape=jax.ShapeDtypeStruct((n_blocks * tm * TILE_ROWS, LANES), F32),
        compiler_params=_cparams("arbitrary"),
        name="moe_experts",
    )(block_expert, n_real, rt, rt, x_tiles, rg, w_in,
      b_in.reshape(nl, ne, 1, dh2), w_out, b_out.reshape(nl, ne, 1, d))


def _moe_combine_kernel(dc_ref, dn_ref, x_ref, yb_hbm, g_ref, b_ref, o_ref, buf, ffn_ref, sem):
    i = pl.program_id(0)
    n = pl.num_programs(0)
    tk = x_ref.shape[0]
    rows = tk * TILE_ROWS
    slot = lax.rem(i, 2)

    @pl.when(i == 0)
    def _first_gather():
        _gather_token_tiles(yb_hbm, dc_ref, TOP_K * tk, buf.at[0], sem.at[0], priorities=2)

    @pl.when(i + 1 < n)
    def _next_gather():
        _gather_token_tiles(yb_hbm, dn_ref, TOP_K * tk, buf.at[1 - slot], sem.at[1 - slot], priorities=2)

    pltpu.make_async_copy(buf.at[slot], buf.at[slot], sem.at[slot]).wait()
    ffn = buf[slot, 0:rows]
    for k in range(1, TOP_K):
        ffn = ffn + buf[slot, k * rows:(k + 1) * rows]
    ffn_ref[...] = ffn
    o_ref[...] = _layer_norm_rows(DEEPNORM_ALPHA * x_ref[...] + _load_token_tiles(ffn_ref, tk),
                                  g_ref[...], b_ref[...])


def _moe_combine_ln(xn, dest, yb_tiles, g, b):
    t, d = xn.shape
    tk = MOE_TK
    n = t // tk
    dest_tiles = (dest * TILE_ROWS).reshape(n, tk, TOP_K).transpose(0, 2, 1).reshape(n, 1, TOP_K * tk)
    smem_rows = lambda imap: pl.BlockSpec((None, 1, TOP_K * tk), imap, memory_space=pltpu.SMEM)
    row = lambda i: (i, 0)
    fixed = lambda i: (0, 0)
    return pl.pallas_call(
        _moe_combine_kernel,
        grid=(n,),
        in_specs=[smem_rows(lambda i: (i, 0, 0)), smem_rows(lambda i: (jnp.minimum(i + 1, n - 1), 0, 0)),
                  pl.BlockSpec((tk, d), row), pl.BlockSpec(memory_space=pl.ANY),
                  pl.BlockSpec((1, d), fixed), pl.BlockSpec((1, d), fixed)],
        out_specs=pl.BlockSpec((tk, d), row),
        out_shape=jax.ShapeDtypeStruct((t, d), F32),
        scratch_shapes=[pltpu.VMEM((2, TOP_K * tk * TILE_ROWS, LANES), F32),
                        pltpu.VMEM((tk * TILE_ROWS, LANES), F32), pltpu.SemaphoreType.DMA((2,))],
        compiler_params=_cparams("arbitrary"),
        name="moe_combine_ln",
    )(dest_tiles, dest_tiles, xn, yb_tiles, g.reshape(1, d), b.reshape(1, d))


def _moe_ffn_ln(xn, x_tiles, route, counts, layer, w_in, b_in, w_out, b_out, g, b):
    t, d = xn.shape
    tm = MOE_TM
    flat_e = route[:, :TOP_K].astype(jnp.int32).reshape(-1)
    rank = route[:, TOP_K:2 * TOP_K].astype(jnp.int32).reshape(-1)
    gate = route[:, 2 * TOP_K:3 * TOP_K].reshape(-1)
    counts = counts[0, :N_EXPERTS].astype(jnp.int32)
    blocks_per_e = (counts + tm - 1) // tm
    blk_end = jnp.cumsum(blocks_per_e)
    blk_start = blk_end - blocks_per_e
    row_start = jnp.cumsum(counts) - counts
    dest = blk_start[flat_e] * tm + rank
    n_blocks = (t * TOP_K) // tm + N_EXPERTS
    block_expert = jnp.minimum(
        jnp.sum((blk_end[None, :] <= jnp.arange(n_blocks, dtype=jnp.int32)[:, None]).astype(jnp.int32), axis=1),
        N_EXPERTS - 1)
    n_real = blk_end[-1:].astype(jnp.int32)
    order = jnp.argsort(flat_e, stable=True).astype(jnp.int32)
    blk = jnp.arange(n_blocks, dtype=jnp.int32)
    rank_row = ((blk - blk_start[block_expert]) * tm)[:, None] + jnp.arange(tm, dtype=jnp.int32)[None, :]
    valid = rank_row < counts[block_expert][:, None]
    pair = order[jnp.clip(row_start[block_expert][:, None] + rank_row, 0, t * TOP_K - 1)]
    row_tok = jnp.where(valid, pair // TOP_K, 0)
    row_gate = jnp.where(valid, gate[pair], 0.0)
    yb_tiles = _moe_experts(x_tiles, row_tok, row_gate, block_expert, n_real, layer, w_in, b_in, w_out, b_out)
    return _moe_combine_ln(xn, dest, yb_tiles, g, b)


def _softplus(x):
    return jnp.maximum(x, 0.0) + jnp.log(1.0 + jnp.exp(-jnp.abs(x)))


def _gelu_tanh(x):
    return 0.5 * x * (1.0 + jnp.tanh(0.7978845608028654 * (x + 0.044715 * (x * x * x))))


def _lru_kernel(xa_ref, ga_ref, cw_ref, cb_ref, wr_ref, br_ref, wi_ref, bi_ref, lam_ref, o_ref,
                tail_ref, h_ref, a_ref, u_ref):
    tc, bsz, w = xa_ref.shape

    @pl.when(pl.program_id(0) == 0)
    def _init():
        tail_ref[...] = jnp.zeros_like(tail_ref)
        h_ref[...] = jnp.zeros_like(h_ref)

    xa = xa_ref[...]
    ext = jnp.concatenate([tail_ref[...], xa], axis=0)
    xc = cb_ref[...] + ext[0:tc] * cw_ref[0]
    for tap in range(1, CONV_WIDTH):
        xc = xc + ext[tap:tap + tc] * cw_ref[tap]
    tail_ref[...] = xa[tc - (CONV_WIDTH - 1):]
    x2 = xc.reshape(tc * bsz, w)
    xb = x2.astype(BF16)
    r = jax.nn.sigmoid(_dot(xb, wr_ref[...]) + br_ref[...])
    ig = jax.nn.sigmoid(_dot(xb, wi_ref[...]) + bi_ref[...])
    log_a = (-LRU_C) * r * _softplus(-lam_ref[...])
    a_ref[...] = jnp.exp(log_a).reshape(tc, bsz, w)
    u_ref[...] = (jnp.sqrt(1.0 - jnp.exp(2.0 * log_a)) * (ig * x2)).reshape(tc, bsz, w)

    def step(t, h):
        h = a_ref[t] * h + u_ref[t]
        u_ref[t] = h
        return h

    h_ref[...] = lax.fori_loop(0, tc, step, h_ref[...], unroll=8)
    o_ref[...] = (u_ref[...] * _gelu_tanh(ga_ref[...])).astype(o_ref.dtype)


def _block_diag(w):
    h, i, j = w.shape
    return jnp.einsum('hij,hk->hikj', w, jnp.eye(h, dtype=w.dtype)).reshape(h * i, h * j)


def _rg_lru(proj, conv_w, conv_b, gate_r_w, gate_r_b, gate_i_w, gate_i_b, lru_lambda):
    seq, bsz, _ = proj.shape
    w = A_WIDTH
    tc = LRU_TC
    fixed2 = lambda i: (0, 0)
    fixed3 = lambda i: (0, 0, 0)
    return pl.pallas_call(
        _lru_kernel,
        grid=(seq // tc,),
        in_specs=[pl.BlockSpec((tc, bsz, w), lambda i: (i, 0, 0)),
                  pl.BlockSpec((tc, bsz, w), lambda i: (i, 0, 1)),
                  pl.BlockSpec((CONV_WIDTH, 1, w), fixed3), pl.BlockSpec((1, w), fixed2),
                  pl.BlockSpec((w, w), fixed2), pl.BlockSpec((1, w), fixed2),
                  pl.BlockSpec((w, w), fixed2), pl.BlockSpec((1, w), fixed2),
                  pl.BlockSpec((1, w), fixed2)],
        out_specs=pl.BlockSpec((tc, bsz, w), lambda i: (i, 0, 0)),
        out_shape=jax.ShapeDtypeStruct((seq, bsz, w), BF16),
        scratch_shapes=[pltpu.VMEM((CONV_WIDTH - 1, bsz, w), F32), pltpu.VMEM((bsz, w), F32),
                        pltpu.VMEM((tc, bsz, w), F32), pltpu.VMEM((tc, bsz, w), F32)],
        compiler_params=_cparams("arbitrary"),
        name="rg_lru",
    )(proj, proj, conv_w.reshape(CONV_WIDTH, 1, w), conv_b.reshape(1, w),
      _block_diag(gate_r_w).astype(BF16), gate_r_b.reshape(1, w),
      _block_diag(gate_i_w).astype(BF16), gate_i_b.reshape(1, w), lru_lambda.reshape(1, w))


def _s5_kernel(u_ref, wbr_ref, wbi_ref, ar_ref, ai_ref, ccr_ref, cci_ref, d_ref, gw_ref, gb_ref, o_ref,
               xr_ref, xi_ref, sr_ref, si_ref):
    tc, bsz, w = u_ref.shape
    nstate = xr_ref.shape[-1]
    sh = nstate // S5_HALF
    wh = w // S5_HALF

    @pl.when(pl.program_id(0) == 0)
    def _init():
        sr_ref[...] = jnp.zeros_like(sr_ref)
        si_ref[...] = jnp.zeros_like(si_ref)

    u2 = u_ref[...].reshape(tc * bsz, w)
    ub = u2.astype(BF16)
    for hf in range(S5_HALF):
        uh = ub[:, hf * wh:(hf + 1) * wh]
        xr_ref[:, :, hf * sh:(hf + 1) * sh] = _dot(uh, wbr_ref[hf]).reshape(tc, bsz, sh)
        xi_ref[:, :, hf * sh:(hf + 1) * sh] = _dot(uh, wbi_ref[hf]).reshape(tc, bsz, sh)

    for hf in range(S5_HALF):
        lo, hi = hf * sh, (hf + 1) * sh
        ar = jnp.broadcast_to(ar_ref[:, lo:hi], (bsz, sh))
        ai = jnp.broadcast_to(ai_ref[:, lo:hi], (bsz, sh))

        def step(t, carry, lo=lo, hi=hi, ar=ar, ai=ai):
            xr, xi = carry
            nxr = ar * xr - ai * xi + xr_ref[t, :, lo:hi]
            nxi = ar * xi + ai * xr + xi_ref[t, :, lo:hi]
            xr_ref[t, :, lo:hi] = nxr
            xi_ref[t, :, lo:hi] = nxi
            return nxr, nxi

        xr, xi = lax.fori_loop(0, tc, step, (sr_ref[:, lo:hi], si_ref[:, lo:hi]), unroll=4)
        sr_ref[:, lo:hi] = xr
        si_ref[:, lo:hi] = xi

    ys = []
    for hf in range(S5_HALF):
        lo, hi = hf * sh, (hf + 1) * sh
        xrh = xr_ref[:, :, lo:hi].reshape(tc * bsz, sh).astype(BF16)
        xih = xi_ref[:, :, lo:hi].reshape(tc * bsz, sh).astype(BF16)
        ys.append(_dot(xrh, ccr_ref[hf]) - _dot(xih, cci_ref[hf]))
    y = jnp.concatenate(ys, axis=1) + d_ref[...] * u2
    s = _gelu_tanh(y)
    yb = s * jax.nn.sigmoid(_dot(s.astype(BF16), gw_ref[...]) + gb_ref[...])
    o_ref[...] = yb.reshape(tc, bsz, w).astype(o_ref.dtype)


def _s5_glu(proj, lam_re, lam_im, log_step, b_re, b_im, c_re, c_im, d_skip, glu_w, glu_b):
    seq, bsz, _ = proj.shape
    w = B_WIDTH
    tc = S5_TC
    nstate = S5_GROUPS * S5_STATE
    gh = S5_GROUPS // S5_HALF
    lr = jnp.minimum(lam_re, -1e-4)
    li = lam_im
    step = jnp.exp(log_step)[:, None]
    mag = jnp.exp(lr * step)
    ar = mag * jnp.cos(li * step)
    ai = mag * jnp.sin(li * step)
    inv = 1.0 / (lr * lr + li * li)
    zr = ((ar - 1.0) * lr + ai * li) * inv
    zi = (ai * lr - (ar - 1.0) * li) * inv
    bbr = zr[..., None] * b_re - zi[..., None] * b_im
    bbi = zr[..., None] * b_im + zi[..., None] * b_re
    eye = jnp.eye(gh, dtype=F32)

    def expand_in(bb):
        bb = bb.reshape(S5_HALF, gh, S5_STATE, S5_GROUP_CH)
        return jnp.einsum('fgph,gk->fghkp', bb, eye).reshape(
            S5_HALF, gh * S5_GROUP_CH, gh * S5_STATE).astype(BF16)

    def expand_out(c):
        c = c.reshape(S5_HALF, gh, S5_GROUP_CH, S5_STATE)
        return jnp.einsum('fghp,gk->fgpkh', c, eye).reshape(
            S5_HALF, gh * S5_STATE, gh * S5_GROUP_CH).astype(BF16)

    fixed2 = lambda i: (0, 0)
    fixed3 = lambda i: (0, 0, 0)
    return pl.pallas_call(
        _s5_kernel,
        grid=(seq // tc,),
        in_specs=[pl.BlockSpec((tc, bsz, w), lambda i: (i, 0, 2)),
                  pl.BlockSpec((S5_HALF, w // S5_HALF, nstate // S5_HALF), fixed3),
                  pl.BlockSpec((S5_HALF, w // S5_HALF, nstate // S5_HALF), fixed3),
                  pl.BlockSpec((1, nstate), fixed2), pl.BlockSpec((1, nstate), fixed2),
                  pl.BlockSpec((S5_HALF, nstate // S5_HALF, w // S5_HALF), fixed3),
                  pl.BlockSpec((S5_HALF, nstate // S5_HALF, w // S5_HALF), fixed3),
                  pl.BlockSpec((1, w), fixed2), pl.BlockSpec((w, w), fixed2), pl.BlockSpec((1, w), fixed2)],
        out_specs=pl.BlockSpec((tc, bsz, w), lambda i: (i, 0, 0)),
        out_shape=jax.ShapeDtypeStruct((seq, bsz, w), BF16),
        scratch_shapes=[pltpu.VMEM((tc, bsz, nstate), F32), pltpu.VMEM((tc, bsz, nstate), F32),
                        pltpu.VMEM((bsz, nstate), F32), pltpu.VMEM((bsz, nstate), F32)],
        compiler_params=_cparams("arbitrary"),
        name="s5_glu",
    )(proj, expand_in(bbr), expand_in(bbi), ar.reshape(1, nstate), ai.reshape(1, nstate),
      expand_out(c_re), expand_out(c_im), d_skip.reshape(1, w), glu_w.astype(BF16), glu_b.reshape(1, w))


def _cd_prep_kernel(xc_ref, q_ref, k_ref, pos_ref, cw_ref, cb_ref, wq_ref, wk_ref, wv_ref, wif_ref,
                    bif_ref, invf_ref, mq_ref, mk_ref, mv_ref, gt_ref, qd_ref, kd_ref, tail_ref):
    tl, w = xc_ref.shape
    pad = tail_ref.shape[0]

    @pl.when(pl.program_id(1) == 0)
    def _init():
        tail_ref[...] = jnp.zeros_like(tail_ref)

    xc = xc_ref[...]
    ext = jnp.concatenate([tail_ref[...], xc], axis=0)
    conv = cb_ref[...]
    for tap in range(CONV_WIDTH):
        off = pad - (CONV_WIDTH - 1) + tap
        conv = conv + ext[off:off + tl] * cw_ref[tap]
    tail_ref[...] = xc[tl - pad:]
    xconv = (conv * jax.nn.sigmoid(conv)).astype(BF16)
    q = _dot(xconv, wq_ref[...]).astype(BF16)
    k = (_dot(xconv, wk_ref[...]) * (MLSTM_HEAD_DIM ** -0.5)).astype(BF16)
    v = _dot(xc.astype(BF16), wv_ref[...]).astype(BF16)
    mq_ref[...] = q
    mk_ref[...] = k
    mv_ref[...] = v
    gt_ref[...] = (_dot(q, wif_ref[0]) + _dot(k, wif_ref[1]) + _dot(v, wif_ref[2]) + bif_ref[...])

    ang = pos_ref[...].astype(F32) * invf_ref[...]
    cos = jnp.cos(ang)
    sin = jnp.sin(ang)
    lane = lax.broadcasted_iota(jnp.int32, (1, LANES), 1)
    half = ROPE_DIMS // 2
    sin_lo = jnp.where(lane < half, -sin, 0.0)
    sin_hi = jnp.where((lane >= half) & (lane < ROPE_DIMS), sin, 0.0)
    for src, dst in ((q_ref, qd_ref), (k_ref, kd_ref)):
        for h in range(DSW_HEADS):
            sl = slice(h * DSW_HEAD_DIM, (h + 1) * DSW_HEAD_DIM)
            t = src[:, sl]
            rot = (t * cos + pltpu.roll(t, LANES - half, axis=1) * sin_lo
                   + pltpu.roll(t, half, axis=1) * sin_hi)
            dst[:, sl] = rot


def _cd_prep(proj, positions, conv_w, conv_b, w_q, w_k, w_v, w_if, b_if):
    bsz, seq, _ = proj.shape
    w = C_WIDTH
    tl = PROJ_TL
    half = ROPE_DIMS // 2
    inv_freq = ROPE_THETA ** (-jnp.arange(half, dtype=F32) / half)
    invf = jnp.zeros((1, LANES), F32).at[0, :half].set(inv_freq).at[0, half:ROPE_DIMS].set(inv_freq)
    wif = jnp.zeros((3, w, LANES), F32).at[:, :, :2 * MLSTM_HEADS].set(w_if.reshape(3, w, 2 * MLSTM_HEADS))
    bif = jnp.zeros((1, LANES), F32).at[0, :2 * MLSTM_HEADS].set(b_if)
    col = lambda c: (lambda b, i: (b, i, c))
    fixed2 = lambda b, i: (0, 0)
    fixed3 = lambda b, i: (0, 0, 0)
    tok_bf = jax.ShapeDtypeStruct((bsz, seq, w), BF16)
    return pl.pallas_call(
        _cd_prep_kernel,
        grid=(bsz, seq // tl),
        in_specs=[pl.BlockSpec((None, tl, w), col(0)), pl.BlockSpec((None, tl, w), col(2)),
                  pl.BlockSpec((None, tl, w), col(3)), pl.BlockSpec((None, tl, 1), col(0)),
                  pl.BlockSpec((CONV_WIDTH, 1, w), fixed3), pl.BlockSpec((1, w), fixed2),
                  pl.BlockSpec((w, w), fixed2), pl.BlockSpec((w, w), fixed2), pl.BlockSpec((w, w), fixed2),
                  pl.BlockSpec((3, w, LANES), fixed3), pl.BlockSpec((1, LANES), fixed2),
                  pl.BlockSpec((1, LANES), fixed2)],
        out_specs=[pl.BlockSpec((None, tl, w), col(0))] * 3 + [pl.BlockSpec((None, tl, LANES), col(0))]
                  + [pl.BlockSpec((None, tl, w), col(0))] * 2,
        out_shape=[tok_bf, tok_bf, tok_bf, jax.ShapeDtypeStruct((bsz, seq, LANES), F32),
                   jax.ShapeDtypeStruct((bsz, seq, w), F32), jax.ShapeDtypeStruct((bsz, seq, w), F32)],
        scratch_shapes=[pltpu.VMEM((8, w), F32)],
        compiler_params=_cparams("arbitrary", "arbitrary"),
        name="cd_prep",
    )(proj, proj, proj, positions.reshape(bsz, seq, 1), conv_w.reshape(CONV_WIDTH, 1, w),
      conv_b.reshape(1, w), _block_diag(w_q).astype(BF16), _block_diag(w_k).astype(BF16),
      _block_diag(w_v).astype(BF16), wif.astype(BF16), bif, invf)


def _cumsum_rows(x):
    n = x.shape[0]
    row = lax.broadcasted_iota(jnp.int32, x.shape, 0)
    shift = 1
    while shift < n:
        x = x + jnp.where(row >= shift, pltpu.roll(x, shift, axis=0), 0.0)
        shift *= 2
    return x


def _mlstm_kernel(q_ref, k_ref, v_ref, g_ref, op_ref, o_ref, c_ref, n_ref, m_ref):
    cs = q_ref.shape[0]
    dh = MLSTM_HEAD_DIM

    @pl.when(pl.program_id(1) == 0)
    def _init():
        c_ref[...] = jnp.zeros_like(c_ref)
        n_ref[...] = jnp.zeros_like(n_ref)
        m_ref[...] = jnp.full_like(m_ref, NEG_INF)

    gates = g_ref[...]
    cum = _cumsum_rows(jax.nn.log_sigmoid(gates))
    gates_t = gates.T
    cum_t = cum.T
    causal = (lax.broadcasted_iota(jnp.int32, (cs, cs), 0) >= lax.broadcasted_iota(jnp.int32, (cs, cs), 1))
    for h in range(MLSTM_HEADS):
        sl = slice(h * dh, (h + 1) * dh)
        fcol = MLSTM_HEADS + h
        li_c, li_r = gates[:, h:h + 1], gates_t[h:h + 1, :]
        cum_c, cum_r = cum[:, fcol:fcol + 1], cum_t[fcol:fcol + 1, :]
        m_prev = m_ref[h][:, 0:1]
        q, k, v = q_ref[:, sl], k_ref[:, sl], v_ref[:, sl]
        log_d = jnp.where(causal, cum_c - cum_r + li_r, NEG_INF)
        log_inter = cum_c + m_prev
        m_s = jnp.maximum(jnp.max(log_d, axis=1, keepdims=True), log_inter)
        s = _dot_nt(q, k) * jnp.exp(log_d - m_s)
        inter = jnp.exp(log_inter - m_s)
        c_prev = c_ref[h]
        n_prev = n_ref[h]
        num = _dot(s.astype(BF16), v) + inter * _dot(q, c_prev.astype(BF16))
        den = (jnp.sum(s, axis=1, keepdims=True)
               + inter * jnp.sum(q.astype(F32) * n_prev, axis=1, keepdims=True))
        hh = num / jnp.maximum(jnp.abs(den), jnp.exp(-m_s))
        o_ref[:, sl] = (jax.nn.sigmoid(op_ref[:, sl]) * hh).astype(o_ref.dtype)
        chunk_f = cum_c[cs - 1:cs, :]
        to_end = chunk_f - cum_c + li_c
        m_c = jnp.max(to_end, axis=0, keepdims=True)
        m_new = jnp.maximum(chunk_f + m_prev, m_c)
        s_old = jnp.exp(chunk_f + m_prev - m_new)
        s_new = jnp.exp(m_c - m_new)
        kw = k.astype(F32) * jnp.exp(to_end - m_c)
        c_ref[h] = s_old * c_prev + s_new * _dot(kw.T.astype(BF16), v)
        n_ref[h] = s_old * n_prev + s_new * jnp.sum(kw, axis=0, keepdims=True)
        m_ref[h] = jnp.broadcast_to(m_new, (1, LANES))


def _mlstm(mq, mk, mv, gates, proj):
    bsz, seq, w = mq.shape
    cs = MLSTM_CS
    tok = lambda b, i: (b, i, 0)
    return pl.pallas_call(
        _mlstm_kernel,
        grid=(bsz, seq // cs),
        in_specs=[pl.BlockSpec((None, cs, w), tok)] * 3
                 + [pl.BlockSpec((None, cs, LANES), tok), pl.BlockSpec((None, cs, w), lambda b, i: (b, i, 1))],
        out_specs=pl.BlockSpec((None, cs, w), tok),
        out_shape=jax.ShapeDtypeStruct((bsz, seq, w), BF16),
        scratch_shapes=[pltpu.VMEM((MLSTM_HEADS, MLSTM_HEAD_DIM, MLSTM_HEAD_DIM), F32),
                        pltpu.VMEM((MLSTM_HEADS, 1, MLSTM_HEAD_DIM), F32),
                        pltpu.VMEM((MLSTM_HEADS, 1, LANES), F32)],
        compiler_params=_cparams("arbitrary", "arbitrary"),
        name="mlstm",
    )(mq, mk, mv, gates, proj)


def _dsw_kernel(q_ref, k_ref, v_ref, y_ref, o_scr, lse_scr):
    seq, dh = q_ref.shape
    blk = DSW_BLOCK
    qi = lax.broadcasted_iota(jnp.int32, (blk, 2 * blk), 0)
    ki = lax.broadcasted_iota(jnp.int32, (blk, 2 * blk), 1)
    dist = qi + blk - ki
    scale = dh ** -0.5
    units = seq // blk
    for g, (window, dil) in enumerate(DSW_CONFIGS):
        nsub = seq // dil // blk
        band = (dist >= 0) & (dist <= window // dil)

        def unit(u, carry, g=g, dil=dil, nsub=nsub, band=band):
            r = u // nsub
            sb = u % nsub
            cur = pl.ds(r + sb * (blk * dil), blk, stride=dil)
            prev = pl.ds(r + jnp.maximum(sb - 1, 0) * (blk * dil), blk, stride=dil)
            kk = jnp.concatenate([k_ref[prev, :], k_ref[cur, :]], axis=0).astype(BF16)
            vv = jnp.concatenate([v_ref[prev, :], v_ref[cur, :]], axis=0).astype(BF16)
            s = _dot_nt(q_ref[cur, :].astype(BF16), kk) * scale
            s = jnp.where(band & ((ki >= blk) | (sb > 0)), s, NEG_INF)
            m = jnp.max(s, axis=1, keepdims=True)
            p = jnp.exp(s - m)
            l = jnp.sum(p, axis=1, keepdims=True)
            o_scr[g, cur, :] = _dot(p.astype(BF16), vv) / l
            lse_scr[g, cur, :] = jnp.broadcast_to(m + jnp.log(l), (blk, dh))
            return carry

        lax.fori_loop(0, units, unit, 0, unroll=8)

    lses = [lse_scr[g] for g in range(len(DSW_CONFIGS))]
    mx = functools.reduce(jnp.maximum, lses)
    ws = [jnp.exp(l - mx) for l in lses]
    acc = sum(ws[g] * o_scr[g] for g in range(len(DSW_CONFIGS)))
    y_ref[...] = (acc / sum(ws)).astype(y_ref.dtype)


def _dsw_attention(qd, kd, proj):
    bsz, seq, w = qd.shape
    dh = DSW_HEAD_DIM
    v_col = (proj.shape[-1] - w) // dh
    head = lambda b, h: (b, 0, h)
    return pl.pallas_call(
        _dsw_kernel,
        grid=(bsz, DSW_HEADS),
        in_specs=[pl.BlockSpec((None, seq, dh), head), pl.BlockSpec((None, seq, dh), head),
                  pl.BlockSpec((None, seq, dh), lambda b, h: (b, 0, v_col + h))],
        out_specs=pl.BlockSpec((None, seq, dh), head),
        out_shape=jax.ShapeDtypeStruct((bsz, seq, w), BF16),
        scratch_shapes=[pltpu.VMEM((len(DSW_CONFIGS), seq, dh), F32),
                        pltpu.VMEM((len(DSW_CONFIGS), seq, dh), F32)],
        compiler_params=_cparams("arbitrary", "arbitrary"),
        name="dsw_attention",
    )(qd, kd, proj)


def kernel(x, positions, ab_w_in, ab_conv_w, ab_conv_b, ab_gate_r_w, ab_gate_r_b, ab_gate_i_w, ab_gate_i_b, ab_lru_lambda, ab_s5_lambda_re, ab_s5_lambda_im, ab_s5_log_step, ab_s5_b_re, ab_s5_b_im, ab_s5_c_re, ab_s5_c_im, ab_s5_d, ab_glu_w, ab_glu_b, ab_w_out, cd_w_in, cd_conv_w, cd_conv_b, cd_w_q, cd_w_k, cd_w_v, cd_w_if, cd_b_if, cd_w_out, ln_mix_g, ln_mix_b, ln_ffn_g, ln_ffn_b, moe_router_w, moe_router_b, moe_w_in, moe_b_in, moe_w_out, moe_b_out):
    bsz, seq, d = x.shape
    t = bsz * seq
    for layer in range(DEPTH):
        j = layer // 2
        if layer % 2 == 0:
            proj = _inproj_time_major(x, ab_w_in[j].astype(BF16))
            ya = _rg_lru(proj, ab_conv_w[j], ab_conv_b[j], ab_gate_r_w[j], ab_gate_r_b[j],
                         ab_gate_i_w[j], ab_gate_i_b[j], ab_lru_lambda[j])
            yb = _s5_glu(proj, ab_s5_lambda_re[j], ab_s5_lambda_im[j], ab_s5_log_step[j], ab_s5_b_re[j],
                         ab_s5_b_im[j], ab_s5_c_re[j], ab_s5_c_im[j], ab_s5_d[j], ab_glu_w[j], ab_glu_b[j])
            time_major, w_out = True, ab_w_out[j]
        else:
            proj = _inproj(x, cd_w_in[j].astype(BF16))
            mq, mk, mv, gates, qd, kd = _cd_prep(proj, positions, cd_conv_w[j], cd_conv_b[j], cd_w_q[j],
                                                 cd_w_k[j], cd_w_v[j], cd_w_if[j], cd_b_if[j])
            ya = _mlstm(mq, mk, mv, gates, proj)
            yb = _dsw_attention(qd, kd, proj)
            time_major, w_out = False, cd_w_out[j]
        xn, x_tiles, route, counts = _outproj_ln_router(
            ya, yb, time_major, w_out.astype(BF16), x, ln_mix_g[layer], ln_mix_b[layer],
            moe_router_w[layer], moe_router_b[layer])
        x = _moe_ffn_ln(xn.reshape(t, d), x_tiles, route.reshape(t, LANES), counts, layer, moe_w_in, moe_b_in,
                        moe_w_out, moe_b_out, ln_ffn_g[layer], ln_ffn_b[layer]).reshape(bsz, seq, d)
    return x
```

```python
import functools

import jax
import jax.numpy as jnp
from jax import lax
from jax.experimental import pallas as pl
from jax.experimental.pallas import tpu as pltpu

F32 = jnp.float32
BF16 = jnp.bfloat16

D_MODEL = 1024
DEPTH = 2
A_WIDTH = 512
B_WIDTH = 512
C_WIDTH = 512
D_WIDTH = 512
LRU_HEADS = 8
LRU_HEAD_DIM = A_WIDTH // LRU_HEADS
LRU_C = 8.0
CONV_WIDTH = 4
S5_GROUP_CH = 16
S5_GROUPS = B_WIDTH // S5_GROUP_CH
S5_STATE = 64
MLSTM_HEADS = 4
MLSTM_HEAD_DIM = C_WIDTH // MLSTM_HEADS
DSW_HEADS = 4
DSW_HEAD_DIM = D_WIDTH // DSW_HEADS
DSW_CONFIGS = ((128, 1), (512, 4), (2048, 16))
DSW_BLOCK = 128
ROPE_THETA = 500000.0
ROPE_DIMS = DSW_HEAD_DIM // 4
N_EXPERTS = 32
TOP_K = 4
D_EXPERT = D_MODEL
SWIGLU_LIMIT = 7.0
SWIGLU_ALPHA = 1.702
DEEPNORM_ALPHA = (2 * DEPTH) ** 0.25
LN_EPS = 1e-5

VMEM_LIMIT_BYTES = 56 * 1024 * 1024
LANES = 128
TILE_ROWS = D_MODEL // LANES
MOE_TM = 512
MOE_TK = 256
MOE_NT = 256
PROJ_TL = 512
ROUTE_TS = 128
LRU_TC = 128
S5_TC = 64
S5_HALF = 2
MLSTM_CS = 256
NEG_INF = float("-inf")


def _cparams(*sem):
    return pltpu.CompilerParams(dimension_semantics=sem, vmem_limit_bytes=VMEM_LIMIT_BYTES)


def _dot(a, b):
    return jnp.dot(a, b, preferred_element_type=F32)


def _dot_nt(a, b):
    return lax.dot_general(a, b, (((1,), (1,)), ((), ())), preferred_element_type=F32)


def _matmul_kernel(x_ref, w_ref, o_ref):
    o_ref[...] = _dot(x_ref[...].astype(BF16), w_ref[...])


def _matmul_time_major_kernel(x_ref, w_ref, o_ref, scr):
    bsz, tl, _ = x_ref.shape
    nchunk = scr.shape[0]
    for b in range(bsz):
        p = _dot(x_ref[b].astype(BF16), w_ref[...])
        for c in range(nchunk):
            scr[c, pl.ds(b, tl, stride=bsz), :] = p[:, c * LANES:(c + 1) * LANES]
    for c in range(nchunk):
        o_ref[:, c * LANES:(c + 1) * LANES] = scr[c]


def _inproj_time_major(x3, w_bf16):
    bsz, seq, k = x3.shape
    n = w_bf16.shape[1]
    tl = LRU_TC
    out = pl.pallas_call(
        _matmul_time_major_kernel,
        grid=(seq // tl,),
        in_specs=[pl.BlockSpec((bsz, tl, k), lambda i: (0, i, 0)),
                  pl.BlockSpec((k, n), lambda i: (0, 0))],
        out_specs=pl.BlockSpec((tl * bsz, n), lambda i: (i, 0)),
        out_shape=jax.ShapeDtypeStruct((seq * bsz, n), F32),
        scratch_shapes=[pltpu.VMEM((n // LANES, tl * bsz, LANES), F32)],
        compiler_params=_cparams("arbitrary"),
        name="in_proj_time_major",
    )(x3, w_bf16)
    return out.reshape(seq, bsz, n)


def _inproj(x3, w_bf16):
    bsz, seq, k = x3.shape
    n = w_bf16.shape[1]
    return pl.pallas_call(
        _matmul_kernel,
        grid=(bsz, seq // PROJ_TL),
        in_specs=[pl.BlockSpec((None, PROJ_TL, k), lambda b, i: (b, i, 0)),
                  pl.BlockSpec((k, n), lambda b, i: (0, 0))],
        out_specs=pl.BlockSpec((None, PROJ_TL, n), lambda b, i: (b, i, 0)),
        out_shape=jax.ShapeDtypeStruct((bsz, seq, n), F32),
        compiler_params=_cparams("arbitrary", "arbitrary"),
        name="in_proj",
    )(x3, w_bf16)


def _layer_norm_rows(z, g, b):
    mu = jnp.mean(z, axis=-1, keepdims=True)
    zc = z - mu
    var = jnp.mean(zc * zc, axis=-1, keepdims=True)
    return zc * lax.rsqrt(var + LN_EPS) * g + b


def _outproj_ln_router_kernel(ya_ref, yb_ref, wa_ref, wb_ref, x_ref, g_ref, b_ref, rwh_ref, rwl_ref, rb_ref,
                              tri_ref, xn_ref, xt_ref, rt_ref, cnt_ref):
    @pl.when((pl.program_id(0) == 0) & (pl.program_id(1) == 0))
    def _init():
        cnt_ref[...] = jnp.zeros_like(cnt_ref)

    mix = _dot(ya_ref[...], wa_ref[...]) + _dot(yb_ref[...], wb_ref[...])
    xn = _layer_norm_rows(DEEPNORM_ALPHA * x_ref[...] + mix, g_ref[...], b_ref[...])
    xn_ref[...] = xn
    _store_token_tiles(xt_ref, xn)
    xh = xn.astype(BF16)
    xl = (xn - xh.astype(F32)).astype(BF16)
    logits = (_dot(xh, rwh_ref[...]) + _dot(xl, rwh_ref[...]) + _dot(xh, rwl_ref[...])) + rb_ref[...]
    ts = tri_ref.shape[0]
    lane = lax.broadcasted_iota(jnp.int32, (ts, LANES), 1).astype(F32)
    counts = cnt_ref[...]
    for s in range(logits.shape[0] // ts):
        lg = logits[s * ts:(s + 1) * ts]
        tops, ids, hots = [], [], []
        for _ in range(TOP_K):
            m = jnp.max(lg, axis=1, keepdims=True)
            idx = jnp.min(jnp.where(lg == m, lane, float(LANES)), axis=1, keepdims=True)
            hot = lane == idx
            lg = jnp.where(hot, NEG_INF, lg)
            tops.append(m)
            ids.append(idx)
            hots.append(hot)
        exps = [jnp.exp(m - tops[0]) for m in tops]
        denom = functools.reduce(jnp.add, exps)
        chosen = functools.reduce(jnp.add, [h.astype(F32) for h in hots])
        before = _dot(tri_ref[...], chosen.astype(BF16)) + counts
        route = jnp.zeros((ts, LANES), F32)
        for k in range(TOP_K):
            rank = jnp.sum(jnp.where(hots[k], before, 0.0), axis=1, keepdims=True)
            route = jnp.where(lane == float(k), ids[k], route)
            route = jnp.where(lane == float(TOP_K + k), rank, route)
            route = jnp.where(lane == float(2 * TOP_K + k), exps[k] / denom, route)
        rt_ref[s * ts:(s + 1) * ts, :] = route
        counts = counts + jnp.sum(chosen, axis=0, keepdims=True)
    cnt_ref[...] = counts


def _outproj_ln_router(ya, yb, time_major, w_bf16, x3, g, b, rw, rb):
    bsz, seq, d = x3.shape
    wa, wb = w_bf16[:ya.shape[-1]], w_bf16[ya.shape[-1]:]
    ne = LANES
    rw = jnp.zeros((d, ne), F32).at[:, :N_EXPERTS].set(rw)
    rb = jnp.full((ne,), NEG_INF, F32).at[:N_EXPERTS].set(rb)
    rw_hi = rw.astype(BF16)
    rw_lo = (rw - rw_hi.astype(F32)).astype(BF16)
    tl = PROJ_TL
    tri = jnp.tril(jnp.ones((ROUTE_TS, ROUTE_TS), BF16), -1)
    if time_major:
        ya, yb = ya.reshape(seq, -1), yb.reshape(seq, -1)
        y_specs = [pl.BlockSpec((tl, wa.shape[0]), lambda bi, i: (i, bi)),
                   pl.BlockSpec((tl, wb.shape[0]), lambda bi, i: (i, bi))]
    else:
        y_specs = [pl.BlockSpec((None, tl, wa.shape[0]), lambda bi, i: (bi, i, 0)),
                   pl.BlockSpec((None, tl, wb.shape[0]), lambda bi, i: (bi, i, 0))]
    fixed = lambda bi, i: (0, 0)
    tok = lambda bi, i: (bi, i, 0)
    return pl.pallas_call(
        _outproj_ln_router_kernel,
        grid=(bsz, seq // tl),
        in_specs=y_specs + [pl.BlockSpec(wa.shape, fixed), pl.BlockSpec(wb.shape, fixed),
                            pl.BlockSpec((None, tl, d), tok),
                            pl.BlockSpec((1, d), fixed), pl.BlockSpec((1, d), fixed),
                            pl.BlockSpec((d, ne), fixed), pl.BlockSpec((d, ne), fixed),
                            pl.BlockSpec((1, ne), fixed), pl.BlockSpec((ROUTE_TS, ROUTE_TS), fixed)],
        out_specs=[pl.BlockSpec((None, tl, d), tok),
                   pl.BlockSpec((tl * TILE_ROWS, LANES), lambda bi, i: (bi * (seq // tl) + i, 0)),
                   pl.BlockSpec((None, tl, ne), tok), pl.BlockSpec((1, ne), fixed)],
        out_shape=[jax.ShapeDtypeStruct((bsz, seq, d), F32),
                   jax.ShapeDtypeStruct((bsz * seq * TILE_ROWS, LANES), F32),
                   jax.ShapeDtypeStruct((bsz, seq, ne), F32), jax.ShapeDtypeStruct((1, ne), F32)],
        compiler_params=_cparams("arbitrary", "arbitrary"),
        name="out_proj_ln_router",
    )(ya, yb, wa, wb, x3, g.reshape(1, d), b.reshape(1, d), rw_hi, rw_lo, rb.reshape(1, ne), tri)


def _store_token_tiles(dst_ref, x):
    n = x.shape[0]
    for j in range(TILE_ROWS):
        dst_ref[pl.ds(j, n, stride=TILE_ROWS), :] = x[:, j * LANES:(j + 1) * LANES]


def _load_token_tiles(src_ref, n):
    return jnp.concatenate([src_ref[pl.ds(j, n, stride=TILE_ROWS), :] for j in range(TILE_ROWS)], axis=1)


def _gather_token_tiles(src_hbm, idx_ref, n, dst, sem, priorities):
    def body(g, carry):
        base = pl.multiple_of(g * (TILE_ROWS * TILE_ROWS), TILE_ROWS * TILE_ROWS)
        for u in range(TILE_ROWS):
            src_row = pl.multiple_of(idx_ref[0, g * TILE_ROWS + u], TILE_ROWS)
            pltpu.make_async_copy(src_hbm.at[pl.ds(src_row, TILE_ROWS)],
                                  dst.at[pl.ds(base + u * TILE_ROWS, TILE_ROWS)], sem).start(priority=u % priorities)
        return carry
    lax.fori_loop(0, n // TILE_ROWS, body, 0, unroll=2)


def _moe_kernel(be_ref, nreal_ref, rt_ref, rtn_ref, x_hbm, rg_ref, win_ref, bin_ref, wout_ref, bout_ref, o_ref,
                win_bf, wout_bf, xbuf, sem):
    i = pl.program_id(0)
    last = pl.num_programs(0) - 1
    n_real = nreal_ref[0]
    e = be_ref[i]
    prev = be_ref[jnp.maximum(i - 1, 0)]
    tm = xbuf.shape[1] // TILE_ROWS
    slot = lax.rem(i, 2)
    nt = MOE_NT
    kh = D_MODEL // 2
    n_groups = 2 * 3 * (D_EXPERT // nt)
    bounds = [tm * k // n_groups for k in range(n_groups + 1)]
    group = [0]

    def wait(s):
        pltpu.make_async_copy(xbuf.at[s], xbuf.at[s], sem.at[s]).wait()

    def start_group(s):
        k = group[0]
        group[0] += 1
        for r in range(bounds[k], bounds[k + 1]):
            src_row = pl.multiple_of(rtn_ref[0, r], TILE_ROWS)
            pltpu.make_async_copy(x_hbm.at[pl.ds(src_row, TILE_ROWS)],
                                  xbuf.at[s, pl.ds(r * TILE_ROWS, TILE_ROWS)], sem.at[s]).start(priority=r % 2)

    def dot2(a, w_ref, lo, hi, s):
        first = _dot(a[:, :kh], w_ref[:kh, lo:hi])
        start_group(s)
        second = _dot(a[:, kh:], w_ref[kh:, lo:hi])
        start_group(s)
        return first + second

    @pl.when(i == 0)
    def _first_gather():
        _gather_token_tiles(x_hbm, rt_ref, tm, xbuf.at[0], sem.at[0], priorities=1)

    @pl.when((i == 0) | (e != prev))
    def _load_expert():
        win_bf[...] = win_ref[...].astype(BF16)
        wout_bf[...] = wout_ref[...].astype(BF16)

    computes = (i < n_real) | (i == 0)

    @pl.when(computes)
    def _compute():
        wait(slot)
        x = _load_token_tiles(xbuf.at[slot], tm).astype(BF16)
        ys = []
        for t in range(D_EXPERT // nt):
            lo, hi = t * nt, (t + 1) * nt
            hg = dot2(x, win_bf, lo, hi, 1 - slot) + bin_ref[:, lo:hi]
            hl = dot2(x, win_bf, D_EXPERT + lo, D_EXPERT + hi, 1 - slot) + bin_ref[:, D_EXPERT + lo:D_EXPERT + hi]
            g = jnp.minimum(hg, SWIGLU_LIMIT)
            lin = jnp.clip(hl, -SWIGLU_LIMIT, SWIGLU_LIMIT)
            ys.append((g * jax.nn.sigmoid(SWIGLU_ALPHA * g) * (lin + 1.0)).astype(BF16))
        y = jnp.concatenate(ys, axis=1)
        gates = rg_ref[...].T
        for t in range(D_MODEL // nt):
            lo, hi = t * nt, (t + 1) * nt
            out = dot2(y, wout_bf, lo, hi, 1 - slot) + bout_ref[:, lo:hi]
            out = jnp.concatenate(
                [out[j * LANES:(j + 1) * LANES] * gates[:, j:j + 1] for j in range(tm // LANES)], axis=0)
            for c in range(nt // LANES):
                o_ref[pl.ds(t * (nt // LANES) + c, tm, stride=TILE_ROWS), :] = out[:, c * LANES:(c + 1) * LANES]

        @pl.when(i == last)
        def _drain():
            wait(1 - slot)

    @pl.when(jnp.logical_not(computes))
    def _unused_block():
        @pl.when((i <= n_real) | (i == 1))
        def _drain():
            wait(slot)
        o_ref[...] = jnp.zeros_like(o_ref)


def _moe_experts(x_tiles, row_tok, row_gate, block_expert, n_real, layer, w_in, b_in, w_out, b_out):
    d = D_MODEL
    tm = MOE_TM
    n_blocks = row_tok.shape[0]
    nl, ne, _, dh2 = w_in.shape
    rt = (row_tok * TILE_ROWS).reshape(n_blocks, 1, tm)
    rg = jnp.pad(row_gate.reshape(n_blocks, tm // LANES, LANES), ((0, 0), (0, TILE_ROWS - tm // LANES), (0, 0)))
    smem_rows = lambda imap: pl.BlockSpec((None, 1, tm), imap, memory_space=pltpu.SMEM)
    grid_spec = pltpu.PrefetchScalarGridSpec(
        num_scalar_prefetch=2,
        grid=(n_blocks,),
        in_specs=[
            smem_rows(lambda i, be, nr: (i, 0, 0)),
            smem_rows(lambda i, be, nr: (jnp.minimum(i + 1, n_blocks - 1), 0, 0)),
            pl.BlockSpec(memory_space=pl.ANY),
            pl.BlockSpec((None, TILE_ROWS, LANES), lambda i, be, nr: (i, 0, 0)),
            pl.BlockSpec((None, None, d, dh2), lambda i, be, nr: (layer, be[i], 0, 0)),
            pl.BlockSpec((None, None, 1, dh2), lambda i, be, nr: (layer, be[i], 0, 0)),
            pl.BlockSpec((None, None, D_EXPERT, d), lambda i, be, nr: (layer, be[i], 0, 0)),
            pl.BlockSpec((None, None, 1, d), lambda i, be, nr: (layer, be[i], 0, 0)),
        ],
        out_specs=pl.BlockSpec((tm * TILE_ROWS, LANES), lambda i, be, nr: (i, 0)),
        scratch_shapes=[pltpu.VMEM((d, dh2), BF16), pltpu.VMEM((D_EXPERT, d), BF16),
                        pltpu.VMEM((2, tm * TILE_ROWS, LANES), F32), pltpu.SemaphoreType.DMA((2,))],
    )
    return pl.pallas_call(
        _moe_kernel,
        grid_spec=grid_spec,
        out_shape=jax.ShapeDtypeStruct((n_blocks * tm * TILE_ROWS, LANES), F32),
        compiler_params=_cparams("arbitrary"),
        name="moe_experts",
    )(block_expert, n_real, rt, rt, x_tiles, rg, w_in,
      b_in.reshape(nl, ne, 1, dh2), w_out, b_out.reshape(nl, ne, 1, d))


def _moe_combine_kernel(dc_ref, dn_ref, x_ref, yb_hbm, g_ref, b_ref, o_ref, buf, ffn_ref, sem):
    i = pl.program_id(0)
    n = pl.num_programs(0)
    tk = x_ref.shape[0]
    rows = tk * TILE_ROWS
    slot = lax.rem(i, 2)

    @pl.when(i == 0)
    def _first_gather():
        _gather_token_tiles(yb_hbm, dc_ref, TOP_K * tk, buf.at[0], sem.at[0], priorities=2)

    @pl.when(i + 1 < n)
    def _next_gather():
        _gather_token_tiles(yb_hbm, dn_ref, TOP_K * tk, buf.at[1 - slot], sem.at[1 - slot], priorities=2)

    pltpu.make_async_copy(buf.at[slot], buf.at[slot], sem.at[slot]).wait()
    ffn = buf[slot, 0:rows]
    for k in range(1, TOP_K):
        ffn = ffn + buf[slot, k * rows:(k + 1) * rows]
    ffn_ref[...] = ffn
    o_ref[...] = _layer_norm_rows(DEEPNORM_ALPHA * x_ref[...] + _load_token_tiles(ffn_ref, tk),
                                  g_ref[...], b_ref[...])


def _moe_combine_ln(xn, dest, yb_tiles, g, b):
    t, d = xn.shape
    tk = MOE_TK
    n = t // tk
    dest_tiles = (dest * TILE_ROWS).reshape(n, tk, TOP_K).transpose(0, 2, 1).reshape(n, 1, TOP_K * tk)
    smem_rows = lambda imap: pl.BlockSpec((None, 1, TOP_K * tk), imap, memory_space=pltpu.SMEM)
    row = lambda i: (i, 0)
    fixed = lambda i: (0, 0)
    return pl.pallas_call(
        _moe_combine_kernel,
        grid=(n,),
        in_specs=[smem_rows(lambda i: (i, 0, 0)), smem_rows(lambda i: (jnp.minimum(i + 1, n - 1), 0, 0)),
                  pl.BlockSpec((tk, d), row), pl.BlockSpec(memory_space=pl.ANY),
                  pl.BlockSpec((1, d), fixed), pl.BlockSpec((1, d), fixed)],
        out_specs=pl.BlockSpec((tk, d), row),
        out_shape=jax.ShapeDtypeStruct((t, d), F32),
        scratch_shapes=[pltpu.VMEM((2, TOP_K * tk * TILE_ROWS, LANES), F32),
                        pltpu.VMEM((tk * TILE_ROWS, LANES), F32), pltpu.SemaphoreType.DMA((2,))],
        compiler_params=_cparams("arbitrary"),
        name="moe_combine_ln",
    )(dest_tiles, dest_tiles, xn, yb_tiles, g.reshape(1, d), b.reshape(1, d))


def _moe_ffn_ln(xn, x_tiles, route, counts, layer, w_in, b_in, w_out, b_out, g, b):
    t, d = xn.shape
    tm = MOE_TM
    flat_e = route[:, :TOP_K].astype(jnp.int32).reshape(-1)
    rank = route[:, TOP_K:2 * TOP_K].astype(jnp.int32).reshape(-1)
    gate = route[:, 2 * TOP_K:3 * TOP_K].reshape(-1)
    counts = counts[0, :N_EXPERTS].astype(jnp.int32)
    blocks_per_e = (counts + tm - 1) // tm
    blk_end = jnp.cumsum(blocks_per_e)
    blk_start = blk_end - blocks_per_e
    row_start = jnp.cumsum(counts) - counts
    dest = blk_start[flat_e] * tm + rank
    n_blocks = (t * TOP_K) // tm + N_EXPERTS
    block_expert = jnp.minimum(
        jnp.sum((blk_end[None, :] <= jnp.arange(n_blocks, dtype=jnp.int32)[:, None]).astype(jnp.int32), axis=1),
        N_EXPERTS - 1)
    n_real = blk_end[-1:].astype(jnp.int32)
    order = jnp.argsort(flat_e, stable=True).astype(jnp.int32)
    blk = jnp.arange(n_blocks, dtype=jnp.int32)
    rank_row = ((blk - blk_start[block_expert]) * tm)[:, None] + jnp.arange(tm, dtype=jnp.int32)[None, :]
    valid = rank_row < counts[block_expert][:, None]
    pair = order[jnp.clip(row_start[block_expert][:, None] + rank_row, 0, t * TOP_K - 1)]
    row_tok = jnp.where(valid, pair // TOP_K, 0)
    row_gate = jnp.where(valid, gate[pair], 0.0)
    yb_tiles = _moe_experts(x_tiles, row_tok, row_gate, block_expert, n_real, layer, w_in, b_in, w_out, b_out)
    return _moe_combine_ln(xn, dest, yb_tiles, g, b)


def _softplus(x):
    return jnp.maximum(x, 0.0) + jnp.log(1.0 + jnp.exp(-jnp.abs(x)))


def _gelu_tanh(x):
    return 0.5 * x * (1.0 + jnp.tanh(0.7978845608028654 * (x + 0.044715 * (x * x * x))))


def _lru_kernel(xa_ref, ga_ref, cw_ref, cb_ref, wr_ref, br_ref, wi_ref, bi_ref, lam_ref, o_ref,
                tail_ref, h_ref, a_ref, u_ref):
    tc, bsz, w = xa_ref.shape

    @pl.when(pl.program_id(0) == 0)
    def _init():
        tail_ref[...] = jnp.zeros_like(tail_ref)
        h_ref[...] = jnp.zeros_like(h_ref)

    xa = xa_ref[...]
    ext = jnp.concatenate([tail_ref[...], xa], axis=0)
    xc = cb_ref[...] + ext[0:tc] * cw_ref[0]
    for tap in range(1, CONV_WIDTH):
        xc = xc + ext[tap:tap + tc] * cw_ref[tap]
    tail_ref[...] = xa[tc - (CONV_WIDTH - 1):]
    x2 = xc.reshape(tc * bsz, w)
    xb = x2.astype(BF16)
    r = jax.nn.sigmoid(_dot(xb, wr_ref[...]) + br_ref[...])
    ig = jax.nn.sigmoid(_dot(xb, wi_ref[...]) + bi_ref[...])
    log_a = (-LRU_C) * r * _softplus(-lam_ref[...])
    a_ref[...] = jnp.exp(log_a).reshape(tc, bsz, w)
    u_ref[...] = (jnp.sqrt(1.0 - jnp.exp(2.0 * log_a)) * (ig * x2)).reshape(tc, bsz, w)

    def step(t, h):
        h = a_ref[t] * h + u_ref[t]
        u_ref[t] = h
        return h

    h_ref[...] = lax.fori_loop(0, tc, step, h_ref[...], unroll=8)
    o_ref[...] = (u_ref[...] * _gelu_tanh(ga_ref[...])).astype(o_ref.dtype)


def _block_diag(w):
    h, i, j = w.shape
    return jnp.einsum('hij,hk->hikj', w, jnp.eye(h, dtype=w.dtype)).reshape(h * i, h * j)


def _rg_lru(proj, conv_w, conv_b, gate_r_w, gate_r_b, gate_i_w, gate_i_b, lru_lambda):
    seq, bsz, _ = proj.shape
    w = A_WIDTH
    tc = LRU_TC
    fixed2 = lambda i: (0, 0)
    fixed3 = lambda i: (0, 0, 0)
    return pl.pallas_call(
        _lru_kernel,
        grid=(seq // tc,),
        in_specs=[pl.BlockSpec((tc, bsz, w), lambda i: (i, 0, 0)),
                  pl.BlockSpec((tc, bsz, w), lambda i: (i, 0, 1)),
                  pl.BlockSpec((CONV_WIDTH, 1, w), fixed3), pl.BlockSpec((1, w), fixed2),
                  pl.BlockSpec((w, w), fixed2), pl.BlockSpec((1, w), fixed2),
                  pl.BlockSpec((w, w), fixed2), pl.BlockSpec((1, w), fixed2),
                  pl.BlockSpec((1, w), fixed2)],
        out_specs=pl.BlockSpec((tc, bsz, w), lambda i: (i, 0, 0)),
        out_shape=jax.ShapeDtypeStruct((seq, bsz, w), BF16),
        scratch_shapes=[pltpu.VMEM((CONV_WIDTH - 1, bsz, w), F32), pltpu.VMEM((bsz, w), F32),
                        pltpu.VMEM((tc, bsz, w), F32), pltpu.VMEM((tc, bsz, w), F32)],
        compiler_params=_cparams("arbitrary"),
        name="rg_lru",
    )(proj, proj, conv_w.reshape(CONV_WIDTH, 1, w), conv_b.reshape(1, w),
      _block_diag(gate_r_w).astype(BF16), gate_r_b.reshape(1, w),
      _block_diag(gate_i_w).astype(BF16), gate_i_b.reshape(1, w), lru_lambda.reshape(1, w))


def _s5_kernel(u_ref, wbr_ref, wbi_ref, ar_ref, ai_ref, ccr_ref, cci_ref, d_ref, gw_ref, gb_ref, o_ref,
               xr_ref, xi_ref, sr_ref, si_ref):
    tc, bsz, w = u_ref.shape
    nstate = xr_ref.shape[-1]
    sh = nstate // S5_HALF
    wh = w // S5_HALF

    @pl.when(pl.program_id(0) == 0)
    def _init():
        sr_ref[...] = jnp.zeros_like(sr_ref)
        si_ref[...] = jnp.zeros_like(si_ref)

    u2 = u_ref[...].reshape(tc * bsz, w)
    ub = u2.astype(BF16)
    for hf in range(S5_HALF):
        uh = ub[:, hf * wh:(hf + 1) * wh]
        xr_ref[:, :, hf * sh:(hf + 1) * sh] = _dot(uh, wbr_ref[hf]).reshape(tc, bsz, sh)
        xi_ref[:, :, hf * sh:(hf + 1) * sh] = _dot(uh, wbi_ref[hf]).reshape(tc, bsz, sh)

    for hf in range(S5_HALF):
        lo, hi = hf * sh, (hf + 1) * sh
        ar = jnp.broadcast_to(ar_ref[:, lo:hi], (bsz, sh))
        ai = jnp.broadcast_to(ai_ref[:, lo:hi], (bsz, sh))

        def step(t, carry, lo=lo, hi=hi, ar=ar, ai=ai):
            xr, xi = carry
            nxr = ar * xr - ai * xi + xr_ref[t, :, lo:hi]
            nxi = ar * xi + ai * xr + xi_ref[t, :, lo:hi]
            xr_ref[t, :, lo:hi] = nxr
            xi_ref[t, :, lo:hi] = nxi
            return nxr, nxi

        xr, xi = lax.fori_loop(0, tc, step, (sr_ref[:, lo:hi], si_ref[:, lo:hi]), unroll=4)
        sr_ref[:, lo:hi] = xr
        si_ref[:, lo:hi] = xi

    ys = []
    for hf in range(S5_HALF):
        lo, hi = hf * sh, (hf + 1) * sh
        xrh = xr_ref[:, :, lo:hi].reshape(tc * bsz, sh).astype(BF16)
        xih = xi_ref[:, :, lo:hi].reshape(tc * bsz, sh).astype(BF16)
        ys.append(_dot(xrh, ccr_ref[hf]) - _dot(xih, cci_ref[hf]))
    y = jnp.concatenate(ys, axis=1) + d_ref[...] * u2
    s = _gelu_tanh(y)
    yb = s * jax.nn.sigmoid(_dot(s.astype(BF16), gw_ref[...]) + gb_ref[...])
    o_ref[...] = yb.reshape(tc, bsz, w).astype(o_ref.dtype)


def _s5_glu(proj, lam_re, lam_im, log_step, b_re, b_im, c_re, c_im, d_skip, glu_w, glu_b):
    seq, bsz, _ = proj.shape
    w = B_WIDTH
    tc = S5_TC
    nstate = S5_GROUPS * S5_STATE
    gh = S5_GROUPS // S5_HALF
    lr = jnp.minimum(lam_re, -1e-4)
    li = lam_im
    step = jnp.exp(log_step)[:, None]
    mag = jnp.exp(lr * step)
    ar = mag * jnp.cos(li * step)
    ai = mag * jnp.sin(li * step)
    inv = 1.0 / (lr * lr + li * li)
    zr = ((ar - 1.0) * lr + ai * li) * inv
    zi = (ai * lr - (ar - 1.0) * li) * inv
    bbr = zr[..., None] * b_re - zi[..., None] * b_im
    bbi = zr[..., None] * b_im + zi[..., None] * b_re
    eye = jnp.eye(gh, dtype=F32)

    def expand_in(bb):
        bb = bb.reshape(S5_HALF, gh, S5_STATE, S5_GROUP_CH)
        return jnp.einsum('fgph,gk->fghkp', bb, eye).reshape(
            S5_HALF, gh * S5_GROUP_CH, gh * S5_STATE).astype(BF16)

    def expand_out(c):
        c = c.reshape(S5_HALF, gh, S5_GROUP_CH, S5_STATE)
        return jnp.einsum('fghp,gk->fgpkh', c, eye).reshape(
            S5_HALF, gh * S5_STATE, gh * S5_GROUP_CH).astype(BF16)

    fixed2 = lambda i: (0, 0)
    fixed3 = lambda i: (0, 0, 0)
    return pl.pallas_call(
        _s5_kernel,
        grid=(seq // tc,),
        in_specs=[pl.BlockSpec((tc, bsz, w), lambda i: (i, 0, 2)),
                  pl.BlockSpec((S5_HALF, w // S5_HALF, nstate // S5_HALF), fixed3),
                  pl.BlockSpec((S5_HALF, w // S5_HALF, nstate // S5_HALF), fixed3),
                  pl.BlockSpec((1, nstate), fixed2), pl.BlockSpec((1, nstate), fixed2),
                  pl.BlockSpec((S5_HALF, nstate // S5_HALF, w // S5_HALF), fixed3),
                  pl.BlockSpec((S5_HALF, nstate // S5_HALF, w // S5_HALF), fixed3),
                  pl.BlockSpec((1, w), fixed2), pl.BlockSpec((w, w), fixed2), pl.BlockSpec((1, w), fixed2)],
        out_specs=pl.BlockSpec((tc, bsz, w), lambda i: (i, 0, 0)),
        out_shape=jax.ShapeDtypeStruct((seq, bsz, w), BF16),
        scratch_shapes=[pltpu.VMEM((tc, bsz, nstate), F32), pltpu.VMEM((tc, bsz, nstate), F32),
                        pltpu.VMEM((bsz, nstate), F32), pltpu.VMEM((bsz, nstate), F32)],
        compiler_params=_cparams("arbitrary"),
        name="s5_glu",
    )(proj, expand_in(bbr), expand_in(bbi), ar.reshape(1, nstate), ai.reshape(1, nstate),
      expand_out(c_re), expand_out(c_im), d_skip.reshape(1, w), glu_w.astype(BF16), glu_b.reshape(1, w))


def _cd_prep_kernel(xc_ref, q_ref, k_ref, pos_ref, cw_ref, cb_ref, wq_ref, wk_ref, wv_ref, wif_ref,
                    bif_ref, invf_ref, mq_ref, mk_ref, mv_ref, gt_ref, qd_ref, kd_ref, tail_ref):
    tl, w = xc_ref.shape
    pad = tail_ref.shape[0]

    @pl.when(pl.program_id(1) == 0)
    def _init():
        tail_ref[...] = jnp.zeros_like(tail_ref)

    xc = xc_ref[...]
    ext = jnp.concatenate([tail_ref[...], xc], axis=0)
    conv = cb_ref[...]
    for tap in range(CONV_WIDTH):
        off = pad - (CONV_WIDTH - 1) + tap
        conv = conv + ext[off:off + tl] * cw_ref[tap]
    tail_ref[...] = xc[tl - pad:]
    xconv = (conv * jax.nn.sigmoid(conv)).astype(BF16)
    q = _dot(xconv, wq_ref[...]).astype(BF16)
    k = (_dot(xconv, wk_ref[...]) * (MLSTM_HEAD_DIM ** -0.5)).astype(BF16)
    v = _dot(xc.astype(BF16), wv_ref[...]).astype(BF16)
    mq_ref[...] = q
    mk_ref[...] = k
    mv_ref[...] = v
    gt_ref[...] = (_dot(q, wif_ref[0]) + _dot(k, wif_ref[1]) + _dot(v, wif_ref[2]) + bif_ref[...])

    ang = pos_ref[...].astype(F32) * invf_ref[...]
    cos = jnp.cos(ang)
    sin = jnp.sin(ang)
    lane = lax.broadcasted_iota(jnp.int32, (1, LANES), 1)
    half = ROPE_DIMS // 2
    sin_lo = jnp.where(lane < half, -sin, 0.0)
    sin_hi = jnp.where((lane >= half) & (lane < ROPE_DIMS), sin, 0.0)
    for src, dst in ((q_ref, qd_ref), (k_ref, kd_ref)):
        for h in range(DSW_HEADS):
            sl = slice(h * DSW_HEAD_DIM, (h + 1) * DSW_HEAD_DIM)
            t = src[:, sl]
            rot = (t * cos + pltpu.roll(t, LANES - half, axis=1) * sin_lo
                   + pltpu.roll(t, half, axis=1) * sin_hi)
            dst[:, sl] = rot


def _cd_prep(proj, positions, conv_w, conv_b, w_q, w_k, w_v, w_if, b_if):
    bsz, seq, _ = proj.shape
    w = C_WIDTH
    tl = PROJ_TL
    half = ROPE_DIMS // 2
    inv_freq = ROPE_THETA ** (-jnp.arange(half, dtype=F32) / half)
    invf = jnp.zeros((1, LANES), F32).at[0, :half].set(inv_freq).at[0, half:ROPE_DIMS].set(inv_freq)
    wif = jnp.zeros((3, w, LANES), F32).at[:, :, :2 * MLSTM_HEADS].set(w_if.reshape(3, w, 2 * MLSTM_HEADS))
    bif = jnp.zeros((1, LANES), F32).at[0, :2 * MLSTM_HEADS].set(b_if)
    col = lambda c: (lambda b, i: (b, i, c))
    fixed2 = lambda b, i: (0, 0)
    fixed3 = lambda b, i: (0, 0, 0)
    tok_bf = jax.ShapeDtypeStruct((bsz, seq, w), BF16)
    return pl.pallas_call(
        _cd_prep_kernel,
        grid=(bsz, seq // tl),
        in_specs=[pl.BlockSpec((None, tl, w), col(0)), pl.BlockSpec((None, tl, w), col(2)),
                  pl.BlockSpec((None, tl, w), col(3)), pl.BlockSpec((None, tl, 1), col(0)),
                  pl.BlockSpec((CONV_WIDTH, 1, w), fixed3), pl.BlockSpec((1, w), fixed2),
                  pl.BlockSpec((w, w), fixed2), pl.BlockSpec((w, w), fixed2), pl.BlockSpec((w, w), fixed2),
                  pl.BlockSpec((3, w, LANES), fixed3), pl.BlockSpec((1, LANES), fixed2),
                  pl.BlockSpec((1, LANES), fixed2)],
        out_specs=[pl.BlockSpec((None, tl, w), col(0))] * 3 + [pl.BlockSpec((None, tl, LANES), col(0))]
                  + [pl.BlockSpec((None, tl, w), col(0))] * 2,
        out_shape=[tok_bf, tok_bf, tok_bf, jax.ShapeDtypeStruct((bsz, seq, LANES), F32),
                   jax.ShapeDtypeStruct((bsz, seq, w), F32), jax.ShapeDtypeStruct((bsz, seq, w), F32)],
        scratch_shapes=[pltpu.VMEM((8, w), F32)],
        compiler_params=_cparams("arbitrary", "arbitrary"),
        name="cd_prep",
    )(proj, proj, proj, positions.reshape(bsz, seq, 1), conv_w.reshape(CONV_WIDTH, 1, w),
      conv_b.reshape(1, w), _block_diag(w_q).astype(BF16), _block_diag(w_k).astype(BF16),
      _block_diag(w_v).astype(BF16), wif.astype(BF16), bif, invf)


def _cumsum_rows(x):
    n = x.shape[0]
    row = lax.broadcasted_iota(jnp.int32, x.shape, 0)
    shift = 1
    while shift < n:
        x = x + jnp.where(row >= shift, pltpu.roll(x, shift, axis=0), 0.0)
        shift *= 2
    return x


def _mlstm_kernel(q_ref, k_ref, v_ref, g_ref, op_ref, o_ref, c_ref, n_ref, m_ref):
    cs = q_ref.shape[0]
    dh = MLSTM_HEAD_DIM

    @pl.when(pl.program_id(1) == 0)
    def _init():
        c_ref[...] = jnp.zeros_like(c_ref)
        n_ref[...] = jnp.zeros_like(n_ref)
        m_ref[...] = jnp.full_like(m_ref, NEG_INF)

    gates = g_ref[...]
    cum = _cumsum_rows(jax.nn.log_sigmoid(gates))
    gates_t = gates.T
    cum_t = cum.T
    causal = (lax.broadcasted_iota(jnp.int32, (cs, cs), 0) >= lax.broadcasted_iota(jnp.int32, (cs, cs), 1))
    for h in range(MLSTM_HEADS):
        sl = slice(h * dh, (h + 1) * dh)
        fcol = MLSTM_HEADS + h
        li_c, li_r = gates[:, h:h + 1], gates_t[h:h + 1, :]
        cum_c, cum_r = cum[:, fcol:fcol + 1], cum_t[fcol:fcol + 1, :]
        m_prev = m_ref[h][:, 0:1]
        q, k, v = q_ref[:, sl], k_ref[:, sl], v_ref[:, sl]
        log_d = jnp.where(causal, cum_c - cum_r + li_r, NEG_INF)
        log_inter = cum_c + m_prev
        m_s = jnp.maximum(jnp.max(log_d, axis=1, keepdims=True), log_inter)
        s = _dot_nt(q, k) * jnp.exp(log_d - m_s)
        inter = jnp.exp(log_inter - m_s)
        c_prev = c_ref[h]
        n_prev = n_ref[h]
        num = _dot(s.astype(BF16), v) + inter * _dot(q, c_prev.astype(BF16))
        den = (jnp.sum(s, axis=1, keepdims=True)
               + inter * jnp.sum(q.astype(F32) * n_prev, axis=1, keepdims=True))
        hh = num / jnp.maximum(jnp.abs(den), jnp.exp(-m_s))
        o_ref[:, sl] = (jax.nn.sigmoid(op_ref[:, sl]) * hh).astype(o_ref.dtype)
        chunk_f = cum_c[cs - 1:cs, :]
        to_end = chunk_f - cum_c + li_c
        m_c = jnp.max(to_end, axis=0, keepdims=True)
        m_new = jnp.maximum(chunk_f + m_prev, m_c)
        s_old = jnp.exp(chunk_f + m_prev - m_new)
        s_new = jnp.exp(m_c - m_new)
        kw = k.astype(F32) * jnp.exp(to_end - m_c)
        c_ref[h] = s_old * c_prev + s_new * _dot(kw.T.astype(BF16), v)
        n_ref[h] = s_old * n_prev + s_new * jnp.sum(kw, axis=0, keepdims=True)
        m_ref[h] = jnp.broadcast_to(m_new, (1, LANES))


def _mlstm(mq, mk, mv, gates, proj):
    bsz, seq, w = mq.shape
    cs = MLSTM_CS
    tok = lambda b, i: (b, i, 0)
    return pl.pallas_call(
        _mlstm_kernel,
        grid=(bsz, seq // cs),
        in_specs=[pl.BlockSpec((None, cs, w), tok)] * 3
                 + [pl.BlockSpec((None, cs, LANES), tok), pl.BlockSpec((None, cs, w), lambda b, i: (b, i, 1))],
        out_specs=pl.BlockSpec((None, cs, w), tok),
        out_shape=jax.ShapeDtypeStruct((bsz, seq, w), BF16),
        scratch_shapes=[pltpu.VMEM((MLSTM_HEADS, MLSTM_HEAD_DIM, MLSTM_HEAD_DIM), F32),
                        pltpu.VMEM((MLSTM_HEADS, 1, MLSTM_HEAD_DIM), F32),
                        pltpu.VMEM((MLSTM_HEADS, 1, LANES), F32)],
        compiler_params=_cparams("arbitrary", "arbitrary"),
        name="mlstm",
    )(mq, mk, mv, gates, proj)


def _dsw_kernel(q_ref, k_ref, v_ref, y_ref, o_scr, lse_scr):
    seq, dh = q_ref.shape
    blk = DSW_BLOCK
    qi = lax.broadcasted_iota(jnp.int32, (blk, 2 * blk), 0)
    ki = lax.broadcasted_iota(jnp.int32, (blk, 2 * blk), 1)
    dist = qi + blk - ki
    scale = dh ** -0.5
    units = seq // blk
    for g, (window, dil) in enumerate(DSW_CONFIGS):
        nsub = seq // dil // blk
        band = (dist >= 0) & (dist <= window // dil)

        def unit(u, carry, g=g, dil=dil, nsub=nsub, band=band):
            r = u // nsub
            sb = u % nsub
            cur = pl.ds(r + sb * (blk * dil), blk, stride=dil)
            prev = pl.ds(r + jnp.maximum(sb - 1, 0) * (blk * dil), blk, stride=dil)
            kk = jnp.concatenate([k_ref[prev, :], k_ref[cur, :]], axis=0).astype(BF16)
            vv = jnp.concatenate([v_ref[prev, :], v_ref[cur, :]], axis=0).astype(BF16)
            s = _dot_nt(q_ref[cur, :].astype(BF16), kk) * scale
            s = jnp.where(band & ((ki >= blk) | (sb > 0)), s, NEG_INF)
            m = jnp.max(s, axis=1, keepdims=True)
            p = jnp.exp(s - m)
            l = jnp.sum(p, axis=1, keepdims=True)
            o_scr[g, cur, :] = _dot(p.astype(BF16), vv) / l
            lse_scr[g, cur, :] = jnp.broadcast_to(m + jnp.log(l), (blk, dh))
            return carry

        lax.fori_loop(0, units, unit, 0, unroll=8)

    lses = [lse_scr[g] for g in range(len(DSW_CONFIGS))]
    mx = functools.reduce(jnp.maximum, lses)
    ws = [jnp.exp(l - mx) for l in lses]
    acc = sum(ws[g] * o_scr[g] for g in range(len(DSW_CONFIGS)))
    y_ref[...] = (acc / sum(ws)).astype(y_ref.dtype)


def _dsw_attention(qd, kd, proj):
    bsz, seq, w = qd.shape
    dh = DSW_HEAD_DIM
    v_col = (proj.shape[-1] - w) // dh
    head = lambda b, h: (b, 0, h)
    return pl.pallas_call(
        _dsw_kernel,
        grid=(bsz, DSW_HEADS),
        in_specs=[pl.BlockSpec((None, seq, dh), head), pl.BlockSpec((None, seq, dh), head),
                  pl.BlockSpec((None, seq, dh), lambda b, h: (b, 0, v_col + h))],
        out_specs=pl.BlockSpec((None, seq, dh), head),
        out_shape=jax.ShapeDtypeStruct((bsz, seq, w), BF16),
        scratch_shapes=[pltpu.VMEM((len(DSW_CONFIGS), seq, dh), F32),
                        pltpu.VMEM((len(DSW_CONFIGS), seq, dh), F32)],
        compiler_params=_cparams("arbitrary", "arbitrary"),
        name="dsw_attention",
    )(qd, kd, proj)


def kernel(x, positions, ab_w_in, ab_conv_w, ab_conv_b, ab_gate_r_w, ab_gate_r_b, ab_gate_i_w, ab_gate_i_b, ab_lru_lambda, ab_s5_lambda_re, ab_s5_lambda_im, ab_s5_log_step, ab_s5_b_re, ab_s5_b_im, ab_s5_c_re, ab_s5_c_im, ab_s5_d, ab_glu_w, ab_glu_b, ab_w_out, cd_w_in, cd_conv_w, cd_conv_b, cd_w_q, cd_w_k, cd_w_v, cd_w_if, cd_b_if, cd_w_out, ln_mix_g, ln_mix_b, ln_ffn_g, ln_ffn_b, moe_router_w, moe_router_b, moe_w_in, moe_b_in, moe_w_out, moe_b_out):
    bsz, seq, d = x.shape
    t = bsz * seq
    for layer in range(DEPTH):
        j = layer // 2
        if layer % 2 == 0:
            proj = _inproj_time_major(x, ab_w_in[j].astype(BF16))
            ya = _rg_lru(proj, ab_conv_w[j], ab_conv_b[j], ab_gate_r_w[j], ab_gate_r_b[j],
                         ab_gate_i_w[j], ab_gate_i_b[j], ab_lru_lambda[j])
            yb = _s5_glu(proj, ab_s5_lambda_re[j], ab_s5_lambda_im[j], ab_s5_log_step[j], ab_s5_b_re[j],
                         ab_s5_b_im[j], ab_s5_c_re[j], ab_s5_c_im[j], ab_s5_d[j], ab_glu_w[j], ab_glu_b[j])
            time_major, w_out = True, ab_w_out[j]
        else:
            proj = _inproj(x, cd_w_in[j].astype(BF16))
            mq, mk, mv, gates, qd, kd = _cd_prep(proj, positions, cd_conv_w[j], cd_conv_b[j], cd_w_q[j],
                                                 cd_w_k[j], cd_w_v[j], cd_w_if[j], cd_b_if[j])
            ya = _mlstm(mq, mk, mv, gates, proj)
            yb = _dsw_attention(qd, kd, proj)
            time_major, w_out = False, cd_w_out[j]
        xn, x_tiles, route, counts = _outproj_ln_router(
            ya, yb, time_major, w_out.astype(BF16), x, ln_mix_g[layer], ln_mix_b[layer],
            moe_router_w[layer], moe_router_b[layer])
        x = _moe_ffn_ln(xn.reshape(t, d), x_tiles, route.reshape(t, LANES), counts, layer, moe_w_in, moe_b_in,
                        moe_w_out, moe_b_out, ln_ffn_g[layer], ln_ffn_b[layer]).reshape(bsz, seq, d)
    return x
```

```python
import functools

import jax
import jax.numpy as jnp
from jax import lax
from jax.experimental import pallas as pl
from jax.experimental.pallas import tpu as pltpu

F32 = jnp.float32
BF16 = jnp.bfloat16

D_MODEL = 1024
DEPTH = 2
A_WIDTH = 512
B_WIDTH = 512
C_WIDTH = 512
D_WIDTH = 512
LRU_HEADS = 8
LRU_HEAD_DIM = A_WIDTH // LRU_HEADS
LRU_C = 8.0
CONV_WIDTH = 4
S5_GROUP_CH = 16
S5_GROUPS = B_WIDTH // S5_GROUP_CH
S5_STATE = 64
MLSTM_HEADS = 4
MLSTM_HEAD_DIM = C_WIDTH // MLSTM_HEADS
DSW_HEADS = 4
DSW_HEAD_DIM = D_WIDTH // DSW_HEADS
DSW_CONFIGS = ((128, 1), (512, 4), (2048, 16))
DSW_BLOCK = 128
ROPE_THETA = 500000.0
ROPE_DIMS = DSW_HEAD_DIM // 4
N_EXPERTS = 32
TOP_K = 4
D_EXPERT = D_MODEL
SWIGLU_LIMIT = 7.0
SWIGLU_ALPHA = 1.702
DEEPNORM_ALPHA = (2 * DEPTH) ** 0.25
LN_EPS = 1e-5

VMEM_LIMIT_BYTES = 56 * 1024 * 1024
LANES = 128
TILE_ROWS = D_MODEL // LANES
MOE_TM = 512
MOE_TK = 256
PROJ_TL = 512
ROUTE_ROWS = 16
LRU_TC = 128
S5_TC = 64
S5_HALF = 2
MLSTM_CS = 256
NEG_INF = float("-inf")


def _cparams(*sem):
    return pltpu.CompilerParams(dimension_semantics=sem, vmem_limit_bytes=VMEM_LIMIT_BYTES)


def _dot(a, b):
    return jnp.dot(a, b, preferred_element_type=F32)


def _dot_nt(a, b):
    return lax.dot_general(a, b, (((1,), (1,)), ((), ())), preferred_element_type=F32)


def _matmul_kernel(x_ref, w_ref, o_ref):
    o_ref[...] = _dot(x_ref[...].astype(BF16), w_ref[...])


def _matmul_time_major_kernel(x_ref, w_ref, o_ref, scr):
    bsz, tl, _ = x_ref.shape
    nchunk = scr.shape[0]
    for b in range(bsz):
        p = _dot(x_ref[b].astype(BF16), w_ref[...])
        for c in range(nchunk):
            scr[c, pl.ds(b, tl, stride=bsz), :] = p[:, c * LANES:(c + 1) * LANES]
    for c in range(nchunk):
        o_ref[:, c * LANES:(c + 1) * LANES] = scr[c]


def _inproj_time_major(x3, w_bf16):
    bsz, seq, k = x3.shape
    n = w_bf16.shape[1]
    tl = LRU_TC
    out = pl.pallas_call(
        _matmul_time_major_kernel,
        grid=(seq // tl,),
        in_specs=[pl.BlockSpec((bsz, tl, k), lambda i: (0, i, 0)),
                  pl.BlockSpec((k, n), lambda i: (0, 0))],
        out_specs=pl.BlockSpec((tl * bsz, n), lambda i: (i, 0)),
        out_shape=jax.ShapeDtypeStruct((seq * bsz, n), F32),
        scratch_shapes=[pltpu.VMEM((n // LANES, tl * bsz, LANES), F32)],
        compiler_params=_cparams("arbitrary"),
        name="in_proj_time_major",
    )(x3, w_bf16)
    return out.reshape(seq, bsz, n)


def _inproj(x3, w_bf16):
    bsz, seq, k = x3.shape
    n = w_bf16.shape[1]
    return pl.pallas_call(
        _matmul_kernel,
        grid=(bsz, seq // PROJ_TL),
        in_specs=[pl.BlockSpec((None, PROJ_TL, k), lambda b, i: (b, i, 0)),
                  pl.BlockSpec((k, n), lambda b, i: (0, 0))],
        out_specs=pl.BlockSpec((None, PROJ_TL, n), lambda b, i: (b, i, 0)),
        out_shape=jax.ShapeDtypeStruct((bsz, seq, n), F32),
        compiler_params=_cparams("arbitrary", "arbitrary"),
        name="in_proj",
    )(x3, w_bf16)


def _layer_norm_rows(z, g, b):
    mu = jnp.mean(z, axis=-1, keepdims=True)
    zc = z - mu
    var = jnp.mean(zc * zc, axis=-1, keepdims=True)
    return zc * lax.rsqrt(var + LN_EPS) * g + b


def _outproj_ln_router_kernel(ya_ref, yb_ref, wa_ref, wb_ref, x_ref, g_ref, b_ref, rwh_ref, rwl_ref, rb_ref,
                              tri_ref, xn_ref, xt_ref, rt_ref, cnt_ref):
    @pl.when((pl.program_id(0) == 0) & (pl.program_id(1) == 0))
    def _init():
        cnt_ref[...] = jnp.zeros_like(cnt_ref)

    mix = _dot(ya_ref[...], wa_ref[...]) + _dot(yb_ref[...], wb_ref[...])
    xn = _layer_norm_rows(DEEPNORM_ALPHA * x_ref[...] + mix, g_ref[...], b_ref[...])
    xn_ref[...] = xn
    _store_token_tiles(xt_ref, xn)
    xh = xn.astype(BF16)
    xl = (xn - xh.astype(F32)).astype(BF16)
    lg = (_dot_nt(rwh_ref[...], xh) + _dot_nt(rwh_ref[...], xl) + _dot_nt(rwl_ref[...], xh)) + rb_ref[...]
    ne, tl = lg.shape
    row = lax.broadcasted_iota(jnp.int32, (ne, tl), 0).astype(F32)
    tops, ids, hots = [], [], []
    for _ in range(TOP_K):
        m = jnp.max(lg, axis=0, keepdims=True)
        idx = jnp.min(jnp.where(lg == m, row, float(ne)), axis=0, keepdims=True)
        hot = row == idx
        lg = jnp.where(hot, NEG_INF, lg)
        tops.append(m)
        ids.append(idx)
        hots.append(hot)
    exps = [jnp.exp(m - tops[0]) for m in tops]
    denom = functools.reduce(jnp.add, exps)
    chosen = functools.reduce(jnp.add, [h.astype(F32) for h in hots])
    before = _dot(chosen.astype(BF16), tri_ref[...]) + cnt_ref[...]
    out_row = lax.broadcasted_iota(jnp.int32, rt_ref.shape, 0)
    route = jnp.zeros(rt_ref.shape, F32)
    for k in range(TOP_K):
        rank = jnp.sum(jnp.where(hots[k], before, 0.0), axis=0, keepdims=True)
        route = jnp.where(out_row == k, ids[k], route)
        route = jnp.where(out_row == TOP_K + k, rank, route)
        route = jnp.where(out_row == 2 * TOP_K + k, exps[k] / denom, route)
    rt_ref[...] = route
    cnt_ref[...] = cnt_ref[...] + jnp.sum(chosen, axis=1, keepdims=True)


def _outproj_ln_router(ya, yb, time_major, w_bf16, x3, g, b, rw, rb):
    bsz, seq, d = x3.shape
    wa, wb = w_bf16[:ya.shape[-1]], w_bf16[ya.shape[-1]:]
    ne = N_EXPERTS
    rw = rw.T
    rw_hi = rw.astype(BF16)
    rw_lo = (rw - rw_hi.astype(F32)).astype(BF16)
    tl = PROJ_TL
    tri = jnp.triu(jnp.ones((tl, tl), BF16), 1)
    if time_major:
        ya, yb = ya.reshape(seq, -1), yb.reshape(seq, -1)
        y_specs = [pl.BlockSpec((tl, wa.shape[0]), lambda bi, i: (i, bi)),
                   pl.BlockSpec((tl, wb.shape[0]), lambda bi, i: (i, bi))]
    else:
        y_specs = [pl.BlockSpec((None, tl, wa.shape[0]), lambda bi, i: (bi, i, 0)),
                   pl.BlockSpec((None, tl, wb.shape[0]), lambda bi, i: (bi, i, 0))]
    fixed = lambda bi, i: (0, 0)
    tok = lambda bi, i: (bi, i, 0)
    return pl.pallas_call(
        _outproj_ln_router_kernel,
        grid=(bsz, seq // tl),
        in_specs=y_specs + [pl.BlockSpec(wa.shape, fixed), pl.BlockSpec(wb.shape, fixed),
                            pl.BlockSpec((None, tl, d), tok),
                            pl.BlockSpec((1, d), fixed), pl.BlockSpec((1, d), fixed),
                            pl.BlockSpec((ne, d), fixed), pl.BlockSpec((ne, d), fixed),
                            pl.BlockSpec((ne, 1), fixed), pl.BlockSpec((tl, tl), fixed)],
        out_specs=[pl.BlockSpec((None, tl, d), tok),
                   pl.BlockSpec((tl * TILE_ROWS, LANES), lambda bi, i: (bi * (seq // tl) + i, 0)),
                   pl.BlockSpec((ROUTE_ROWS, tl), lambda bi, i: (0, bi * (seq // tl) + i)),
                   pl.BlockSpec((ne, 1), fixed)],
        out_shape=[jax.ShapeDtypeStruct((bsz, seq, d), F32),
                   jax.ShapeDtypeStruct((bsz * seq * TILE_ROWS, LANES), F32),
                   jax.ShapeDtypeStruct((ROUTE_ROWS, bsz * seq), F32), jax.ShapeDtypeStruct((ne, 1), F32)],
        compiler_params=_cparams("arbitrary", "arbitrary"),
        name="out_proj_ln_router",
    )(ya, yb, wa, wb, x3, g.reshape(1, d), b.reshape(1, d), rw_hi, rw_lo, rb.reshape(ne, 1), tri)


def _store_token_tiles(dst_ref, x):
    n = x.shape[0]
    for j in range(TILE_ROWS):
        dst_ref[pl.ds(j, n, stride=TILE_ROWS), :] = x[:, j * LANES:(j + 1) * LANES]


def _load_token_tiles(src_ref, n):
    return jnp.concatenate([src_ref[pl.ds(j, n, stride=TILE_ROWS), :] for j in range(TILE_ROWS)], axis=1)


def _gather_token_tiles(src_hbm, idx_ref, n, dst, sem, priorities):
    def body(g, carry):
        base = pl.multiple_of(g * (TILE_ROWS * TILE_ROWS), TILE_ROWS * TILE_ROWS)
        for u in range(TILE_ROWS):
            src_row = pl.multiple_of(idx_ref[0, g * TILE_ROWS + u], TILE_ROWS)
            pltpu.make_async_copy(src_hbm.at[pl.ds(src_row, TILE_ROWS)],
                                  dst.at[pl.ds(base + u * TILE_ROWS, TILE_ROWS)], sem).start(priority=u % priorities)
        return carry
    lax.fori_loop(0, n // TILE_ROWS, body, 0, unroll=2)


def _moe_kernel(be_ref, nreal_ref, rt_ref, rtn_ref, x_hbm, rg_ref, win_ref, bin_ref, wout_ref, bout_ref, o_ref,
                win_bf, wout_bf, xbuf, sem):
    i = pl.program_id(0)
    n_real = nreal_ref[0]
    e = be_ref[i]
    prev = be_ref[jnp.maximum(i - 1, 0)]
    tm = xbuf.shape[1] // TILE_ROWS
    slot = lax.rem(i, 2)

    @pl.when(i == 0)
    def _first_gather():
        _gather_token_tiles(x_hbm, rt_ref, tm, xbuf.at[0], sem.at[0], priorities=1)

    @pl.when(i + 1 < n_real)
    def _next_gather():
        _gather_token_tiles(x_hbm, rtn_ref, tm, xbuf.at[1 - slot], sem.at[1 - slot], priorities=1)

    @pl.when((i == 0) | (e != prev))
    def _load_expert():
        win_bf[...] = win_ref[...].astype(BF16)
        wout_bf[...] = wout_ref[...].astype(BF16)

    @pl.when((i < n_real) | (i == 0))
    def _compute():
        pltpu.make_async_copy(xbuf.at[slot], xbuf.at[slot], sem.at[slot]).wait()
        x = _load_token_tiles(xbuf.at[slot], tm).astype(BF16)
        h = _dot(x, win_bf[...]) + bin_ref[...]
        g = jnp.minimum(h[:, :D_EXPERT], SWIGLU_LIMIT)
        lin = jnp.clip(h[:, D_EXPERT:], -SWIGLU_LIMIT, SWIGLU_LIMIT)
        y = g * jax.nn.sigmoid(SWIGLU_ALPHA * g) * (lin + 1.0)
        out = _dot(y.astype(BF16), wout_bf[...]) + bout_ref[...]
        gates = rg_ref[...].T
        out = jnp.concatenate([out[j * LANES:(j + 1) * LANES] * gates[:, j:j + 1] for j in range(tm // LANES)],
                              axis=0)
        _store_token_tiles(o_ref, out)

    @pl.when((i >= n_real) & (i > 0))
    def _unused_block():
        o_ref[...] = jnp.zeros_like(o_ref)


def _moe_experts(x_tiles, row_tok, row_gate, block_expert, n_real, layer, w_in, b_in, w_out, b_out):
    d = D_MODEL
    tm = MOE_TM
    n_blocks = row_tok.shape[0]
    nl, ne, _, dh2 = w_in.shape
    rt = (row_tok * TILE_ROWS).reshape(n_blocks, 1, tm)
    rg = jnp.pad(row_gate.reshape(n_blocks, tm // LANES, LANES), ((0, 0), (0, TILE_ROWS - tm // LANES), (0, 0)))
    smem_rows = lambda imap: pl.BlockSpec((None, 1, tm), imap, memory_space=pltpu.SMEM)
    grid_spec = pltpu.PrefetchScalarGridSpec(
        num_scalar_prefetch=2,
        grid=(n_blocks,),
        in_specs=[
            smem_rows(lambda i, be, nr: (i, 0, 0)),
            smem_rows(lambda i, be, nr: (jnp.minimum(i + 1, n_blocks - 1), 0, 0)),
            pl.BlockSpec(memory_space=pl.ANY),
            pl.BlockSpec((None, TILE_ROWS, LANES), lambda i, be, nr: (i, 0, 0)),
            pl.BlockSpec((None, None, d, dh2), lambda i, be, nr: (layer, be[i], 0, 0)),
            pl.BlockSpec((None, None, 1, dh2), lambda i, be, nr: (layer, be[i], 0, 0)),
            pl.BlockSpec((None, None, D_EXPERT, d), lambda i, be, nr: (layer, be[i], 0, 0)),
            pl.BlockSpec((None, None, 1, d), lambda i, be, nr: (layer, be[i], 0, 0)),
        ],
        out_specs=pl.BlockSpec((tm * TILE_ROWS, LANES), lambda i, be, nr: (i, 0)),
        scratch_shapes=[pltpu.VMEM((d, dh2), BF16), pltpu.VMEM((D_EXPERT, d), BF16),
                        pltpu.VMEM((2, tm * TILE_ROWS, LANES), F32), pltpu.SemaphoreType.DMA((2,))],
    )
    return pl.pallas_call(
        _moe_kernel,
        grid_spec=grid_spec,
        out_shape=jax.ShapeDtypeStruct((n_blocks * tm * TILE_ROWS, LANES), F32),
        compiler_params=_cparams("arbitrary"),
        name="moe_experts",
    )(block_expert, n_real, rt, rt, x_tiles, rg, w_in,
      b_in.reshape(nl, ne, 1, dh2), w_out, b_out.reshape(nl, ne, 1, d))


def _moe_combine_kernel(dc_ref, dn_ref, x_ref, yb_hbm, g_ref, b_ref, o_ref, buf, ffn_ref, sem):
    i = pl.program_id(0)
    n = pl.num_programs(0)
    tk = x_ref.shape[0]
    rows = tk * TILE_ROWS
    slot = lax.rem(i, 2)

    @pl.when(i == 0)
    def _first_gather():
        _gather_token_tiles(yb_hbm, dc_ref, TOP_K * tk, buf.at[0], sem.at[0], priorities=2)

    @pl.when(i + 1 < n)
    def _next_gather():
        _gather_token_tiles(yb_hbm, dn_ref, TOP_K * tk, buf.at[1 - slot], sem.at[1 - slot], priorities=2)

    pltpu.make_async_copy(buf.at[slot], buf.at[slot], sem.at[slot]).wait()
    ffn = buf[slot, 0:rows]
    for k in range(1, TOP_K):
        ffn = ffn + buf[slot, k * rows:(k + 1) * rows]
    ffn_ref[...] = ffn
    o_ref[...] = _layer_norm_rows(DEEPNORM_ALPHA * x_ref[...] + _load_token_tiles(ffn_ref, tk),
                                  g_ref[...], b_ref[...])


def _moe_combine_ln(xn, dest, yb_tiles, g, b):
    t, d = xn.shape
    tk = MOE_TK
    n = t // tk
    dest_tiles = (dest * TILE_ROWS).reshape(n, tk, TOP_K).transpose(0, 2, 1).reshape(n, 1, TOP_K * tk)
    smem_rows = lambda imap: pl.BlockSpec((None, 1, TOP_K * tk), imap, memory_space=pltpu.SMEM)
    row = lambda i: (i, 0)
    fixed = lambda i: (0, 0)
    return pl.pallas_call(
        _moe_combine_kernel,
        grid=(n,),
        in_specs=[smem_rows(lambda i: (i, 0, 0)), smem_rows(lambda i: (jnp.minimum(i + 1, n - 1), 0, 0)),
                  pl.BlockSpec((tk, d), row), pl.BlockSpec(memory_space=pl.ANY),
                  pl.BlockSpec((1, d), fixed), pl.BlockSpec((1, d), fixed)],
        out_specs=pl.BlockSpec((tk, d), row),
        out_shape=jax.ShapeDtypeStruct((t, d), F32),
        scratch_shapes=[pltpu.VMEM((2, TOP_K * tk * TILE_ROWS, LANES), F32),
                        pltpu.VMEM((tk * TILE_ROWS, LANES), F32), pltpu.SemaphoreType.DMA((2,))],
        compiler_params=_cparams("arbitrary"),
        name="moe_combine_ln",
    )(dest_tiles, dest_tiles, xn, yb_tiles, g.reshape(1, d), b.reshape(1, d))


def _moe_ffn_ln(xn, x_tiles, route, counts, layer, w_in, b_in, w_out, b_out, g, b):
    t, d = xn.shape
    tm = MOE_TM
    flat_e = route[:TOP_K].astype(jnp.int32).T.reshape(-1)
    rank = route[TOP_K:2 * TOP_K].astype(jnp.int32).T.reshape(-1)
    gate = route[2 * TOP_K:3 * TOP_K].T.reshape(-1)
    counts = counts[:, 0].astype(jnp.int32)
    blocks_per_e = (counts + tm - 1) // tm
    blk_end = jnp.cumsum(blocks_per_e)
    blk_start = blk_end - blocks_per_e
    row_start = jnp.cumsum(counts) - counts
    dest = blk_start[flat_e] * tm + rank
    n_blocks = (t * TOP_K) // tm + N_EXPERTS
    block_expert = jnp.minimum(
        jnp.sum((blk_end[None, :] <= jnp.arange(n_blocks, dtype=jnp.int32)[:, None]).astype(jnp.int32), axis=1),
        N_EXPERTS - 1)
    n_real = blk_end[-1:].astype(jnp.int32)
    order = jnp.argsort(flat_e, stable=True).astype(jnp.int32)
    blk = jnp.arange(n_blocks, dtype=jnp.int32)
    rank_row = ((blk - blk_start[block_expert]) * tm)[:, None] + jnp.arange(tm, dtype=jnp.int32)[None, :]
    valid = rank_row < counts[block_expert][:, None]
    pair = order[jnp.clip(row_start[block_expert][:, None] + rank_row, 0, t * TOP_K - 1)]
    row_tok = jnp.where(valid, pair // TOP_K, 0)
    row_gate = jnp.where(valid, gate[pair], 0.0)
    yb_tiles = _moe_experts(x_tiles, row_tok, row_gate, block_expert, n_real, layer, w_in, b_in, w_out, b_out)
    return _moe_combine_ln(xn, dest, yb_tiles, g, b)


def _softplus(x):
    return jnp.maximum(x, 0.0) + jnp.log(1.0 + jnp.exp(-jnp.abs(x)))


def _gelu_tanh(x):
    return 0.5 * x * (1.0 + jnp.tanh(0.7978845608028654 * (x + 0.044715 * (x * x * x))))


def _lru_kernel(xa_ref, ga_ref, cw_ref, cb_ref, wr_ref, br_ref, wi_ref, bi_ref, lam_ref, o_ref,
                tail_ref, h_ref, a_ref, u_ref):
    tc, bsz, w = xa_ref.shape

    @pl.when(pl.program_id(0) == 0)
    def _init():
        tail_ref[...] = jnp.zeros_like(tail_ref)
        h_ref[...] = jnp.zeros_like(h_ref)

    xa = xa_ref[...]
    ext = jnp.concatenate([tail_ref[...], xa], axis=0)
    xc = cb_ref[...] + ext[0:tc] * cw_ref[0]
    for tap in range(1, CONV_WIDTH):
        xc = xc + ext[tap:tap + tc] * cw_ref[tap]
    tail_ref[...] = xa[tc - (CONV_WIDTH - 1):]
    x2 = xc.reshape(tc * bsz, w)
    xb = x2.astype(BF16)
    r = jax.nn.sigmoid(_dot(xb, wr_ref[...]) + br_ref[...])
    ig = jax.nn.sigmoid(_dot(xb, wi_ref[...]) + bi_ref[...])
    log_a = (-LRU_C) * r * _softplus(-lam_ref[...])
    a_ref[...] = jnp.exp(log_a).reshape(tc, bsz, w)
    u_ref[...] = (jnp.sqrt(1.0 - jnp.exp(2.0 * log_a)) * (ig * x2)).reshape(tc, bsz, w)

    def step(t, h):
        h = a_ref[t] * h + u_ref[t]
        u_ref[t] = h
        return h

    h_ref[...] = lax.fori_loop(0, tc, step, h_ref[...], unroll=8)
    o_ref[...] = (u_ref[...] * _gelu_tanh(ga_ref[...])).astype(o_ref.dtype)


def _block_diag(w):
    h, i, j = w.shape
    return jnp.einsum('hij,hk->hikj', w, jnp.eye(h, dtype=w.dtype)).reshape(h * i, h * j)


def _rg_lru(proj, conv_w, conv_b, gate_r_w, gate_r_b, gate_i_w, gate_i_b, lru_lambda):
    seq, bsz, _ = proj.shape
    w = A_WIDTH
    tc = LRU_TC
    fixed2 = lambda i: (0, 0)
    fixed3 = lambda i: (0, 0, 0)
    return pl.pallas_call(
        _lru_kernel,
        grid=(seq // tc,),
        in_specs=[pl.BlockSpec((tc, bsz, w), lambda i: (i, 0, 0)),
                  pl.BlockSpec((tc, bsz, w), lambda i: (i, 0, 1)),
                  pl.BlockSpec((CONV_WIDTH, 1, w), fixed3), pl.BlockSpec((1, w), fixed2),
                  pl.BlockSpec((w, w), fixed2), pl.BlockSpec((1, w), fixed2),
                  pl.BlockSpec((w, w), fixed2), pl.BlockSpec((1, w), fixed2),
                  pl.BlockSpec((1, w), fixed2)],
        out_specs=pl.BlockSpec((tc, bsz, w), lambda i: (i, 0, 0)),
        out_shape=jax.ShapeDtypeStruct((seq, bsz, w), BF16),
        scratch_shapes=[pltpu.VMEM((CONV_WIDTH - 1, bsz, w), F32), pltpu.VMEM((bsz, w), F32),
                        pltpu.VMEM((tc, bsz, w), F32), pltpu.VMEM((tc, bsz, w), F32)],
        compiler_params=_cparams("arbitrary"),
        name="rg_lru",
    )(proj, proj, conv_w.reshape(CONV_WIDTH, 1, w), conv_b.reshape(1, w),
      _block_diag(gate_r_w).astype(BF16), gate_r_b.reshape(1, w),
      _block_diag(gate_i_w).astype(BF16), gate_i_b.reshape(1, w), lru_lambda.reshape(1, w))


def _s5_kernel(u_ref, wbr_ref, wbi_ref, ar_ref, ai_ref, ccr_ref, cci_ref, d_ref, gw_ref, gb_ref, o_ref,
               xr_ref, xi_ref, sr_ref, si_ref):
    tc, bsz, w = u_ref.shape
    nstate = xr_ref.shape[-1]
    sh = nstate // S5_HALF
    wh = w // S5_HALF

    @pl.when(pl.program_id(0) == 0)
    def _init():
        sr_ref[...] = jnp.zeros_like(sr_ref)
        si_ref[...] = jnp.zeros_like(si_ref)

    u2 = u_ref[...].reshape(tc * bsz, w)
    ub = u2.astype(BF16)
    for hf in range(S5_HALF):
        uh = ub[:, hf * wh:(hf + 1) * wh]
        xr_ref[:, :, hf * sh:(hf + 1) * sh] = _dot(uh, wbr_ref[hf]).reshape(tc, bsz, sh)
        xi_ref[:, :, hf * sh:(hf + 1) * sh] = _dot(uh, wbi_ref[hf]).reshape(tc, bsz, sh)

    for hf in range(S5_HALF):
        lo, hi = hf * sh, (hf + 1) * sh
        ar = jnp.broadcast_to(ar_ref[:, lo:hi], (bsz, sh))
        ai = jnp.broadcast_to(ai_ref[:, lo:hi], (bsz, sh))

        def step(t, carry, lo=lo, hi=hi, ar=ar, ai=ai):
            xr, xi = carry
            nxr = ar * xr - ai * xi + xr_ref[t, :, lo:hi]
            nxi = ar * xi + ai * xr + xi_ref[t, :, lo:hi]
            xr_ref[t, :, lo:hi] = nxr
            xi_ref[t, :, lo:hi] = nxi
            return nxr, nxi

        xr, xi = lax.fori_loop(0, tc, step, (sr_ref[:, lo:hi], si_ref[:, lo:hi]), unroll=4)
        sr_ref[:, lo:hi] = xr
        si_ref[:, lo:hi] = xi

    ys = []
    for hf in range(S5_HALF):
        lo, hi = hf * sh, (hf + 1) * sh
        xrh = xr_ref[:, :, lo:hi].reshape(tc * bsz, sh).astype(BF16)
        xih = xi_ref[:, :, lo:hi].reshape(tc * bsz, sh).astype(BF16)
        ys.append(_dot(xrh, ccr_ref[hf]) - _dot(xih, cci_ref[hf]))
    y = jnp.concatenate(ys, axis=1) + d_ref[...] * u2
    s = _gelu_tanh(y)
    yb = s * jax.nn.sigmoid(_dot(s.astype(BF16), gw_ref[...]) + gb_ref[...])
    o_ref[...] = yb.reshape(tc, bsz, w).astype(o_ref.dtype)


def _s5_glu(proj, lam_re, lam_im, log_step, b_re, b_im, c_re, c_im, d_skip, glu_w, glu_b):
    seq, bsz, _ = proj.shape
    w = B_WIDTH
    tc = S5_TC
    nstate = S5_GROUPS * S5_STATE
    gh = S5_GROUPS // S5_HALF
    lr = jnp.minimum(lam_re, -1e-4)
    li = lam_im
    step = jnp.exp(log_step)[:, None]
    mag = jnp.exp(lr * step)
    ar = mag * jnp.cos(li * step)
    ai = mag * jnp.sin(li * step)
    inv = 1.0 / (lr * lr + li * li)
    zr = ((ar - 1.0) * lr + ai * li) * inv
    zi = (ai * lr - (ar - 1.0) * li) * inv
    bbr = zr[..., None] * b_re - zi[..., None] * b_im
    bbi = zr[..., None] * b_im + zi[..., None] * b_re
    eye = jnp.eye(gh, dtype=F32)

    def expand_in(bb):
        bb = bb.reshape(S5_HALF, gh, S5_STATE, S5_GROUP_CH)
        return jnp.einsum('fgph,gk->fghkp', bb, eye).reshape(
            S5_HALF, gh * S5_GROUP_CH, gh * S5_STATE).astype(BF16)

    def expand_out(c):
        c = c.reshape(S5_HALF, gh, S5_GROUP_CH, S5_STATE)
        return jnp.einsum('fghp,gk->fgpkh', c, eye).reshape(
            S5_HALF, gh * S5_STATE, gh * S5_GROUP_CH).astype(BF16)

    fixed2 = lambda i: (0, 0)
    fixed3 = lambda i: (0, 0, 0)
    return pl.pallas_call(
        _s5_kernel,
        grid=(seq // tc,),
        in_specs=[pl.BlockSpec((tc, bsz, w), lambda i: (i, 0, 2)),
                  pl.BlockSpec((S5_HALF, w // S5_HALF, nstate // S5_HALF), fixed3),
                  pl.BlockSpec((S5_HALF, w // S5_HALF, nstate // S5_HALF), fixed3),
                  pl.BlockSpec((1, nstate), fixed2), pl.BlockSpec((1, nstate), fixed2),
                  pl.BlockSpec((S5_HALF, nstate // S5_HALF, w // S5_HALF), fixed3),
                  pl.BlockSpec((S5_HALF, nstate // S5_HALF, w // S5_HALF), fixed3),
                  pl.BlockSpec((1, w), fixed2), pl.BlockSpec((w, w), fixed2), pl.BlockSpec((1, w), fixed2)],
        out_specs=pl.BlockSpec((tc, bsz, w), lambda i: (i, 0, 0)),
        out_shape=jax.ShapeDtypeStruct((seq, bsz, w), BF16),
        scratch_shapes=[pltpu.VMEM((tc, bsz, nstate), F32), pltpu.VMEM((tc, bsz, nstate), F32),
                        pltpu.VMEM((bsz, nstate), F32), pltpu.VMEM((bsz, nstate), F32)],
        compiler_params=_cparams("arbitrary"),
        name="s5_glu",
    )(proj, expand_in(bbr), expand_in(bbi), ar.reshape(1, nstate), ai.reshape(1, nstate),
      expand_out(c_re), expand_out(c_im), d_skip.reshape(1, w), glu_w.astype(BF16), glu_b.reshape(1, w))


def _cd_prep_kernel(xc_ref, q_ref, k_ref, pos_ref, cw_ref, cb_ref, wq_ref, wk_ref, wv_ref, wif_ref,
                    bif_ref, invf_ref, mq_ref, mk_ref, mv_ref, gt_ref, qd_ref, kd_ref, tail_ref):
    tl, w = xc_ref.shape
    pad = tail_ref.shape[0]

    @pl.when(pl.program_id(1) == 0)
    def _init():
        tail_ref[...] = jnp.zeros_like(tail_ref)

    xc = xc_ref[...]
    ext = jnp.concatenate([tail_ref[...], xc], axis=0)
    conv = cb_ref[...]
    for tap in range(CONV_WIDTH):
        off = pad - (CONV_WIDTH - 1) + tap
        conv = conv + ext[off:off + tl] * cw_ref[tap]
    tail_ref[...] = xc[tl - pad:]
    xconv = (conv * jax.nn.sigmoid(conv)).astype(BF16)
    q = _dot(xconv, wq_ref[...]).astype(BF16)
    k = (_dot(xconv, wk_ref[...]) * (MLSTM_HEAD_DIM ** -0.5)).astype(BF16)
    v = _dot(xc.astype(BF16), wv_ref[...]).astype(BF16)
    mq_ref[...] = q
    mk_ref[...] = k
    mv_ref[...] = v
    gt_ref[...] = (_dot(q, wif_ref[0]) + _dot(k, wif_ref[1]) + _dot(v, wif_ref[2]) + bif_ref[...])

    ang = pos_ref[...].astype(F32) * invf_ref[...]
    cos = jnp.cos(ang)
    sin = jnp.sin(ang)
    lane = lax.broadcasted_iota(jnp.int32, (1, LANES), 1)
    half = ROPE_DIMS // 2
    sin_lo = jnp.where(lane < half, -sin, 0.0)
    sin_hi = jnp.where((lane >= half) & (lane < ROPE_DIMS), sin, 0.0)
    for src, dst in ((q_ref, qd_ref), (k_ref, kd_ref)):
        for h in range(DSW_HEADS):
            sl = slice(h * DSW_HEAD_DIM, (h + 1) * DSW_HEAD_DIM)
            t = src[:, sl]
            rot = (t * cos + pltpu.roll(t, LANES - half, axis=1) * sin_lo
                   + pltpu.roll(t, half, axis=1) * sin_hi)
            dst[:, sl] = rot


def _cd_prep(proj, positions, conv_w, conv_b, w_q, w_k, w_v, w_if, b_if):
    bsz, seq, _ = proj.shape
    w = C_WIDTH
    tl = PROJ_TL
    half = ROPE_DIMS // 2
    inv_freq = ROPE_THETA ** (-jnp.arange(half, dtype=F32) / half)
    invf = jnp.zeros((1, LANES), F32).at[0, :half].set(inv_freq).at[0, half:ROPE_DIMS].set(inv_freq)
    wif = jnp.zeros((3, w, LANES), F32).at[:, :, :2 * MLSTM_HEADS].set(w_if.reshape(3, w, 2 * MLSTM_HEADS))
    bif = jnp.zeros((1, LANES), F32).at[0, :2 * MLSTM_HEADS].set(b_if)
    col = lambda c: (lambda b, i: (b, i, c))
    fixed2 = lambda b, i: (0, 0)
    fixed3 = lambda b, i: (0, 0, 0)
    tok_bf = jax.ShapeDtypeStruct((bsz, seq, w), BF16)
    return pl.pallas_call(
        _cd_prep_kernel,
        grid=(bsz, seq // tl),
        in_specs=[pl.BlockSpec((None, tl, w), col(0)), pl.BlockSpec((None, tl, w), col(2)),
                  pl.BlockSpec((None, tl, w), col(3)), pl.BlockSpec((None, tl, 1), col(0)),
                  pl.BlockSpec((CONV_WIDTH, 1, w), fixed3), pl.BlockSpec((1, w), fixed2),
                  pl.BlockSpec((w, w), fixed2), pl.BlockSpec((w, w), fixed2), pl.BlockSpec((w, w), fixed2),
                  pl.BlockSpec((3, w, LANES), fixed3), pl.BlockSpec((1, LANES), fixed2),
                  pl.BlockSpec((1, LANES), fixed2)],
        out_specs=[pl.BlockSpec((None, tl, w), col(0))] * 3 + [pl.BlockSpec((None, tl, LANES), col(0))]
                  + [pl.BlockSpec((None, tl, w), col(0))] * 2,
        out_shape=[tok_bf, tok_bf, tok_bf, jax.ShapeDtypeStruct((bsz, seq, LANES), F32),
                   jax.ShapeDtypeStruct((bsz, seq, w), F32), jax.ShapeDtypeStruct((bsz, seq, w), F32)],
        scratch_shapes=[pltpu.VMEM((8, w), F32)],
        compiler_params=_cparams("arbitrary", "arbitrary"),
        name="cd_prep",
    )(proj, proj, proj, positions.reshape(bsz, seq, 1), conv_w.reshape(CONV_WIDTH, 1, w),
      conv_b.reshape(1, w), _block_diag(w_q).astype(BF16), _block_diag(w_k).astype(BF16),
      _block_diag(w_v).astype(BF16), wif.astype(BF16), bif, invf)


def _cumsum_rows(x):
    n = x.shape[0]
    row = lax.broadcasted_iota(jnp.int32, x.shape, 0)
    shift = 1
    while shift < n:
        x = x + jnp.where(row >= shift, pltpu.roll(x, shift, axis=0), 0.0)
        shift *= 2
    return x


def _mlstm_kernel(q_ref, k_ref, v_ref, g_ref, op_ref, o_ref, c_ref, n_ref, m_ref):
    cs = q_ref.shape[0]
    dh = MLSTM_HEAD_DIM

    @pl.when(pl.program_id(1) == 0)
    def _init():
        c_ref[...] = jnp.zeros_like(c_ref)
        n_ref[...] = jnp.zeros_like(n_ref)
        m_ref[...] = jnp.full_like(m_ref, NEG_INF)

    gates = g_ref[...]
    cum = _cumsum_rows(jax.nn.log_sigmoid(gates))
    gates_t = gates.T
    cum_t = cum.T
    causal = (lax.broadcasted_iota(jnp.int32, (cs, cs), 0) >= lax.broadcasted_iota(jnp.int32, (cs, cs), 1))
    for h in range(MLSTM_HEADS):
        sl = slice(h * dh, (h + 1) * dh)
        fcol = MLSTM_HEADS + h
        li_c, li_r = gates[:, h:h + 1], gates_t[h:h + 1, :]
        cum_c, cum_r = cum[:, fcol:fcol + 1], cum_t[fcol:fcol + 1, :]
        m_prev = m_ref[h][:, 0:1]
        q, k, v = q_ref[:, sl], k_ref[:, sl], v_ref[:, sl]
        log_d = jnp.where(causal, cum_c - cum_r + li_r, NEG_INF)
        log_inter = cum_c + m_prev
        m_s = jnp.maximum(jnp.max(log_d, axis=1, keepdims=True), log_inter)
        s = _dot_nt(q, k) * jnp.exp(log_d - m_s)
        inter = jnp.exp(log_inter - m_s)
        c_prev = c_ref[h]
        n_prev = n_ref[h]
        num = _dot(s.astype(BF16), v) + inter * _dot(q, c_prev.astype(BF16))
        den = (jnp.sum(s, axis=1, keepdims=True)
               + inter * jnp.sum(q.astype(F32) * n_prev, axis=1, keepdims=True))
        hh = num / jnp.maximum(jnp.abs(den), jnp.exp(-m_s))
        o_ref[:, sl] = (jax.nn.sigmoid(op_ref[:, sl]) * hh).astype(o_ref.dtype)
        chunk_f = cum_c[cs - 1:cs, :]
        to_end = chunk_f - cum_c + li_c
        m_c = jnp.max(to_end, axis=0, keepdims=True)
        m_new = jnp.maximum(chunk_f + m_prev, m_c)
        s_old = jnp.exp(chunk_f + m_prev - m_new)
        s_new = jnp.exp(m_c - m_new)
        kw = k.astype(F32) * jnp.exp(to_end - m_c)
        c_ref[h] = s_old * c_prev + s_new * _dot(kw.T.astype(BF16), v)
        n_ref[h] = s_old * n_prev + s_new * jnp.sum(kw, axis=0, keepdims=True)
        m_ref[h] = jnp.broadcast_to(m_new, (1, LANES))


def _mlstm(mq, mk, mv, gates, proj):
    bsz, seq, w = mq.shape
    cs = MLSTM_CS
    tok = lambda b, i: (b, i, 0)
    return pl.pallas_call(
        _mlstm_kernel,
        grid=(bsz, seq // cs),
        in_specs=[pl.BlockSpec((None, cs, w), tok)] * 3
                 + [pl.BlockSpec((None, cs, LANES), tok), pl.BlockSpec((None, cs, w), lambda b, i: (b, i, 1))],
        out_specs=pl.BlockSpec((None, cs, w), tok),
        out_shape=jax.ShapeDtypeStruct((bsz, seq, w), BF16),
        scratch_shapes=[pltpu.VMEM((MLSTM_HEADS, MLSTM_HEAD_DIM, MLSTM_HEAD_DIM), F32),
                        pltpu.VMEM((MLSTM_HEADS, 1, MLSTM_HEAD_DIM), F32),
                        pltpu.VMEM((MLSTM_HEADS, 1, LANES), F32)],
        compiler_params=_cparams("arbitrary", "arbitrary"),
        name="mlstm",
    )(mq, mk, mv, gates, proj)


def _dsw_kernel(q_ref, k_ref, v_ref, y_ref, o_scr, lse_scr):
    seq, dh = q_ref.shape
    blk = DSW_BLOCK
    qi = lax.broadcasted_iota(jnp.int32, (blk, 2 * blk), 0)
    ki = lax.broadcasted_iota(jnp.int32, (blk, 2 * blk), 1)
    dist = qi + blk - ki
    scale = dh ** -0.5
    units = seq // blk
    for g, (window, dil) in enumerate(DSW_CONFIGS):
        nsub = seq // dil // blk
        band = (dist >= 0) & (dist <= window // dil)

        def unit(u, carry, g=g, dil=dil, nsub=nsub, band=band):
            r = u // nsub
            sb = u % nsub
            cur = pl.ds(r + sb * (blk * dil), blk, stride=dil)
            prev = pl.ds(r + jnp.maximum(sb - 1, 0) * (blk * dil), blk, stride=dil)
            kk = jnp.concatenate([k_ref[prev, :], k_ref[cur, :]], axis=0).astype(BF16)
            vv = jnp.concatenate([v_ref[prev, :], v_ref[cur, :]], axis=0).astype(BF16)
            s = _dot_nt(q_ref[cur, :].astype(BF16), kk) * scale
            s = jnp.where(band & ((ki >= blk) | (sb > 0)), s, NEG_INF)
            m = jnp.max(s, axis=1, keepdims=True)
            p = jnp.exp(s - m)
            l = jnp.sum(p, axis=1, keepdims=True)
            o_scr[g, cur, :] = _dot(p.astype(BF16), vv) / l
            lse_scr[g, cur, :] = jnp.broadcast_to(m + jnp.log(l), (blk, dh))
            return carry

        lax.fori_loop(0, units, unit, 0, unroll=8)

    lses = [lse_scr[g] for g in range(len(DSW_CONFIGS))]
    mx = functools.reduce(jnp.maximum, lses)
    ws = [jnp.exp(l - mx) for l in lses]
    acc = sum(ws[g] * o_scr[g] for g in range(len(DSW_CONFIGS)))
    y_ref[...] = (acc / sum(ws)).astype(y_ref.dtype)


def _dsw_attention(qd, kd, proj):
    bsz, seq, w = qd.shape
    dh = DSW_HEAD_DIM
    v_col = (proj.shape[-1] - w) // dh
    head = lambda b, h: (b, 0, h)
    return pl.pallas_call(
        _dsw_kernel,
        grid=(bsz, DSW_HEADS),
        in_specs=[pl.BlockSpec((None, seq, dh), head), pl.BlockSpec((None, seq, dh), head),
                  pl.BlockSpec((None, seq, dh), lambda b, h: (b, 0, v_col + h))],
        out_specs=pl.BlockSpec((None, seq, dh), head),
        out_shape=jax.ShapeDtypeStruct((bsz, seq, w), BF16),
        scratch_shapes=[pltpu.VMEM((len(DSW_CONFIGS), seq, dh), F32),
                        pltpu.VMEM((len(DSW_CONFIGS), seq, dh), F32)],
        compiler_params=_cparams("arbitrary", "arbitrary"),
        name="dsw_attention",
    )(qd, kd, proj)


def kernel(x, positions, ab_w_in, ab_conv_w, ab_conv_b, ab_gate_r_w, ab_gate_r_b, ab_gate_i_w, ab_gate_i_b, ab_lru_lambda, ab_s5_lambda_re, ab_s5_lambda_im, ab_s5_log_step, ab_s5_b_re, ab_s5_b_im, ab_s5_c_re, ab_s5_c_im, ab_s5_d, ab_glu_w, ab_glu_b, ab_w_out, cd_w_in, cd_conv_w, cd_conv_b, cd_w_q, cd_w_k, cd_w_v, cd_w_if, cd_b_if, cd_w_out, ln_mix_g, ln_mix_b, ln_ffn_g, ln_ffn_b, moe_router_w, moe_router_b, moe_w_in, moe_b_in, moe_w_out, moe_b_out):
    bsz, seq, d = x.shape
    t = bsz * seq
    for layer in range(DEPTH):
        j = layer // 2
        if layer % 2 == 0:
            proj = _inproj_time_major(x, ab_w_in[j].astype(BF16))
            ya = _rg_lru(proj, ab_conv_w[j], ab_conv_b[j], ab_gate_r_w[j], ab_gate_r_b[j],
                         ab_gate_i_w[j], ab_gate_i_b[j], ab_lru_lambda[j])
            yb = _s5_glu(proj, ab_s5_lambda_re[j], ab_s5_lambda_im[j], ab_s5_log_step[j], ab_s5_b_re[j],
                         ab_s5_b_im[j], ab_s5_c_re[j], ab_s5_c_im[j], ab_s5_d[j], ab_glu_w[j], ab_glu_b[j])
            time_major, w_out = True, ab_w_out[j]
        else:
            proj = _inproj(x, cd_w_in[j].astype(BF16))
            mq, mk, mv, gates, qd, kd = _cd_prep(proj, positions, cd_conv_w[j], cd_conv_b[j], cd_w_q[j],
                                                 cd_w_k[j], cd_w_v[j], cd_w_if[j], cd_b_if[j])
            ya = _mlstm(mq, mk, mv, gates, proj)
            yb = _dsw_attention(qd, kd, proj)
            time_major, w_out = False, cd_w_out[j]
        xn, x_tiles, route, counts = _outproj_ln_router(
            ya, yb, time_major, w_out.astype(BF16), x, ln_mix_g[layer], ln_mix_b[layer],
            moe_router_w[layer], moe_router_b[layer])
        x = _moe_ffn_ln(xn.reshape(t, d), x_tiles, route, counts, layer, moe_w_in, moe_b_in,
                        moe_w_out, moe_b_out, ln_ffn_g[layer], ln_ffn_b[layer]).reshape(bsz, seq, d)
    return x
```

```python
import functools

import jax
import jax.numpy as jnp
from jax import lax
from jax.experimental import pallas as pl
from jax.experimental.pallas import tpu as pltpu

F32 = jnp.float32
BF16 = jnp.bfloat16

D_MODEL = 1024
DEPTH = 2
A_WIDTH = 512
B_WIDTH = 512
C_WIDTH = 512
D_WIDTH = 512
LRU_HEADS = 8
LRU_HEAD_DIM = A_WIDTH // LRU_HEADS
LRU_C = 8.0
CONV_WIDTH = 4
S5_GROUP_CH = 16
S5_GROUPS = B_WIDTH // S5_GROUP_CH
S5_STATE = 64
MLSTM_HEADS = 4
MLSTM_HEAD_DIM = C_WIDTH // MLSTM_HEADS
DSW_HEADS = 4
DSW_HEAD_DIM = D_WIDTH // DSW_HEADS
DSW_CONFIGS = ((128, 1), (512, 4), (2048, 16))
DSW_BLOCK = 128
ROPE_THETA = 500000.0
ROPE_DIMS = DSW_HEAD_DIM // 4
N_EXPERTS = 32
TOP_K = 4
D_EXPERT = D_MODEL
SWIGLU_LIMIT = 7.0
SWIGLU_ALPHA = 1.702
DEEPNORM_ALPHA = (2 * DEPTH) ** 0.25
LN_EPS = 1e-5

VMEM_LIMIT_BYTES = 56 * 1024 * 1024
LANES = 128
TILE_ROWS = D_MODEL // LANES
MOE_TM = 512
MOE_TK = 512
PROJ_TL = 512
ROUTE_ROWS = 16
LRU_TC = 128
S5_TC = 128
S5_HALF = 2
MLSTM_CS = 256
NEG_INF = float("-inf")


def _cparams(*sem):
    return pltpu.CompilerParams(dimension_semantics=sem, vmem_limit_bytes=VMEM_LIMIT_BYTES)


def _dot(a, b):
    return jnp.dot(a, b, preferred_element_type=F32)


def _dot_nt(a, b):
    return lax.dot_general(a, b, (((1,), (1,)), ((), ())), preferred_element_type=F32)


def _matmul_kernel(x_ref, w_ref, o_ref):
    o_ref[...] = _dot(x_ref[...].astype(BF16), w_ref[...])


def _matmul_time_major_kernel(x_ref, w_ref, o_ref, scr):
    bsz, tl, _ = x_ref.shape
    nchunk = scr.shape[0]
    for b in range(bsz):
        p = _dot(x_ref[b].astype(BF16), w_ref[...])
        for c in range(nchunk):
            scr[c, pl.ds(b, tl, stride=bsz), :] = p[:, c * LANES:(c + 1) * LANES]
    for c in range(nchunk):
        o_ref[:, c * LANES:(c + 1) * LANES] = scr[c]


def _inproj_time_major(x3, w_bf16):
    bsz, seq, k = x3.shape
    n = w_bf16.shape[1]
    tl = LRU_TC
    out = pl.pallas_call(
        _matmul_time_major_kernel,
        grid=(seq // tl,),
        in_specs=[pl.BlockSpec((bsz, tl, k), lambda i: (0, i, 0)),
                  pl.BlockSpec((k, n), lambda i: (0, 0))],
        out_specs=pl.BlockSpec((tl * bsz, n), lambda i: (i, 0)),
        out_shape=jax.ShapeDtypeStruct((seq * bsz, n), F32),
        scratch_shapes=[pltpu.VMEM((n // LANES, tl * bsz, LANES), F32)],
        compiler_params=_cparams("arbitrary"),
        name="in_proj_time_major",
    )(x3, w_bf16)
    return out.reshape(seq, bsz, n)


def _inproj(x3, w_bf16):
    bsz, seq, k = x3.shape
    n = w_bf16.shape[1]
    return pl.pallas_call(
        _matmul_kernel,
        grid=(bsz, seq // PROJ_TL),
        in_specs=[pl.BlockSpec((None, PROJ_TL, k), lambda b, i: (b, i, 0)),
                  pl.BlockSpec((k, n), lambda b, i: (0, 0))],
        out_specs=pl.BlockSpec((None, PROJ_TL, n), lambda b, i: (b, i, 0)),
        out_shape=jax.ShapeDtypeStruct((bsz, seq, n), F32),
        compiler_params=_cparams("arbitrary", "arbitrary"),
        name="in_proj",
    )(x3, w_bf16)


def _layer_norm_rows(z, g, b):
    mu = jnp.mean(z, axis=-1, keepdims=True)
    zc = z - mu
    var = jnp.mean(zc * zc, axis=-1, keepdims=True)
    return zc * lax.rsqrt(var + LN_EPS) * g + b


def _outproj_ln_router_kernel(ya_ref, yb_ref, wa_ref, wb_ref, x_ref, g_ref, b_ref, rwh_ref, rwl_ref, rb_ref,
                              tri_ref, xn_ref, xt_ref, rt_ref, cnt_ref):
    @pl.when((pl.program_id(0) == 0) & (pl.program_id(1) == 0))
    def _init():
        cnt_ref[...] = jnp.zeros_like(cnt_ref)

    mix = _dot(ya_ref[...], wa_ref[...]) + _dot(yb_ref[...], wb_ref[...])
    xn = _layer_norm_rows(DEEPNORM_ALPHA * x_ref[...] + mix, g_ref[...], b_ref[...])
    xn_ref[...] = xn
    _store_token_tiles(xt_ref, xn)
    xh = xn.astype(BF16)
    xl = (xn - xh.astype(F32)).astype(BF16)
    lg = (_dot_nt(rwh_ref[...], xh) + _dot_nt(rwh_ref[...], xl) + _dot_nt(rwl_ref[...], xh)) + rb_ref[...]
    ne, tl = lg.shape
    row = lax.broadcasted_iota(jnp.int32, (ne, tl), 0).astype(F32)
    tops, ids, hots = [], [], []
    for _ in range(TOP_K):
        m = jnp.max(lg, axis=0, keepdims=True)
        idx = jnp.min(jnp.where(lg == m, row, float(ne)), axis=0, keepdims=True)
        hot = row == idx
        lg = jnp.where(hot, NEG_INF, lg)
        tops.append(m)
        ids.append(idx)
        hots.append(hot)
    exps = [jnp.exp(m - tops[0]) for m in tops]
    denom = functools.reduce(jnp.add, exps)
    chosen = functools.reduce(jnp.add, [h.astype(F32) for h in hots])
    before = _dot(chosen.astype(BF16), tri_ref[...]) + cnt_ref[...]
    out_row = lax.broadcasted_iota(jnp.int32, rt_ref.shape, 0)
    route = jnp.zeros(rt_ref.shape, F32)
    for k in range(TOP_K):
        rank = jnp.sum(jnp.where(hots[k], before, 0.0), axis=0, keepdims=True)
        route = jnp.where(out_row == k, ids[k], route)
        route = jnp.where(out_row == TOP_K + k, rank, route)
        route = jnp.where(out_row == 2 * TOP_K + k, exps[k] / denom, route)
    rt_ref[...] = route
    cnt_ref[...] = cnt_ref[...] + jnp.sum(chosen, axis=1, keepdims=True)


def _outproj_ln_router(ya, yb, time_major, w_bf16, x3, g, b, rw, rb):
    bsz, seq, d = x3.shape
    wa, wb = w_bf16[:ya.shape[-1]], w_bf16[ya.shape[-1]:]
    ne = N_EXPERTS
    rw = rw.T
    rw_hi = rw.astype(BF16)
    rw_lo = (rw - rw_hi.astype(F32)).astype(BF16)
    tl = PROJ_TL
    tri = jnp.triu(jnp.ones((tl, tl), BF16), 1)
    if time_major:
        ya, yb = ya.reshape(seq, -1), yb.reshape(seq, -1)
        y_specs = [pl.BlockSpec((tl, wa.shape[0]), lambda bi, i: (i, bi)),
                   pl.BlockSpec((tl, wb.shape[0]), lambda bi, i: (i, bi))]
    else:
        y_specs = [pl.BlockSpec((None, tl, wa.shape[0]), lambda bi, i: (bi, i, 0)),
                   pl.BlockSpec((None, tl, wb.shape[0]), lambda bi, i: (bi, i, 0))]
    fixed = lambda bi, i: (0, 0)
    tok = lambda bi, i: (bi, i, 0)
    return pl.pallas_call(
        _outproj_ln_router_kernel,
        grid=(bsz, seq // tl),
        in_specs=y_specs + [pl.BlockSpec(wa.shape, fixed), pl.BlockSpec(wb.shape, fixed),
                            pl.BlockSpec((None, tl, d), tok),
                            pl.BlockSpec((1, d), fixed), pl.BlockSpec((1, d), fixed),
                            pl.BlockSpec((ne, d), fixed), pl.BlockSpec((ne, d), fixed),
                            pl.BlockSpec((ne, 1), fixed), pl.BlockSpec((tl, tl), fixed)],
        out_specs=[pl.BlockSpec((None, tl, d), tok),
                   pl.BlockSpec((tl * TILE_ROWS, LANES), lambda bi, i: (bi * (seq // tl) + i, 0)),
                   pl.BlockSpec((ROUTE_ROWS, tl), lambda bi, i: (0, bi * (seq // tl) + i)),
                   pl.BlockSpec((ne, 1), fixed)],
        out_shape=[jax.ShapeDtypeStruct((bsz, seq, d), F32),
                   jax.ShapeDtypeStruct((bsz * seq * TILE_ROWS, LANES), F32),
                   jax.ShapeDtypeStruct((ROUTE_ROWS, bsz * seq), F32), jax.ShapeDtypeStruct((ne, 1), F32)],
        compiler_params=_cparams("arbitrary", "arbitrary"),
        name="out_proj_ln_router",
    )(ya, yb, wa, wb, x3, g.reshape(1, d), b.reshape(1, d), rw_hi, rw_lo, rb.reshape(ne, 1), tri)


def _store_token_tiles(dst_ref, x):
    n = x.shape[0]
    for j in range(TILE_ROWS):
        dst_ref[pl.ds(j, n, stride=TILE_ROWS), :] = x[:, j * LANES:(j + 1) * LANES]


def _load_token_tiles(src_ref, n):
    return jnp.concatenate([src_ref[pl.ds(j, n, stride=TILE_ROWS), :] for j in range(TILE_ROWS)], axis=1)


def _gather_token_tiles(src_hbm, idx_ref, n, dst, sem, priorities):
    def body(g, carry):
        base = pl.multiple_of(g * (TILE_ROWS * TILE_ROWS), TILE_ROWS * TILE_ROWS)
        for u in range(TILE_ROWS):
            src_row = pl.multiple_of(idx_ref[0, g * TILE_ROWS + u], TILE_ROWS)
            pltpu.make_async_copy(src_hbm.at[pl.ds(src_row, TILE_ROWS)],
                                  dst.at[pl.ds(base + u * TILE_ROWS, TILE_ROWS)], sem).start(priority=u % priorities)
        return carry
    lax.fori_loop(0, n // TILE_ROWS, body, 0, unroll=2)


def _moe_kernel(be_ref, nreal_ref, rt_ref, rtn_ref, x_hbm, rg_ref, win_ref, bin_ref, wout_ref, bout_ref, o_ref,
                win_bf, wout_bf, xbuf, sem):
    i = pl.program_id(0)
    n_real = nreal_ref[0]
    e = be_ref[i]
    prev = be_ref[jnp.maximum(i - 1, 0)]
    tm = xbuf.shape[1] // TILE_ROWS
    slot = lax.rem(i, 2)

    @pl.when(i == 0)
    def _first_gather():
        _gather_token_tiles(x_hbm, rt_ref, tm, xbuf.at[0], sem.at[0], priorities=1)

    @pl.when(i + 1 < n_real)
    def _next_gather():
        _gather_token_tiles(x_hbm, rtn_ref, tm, xbuf.at[1 - slot], sem.at[1 - slot], priorities=1)

    @pl.when((i == 0) | (e != prev))
    def _load_expert():
        win_bf[...] = win_ref[...].astype(BF16)
        wout_bf[...] = wout_ref[...].astype(BF16)

    @pl.when((i < n_real) | (i == 0))
    def _compute():
        pltpu.make_async_copy(xbuf.at[slot], xbuf.at[slot], sem.at[slot]).wait()
        x = _load_token_tiles(xbuf.at[slot], tm).astype(BF16)
        h = _dot(x, win_bf[...]) + bin_ref[...]
        g = jnp.minimum(h[:, :D_EXPERT], SWIGLU_LIMIT)
        lin = jnp.clip(h[:, D_EXPERT:], -SWIGLU_LIMIT, SWIGLU_LIMIT)
        y = g * jax.nn.sigmoid(SWIGLU_ALPHA * g) * (lin + 1.0)
        out = _dot(y.astype(BF16), wout_bf[...]) + bout_ref[...]
        gates = rg_ref[...].T
        out = jnp.concatenate([out[j * LANES:(j + 1) * LANES] * gates[:, j:j + 1] for j in range(tm // LANES)],
                              axis=0)
        _store_token_tiles(o_ref, out)

    @pl.when((i >= n_real) & (i > 0))
    def _unused_block():
        o_ref[...] = jnp.zeros_like(o_ref)


def _moe_experts(x_tiles, row_tok, row_gate, block_expert, n_real, layer, w_in, b_in, w_out, b_out):
    d = D_MODEL
    tm = MOE_TM
    n_blocks = row_tok.shape[0]
    nl, ne, _, dh2 = w_in.shape
    rt = (row_tok * TILE_ROWS).reshape(n_blocks, 1, tm)
    rg = jnp.pad(row_gate.reshape(n_blocks, tm // LANES, LANES), ((0, 0), (0, TILE_ROWS - tm // LANES), (0, 0)))
    smem_rows = lambda imap: pl.BlockSpec((None, 1, tm), imap, memory_space=pltpu.SMEM)
    grid_spec = pltpu.PrefetchScalarGridSpec(
        num_scalar_prefetch=2,
        grid=(n_blocks,),
        in_specs=[
            smem_rows(lambda i, be, nr: (i, 0, 0)),
            smem_rows(lambda i, be, nr: (jnp.minimum(i + 1, n_blocks - 1), 0, 0)),
            pl.BlockSpec(memory_space=pl.ANY),
            pl.BlockSpec((None, TILE_ROWS, LANES), lambda i, be, nr: (i, 0, 0)),
            pl.BlockSpec((None, None, d, dh2), lambda i, be, nr: (layer, be[i], 0, 0)),
            pl.BlockSpec((None, None, 1, dh2), lambda i, be, nr: (layer, be[i], 0, 0)),
            pl.BlockSpec((None, None, D_EXPERT, d), lambda i, be, nr: (layer, be[i], 0, 0)),
            pl.BlockSpec((None, None, 1, d), lambda i, be, nr: (layer, be[i], 0, 0)),
        ],
        out_specs=pl.BlockSpec((tm * TILE_ROWS, LANES), lambda i, be, nr: (i, 0)),
        scratch_shapes=[pltpu.VMEM((d, dh2), BF16), pltpu.VMEM((D_EXPERT, d), BF16),
                        pltpu.VMEM((2, tm * TILE_ROWS, LANES), F32), pltpu.SemaphoreType.DMA((2,))],
    )
    return pl.pallas_call(
        _moe_kernel,
        grid_spec=grid_spec,
        out_shape=jax.ShapeDtypeStruct((n_blocks * tm * TILE_ROWS, LANES), F32),
        compiler_params=_cparams("arbitrary"),
        name="moe_experts",
    )(block_expert, n_real, rt, rt, x_tiles, rg, w_in,
      b_in.reshape(nl, ne, 1, dh2), w_out, b_out.reshape(nl, ne, 1, d))


def _moe_combine_kernel(dc_ref, dn_ref, x_ref, yb_hbm, g_ref, b_ref, o_ref, buf, ffn_ref, sem):
    i = pl.program_id(0)
    n = pl.num_programs(0)
    tk = x_ref.shape[0]
    rows = tk * TILE_ROWS
    slot = lax.rem(i, 2)

    @pl.when(i == 0)
    def _first_gather():
        _gather_token_tiles(yb_hbm, dc_ref, TOP_K * tk, buf.at[0], sem.at[0], priorities=2)

    @pl.when(i + 1 < n)
    def _next_gather():
        _gather_token_tiles(yb_hbm, dn_ref, TOP_K * tk, buf.at[1 - slot], sem.at[1 - slot], priorities=2)

    pltpu.make_async_copy(buf.at[slot], buf.at[slot], sem.at[slot]).wait()
    ffn = buf[slot, 0:rows]
    for k in range(1, TOP_K):
        ffn = ffn + buf[slot, k * rows:(k + 1) * rows]
    ffn_ref[...] = ffn
    o_ref[...] = _layer_norm_rows(DEEPNORM_ALPHA * x_ref[...] + _load_token_tiles(ffn_ref, tk),
                                  g_ref[...], b_ref[...])


def _moe_combine_ln(xn, dest, yb_tiles, g, b):
    t, d = xn.shape
    tk = MOE_TK
    n = t // tk
    dest_tiles = (dest * TILE_ROWS).reshape(n, tk, TOP_K).transpose(0, 2, 1).reshape(n, 1, TOP_K * tk)
    smem_rows = lambda imap: pl.BlockSpec((None, 1, TOP_K * tk), imap, memory_space=pltpu.SMEM)
    row = lambda i: (i, 0)
    fixed = lambda i: (0, 0)
    return pl.pallas_call(
        _moe_combine_kernel,
        grid=(n,),
        in_specs=[smem_rows(lambda i: (i, 0, 0)), smem_rows(lambda i: (jnp.minimum(i + 1, n - 1), 0, 0)),
                  pl.BlockSpec((tk, d), row), pl.BlockSpec(memory_space=pl.ANY),
                  pl.BlockSpec((1, d), fixed), pl.BlockSpec((1, d), fixed)],
        out_specs=pl.BlockSpec((tk, d), row),
        out_shape=jax.ShapeDtypeStruct((t, d), F32),
        scratch_shapes=[pltpu.VMEM((2, TOP_K * tk * TILE_ROWS, LANES), F32),
                        pltpu.VMEM((tk * TILE_ROWS, LANES), F32), pltpu.SemaphoreType.DMA((2,))],
        compiler_params=_cparams("arbitrary"),
        name="moe_combine_ln",
    )(dest_tiles, dest_tiles, xn, yb_tiles, g.reshape(1, d), b.reshape(1, d))


def _moe_ffn_ln(xn, x_tiles, route, counts, layer, w_in, b_in, w_out, b_out, g, b):
    t, d = xn.shape
    tm = MOE_TM
    flat_e = route[:TOP_K].astype(jnp.int32).T.reshape(-1)
    rank = route[TOP_K:2 * TOP_K].astype(jnp.int32).T.reshape(-1)
    gate = route[2 * TOP_K:3 * TOP_K].T.reshape(-1)
    counts = counts[:, 0].astype(jnp.int32)
    blocks_per_e = (counts + tm - 1) // tm
    blk_end = jnp.cumsum(blocks_per_e)
    blk_start = blk_end - blocks_per_e
    row_start = jnp.cumsum(counts) - counts
    dest = blk_start[flat_e] * tm + rank
    n_blocks = (t * TOP_K) // tm + N_EXPERTS
    block_expert = jnp.minimum(
        jnp.sum((blk_end[None, :] <= jnp.arange(n_blocks, dtype=jnp.int32)[:, None]).astype(jnp.int32), axis=1),
        N_EXPERTS - 1)
    n_real = blk_end[-1:].astype(jnp.int32)
    order = jnp.argsort(flat_e, stable=True).astype(jnp.int32)
    blk = jnp.arange(n_blocks, dtype=jnp.int32)
    rank_row = ((blk - blk_start[block_expert]) * tm)[:, None] + jnp.arange(tm, dtype=jnp.int32)[None, :]
    valid = rank_row < counts[block_expert][:, None]
    pair = order[jnp.clip(row_start[block_expert][:, None] + rank_row, 0, t * TOP_K - 1)]
    row_tok = jnp.where(valid, pair // TOP_K, 0)
    row_gate = jnp.where(valid, gate[pair], 0.0)
    yb_tiles = _moe_experts(x_tiles, row_tok, row_gate, block_expert, n_real, layer, w_in, b_in, w_out, b_out)
    return _moe_combine_ln(xn, dest, yb_tiles, g, b)


def _softplus(x):
    return jnp.maximum(x, 0.0) + jnp.log(1.0 + jnp.exp(-jnp.abs(x)))


def _gelu_tanh(x):
    return 0.5 * x * (1.0 + jnp.tanh(0.7978845608028654 * (x + 0.044715 * (x * x * x))))


def _lru_kernel(xa_ref, ga_ref, cw_ref, cb_ref, wr_ref, br_ref, wi_ref, bi_ref, lam_ref, o_ref,
                tail_ref, h_ref, a_ref, u_ref):
    tc, bsz, w = xa_ref.shape

    @pl.when(pl.program_id(0) == 0)
    def _init():
        tail_ref[...] = jnp.zeros_like(tail_ref)
        h_ref[...] = jnp.zeros_like(h_ref)

    xa = xa_ref[...]
    ext = jnp.concatenate([tail_ref[...], xa], axis=0)
    xc = cb_ref[...] + ext[0:tc] * cw_ref[0]
    for tap in range(1, CONV_WIDTH):
        xc = xc + ext[tap:tap + tc] * cw_ref[tap]
    tail_ref[...] = xa[tc - (CONV_WIDTH - 1):]
    x2 = xc.reshape(tc * bsz, w)
    xb = x2.astype(BF16)
    r = jax.nn.sigmoid(_dot(xb, wr_ref[...]) + br_ref[...])
    ig = jax.nn.sigmoid(_dot(xb, wi_ref[...]) + bi_ref[...])
    log_a = (-LRU_C) * r * _softplus(-lam_ref[...])
    a_ref[...] = jnp.exp(log_a).reshape(tc, bsz, w)
    u_ref[...] = (jnp.sqrt(1.0 - jnp.exp(2.0 * log_a)) * (ig * x2)).reshape(tc, bsz, w)

    def step(t, h):
        h = a_ref[t] * h + u_ref[t]
        u_ref[t] = h
        return h

    h_ref[...] = lax.fori_loop(0, tc, step, h_ref[...], unroll=8)
    o_ref[...] = (u_ref[...] * _gelu_tanh(ga_ref[...])).astype(o_ref.dtype)


def _block_diag(w):
    h, i, j = w.shape
    return jnp.einsum('hij,hk->hikj', w, jnp.eye(h, dtype=w.dtype)).reshape(h * i, h * j)


def _rg_lru(proj, conv_w, conv_b, gate_r_w, gate_r_b, gate_i_w, gate_i_b, lru_lambda):
    seq, bsz, _ = proj.shape
    w = A_WIDTH
    tc = LRU_TC
    fixed2 = lambda i: (0, 0)
    fixed3 = lambda i: (0, 0, 0)
    return pl.pallas_call(
        _lru_kernel,
        grid=(seq // tc,),
        in_specs=[pl.BlockSpec((tc, bsz, w), lambda i: (i, 0, 0)),
                  pl.BlockSpec((tc, bsz, w), lambda i: (i, 0, 1)),
                  pl.BlockSpec((CONV_WIDTH, 1, w), fixed3), pl.BlockSpec((1, w), fixed2),
                  pl.BlockSpec((w, w), fixed2), pl.BlockSpec((1, w), fixed2),
                  pl.BlockSpec((w, w), fixed2), pl.BlockSpec((1, w), fixed2),
                  pl.BlockSpec((1, w), fixed2)],
        out_specs=pl.BlockSpec((tc, bsz, w), lambda i: (i, 0, 0)),
        out_shape=jax.ShapeDtypeStruct((seq, bsz, w), BF16),
        scratch_shapes=[pltpu.VMEM((CONV_WIDTH - 1, bsz, w), F32), pltpu.VMEM((bsz, w), F32),
                        pltpu.VMEM((tc, bsz, w), F32), pltpu.VMEM((tc, bsz, w), F32)],
        compiler_params=_cparams("arbitrary"),
        name="rg_lru",
    )(proj, proj, conv_w.reshape(CONV_WIDTH, 1, w), conv_b.reshape(1, w),
      _block_diag(gate_r_w).astype(BF16), gate_r_b.reshape(1, w),
      _block_diag(gate_i_w).astype(BF16), gate_i_b.reshape(1, w), lru_lambda.reshape(1, w))


def _s5_kernel(u_ref, wbr_ref, wbi_ref, ar_ref, ai_ref, ccr_ref, cci_ref, d_ref, gw_ref, gb_ref, o_ref,
               xr_ref, xi_ref, sr_ref, si_ref):
    tc, bsz, w = u_ref.shape
    nstate = xr_ref.shape[-1]
    sh = nstate // S5_HALF
    wh = w // S5_HALF

    @pl.when(pl.program_id(0) == 0)
    def _init():
        sr_ref[...] = jnp.zeros_like(sr_ref)
        si_ref[...] = jnp.zeros_like(si_ref)

    u2 = u_ref[...].reshape(tc * bsz, w)
    ub = u2.astype(BF16)
    for hf in range(S5_HALF):
        uh = ub[:, hf * wh:(hf + 1) * wh]
        xr_ref[:, :, hf * sh:(hf + 1) * sh] = _dot(uh, wbr_ref[hf]).reshape(tc, bsz, sh)
        xi_ref[:, :, hf * sh:(hf + 1) * sh] = _dot(uh, wbi_ref[hf]).reshape(tc, bsz, sh)

    for hf in range(S5_HALF):
        lo, hi = hf * sh, (hf + 1) * sh
        ar = jnp.broadcast_to(ar_ref[:, lo:hi], (bsz, sh))
        ai = jnp.broadcast_to(ai_ref[:, lo:hi], (bsz, sh))

        def step(t, carry, lo=lo, hi=hi, ar=ar, ai=ai):
            xr, xi = carry
            nxr = ar * xr - ai * xi + xr_ref[t, :, lo:hi]
            nxi = ar * xi + ai * xr + xi_ref[t, :, lo:hi]
            xr_ref[t, :, lo:hi] = nxr
            xi_ref[t, :, lo:hi] = nxi
            return nxr, nxi

        xr, xi = lax.fori_loop(0, tc, step, (sr_ref[:, lo:hi], si_ref[:, lo:hi]), unroll=4)
        sr_ref[:, lo:hi] = xr
        si_ref[:, lo:hi] = xi

    ys = []
    for hf in range(S5_HALF):
        lo, hi = hf * sh, (hf + 1) * sh
        xrh = xr_ref[:, :, lo:hi].reshape(tc * bsz, sh).astype(BF16)
        xih = xi_ref[:, :, lo:hi].reshape(tc * bsz, sh).astype(BF16)
        ys.append(_dot(xrh, ccr_ref[hf]) - _dot(xih, cci_ref[hf]))
    y = jnp.concatenate(ys, axis=1) + d_ref[...] * u2
    s = _gelu_tanh(y)
    yb = s * jax.nn.sigmoid(_dot(s.astype(BF16), gw_ref[...]) + gb_ref[...])
    o_ref[...] = yb.reshape(tc, bsz, w).astype(o_ref.dtype)


def _s5_glu(proj, lam_re, lam_im, log_step, b_re, b_im, c_re, c_im, d_skip, glu_w, glu_b):
    seq, bsz, _ = proj.shape
    w = B_WIDTH
    tc = S5_TC
    nstate = S5_GROUPS * S5_STATE
    gh = S5_GROUPS // S5_HALF
    lr = jnp.minimum(lam_re, -1e-4)
    li = lam_im
    step = jnp.exp(log_step)[:, None]
    mag = jnp.exp(lr * step)
    ar = mag * jnp.cos(li * step)
    ai = mag * jnp.sin(li * step)
    inv = 1.0 / (lr * lr + li * li)
    zr = ((ar - 1.0) * lr + ai * li) * inv
    zi = (ai * lr - (ar - 1.0) * li) * inv
    bbr = zr[..., None] * b_re - zi[..., None] * b_im
    bbi = zr[..., None] * b_im + zi[..., None] * b_re
    eye = jnp.eye(gh, dtype=F32)

    def expand_in(bb):
        bb = bb.reshape(S5_HALF, gh, S5_STATE, S5_GROUP_CH)
        return jnp.einsum('fgph,gk->fghkp', bb, eye).reshape(
            S5_HALF, gh * S5_GROUP_CH, gh * S5_STATE).astype(BF16)

    def expand_out(c):
        c = c.reshape(S5_HALF, gh, S5_GROUP_CH, S5_STATE)
        return jnp.einsum('fghp,gk->fgpkh', c, eye).reshape(
            S5_HALF, gh * S5_STATE, gh * S5_GROUP_CH).astype(BF16)

    fixed2 = lambda i: (0, 0)
    fixed3 = lambda i: (0, 0, 0)
    return pl.pallas_call(
        _s5_kernel,
        grid=(seq // tc,),
        in_specs=[pl.BlockSpec((tc, bsz, w), lambda i: (i, 0, 2)),
                  pl.BlockSpec((S5_HALF, w // S5_HALF, nstate // S5_HALF), fixed3),
                  pl.BlockSpec((S5_HALF, w // S5_HALF, nstate // S5_HALF), fixed3),
                  pl.BlockSpec((1, nstate), fixed2), pl.BlockSpec((1, nstate), fixed2),
                  pl.BlockSpec((S5_HALF, nstate // S5_HALF, w // S5_HALF), fixed3),
                  pl.BlockSpec((S5_HALF, nstate // S5_HALF, w // S5_HALF), fixed3),
                  pl.BlockSpec((1, w), fixed2), pl.BlockSpec((w, w), fixed2), pl.BlockSpec((1, w), fixed2)],
        out_specs=pl.BlockSpec((tc, bsz, w), lambda i: (i, 0, 0)),
        out_shape=jax.ShapeDtypeStruct((seq, bsz, w), BF16),
        scratch_shapes=[pltpu.VMEM((tc, bsz, nstate), F32), pltpu.VMEM((tc, bsz, nstate), F32),
                        pltpu.VMEM((bsz, nstate), F32), pltpu.VMEM((bsz, nstate), F32)],
        compiler_params=_cparams("arbitrary"),
        name="s5_glu",
    )(proj, expand_in(bbr), expand_in(bbi), ar.reshape(1, nstate), ai.reshape(1, nstate),
      expand_out(c_re), expand_out(c_im), d_skip.reshape(1, w), glu_w.astype(BF16), glu_b.reshape(1, w))


def _cd_prep_kernel(xc_ref, q_ref, k_ref, pos_ref, cw_ref, cb_ref, wq_ref, wk_ref, wv_ref, wif_ref,
                    bif_ref, invf_ref, mq_ref, mk_ref, mv_ref, gt_ref, qd_ref, kd_ref, tail_ref):
    tl, w = xc_ref.shape
    pad = tail_ref.shape[0]

    @pl.when(pl.program_id(1) == 0)
    def _init():
        tail_ref[...] = jnp.zeros_like(tail_ref)

    xc = xc_ref[...]
    ext = jnp.concatenate([tail_ref[...], xc], axis=0)
    conv = cb_ref[...]
    for tap in range(CONV_WIDTH):
        off = pad - (CONV_WIDTH - 1) + tap
        conv = conv + ext[off:off + tl] * cw_ref[tap]
    tail_ref[...] = xc[tl - pad:]
    xconv = (conv * jax.nn.sigmoid(conv)).astype(BF16)
    q = _dot(xconv, wq_ref[...]).astype(BF16)
    k = (_dot(xconv, wk_ref[...]) * (MLSTM_HEAD_DIM ** -0.5)).astype(BF16)
    v = _dot(xc.astype(BF16), wv_ref[...]).astype(BF16)
    mq_ref[...] = q
    mk_ref[...] = k
    mv_ref[...] = v
    gt_ref[...] = (_dot(q, wif_ref[0]) + _dot(k, wif_ref[1]) + _dot(v, wif_ref[2]) + bif_ref[...])

    ang = pos_ref[...].astype(F32) * invf_ref[...]
    cos = jnp.cos(ang)
    sin = jnp.sin(ang)
    lane = lax.broadcasted_iota(jnp.int32, (1, LANES), 1)
    half = ROPE_DIMS // 2
    sin_lo = jnp.where(lane < half, -sin, 0.0)
    sin_hi = jnp.where((lane >= half) & (lane < ROPE_DIMS), sin, 0.0)
    for src, dst in ((q_ref, qd_ref), (k_ref, kd_ref)):
        for h in range(DSW_HEADS):
            sl = slice(h * DSW_HEAD_DIM, (h + 1) * DSW_HEAD_DIM)
            t = src[:, sl]
            rot = (t * cos + pltpu.roll(t, LANES - half, axis=1) * sin_lo
                   + pltpu.roll(t, half, axis=1) * sin_hi)
            dst[:, sl] = rot


def _cd_prep(proj, positions, conv_w, conv_b, w_q, w_k, w_v, w_if, b_if):
    bsz, seq, _ = proj.shape
    w = C_WIDTH
    tl = PROJ_TL
    half = ROPE_DIMS // 2
    inv_freq = ROPE_THETA ** (-jnp.arange(half, dtype=F32) / half)
    invf = jnp.zeros((1, LANES), F32).at[0, :half].set(inv_freq).at[0, half:ROPE_DIMS].set(inv_freq)
    wif = jnp.zeros((3, w, LANES), F32).at[:, :, :2 * MLSTM_HEADS].set(w_if.reshape(3, w, 2 * MLSTM_HEADS))
    bif = jnp.zeros((1, LANES), F32).at[0, :2 * MLSTM_HEADS].set(b_if)
    col = lambda c: (lambda b, i: (b, i, c))
    fixed2 = lambda b, i: (0, 0)
    fixed3 = lambda b, i: (0, 0, 0)
    tok_bf = jax.ShapeDtypeStruct((bsz, seq, w), BF16)
    return pl.pallas_call(
        _cd_prep_kernel,
        grid=(bsz, seq // tl),
        in_specs=[pl.BlockSpec((None, tl, w), col(0)), pl.BlockSpec((None, tl, w), col(2)),
                  pl.BlockSpec((None, tl, w), col(3)), pl.BlockSpec((None, tl, 1), col(0)),
                  pl.BlockSpec((CONV_WIDTH, 1, w), fixed3), pl.BlockSpec((1, w), fixed2),
                  pl.BlockSpec((w, w), fixed2), pl.BlockSpec((w, w), fixed2), pl.BlockSpec((w, w), fixed2),
                  pl.BlockSpec((3, w, LANES), fixed3), pl.BlockSpec((1, LANES), fixed2),
                  pl.BlockSpec((1, LANES), fixed2)],
        out_specs=[pl.BlockSpec((None, tl, w), col(0))] * 3 + [pl.BlockSpec((None, tl, LANES), col(0))]
                  + [pl.BlockSpec((None, tl, w), col(0))] * 2,
        out_shape=[tok_bf, tok_bf, tok_bf, jax.ShapeDtypeStruct((bsz, seq, LANES), F32),
                   jax.ShapeDtypeStruct((bsz, seq, w), F32), jax.ShapeDtypeStruct((bsz, seq, w), F32)],
        scratch_shapes=[pltpu.VMEM((8, w), F32)],
        compiler_params=_cparams("arbitrary", "arbitrary"),
        name="cd_prep",
    )(proj, proj, proj, positions.reshape(bsz, seq, 1), conv_w.reshape(CONV_WIDTH, 1, w),
      conv_b.reshape(1, w), _block_diag(w_q).astype(BF16), _block_diag(w_k).astype(BF16),
      _block_diag(w_v).astype(BF16), wif.astype(BF16), bif, invf)


def _cumsum_rows(x):
    n = x.shape[0]
    row = lax.broadcasted_iota(jnp.int32, x.shape, 0)
    shift = 1
    while shift < n:
        x = x + jnp.where(row >= shift, pltpu.roll(x, shift, axis=0), 0.0)
        shift *= 2
    return x


def _mlstm_kernel(q_ref, k_ref, v_ref, g_ref, op_ref, o_ref, c_ref, n_ref, m_ref):
    cs = q_ref.shape[0]
    dh = MLSTM_HEAD_DIM

    @pl.when(pl.program_id(1) == 0)
    def _init():
        c_ref[...] = jnp.zeros_like(c_ref)
        n_ref[...] = jnp.zeros_like(n_ref)
        m_ref[...] = jnp.full_like(m_ref, NEG_INF)

    gates = g_ref[...]
    cum = _cumsum_rows(jax.nn.log_sigmoid(gates))
    gates_t = gates.T
    cum_t = cum.T
    causal = (lax.broadcasted_iota(jnp.int32, (cs, cs), 0) >= lax.broadcasted_iota(jnp.int32, (cs, cs), 1))
    for h in range(MLSTM_HEADS):
        sl = slice(h * dh, (h + 1) * dh)
        fcol = MLSTM_HEADS + h
        li_c, li_r = gates[:, h:h + 1], gates_t[h:h + 1, :]
        cum_c, cum_r = cum[:, fcol:fcol + 1], cum_t[fcol:fcol + 1, :]
        m_prev = m_ref[h][:, 0:1]
        q, k, v = q_ref[:, sl], k_ref[:, sl], v_ref[:, sl]
        log_d = jnp.where(causal, cum_c - cum_r + li_r, NEG_INF)
        log_inter = cum_c + m_prev
        m_s = jnp.maximum(jnp.max(log_d, axis=1, keepdims=True), log_inter)
        s = _dot_nt(q, k) * jnp.exp(log_d - m_s)
        inter = jnp.exp(log_inter - m_s)
        c_prev = c_ref[h]
        n_prev = n_ref[h]
        num = _dot(s.astype(BF16), v) + inter * _dot(q, c_prev.astype(BF16))
        den = (jnp.sum(s, axis=1, keepdims=True)
               + inter * jnp.sum(q.astype(F32) * n_prev, axis=1, keepdims=True))
        hh = num / jnp.maximum(jnp.abs(den), jnp.exp(-m_s))
        o_ref[:, sl] = (jax.nn.sigmoid(op_ref[:, sl]) * hh).astype(o_ref.dtype)
        chunk_f = cum_c[cs - 1:cs, :]
        to_end = chunk_f - cum_c + li_c
        m_c = jnp.max(to_end, axis=0, keepdims=True)
        m_new = jnp.maximum(chunk_f + m_prev, m_c)
        s_old = jnp.exp(chunk_f + m_prev - m_new)
        s_new = jnp.exp(m_c - m_new)
        kw = k.astype(F32) * jnp.exp(to_end - m_c)
        c_ref[h] = s_old * c_prev + s_new * _dot(kw.T.astype(BF16), v)
        n_ref[h] = s_old * n_prev + s_new * jnp.sum(kw, axis=0, keepdims=True)
        m_ref[h] = jnp.broadcast_to(m_new, (1, LANES))


def _mlstm(mq, mk, mv, gates, proj):
    bsz, seq, w = mq.shape
    cs = MLSTM_CS
    tok = lambda b, i: (b, i, 0)
    return pl.pallas_call(
        _mlstm_kernel,
        grid=(bsz, seq // cs),
        in_specs=[pl.BlockSpec((None, cs, w), tok)] * 3
                 + [pl.BlockSpec((None, cs, LANES), tok), pl.BlockSpec((None, cs, w), lambda b, i: (b, i, 1))],
        out_specs=pl.BlockSpec((None, cs, w), tok),
        out_shape=jax.ShapeDtypeStruct((bsz, seq, w), BF16),
        scratch_shapes=[pltpu.VMEM((MLSTM_HEADS, MLSTM_HEAD_DIM, MLSTM_HEAD_DIM), F32),
                        pltpu.VMEM((MLSTM_HEADS, 1, MLSTM_HEAD_DIM), F32),
                        pltpu.VMEM((MLSTM_HEADS, 1, LANES), F32)],
        compiler_params=_cparams("arbitrary", "arbitrary"),
        name="mlstm",
    )(mq, mk, mv, gates, proj)


def _dsw_kernel(q_ref, k_ref, v_ref, y_ref, o_scr, lse_scr):
    seq, dh = q_ref.shape
    blk = DSW_BLOCK
    qi = lax.broadcasted_iota(jnp.int32, (blk, 2 * blk), 0)
    ki = lax.broadcasted_iota(jnp.int32, (blk, 2 * blk), 1)
    dist = qi + blk - ki
    scale = dh ** -0.5
    units = seq // blk
    for g, (window, dil) in enumerate(DSW_CONFIGS):
        nsub = seq // dil // blk
        band = (dist >= 0) & (dist <= window // dil)

        def unit(u, carry, g=g, dil=dil, nsub=nsub, band=band):
            r = u // nsub
            sb = u % nsub
            cur = pl.ds(r + sb * (blk * dil), blk, stride=dil)
            prev = pl.ds(r + jnp.maximum(sb - 1, 0) * (blk * dil), blk, stride=dil)
            kk = jnp.concatenate([k_ref[prev, :], k_ref[cur, :]], axis=0).astype(BF16)
            vv = jnp.concatenate([v_ref[prev, :], v_ref[cur, :]], axis=0).astype(BF16)
            s = _dot_nt(q_ref[cur, :].astype(BF16), kk) * scale
            s = jnp.where(band & ((ki >= blk) | (sb > 0)), s, NEG_INF)
            m = jnp.max(s, axis=1, keepdims=True)
            p = jnp.exp(s - m)
            l = jnp.sum(p, axis=1, keepdims=True)
            o_scr[g, cur, :] = _dot(p.astype(BF16), vv) / l
            lse_scr[g, cur, :] = jnp.broadcast_to(m + jnp.log(l), (blk, dh))
            return carry

        lax.fori_loop(0, units, unit, 0, unroll=8)

    lses = [lse_scr[g] for g in range(len(DSW_CONFIGS))]
    mx = functools.reduce(jnp.maximum, lses)
    ws = [jnp.exp(l - mx) for l in lses]
    acc = sum(ws[g] * o_scr[g] for g in range(len(DSW_CONFIGS)))
    y_ref[...] = (acc / sum(ws)).astype(y_ref.dtype)


def _dsw_attention(qd, kd, proj):
    bsz, seq, w = qd.shape
    dh = DSW_HEAD_DIM
    v_col = (proj.shape[-1] - w) // dh
    head = lambda b, h: (b, 0, h)
    return pl.pallas_call(
        _dsw_kernel,
        grid=(bsz, DSW_HEADS),
        in_specs=[pl.BlockSpec((None, seq, dh), head), pl.BlockSpec((None, seq, dh), head),
                  pl.BlockSpec((None, seq, dh), lambda b, h: (b, 0, v_col + h))],
        out_specs=pl.BlockSpec((None, seq, dh), head),
        out_shape=jax.ShapeDtypeStruct((bsz, seq, w), BF16),
        scratch_shapes=[pltpu.VMEM((len(DSW_CONFIGS), seq, dh), F32),
                        pltpu.VMEM((len(DSW_CONFIGS), seq, dh), F32)],
        compiler_params=_cparams("arbitrary", "arbitrary"),
        name="dsw_attention",
    )(qd, kd, proj)


def kernel(x, positions, ab_w_in, ab_conv_w, ab_conv_b, ab_gate_r_w, ab_gate_r_b, ab_gate_i_w, ab_gate_i_b, ab_lru_lambda, ab_s5_lambda_re, ab_s5_lambda_im, ab_s5_log_step, ab_s5_b_re, ab_s5_b_im, ab_s5_c_re, ab_s5_c_im, ab_s5_d, ab_glu_w, ab_glu_b, ab_w_out, cd_w_in, cd_conv_w, cd_conv_b, cd_w_q, cd_w_k, cd_w_v, cd_w_if, cd_b_if, cd_w_out, ln_mix_g, ln_mix_b, ln_ffn_g, ln_ffn_b, moe_router_w, moe_router_b, moe_w_in, moe_b_in, moe_w_out, moe_b_out):
    bsz, seq, d = x.shape
    t = bsz * seq
    for layer in range(DEPTH):
        j = layer // 2
        if layer % 2 == 0:
            proj = _inproj_time_major(x, ab_w_in[j].astype(BF16))
            ya = _rg_lru(proj, ab_conv_w[j], ab_conv_b[j], ab_gate_r_w[j], ab_gate_r_b[j],
                         ab_gate_i_w[j], ab_gate_i_b[j], ab_lru_lambda[j])
            yb = _s5_glu(proj, ab_s5_lambda_re[j], ab_s5_lambda_im[j], ab_s5_log_step[j], ab_s5_b_re[j],
                         ab_s5_b_im[j], ab_s5_c_re[j], ab_s5_c_im[j], ab_s5_d[j], ab_glu_w[j], ab_glu_b[j])
            time_major, w_out = True, ab_w_out[j]
        else:
            proj = _inproj(x, cd_w_in[j].astype(BF16))
            mq, mk, mv, gates, qd, kd = _cd_prep(proj, positions, cd_conv_w[j], cd_conv_b[j], cd_w_q[j],
                                                 cd_w_k[j], cd_w_v[j], cd_w_if[j], cd_b_if[j])
            ya = _mlstm(mq, mk, mv, gates, proj)
            yb = _dsw_attention(qd, kd, proj)
            time_major, w_out = False, cd_w_out[j]
        xn, x_tiles, route, counts = _outproj_ln_router(
            ya, yb, time_major, w_out.astype(BF16), x, ln_mix_g[layer], ln_mix_b[layer],
            moe_router_w[layer], moe_router_b[layer])
        x = _moe_ffn_ln(xn.reshape(t, d), x_tiles, route, counts, layer, moe_w_in, moe_b_in,
                        moe_w_out, moe_b_out, ln_ffn_g[layer], ln_ffn_b[layer]).reshape(bsz, seq, d)
    return x
```

```python
import functools

import jax
import jax.numpy as jnp
from jax import lax
from jax.experimental import pallas as pl
from jax.experimental.pallas import tpu as pltpu

F32 = jnp.float32
BF16 = jnp.bfloat16

D_MODEL = 1024
DEPTH = 2
A_WIDTH = 512
B_WIDTH = 512
C_WIDTH = 512
D_WIDTH = 512
LRU_HEADS = 8
LRU_HEAD_DIM = A_WIDTH // LRU_HEADS
LRU_C = 8.0
CONV_WIDTH = 4
S5_GROUP_CH = 16
S5_GROUPS = B_WIDTH // S5_GROUP_CH
S5_STATE = 64
MLSTM_HEADS = 4
MLSTM_HEAD_DIM = C_WIDTH // MLSTM_HEADS
DSW_HEADS = 4
DSW_HEAD_DIM = D_WIDTH // DSW_HEADS
DSW_CONFIGS = ((128, 1), (512, 4), (2048, 16))
DSW_BLOCK = 128
ROPE_THETA = 500000.0
ROPE_DIMS = DSW_HEAD_DIM // 4
N_EXPERTS = 32
TOP_K = 4
D_EXPERT = D_MODEL
SWIGLU_LIMIT = 7.0
SWIGLU_ALPHA = 1.702
DEEPNORM_ALPHA = (2 * DEPTH) ** 0.25
LN_EPS = 1e-5

VMEM_LIMIT_BYTES = 56 * 1024 * 1024
LANES = 128
TILE_ROWS = D_MODEL // LANES
MOE_TM = 512
MOE_TK = 256
PROJ_TL = 512
ROUTE_ROWS = 16
LRU_TC = 128
S5_TC = 128
S5_HALF = 2
MLSTM_CS = 256
NEG_INF = float("-inf")


def _cparams(*sem):
    return pltpu.CompilerParams(dimension_semantics=sem, vmem_limit_bytes=VMEM_LIMIT_BYTES)


def _dot(a, b):
    return jnp.dot(a, b, preferred_element_type=F32)


def _dot_nt(a, b):
    return lax.dot_general(a, b, (((1,), (1,)), ((), ())), preferred_element_type=F32)


def _matmul_kernel(x_ref, w_ref, o_ref):
    o_ref[...] = _dot(x_ref[...].astype(BF16), w_ref[...])


def _matmul_time_major_kernel(x_ref, w_ref, o_ref, scr):
    bsz, tl, _ = x_ref.shape
    nchunk = scr.shape[0]
    for b in range(bsz):
        p = _dot(x_ref[b].astype(BF16), w_ref[...])
        for c in range(nchunk):
            scr[c, pl.ds(b, tl, stride=bsz), :] = p[:, c * LANES:(c + 1) * LANES]
    for c in range(nchunk):
        o_ref[:, c * LANES:(c + 1) * LANES] = scr[c]


def _inproj_time_major(x3, w_bf16):
    bsz, seq, k = x3.shape
    n = w_bf16.shape[1]
    tl = LRU_TC
    out = pl.pallas_call(
        _matmul_time_major_kernel,
        grid=(seq // tl,),
        in_specs=[pl.BlockSpec((bsz, tl, k), lambda i: (0, i, 0)),
                  pl.BlockSpec((k, n), lambda i: (0, 0))],
        out_specs=pl.BlockSpec((tl * bsz, n), lambda i: (i, 0)),
        out_shape=jax.ShapeDtypeStruct((seq * bsz, n), F32),
        scratch_shapes=[pltpu.VMEM((n // LANES, tl * bsz, LANES), F32)],
        compiler_params=_cparams("arbitrary"),
        name="in_proj_time_major",
    )(x3, w_bf16)
    return out.reshape(seq, bsz, n)


def _inproj(x3, w_bf16):
    bsz, seq, k = x3.shape
    n = w_bf16.shape[1]
    return pl.pallas_call(
        _matmul_kernel,
        grid=(bsz, seq // PROJ_TL),
        in_specs=[pl.BlockSpec((None, PROJ_TL, k), lambda b, i: (b, i, 0)),
                  pl.BlockSpec((k, n), lambda b, i: (0, 0))],
        out_specs=pl.BlockSpec((None, PROJ_TL, n), lambda b, i: (b, i, 0)),
        out_shape=jax.ShapeDtypeStruct((bsz, seq, n), F32),
        compiler_params=_cparams("arbitrary", "arbitrary"),
        name="in_proj",
    )(x3, w_bf16)


def _layer_norm_rows(z, g, b):
    mu = jnp.mean(z, axis=-1, keepdims=True)
    zc = z - mu
    var = jnp.mean(zc * zc, axis=-1, keepdims=True)
    return zc * lax.rsqrt(var + LN_EPS) * g + b


def _outproj_ln_router_kernel(ya_ref, yb_ref, wa_ref, wb_ref, x_ref, g_ref, b_ref, rwh_ref, rwl_ref, rb_ref,
                              tri_ref, xn_ref, xt_ref, rt_ref, cnt_ref):
    @pl.when((pl.program_id(0) == 0) & (pl.program_id(1) == 0))
    def _init():
        cnt_ref[...] = jnp.zeros_like(cnt_ref)

    mix = _dot(ya_ref[...], wa_ref[...]) + _dot(yb_ref[...], wb_ref[...])
    xn = _layer_norm_rows(DEEPNORM_ALPHA * x_ref[...] + mix, g_ref[...], b_ref[...])
    xn_ref[...] = xn
    _store_token_tiles(xt_ref, xn)
    xh = xn.astype(BF16)
    xl = (xn - xh.astype(F32)).astype(BF16)
    lg = (_dot_nt(rwh_ref[...], xh) + _dot_nt(rwh_ref[...], xl) + _dot_nt(rwl_ref[...], xh)) + rb_ref[...]
    ne, tl = lg.shape
    row = lax.broadcasted_iota(jnp.int32, (ne, tl), 0).astype(F32)
    tops, ids, hots = [], [], []
    for _ in range(TOP_K):
        m = jnp.max(lg, axis=0, keepdims=True)
        idx = jnp.min(jnp.where(lg == m, row, float(ne)), axis=0, keepdims=True)
        hot = row == idx
        lg = jnp.where(hot, NEG_INF, lg)
        tops.append(m)
        ids.append(idx)
        hots.append(hot)
    exps = [jnp.exp(m - tops[0]) for m in tops]
    denom = functools.reduce(jnp.add, exps)
    chosen = functools.reduce(jnp.add, [h.astype(F32) for h in hots])
    before = _dot(chosen.astype(BF16), tri_ref[...]) + cnt_ref[...]
    out_row = lax.broadcasted_iota(jnp.int32, rt_ref.shape, 0)
    route = jnp.zeros(rt_ref.shape, F32)
    for k in range(TOP_K):
        rank = jnp.sum(jnp.where(hots[k], before, 0.0), axis=0, keepdims=True)
        route = jnp.where(out_row == k, ids[k], route)
        route = jnp.where(out_row == TOP_K + k, rank, route)
        route = jnp.where(out_row == 2 * TOP_K + k, exps[k] / denom, route)
    rt_ref[...] = route
    cnt_ref[...] = cnt_ref[...] + jnp.sum(chosen, axis=1, keepdims=True)


def _outproj_ln_router(ya, yb, time_major, w_bf16, x3, g, b, rw, rb):
    bsz, seq, d = x3.shape
    wa, wb = w_bf16[:ya.shape[-1]], w_bf16[ya.shape[-1]:]
    ne = N_EXPERTS
    rw = rw.T
    rw_hi = rw.astype(BF16)
    rw_lo = (rw - rw_hi.astype(F32)).astype(BF16)
    tl = PROJ_TL
    tri = jnp.triu(jnp.ones((tl, tl), BF16), 1)
    if time_major:
        ya, yb = ya.reshape(seq, -1), yb.reshape(seq, -1)
        y_specs = [pl.BlockSpec((tl, wa.shape[0]), lambda bi, i: (i, bi)),
                   pl.BlockSpec((tl, wb.shape[0]), lambda bi, i: (i, bi))]
    else:
        y_specs = [pl.BlockSpec((None, tl, wa.shape[0]), lambda bi, i: (bi, i, 0)),
                   pl.BlockSpec((None, tl, wb.shape[0]), lambda bi, i: (bi, i, 0))]
    fixed = lambda bi, i: (0, 0)
    tok = lambda bi, i: (bi, i, 0)
    return pl.pallas_call(
        _outproj_ln_router_kernel,
        grid=(bsz, seq // tl),
        in_specs=y_specs + [pl.BlockSpec(wa.shape, fixed), pl.BlockSpec(wb.shape, fixed),
                            pl.BlockSpec((None, tl, d), tok),
                            pl.BlockSpec((1, d), fixed), pl.BlockSpec((1, d), fixed),
                            pl.BlockSpec((ne, d), fixed), pl.BlockSpec((ne, d), fixed),
                            pl.BlockSpec((ne, 1), fixed), pl.BlockSpec((tl, tl), fixed)],
        out_specs=[pl.BlockSpec((None, tl, d), tok),
                   pl.BlockSpec((tl * TILE_ROWS, LANES), lambda bi, i: (bi * (seq // tl) + i, 0)),
                   pl.BlockSpec((ROUTE_ROWS, tl), lambda bi, i: (0, bi * (seq // tl) + i)),
                   pl.BlockSpec((ne, 1), fixed)],
        out_shape=[jax.ShapeDtypeStruct((bsz, seq, d), F32),
                   jax.ShapeDtypeStruct((bsz * seq * TILE_ROWS, LANES), F32),
                   jax.ShapeDtypeStruct((ROUTE_ROWS, bsz * seq), F32), jax.ShapeDtypeStruct((ne, 1), F32)],
        compiler_params=_cparams("arbitrary", "arbitrary"),
        name="out_proj_ln_router",
    )(ya, yb, wa, wb, x3, g.reshape(1, d), b.reshape(1, d), rw_hi, rw_lo, rb.reshape(ne, 1), tri)


def _store_token_tiles(dst_ref, x):
    n = x.shape[0]
    for j in range(TILE_ROWS):
        dst_ref[pl.ds(j, n, stride=TILE_ROWS), :] = x[:, j * LANES:(j + 1) * LANES]


def _load_token_tiles(src_ref, n):
    return jnp.concatenate([src_ref[pl.ds(j, n, stride=TILE_ROWS), :] for j in range(TILE_ROWS)], axis=1)


def _gather_token_tiles(src_hbm, idx_ref, n, dst, sem, priorities):
    def body(g, carry):
        base = pl.multiple_of(g * (TILE_ROWS * TILE_ROWS), TILE_ROWS * TILE_ROWS)
        for u in range(TILE_ROWS):
            src_row = pl.multiple_of(idx_ref[0, g * TILE_ROWS + u], TILE_ROWS)
            pltpu.make_async_copy(src_hbm.at[pl.ds(src_row, TILE_ROWS)],
                                  dst.at[pl.ds(base + u * TILE_ROWS, TILE_ROWS)], sem).start(priority=u % priorities)
        return carry
    lax.fori_loop(0, n // TILE_ROWS, body, 0, unroll=2)


def _moe_kernel(be_ref, nreal_ref, rt_ref, rtn_ref, x_hbm, rg_ref, win_ref, bin_ref, wout_ref, bout_ref, o_ref,
                win_bf, wout_bf, xbuf, sem):
    i = pl.program_id(0)
    n_real = nreal_ref[0]
    e = be_ref[i]
    prev = be_ref[jnp.maximum(i - 1, 0)]
    tm = xbuf.shape[1] // TILE_ROWS
    slot = lax.rem(i, 2)

    @pl.when(i == 0)
    def _first_gather():
        _gather_token_tiles(x_hbm, rt_ref, tm, xbuf.at[0], sem.at[0], priorities=1)

    @pl.when(i + 1 < n_real)
    def _next_gather():
        _gather_token_tiles(x_hbm, rtn_ref, tm, xbuf.at[1 - slot], sem.at[1 - slot], priorities=1)

    @pl.when((i == 0) | (e != prev))
    def _load_expert():
        win_bf[...] = win_ref[...].astype(BF16)
        wout_bf[...] = wout_ref[...].astype(BF16)

    @pl.when((i < n_real) | (i == 0))
    def _compute():
        pltpu.make_async_copy(xbuf.at[slot], xbuf.at[slot], sem.at[slot]).wait()
        x = _load_token_tiles(xbuf.at[slot], tm).astype(BF16)
        h = _dot(x, win_bf[...]) + bin_ref[...]
        g = jnp.minimum(h[:, :D_EXPERT], SWIGLU_LIMIT)
        lin = jnp.clip(h[:, D_EXPERT:], -SWIGLU_LIMIT, SWIGLU_LIMIT)
        y = g * jax.nn.sigmoid(SWIGLU_ALPHA * g) * (lin + 1.0)
        out = _dot(y.astype(BF16), wout_bf[...]) + bout_ref[...]
        gates = rg_ref[...].T
        out = jnp.concatenate([out[j * LANES:(j + 1) * LANES] * gates[:, j:j + 1] for j in range(tm // LANES)],
                              axis=0)
        _store_token_tiles(o_ref, out)

    @pl.when((i >= n_real) & (i > 0))
    def _unused_block():
        o_ref[...] = jnp.zeros_like(o_ref)


def _moe_experts(x_tiles, row_tok, row_gate, block_expert, n_real, layer, w_in, b_in, w_out, b_out):
    d = D_MODEL
    tm = MOE_TM
    n_blocks = row_tok.shape[0]
    nl, ne, _, dh2 = w_in.shape
    rt = (row_tok * TILE_ROWS).reshape(n_blocks, 1, tm)
    rg = jnp.pad(row_gate.reshape(n_blocks, tm // LANES, LANES), ((0, 0), (0, TILE_ROWS - tm // LANES), (0, 0)))
    smem_rows = lambda imap: pl.BlockSpec((None, 1, tm), imap, memory_space=pltpu.SMEM)
    grid_spec = pltpu.PrefetchScalarGridSpec(
        num_scalar_prefetch=2,
        grid=(n_blocks,),
        in_specs=[
            smem_rows(lambda i, be, nr: (i, 0, 0)),
            smem_rows(lambda i, be, nr: (jnp.minimum(i + 1, n_blocks - 1), 0, 0)),
            pl.BlockSpec(memory_space=pl.ANY),
            pl.BlockSpec((None, TILE_ROWS, LANES), lambda i, be, nr: (i, 0, 0)),
            pl.BlockSpec((None, None, d, dh2), lambda i, be, nr: (layer, be[i], 0, 0)),
            pl.BlockSpec((None, None, 1, dh2), lambda i, be, nr: (layer, be[i], 0, 0)),
            pl.BlockSpec((None, None, D_EXPERT, d), lambda i, be, nr: (layer, be[i], 0, 0)),
            pl.BlockSpec((None, None, 1, d), lambda i, be, nr: (layer, be[i], 0, 0)),
        ],
        out_specs=pl.BlockSpec((tm * TILE_ROWS, LANES), lambda i, be, nr: (i, 0)),
        scratch_shapes=[pltpu.VMEM((d, dh2), BF16), pltpu.VMEM((D_EXPERT, d), BF16),
                        pltpu.VMEM((2, tm * TILE_ROWS, LANES), F32), pltpu.SemaphoreType.DMA((2,))],
    )
    return pl.pallas_call(
        _moe_kernel,
        grid_spec=grid_spec,
        out_shape=jax.ShapeDtypeStruct((n_blocks * tm * TILE_ROWS, LANES), F32),
        compiler_params=_cparams("arbitrary"),
        name="moe_experts",
    )(block_expert, n_real, rt, rt, x_tiles, rg, w_in,
      b_in.reshape(nl, ne, 1, dh2), w_out, b_out.reshape(nl, ne, 1, d))


def _moe_combine_kernel(dc_ref, dn_ref, x_ref, yb_hbm, g_ref, b_ref, o_ref, buf, ffn_ref, sem):
    i = pl.program_id(0)
    n = pl.num_programs(0)
    tk = x_ref.shape[0]
    rows = tk * TILE_ROWS
    slot = lax.rem(i, 2)

    @pl.when(i == 0)
    def _first_gather():
        _gather_token_tiles(yb_hbm, dc_ref, TOP_K * tk, buf.at[0], sem.at[0], priorities=2)

    @pl.when(i + 1 < n)
    def _next_gather():
        _gather_token_tiles(yb_hbm, dn_ref, TOP_K * tk, buf.at[1 - slot], sem.at[1 - slot], priorities=2)

    pltpu.make_async_copy(buf.at[slot], buf.at[slot], sem.at[slot]).wait()
    ffn = buf[slot, 0:rows]
    for k in range(1, TOP_K):
        ffn = ffn + buf[slot, k * rows:(k + 1) * rows]
    ffn_ref[...] = ffn
    o_ref[...] = _layer_norm_rows(DEEPNORM_ALPHA * x_ref[...] + _load_token_tiles(ffn_ref, tk),
                                  g_ref[...], b_ref[...])


def _moe_combine_ln(xn, dest, yb_tiles, g, b):
    t, d = xn.shape
    tk = MOE_TK
    n = t // tk
    dest_tiles = (dest * TILE_ROWS).reshape(n, tk, TOP_K).transpose(0, 2, 1).reshape(n, 1, TOP_K * tk)
    smem_rows = lambda imap: pl.BlockSpec((None, 1, TOP_K * tk), imap, memory_space=pltpu.SMEM)
    row = lambda i: (i, 0)
    fixed = lambda i: (0, 0)
    return pl.pallas_call(
        _moe_combine_kernel,
        grid=(n,),
        in_specs=[smem_rows(lambda i: (i, 0, 0)), smem_rows(lambda i: (jnp.minimum(i + 1, n - 1), 0, 0)),
                  pl.BlockSpec((tk, d), row), pl.BlockSpec(memory_space=pl.ANY),
                  pl.BlockSpec((1, d), fixed), pl.BlockSpec((1, d), fixed)],
        out_specs=pl.BlockSpec((tk, d), row),
        out_shape=jax.ShapeDtypeStruct((t, d), F32),
        scratch_shapes=[pltpu.VMEM((2, TOP_K * tk * TILE_ROWS, LANES), F32),
                        pltpu.VMEM((tk * TILE_ROWS, LANES), F32), pltpu.SemaphoreType.DMA((2,))],
        compiler_params=_cparams("arbitrary"),
        name="moe_combine_ln",
    )(dest_tiles, dest_tiles, xn, yb_tiles, g.reshape(1, d), b.reshape(1, d))


def _moe_ffn_ln(xn, x_tiles, route, counts, layer, w_in, b_in, w_out, b_out, g, b):
    t, d = xn.shape
    tm = MOE_TM
    flat_e = route[:TOP_K].astype(jnp.int32).T.reshape(-1)
    rank = route[TOP_K:2 * TOP_K].astype(jnp.int32).T.reshape(-1)
    gate = route[2 * TOP_K:3 * TOP_K].T.reshape(-1)
    counts = counts[:, 0].astype(jnp.int32)
    blocks_per_e = (counts + tm - 1) // tm
    blk_end = jnp.cumsum(blocks_per_e)
    blk_start = blk_end - blocks_per_e
    row_start = jnp.cumsum(counts) - counts
    dest = blk_start[flat_e] * tm + rank
    n_blocks = (t * TOP_K) // tm + N_EXPERTS
    block_expert = jnp.minimum(
        jnp.sum((blk_end[None, :] <= jnp.arange(n_blocks, dtype=jnp.int32)[:, None]).astype(jnp.int32), axis=1),
        N_EXPERTS - 1)
    n_real = blk_end[-1:].astype(jnp.int32)
    order = jnp.argsort(flat_e, stable=True).astype(jnp.int32)
    blk = jnp.arange(n_blocks, dtype=jnp.int32)
    rank_row = ((blk - blk_start[block_expert]) * tm)[:, None] + jnp.arange(tm, dtype=jnp.int32)[None, :]
    valid = rank_row < counts[block_expert][:, None]
    pair = order[jnp.clip(row_start[block_expert][:, None] + rank_row, 0, t * TOP_K - 1)]
    row_tok = jnp.where(valid, pair // TOP_K, 0)
    row_gate = jnp.where(valid, gate[pair], 0.0)
    yb_tiles = _moe_experts(x_tiles, row_tok, row_gate, block_expert, n_real, layer, w_in, b_in, w_out, b_out)
    return _moe_combine_ln(xn, dest, yb_tiles, g, b)


def _softplus(x):
    return jnp.maximum(x, 0.0) + jnp.log(1.0 + jnp.exp(-jnp.abs(x)))


def _gelu_tanh(x):
    return 0.5 * x * (1.0 + jnp.tanh(0.7978845608028654 * (x + 0.044715 * (x * x * x))))


def _lru_kernel(xa_ref, ga_ref, cw_ref, cb_ref, wr_ref, br_ref, wi_ref, bi_ref, lam_ref, o_ref,
                tail_ref, h_ref, a_ref, u_ref):
    tc, bsz, w = xa_ref.shape

    @pl.when(pl.program_id(0) == 0)
    def _init():
        tail_ref[...] = jnp.zeros_like(tail_ref)
        h_ref[...] = jnp.zeros_like(h_ref)

    xa = xa_ref[...]
    ext = jnp.concatenate([tail_ref[...], xa], axis=0)
    xc = cb_ref[...] + ext[0:tc] * cw_ref[0]
    for tap in range(1, CONV_WIDTH):
        xc = xc + ext[tap:tap + tc] * cw_ref[tap]
    tail_ref[...] = xa[tc - (CONV_WIDTH - 1):]
    x2 = xc.reshape(tc * bsz, w)
    xb = x2.astype(BF16)
    r = jax.nn.sigmoid(_dot(xb, wr_ref[...]) + br_ref[...])
    ig = jax.nn.sigmoid(_dot(xb, wi_ref[...]) + bi_ref[...])
    log_a = (-LRU_C) * r * _softplus(-lam_ref[...])
    a_ref[...] = jnp.exp(log_a).reshape(tc, bsz, w)
    u_ref[...] = (jnp.sqrt(1.0 - jnp.exp(2.0 * log_a)) * (ig * x2)).reshape(tc, bsz, w)

    def step(t, h):
        h = a_ref[t] * h + u_ref[t]
        u_ref[t] = h
        return h

    h_ref[...] = lax.fori_loop(0, tc, step, h_ref[...], unroll=8)
    o_ref[...] = (u_ref[...] * _gelu_tanh(ga_ref[...])).astype(o_ref.dtype)


def _block_diag(w):
    h, i, j = w.shape
    return jnp.einsum('hij,hk->hikj', w, jnp.eye(h, dtype=w.dtype)).reshape(h * i, h * j)


def _rg_lru(proj, conv_w, conv_b, gate_r_w, gate_r_b, gate_i_w, gate_i_b, lru_lambda):
    seq, bsz, _ = proj.shape
    w = A_WIDTH
    tc = LRU_TC
    fixed2 = lambda i: (0, 0)
    fixed3 = lambda i: (0, 0, 0)
    return pl.pallas_call(
        _lru_kernel,
        grid=(seq // tc,),
        in_specs=[pl.BlockSpec((tc, bsz, w), lambda i: (i, 0, 0)),
                  pl.BlockSpec((tc, bsz, w), lambda i: (i, 0, 1)),
                  pl.BlockSpec((CONV_WIDTH, 1, w), fixed3), pl.BlockSpec((1, w), fixed2),
                  pl.BlockSpec((w, w), fixed2), pl.BlockSpec((1, w), fixed2),
                  pl.BlockSpec((w, w), fixed2), pl.BlockSpec((1, w), fixed2),
                  pl.BlockSpec((1, w), fixed2)],
        out_specs=pl.BlockSpec((tc, bsz, w), lambda i: (i, 0, 0)),
        out_shape=jax.ShapeDtypeStruct((seq, bsz, w), BF16),
        scratch_shapes=[pltpu.VMEM((CONV_WIDTH - 1, bsz, w), F32), pltpu.VMEM((bsz, w), F32),
                        pltpu.VMEM((tc, bsz, w), F32), pltpu.VMEM((tc, bsz, w), F32)],
        compiler_params=_cparams("arbitrary"),
        name="rg_lru",
    )(proj, proj, conv_w.reshape(CONV_WIDTH, 1, w), conv_b.reshape(1, w),
      _block_diag(gate_r_w).astype(BF16), gate_r_b.reshape(1, w),
      _block_diag(gate_i_w).astype(BF16), gate_i_b.reshape(1, w), lru_lambda.reshape(1, w))


def _s5_kernel(u_ref, wbr_ref, wbi_ref, ar_ref, ai_ref, ccr_ref, cci_ref, d_ref, gw_ref, gb_ref, o_ref,
               xr_ref, xi_ref, sr_ref, si_ref):
    tc, bsz, w = u_ref.shape
    nstate = xr_ref.shape[-1]
    sh = nstate // S5_HALF
    wh = w // S5_HALF

    @pl.when(pl.program_id(0) == 0)
    def _init():
        sr_ref[...] = jnp.zeros_like(sr_ref)
        si_ref[...] = jnp.zeros_like(si_ref)

    u2 = u_ref[...].reshape(tc * bsz, w)
    ub = u2.astype(BF16)
    for hf in range(S5_HALF):
        uh = ub[:, hf * wh:(hf + 1) * wh]
        xr_ref[:, :, hf * sh:(hf + 1) * sh] = _dot(uh, wbr_ref[hf]).reshape(tc, bsz, sh)
        xi_ref[:, :, hf * sh:(hf + 1) * sh] = _dot(uh, wbi_ref[hf]).reshape(tc, bsz, sh)

    for hf in range(S5_HALF):
        lo, hi = hf * sh, (hf + 1) * sh
        ar = jnp.broadcast_to(ar_ref[:, lo:hi], (bsz, sh))
        ai = jnp.broadcast_to(ai_ref[:, lo:hi], (bsz, sh))

        def step(t, carry, lo=lo, hi=hi, ar=ar, ai=ai):
            xr, xi = carry
            nxr = ar * xr - ai * xi + xr_ref[t, :, lo:hi]
            nxi = ar * xi + ai * xr + xi_ref[t, :, lo:hi]
            xr_ref[t, :, lo:hi] = nxr
            xi_ref[t, :, lo:hi] = nxi
            return nxr, nxi

        xr, xi = lax.fori_loop(0, tc, step, (sr_ref[:, lo:hi], si_ref[:, lo:hi]), unroll=4)
        sr_ref[:, lo:hi] = xr
        si_ref[:, lo:hi] = xi

    ys = []
    for hf in range(S5_HALF):
        lo, hi = hf * sh, (hf + 1) * sh
        xrh = xr_ref[:, :, lo:hi].reshape(tc * bsz, sh).astype(BF16)
        xih = xi_ref[:, :, lo:hi].reshape(tc * bsz, sh).astype(BF16)
        ys.append(_dot(xrh, ccr_ref[hf]) - _dot(xih, cci_ref[hf]))
    y = jnp.concatenate(ys, axis=1) + d_ref[...] * u2
    s = _gelu_tanh(y)
    yb = s * jax.nn.sigmoid(_dot(s.astype(BF16), gw_ref[...]) + gb_ref[...])
    o_ref[...] = yb.reshape(tc, bsz, w).astype(o_ref.dtype)


def _s5_glu(proj, lam_re, lam_im, log_step, b_re, b_im, c_re, c_im, d_skip, glu_w, glu_b):
    seq, bsz, _ = proj.shape
    w = B_WIDTH
    tc = S5_TC
    nstate = S5_GROUPS * S5_STATE
    gh = S5_GROUPS // S5_HALF
    lr = jnp.minimum(lam_re, -1e-4)
    li = lam_im
    step = jnp.exp(log_step)[:, None]
    mag = jnp.exp(lr * step)
    ar = mag * jnp.cos(li * step)
    ai = mag * jnp.sin(li * step)
    inv = 1.0 / (lr * lr + li * li)
    zr = ((ar - 1.0) * lr + ai * li) * inv
    zi = (ai * lr - (ar - 1.0) * li) * inv
    bbr = zr[..., None] * b_re - zi[..., None] * b_im
    bbi = zr[..., None] * b_im + zi[..., None] * b_re
    eye = jnp.eye(gh, dtype=F32)

    def expand_in(bb):
        bb = bb.reshape(S5_HALF, gh, S5_STATE, S5_GROUP_CH)
        return jnp.einsum('fgph,gk->fghkp', bb, eye).reshape(
            S5_HALF, gh * S5_GROUP_CH, gh * S5_STATE).astype(BF16)

    def expand_out(c):
        c = c.reshape(S5_HALF, gh, S5_GROUP_CH, S5_STATE)
        return jnp.einsum('fghp,gk->fgpkh', c, eye).reshape(
            S5_HALF, gh * S5_STATE, gh * S5_GROUP_CH).astype(BF16)

    fixed2 = lambda i: (0, 0)
    fixed3 = lambda i: (0, 0, 0)
    return pl.pallas_call(
        _s5_kernel,
        grid=(seq // tc,),
        in_specs=[pl.BlockSpec((tc, bsz, w), lambda i: (i, 0, 2)),
                  pl.BlockSpec((S5_HALF, w // S5_HALF, nstate // S5_HALF), fixed3),
                  pl.BlockSpec((S5_HALF, w // S5_HALF, nstate // S5_HALF), fixed3),
                  pl.BlockSpec((1, nstate), fixed2), pl.BlockSpec((1, nstate), fixed2),
                  pl.BlockSpec((S5_HALF, nstate // S5_HALF, w // S5_HALF), fixed3),
                  pl.BlockSpec((S5_HALF, nstate // S5_HALF, w // S5_HALF), fixed3),
                  pl.BlockSpec((1, w), fixed2), pl.BlockSpec((w, w), fixed2), pl.BlockSpec((1, w), fixed2)],
        out_specs=pl.BlockSpec((tc, bsz, w), lambda i: (i, 0, 0)),
        out_shape=jax.ShapeDtypeStruct((seq, bsz, w), BF16),
        scratch_shapes=[pltpu.VMEM((tc, bsz, nstate), F32), pltpu.VMEM((tc, bsz, nstate), F32),
                        pltpu.VMEM((bsz, nstate), F32), pltpu.VMEM((bsz, nstate), F32)],
        compiler_params=_cparams("arbitrary"),
        name="s5_glu",
    )(proj, expand_in(bbr), expand_in(bbi), ar.reshape(1, nstate), ai.reshape(1, nstate),
      expand_out(c_re), expand_out(c_im), d_skip.reshape(1, w), glu_w.astype(BF16), glu_b.reshape(1, w))


def _cd_prep_kernel(xc_ref, q_ref, k_ref, pos_ref, cw_ref, cb_ref, wq_ref, wk_ref, wv_ref, wif_ref,
                    bif_ref, invf_ref, mq_ref, mk_ref, mv_ref, gt_ref, qd_ref, kd_ref, tail_ref):
    tl, w = xc_ref.shape
    pad = tail_ref.shape[0]

    @pl.when(pl.program_id(1) == 0)
    def _init():
        tail_ref[...] = jnp.zeros_like(tail_ref)

    xc = xc_ref[...]
    ext = jnp.concatenate([tail_ref[...], xc], axis=0)
    conv = cb_ref[...]
    for tap in range(CONV_WIDTH):
        off = pad - (CONV_WIDTH - 1) + tap
        conv = conv + ext[off:off + tl] * cw_ref[tap]
    tail_ref[...] = xc[tl - pad:]
    xconv = (conv * jax.nn.sigmoid(conv)).astype(BF16)
    q = _dot(xconv, wq_ref[...]).astype(BF16)
    k = (_dot(xconv, wk_ref[...]) * (MLSTM_HEAD_DIM ** -0.5)).astype(BF16)
    v = _dot(xc.astype(BF16), wv_ref[...]).astype(BF16)
    mq_ref[...] = q
    mk_ref[...] = k
    mv_ref[...] = v
    gt_ref[...] = (_dot(q, wif_ref[0]) + _dot(k, wif_ref[1]) + _dot(v, wif_ref[2]) + bif_ref[...])

    ang = pos_ref[...].astype(F32) * invf_ref[...]
    cos = jnp.cos(ang)
    sin = jnp.sin(ang)
    lane = lax.broadcasted_iota(jnp.int32, (1, LANES), 1)
    half = ROPE_DIMS // 2
    sin_lo = jnp.where(lane < half, -sin, 0.0)
    sin_hi = jnp.where((lane >= half) & (lane < ROPE_DIMS), sin, 0.0)
    for src, dst in ((q_ref, qd_ref), (k_ref, kd_ref)):
        for h in range(DSW_HEADS):
            sl = slice(h * DSW_HEAD_DIM, (h + 1) * DSW_HEAD_DIM)
            t = src[:, sl]
            rot = (t * cos + pltpu.roll(t, LANES - half, axis=1) * sin_lo
                   + pltpu.roll(t, half, axis=1) * sin_hi)
            dst[:, sl] = rot


def _cd_prep(proj, positions, conv_w, conv_b, w_q, w_k, w_v, w_if, b_if):
    bsz, seq, _ = proj.shape
    w = C_WIDTH
    tl = PROJ_TL
    half = ROPE_DIMS // 2
    inv_freq = ROPE_THETA ** (-jnp.arange(half, dtype=F32) / half)
    invf = jnp.zeros((1, LANES), F32).at[0, :half].set(inv_freq).at[0, half:ROPE_DIMS].set(inv_freq)
    wif = jnp.zeros((3, w, LANES), F32).at[:, :, :2 * MLSTM_HEADS].set(w_if.reshape(3, w, 2 * MLSTM_HEADS))
    bif = jnp.zeros((1, LANES), F32).at[0, :2 * MLSTM_HEADS].set(b_if)
    col = lambda c: (lambda b, i: (b, i, c))
    fixed2 = lambda b, i: (0, 0)
    fixed3 = lambda b, i: (0, 0, 0)
    tok_bf = jax.ShapeDtypeStruct((bsz, seq, w), BF16)
    return pl.pallas_call(
        _cd_prep_kernel,
        grid=(bsz, seq // tl),
        in_specs=[pl.BlockSpec((None, tl, w), col(0)), pl.BlockSpec((None, tl, w), col(2)),
                  pl.BlockSpec((None, tl, w), col(3)), pl.BlockSpec((None, tl, 1), col(0)),
                  pl.BlockSpec((CONV_WIDTH, 1, w), fixed3), pl.BlockSpec((1, w), fixed2),
                  pl.BlockSpec((w, w), fixed2), pl.BlockSpec((w, w), fixed2), pl.BlockSpec((w, w), fixed2),
                  pl.BlockSpec((3, w, LANES), fixed3), pl.BlockSpec((1, LANES), fixed2),
                  pl.BlockSpec((1, LANES), fixed2)],
        out_specs=[pl.BlockSpec((None, tl, w), col(0))] * 3 + [pl.BlockSpec((None, tl, LANES), col(0))]
                  + [pl.BlockSpec((None, tl, w), col(0))] * 2,
        out_shape=[tok_bf, tok_bf, tok_bf, jax.ShapeDtypeStruct((bsz, seq, LANES), F32),
                   jax.ShapeDtypeStruct((bsz, seq, w), F32), jax.ShapeDtypeStruct((bsz, seq, w), F32)],
        scratch_shapes=[pltpu.VMEM((8, w), F32)],
        compiler_params=_cparams("arbitrary", "arbitrary"),
        name="cd_prep",
    )(proj, proj, proj, positions.reshape(bsz, seq, 1), conv_w.reshape(CONV_WIDTH, 1, w),
      conv_b.reshape(1, w), _block_diag(w_q).astype(BF16), _block_diag(w_k).astype(BF16),
      _block_diag(w_v).astype(BF16), wif.astype(BF16), bif, invf)


def _cumsum_rows(x):
    n = x.shape[0]
    row = lax.broadcasted_iota(jnp.int32, x.shape, 0)
    shift = 1
    while shift < n:
        x = x + jnp.where(row >= shift, pltpu.roll(x, shift, axis=0), 0.0)
        shift *= 2
    return x


def _mlstm_kernel(q_ref, k_ref, v_ref, g_ref, op_ref, o_ref, c_ref, n_ref, m_ref):
    cs = q_ref.shape[0]
    dh = MLSTM_HEAD_DIM

    @pl.when(pl.program_id(1) == 0)
    def _init():
        c_ref[...] = jnp.zeros_like(c_ref)
        n_ref[...] = jnp.zeros_like(n_ref)
        m_ref[...] = jnp.full_like(m_ref, NEG_INF)

    gates = g_ref[...]
    cum = _cumsum_rows(jax.nn.log_sigmoid(gates))
    gates_t = gates.T
    cum_t = cum.T
    causal = (lax.broadcasted_iota(jnp.int32, (cs, cs), 0) >= lax.broadcasted_iota(jnp.int32, (cs, cs), 1))
    for h in range(MLSTM_HEADS):
        sl = slice(h * dh, (h + 1) * dh)
        fcol = MLSTM_HEADS + h
        li_c, li_r = gates[:, h:h + 1], gates_t[h:h + 1, :]
        cum_c, cum_r = cum[:, fcol:fcol + 1], cum_t[fcol:fcol + 1, :]
        m_prev = m_ref[h][:, 0:1]
        q, k, v = q_ref[:, sl], k_ref[:, sl], v_ref[:, sl]
        log_d = jnp.where(causal, cum_c - cum_r + li_r, NEG_INF)
        log_inter = cum_c + m_prev
        m_s = jnp.maximum(jnp.max(log_d, axis=1, keepdims=True), log_inter)
        s = _dot_nt(q, k) * jnp.exp(log_d - m_s)
        inter = jnp.exp(log_inter - m_s)
        c_prev = c_ref[h]
        n_prev = n_ref[h]
        num = _dot(s.astype(BF16), v) + inter * _dot(q, c_prev.astype(BF16))
        den = (jnp.sum(s, axis=1, keepdims=True)
               + inter * jnp.sum(q.astype(F32) * n_prev, axis=1, keepdims=True))
        hh = num / jnp.maximum(jnp.abs(den), jnp.exp(-m_s))
        o_ref[:, sl] = (jax.nn.sigmoid(op_ref[:, sl]) * hh).astype(o_ref.dtype)
        chunk_f = cum_c[cs - 1:cs, :]
        to_end = chunk_f - cum_c + li_c
        m_c = jnp.max(to_end, axis=0, keepdims=True)
        m_new = jnp.maximum(chunk_f + m_prev, m_c)
        s_old = jnp.exp(chunk_f + m_prev - m_new)
        s_new = jnp.exp(m_c - m_new)
        kw = k.astype(F32) * jnp.exp(to_end - m_c)
        c_ref[h] = s_old * c_prev + s_new * _dot(kw.T.astype(BF16), v)
        n_ref[h] = s_old * n_prev + s_new * jnp.sum(kw, axis=0, keepdims=True)
        m_ref[h] = jnp.broadcast_to(m_new, (1, LANES))


def _mlstm(mq, mk, mv, gates, proj):
    bsz, seq, w = mq.shape
    cs = MLSTM_CS
    tok = lambda b, i: (b, i, 0)
    return pl.pallas_call(
        _mlstm_kernel,
        grid=(bsz, seq // cs),
        in_specs=[pl.BlockSpec((None, cs, w), tok)] * 3
                 + [pl.BlockSpec((None, cs, LANES), tok), pl.BlockSpec((None, cs, w), lambda b, i: (b, i, 1))],
        out_specs=pl.BlockSpec((None, cs, w), tok),
        out_shape=jax.ShapeDtypeStruct((bsz, seq, w), BF16),
        scratch_shapes=[pltpu.VMEM((MLSTM_HEADS, MLSTM_HEAD_DIM, MLSTM_HEAD_DIM), F32),
                        pltpu.VMEM((MLSTM_HEADS, 1, MLSTM_HEAD_DIM), F32),
                        pltpu.VMEM((MLSTM_HEADS, 1, LANES), F32)],
        compiler_params=_cparams("arbitrary", "arbitrary"),
        name="mlstm",
    )(mq, mk, mv, gates, proj)


def _dsw_kernel(q_ref, k_ref, v_ref, y_ref, o_scr, lse_scr):
    seq, dh = q_ref.shape
    blk = DSW_BLOCK
    qi = lax.broadcasted_iota(jnp.int32, (blk, 2 * blk), 0)
    ki = lax.broadcasted_iota(jnp.int32, (blk, 2 * blk), 1)
    dist = qi + blk - ki
    scale = dh ** -0.5
    units = seq // blk
    for g, (window, dil) in enumerate(DSW_CONFIGS):
        nsub = seq // dil // blk
        band = (dist >= 0) & (dist <= window // dil)

        def unit(u, carry, g=g, dil=dil, nsub=nsub, band=band):
            r = u // nsub
            sb = u % nsub
            cur = pl.ds(r + sb * (blk * dil), blk, stride=dil)
            prev = pl.ds(r + jnp.maximum(sb - 1, 0) * (blk * dil), blk, stride=dil)
            kk = jnp.concatenate([k_ref[prev, :], k_ref[cur, :]], axis=0).astype(BF16)
            vv = jnp.concatenate([v_ref[prev, :], v_ref[cur, :]], axis=0).astype(BF16)
            s = _dot_nt(q_ref[cur, :].astype(BF16), kk) * scale
            s = jnp.where(band & ((ki >= blk) | (sb > 0)), s, NEG_INF)
            m = jnp.max(s, axis=1, keepdims=True)
            p = jnp.exp(s - m)
            l = jnp.sum(p, axis=1, keepdims=True)
            o_scr[g, cur, :] = _dot(p.astype(BF16), vv) / l
            lse_scr[g, cur, :] = jnp.broadcast_to(m + jnp.log(l), (blk, dh))
            return carry

        lax.fori_loop(0, units, unit, 0, unroll=16)

    lses = [lse_scr[g] for g in range(len(DSW_CONFIGS))]
    mx = functools.reduce(jnp.maximum, lses)
    ws = [jnp.exp(l - mx) for l in lses]
    acc = sum(ws[g] * o_scr[g] for g in range(len(DSW_CONFIGS)))
    y_ref[...] = (acc / sum(ws)).astype(y_ref.dtype)


def _dsw_attention(qd, kd, proj):
    bsz, seq, w = qd.shape
    dh = DSW_HEAD_DIM
    v_col = (proj.shape[-1] - w) // dh
    head = lambda b, h: (b, 0, h)
    return pl.pallas_call(
        _dsw_kernel,
        grid=(bsz, DSW_HEADS),
        in_specs=[pl.BlockSpec((None, seq, dh), head), pl.BlockSpec((None, seq, dh), head),
                  pl.BlockSpec((None, seq, dh), lambda b, h: (b, 0, v_col + h))],
        out_specs=pl.BlockSpec((None, seq, dh), head),
        out_shape=jax.ShapeDtypeStruct((bsz, seq, w), BF16),
        scratch_shapes=[pltpu.VMEM((len(DSW_CONFIGS), seq, dh), F32),
                        pltpu.VMEM((len(DSW_CONFIGS), seq, dh), F32)],
        compiler_params=_cparams("arbitrary", "arbitrary"),
        name="dsw_attention",
    )(qd, kd, proj)


def kernel(x, positions, ab_w_in, ab_conv_w, ab_conv_b, ab_gate_r_w, ab_gate_r_b, ab_gate_i_w, ab_gate_i_b, ab_lru_lambda, ab_s5_lambda_re, ab_s5_lambda_im, ab_s5_log_step, ab_s5_b_re, ab_s5_b_im, ab_s5_c_re, ab_s5_c_im, ab_s5_d, ab_glu_w, ab_glu_b, ab_w_out, cd_w_in, cd_conv_w, cd_conv_b, cd_w_q, cd_w_k, cd_w_v, cd_w_if, cd_b_if, cd_w_out, ln_mix_g, ln_mix_b, ln_ffn_g, ln_ffn_b, moe_router_w, moe_router_b, moe_w_in, moe_b_in, moe_w_out, moe_b_out):
    bsz, seq, d = x.shape
    t = bsz * seq
    for layer in range(DEPTH):
        j = layer // 2
        if layer % 2 == 0:
            proj = _inproj_time_major(x, ab_w_in[j].astype(BF16))
            ya = _rg_lru(proj, ab_conv_w[j], ab_conv_b[j], ab_gate_r_w[j], ab_gate_r_b[j],
                         ab_gate_i_w[j], ab_gate_i_b[j], ab_lru_lambda[j])
            yb = _s5_glu(proj, ab_s5_lambda_re[j], ab_s5_lambda_im[j], ab_s5_log_step[j], ab_s5_b_re[j],
                         ab_s5_b_im[j], ab_s5_c_re[j], ab_s5_c_im[j], ab_s5_d[j], ab_glu_w[j], ab_glu_b[j])
            time_major, w_out = True, ab_w_out[j]
        else:
            proj = _inproj(x, cd_w_in[j].astype(BF16))
            mq, mk, mv, gates, qd, kd = _cd_prep(proj, positions, cd_conv_w[j], cd_conv_b[j], cd_w_q[j],
                                                 cd_w_k[j], cd_w_v[j], cd_w_if[j], cd_b_if[j])
            ya = _mlstm(mq, mk, mv, gates, proj)
            yb = _dsw_attention(qd, kd, proj)
            time_major, w_out = False, cd_w_out[j]
        xn, x_tiles, route, counts = _outproj_ln_router(
            ya, yb, time_major, w_out.astype(BF16), x, ln_mix_g[layer], ln_mix_b[layer],
            moe_router_w[layer], moe_router_b[layer])
        x = _moe_ffn_ln(xn.reshape(t, d), x_tiles, route, counts, layer, moe_w_in, moe_b_in,
                        moe_w_out, moe_b_out, ln_ffn_g[layer], ln_ffn_b[layer]).reshape(bsz, seq, d)
    return x
```

```python
import functools

import jax
import jax.numpy as jnp
from jax import lax
from jax.experimental import pallas as pl
from jax.experimental.pallas import tpu as pltpu

F32 = jnp.float32
BF16 = jnp.bfloat16

D_MODEL = 1024
DEPTH = 2
A_WIDTH = 512
B_WIDTH = 512
C_WIDTH = 512
D_WIDTH = 512
LRU_HEADS = 8
LRU_HEAD_DIM = A_WIDTH // LRU_HEADS
LRU_C = 8.0
CONV_WIDTH = 4
S5_GROUP_CH = 16
S5_GROUPS = B_WIDTH // S5_GROUP_CH
S5_STATE = 64
MLSTM_HEADS = 4
MLSTM_HEAD_DIM = C_WIDTH // MLSTM_HEADS
DSW_HEADS = 4
DSW_HEAD_DIM = D_WIDTH // DSW_HEADS
DSW_CONFIGS = ((128, 1), (512, 4), (2048, 16))
DSW_BLOCK = 128
ROPE_THETA = 500000.0
ROPE_DIMS = DSW_HEAD_DIM // 4
N_EXPERTS = 32
TOP_K = 4
D_EXPERT = D_MODEL
SWIGLU_LIMIT = 7.0
SWIGLU_ALPHA = 1.702
DEEPNORM_ALPHA = (2 * DEPTH) ** 0.25
LN_EPS = 1e-5

VMEM_LIMIT_BYTES = 56 * 1024 * 1024
LANES = 128
TILE_ROWS = D_MODEL // LANES
MOE_TM = 512
MOE_TK = 256
PROJ_TL = 512
ROUTE_ROWS = 16
LRU_TC = 128
S5_TC = 128
S5_HALF = 2
MLSTM_CS = 256
NEG_INF = float("-inf")


def _cparams(*sem):
    return pltpu.CompilerParams(dimension_semantics=sem, vmem_limit_bytes=VMEM_LIMIT_BYTES)


def _dot(a, b):
    return jnp.dot(a, b, preferred_element_type=F32)


def _dot_nt(a, b):
    return lax.dot_general(a, b, (((1,), (1,)), ((), ())), preferred_element_type=F32)


def _matmul_kernel(x_ref, w_ref, o_ref):
    o_ref[...] = _dot(x_ref[...].astype(BF16), w_ref[...])


def _matmul_time_major_kernel(x_ref, w_ref, o_ref, scr):
    bsz, tl, _ = x_ref.shape
    nchunk = scr.shape[0]
    for b in range(bsz):
        p = _dot(x_ref[b].astype(BF16), w_ref[...])
        for c in range(nchunk):
            scr[c, pl.ds(b, tl, stride=bsz), :] = p[:, c * LANES:(c + 1) * LANES]
    for c in range(nchunk):
        o_ref[:, c * LANES:(c + 1) * LANES] = scr[c]


def _inproj_time_major(x3, w_bf16):
    bsz, seq, k = x3.shape
    n = w_bf16.shape[1]
    tl = LRU_TC
    out = pl.pallas_call(
        _matmul_time_major_kernel,
        grid=(seq // tl,),
        in_specs=[pl.BlockSpec((bsz, tl, k), lambda i: (0, i, 0)),
                  pl.BlockSpec((k, n), lambda i: (0, 0))],
        out_specs=pl.BlockSpec((tl * bsz, n), lambda i: (i, 0)),
        out_shape=jax.ShapeDtypeStruct((seq * bsz, n), F32),
        scratch_shapes=[pltpu.VMEM((n // LANES, tl * bsz, LANES), F32)],
        compiler_params=_cparams("arbitrary"),
        name="in_proj_time_major",
    )(x3, w_bf16)
    return out.reshape(seq, bsz, n)


def _inproj(x3, w_bf16):
    bsz, seq, k = x3.shape
    n = w_bf16.shape[1]
    return pl.pallas_call(
        _matmul_kernel,
        grid=(bsz, seq // PROJ_TL),
        in_specs=[pl.BlockSpec((None, PROJ_TL, k), lambda b, i: (b, i, 0)),
                  pl.BlockSpec((k, n), lambda b, i: (0, 0))],
        out_specs=pl.BlockSpec((None, PROJ_TL, n), lambda b, i: (b, i, 0)),
        out_shape=jax.ShapeDtypeStruct((bsz, seq, n), F32),
        compiler_params=_cparams("arbitrary", "arbitrary"),
        name="in_proj",
    )(x3, w_bf16)


def _layer_norm_rows(z, g, b):
    mu = jnp.mean(z, axis=-1, keepdims=True)
    zc = z - mu
    var = jnp.mean(zc * zc, axis=-1, keepdims=True)
    return zc * lax.rsqrt(var + LN_EPS) * g + b


def _outproj_ln_router_kernel(ya_ref, yb_ref, wa_ref, wb_ref, x_ref, g_ref, b_ref, rwh_ref, rwl_ref, rb_ref,
                              tri_ref, xn_ref, xt_ref, rt_ref, cnt_ref):
    @pl.when((pl.program_id(0) == 0) & (pl.program_id(1) == 0))
    def _init():
        cnt_ref[...] = jnp.zeros_like(cnt_ref)

    mix = _dot(ya_ref[...], wa_ref[...]) + _dot(yb_ref[...], wb_ref[...])
    xn = _layer_norm_rows(DEEPNORM_ALPHA * x_ref[...] + mix, g_ref[...], b_ref[...])
    xn_ref[...] = xn
    _store_token_tiles(xt_ref, xn)
    xh = xn.astype(BF16)
    xl = (xn - xh.astype(F32)).astype(BF16)
    lg = (_dot_nt(rwh_ref[...], xh) + _dot_nt(rwh_ref[...], xl) + _dot_nt(rwl_ref[...], xh)) + rb_ref[...]
    ne, tl = lg.shape
    row = lax.broadcasted_iota(jnp.int32, (ne, tl), 0).astype(F32)
    tops, ids, hots = [], [], []
    for _ in range(TOP_K):
        m = jnp.max(lg, axis=0, keepdims=True)
        idx = jnp.min(jnp.where(lg == m, row, float(ne)), axis=0, keepdims=True)
        hot = row == idx
        lg = jnp.where(hot, NEG_INF, lg)
        tops.append(m)
        ids.append(idx)
        hots.append(hot)
    exps = [jnp.exp(m - tops[0]) for m in tops]
    denom = functools.reduce(jnp.add, exps)
    chosen = functools.reduce(jnp.add, [h.astype(F32) for h in hots])
    before = _dot(chosen.astype(BF16), tri_ref[...]) + cnt_ref[...]
    out_row = lax.broadcasted_iota(jnp.int32, rt_ref.shape, 0)
    route = jnp.zeros(rt_ref.shape, F32)
    for k in range(TOP_K):
        rank = jnp.sum(jnp.where(hots[k], before, 0.0), axis=0, keepdims=True)
        route = jnp.where(out_row == k, ids[k], route)
        route = jnp.where(out_row == TOP_K + k, rank, route)
        route = jnp.where(out_row == 2 * TOP_K + k, exps[k] / denom, route)
    rt_ref[...] = route
    cnt_ref[...] = cnt_ref[...] + jnp.sum(chosen, axis=1, keepdims=True)


def _outproj_ln_router(ya, yb, time_major, w_bf16, x3, g, b, rw, rb):
    bsz, seq, d = x3.shape
    wa, wb = w_bf16[:ya.shape[-1]], w_bf16[ya.shape[-1]:]
    ne = N_EXPERTS
    rw = rw.T
    rw_hi = rw.astype(BF16)
    rw_lo = (rw - rw_hi.astype(F32)).astype(BF16)
    tl = PROJ_TL
    tri = jnp.triu(jnp.ones((tl, tl), BF16), 1)
    if time_major:
        ya, yb = ya.reshape(seq, -1), yb.reshape(seq, -1)
        y_specs = [pl.BlockSpec((tl, wa.shape[0]), lambda bi, i: (i, bi)),
                   pl.BlockSpec((tl, wb.shape[0]), lambda bi, i: (i, bi))]
    else:
        y_specs = [pl.BlockSpec((None, tl, wa.shape[0]), lambda bi, i: (bi, i, 0)),
                   pl.BlockSpec((None, tl, wb.shape[0]), lambda bi, i: (bi, i, 0))]
    fixed = lambda bi, i: (0, 0)
    tok = lambda bi, i: (bi, i, 0)
    return pl.pallas_call(
        _outproj_ln_router_kernel,
        grid=(bsz, seq // tl),
        in_specs=y_specs + [pl.BlockSpec(wa.shape, fixed), pl.BlockSpec(wb.shape, fixed),
                            pl.BlockSpec((None, tl, d), tok),
                            pl.BlockSpec((1, d), fixed), pl.BlockSpec((1, d), fixed),
                            pl.BlockSpec((ne, d), fixed), pl.BlockSpec((ne, d), fixed),
                            pl.BlockSpec((ne, 1), fixed), pl.BlockSpec((tl, tl), fixed)],
        out_specs=[pl.BlockSpec((None, tl, d), tok),
                   pl.BlockSpec((tl * TILE_ROWS, LANES), lambda bi, i: (bi * (seq // tl) + i, 0)),
                   pl.BlockSpec((ROUTE_ROWS, tl), lambda bi, i: (0, bi * (seq // tl) + i)),
                   pl.BlockSpec((ne, 1), fixed)],
        out_shape=[jax.ShapeDtypeStruct((bsz, seq, d), F32),
                   jax.ShapeDtypeStruct((bsz * seq * TILE_ROWS, LANES), F32),
                   jax.ShapeDtypeStruct((ROUTE_ROWS, bsz * seq), F32), jax.ShapeDtypeStruct((ne, 1), F32)],
        compiler_params=_cparams("arbitrary", "arbitrary"),
        name="out_proj_ln_router",
    )(ya, yb, wa, wb, x3, g.reshape(1, d), b.reshape(1, d), rw_hi, rw_lo, rb.reshape(ne, 1), tri)


def _store_token_tiles(dst_ref, x):
    n = x.shape[0]
    for j in range(TILE_ROWS):
        dst_ref[pl.ds(j, n, stride=TILE_ROWS), :] = x[:, j * LANES:(j + 1) * LANES]


def _load_token_tiles(src_ref, n):
    return jnp.concatenate([src_ref[pl.ds(j, n, stride=TILE_ROWS), :] for j in range(TILE_ROWS)], axis=1)


def _gather_token_tiles(src_hbm, idx_ref, n, dst, sem, priorities):
    def body(g, carry):
        base = pl.multiple_of(g * (TILE_ROWS * TILE_ROWS), TILE_ROWS * TILE_ROWS)
        for u in range(TILE_ROWS):
            src_row = pl.multiple_of(idx_ref[0, g * TILE_ROWS + u], TILE_ROWS)
            pltpu.make_async_copy(src_hbm.at[pl.ds(src_row, TILE_ROWS)],
                                  dst.at[pl.ds(base + u * TILE_ROWS, TILE_ROWS)], sem).start(priority=u % priorities)
        return carry
    lax.fori_loop(0, n // TILE_ROWS, body, 0, unroll=2)


def _moe_kernel(be_ref, nreal_ref, rt_ref, rtn_ref, x_hbm, rg_ref, win_ref, bin_ref, wout_ref, bout_ref, o_ref,
                win_bf, wout_bf, xbuf, sem):
    i = pl.program_id(0)
    n_real = nreal_ref[0]
    e = be_ref[i]
    prev = be_ref[jnp.maximum(i - 1, 0)]
    tm = xbuf.shape[1] // TILE_ROWS
    slot = lax.rem(i, 2)

    @pl.when(i == 0)
    def _first_gather():
        _gather_token_tiles(x_hbm, rt_ref, tm, xbuf.at[0], sem.at[0], priorities=1)

    @pl.when(i + 1 < n_real)
    def _next_gather():
        _gather_token_tiles(x_hbm, rtn_ref, tm, xbuf.at[1 - slot], sem.at[1 - slot], priorities=1)

    @pl.when((i == 0) | (e != prev))
    def _load_expert():
        win_bf[...] = win_ref[...].astype(BF16)
        wout_bf[...] = wout_ref[...].astype(BF16)

    @pl.when((i < n_real) | (i == 0))
    def _compute():
        pltpu.make_async_copy(xbuf.at[slot], xbuf.at[slot], sem.at[slot]).wait()
        x = _load_token_tiles(xbuf.at[slot], tm).astype(BF16)
        h = _dot(x, win_bf[...]) + bin_ref[...]
        g = jnp.minimum(h[:, :D_EXPERT], SWIGLU_LIMIT)
        lin = jnp.clip(h[:, D_EXPERT:], -SWIGLU_LIMIT, SWIGLU_LIMIT)
        y = g * jax.nn.sigmoid(SWIGLU_ALPHA * g) * (lin + 1.0)
        out = _dot(y.astype(BF16), wout_bf[...]) + bout_ref[...]
        gates = rg_ref[...].T
        out = jnp.concatenate([out[j * LANES:(j + 1) * LANES] * gates[:, j:j + 1] for j in range(tm // LANES)],
                              axis=0)
        _store_token_tiles(o_ref, out)

    @pl.when((i >= n_real) & (i > 0))
    def _unused_block():
        o_ref[...] = jnp.zeros_like(o_ref)


def _moe_experts(x_tiles, row_tok, row_gate, block_expert, n_real, layer, w_in, b_in, w_out, b_out):
    d = D_MODEL
    tm = MOE_TM
    n_blocks = row_tok.shape[0]
    nl, ne, _, dh2 = w_in.shape
    rt = (row_tok * TILE_ROWS).reshape(n_blocks, 1, tm)
    rg = jnp.pad(row_gate.reshape(n_blocks, tm // LANES, LANES), ((0, 0), (0, TILE_ROWS - tm // LANES), (0, 0)))
    smem_rows = lambda imap: pl.BlockSpec((None, 1, tm), imap, memory_space=pltpu.SMEM)
    grid_spec = pltpu.PrefetchScalarGridSpec(
        num_scalar_prefetch=2,
        grid=(n_blocks,),
        in_specs=[
            smem_rows(lambda i, be, nr: (i, 0, 0)),
            smem_rows(lambda i, be, nr: (jnp.minimum(i + 1, n_blocks - 1), 0, 0)),
            pl.BlockSpec(memory_space=pl.ANY),
            pl.BlockSpec((None, TILE_ROWS, LANES), lambda i, be, nr: (i, 0, 0)),
            pl.BlockSpec((None, None, d, dh2), lambda i, be, nr: (layer, be[i], 0, 0)),
            pl.BlockSpec((None, None, 1, dh2), lambda i, be, nr: (layer, be[i], 0, 0)),
            pl.BlockSpec((None, None, D_EXPERT, d), lambda i, be, nr: (layer, be[i], 0, 0)),
            pl.BlockSpec((None, None, 1, d), lambda i, be, nr: (layer, be[i], 0, 0)),
        ],
        out_specs=pl.BlockSpec((tm * TILE_ROWS, LANES), lambda i, be, nr: (i, 0)),
        scratch_shapes=[pltpu.VMEM((d, dh2), BF16), pltpu.VMEM((D_EXPERT, d), BF16),
                        pltpu.VMEM((2, tm * TILE_ROWS, LANES), F32), pltpu.SemaphoreType.DMA((2,))],
    )
    return pl.pallas_call(
        _moe_kernel,
        grid_spec=grid_spec,
        out_shape=jax.ShapeDtypeStruct((n_blocks * tm * TILE_ROWS, LANES), F32),
        compiler_params=_cparams("arbitrary"),
        name="moe_experts",
    )(block_expert, n_real, rt, rt, x_tiles, rg, w_in,
      b_in.reshape(nl, ne, 1, dh2), w_out, b_out.reshape(nl, ne, 1, d))


def _moe_combine_kernel(dc_ref, dn_ref, x_ref, yb_hbm, g_ref, b_ref, o_ref, buf, ffn_ref, sem):
    i = pl.program_id(0)
    n = pl.num_programs(0)
    tk = x_ref.shape[0]
    rows = tk * TILE_ROWS
    slot = lax.rem(i, 2)

    @pl.when(i == 0)
    def _first_gather():
        _gather_token_tiles(yb_hbm, dc_ref, TOP_K * tk, buf.at[0], sem.at[0], priorities=2)

    @pl.when(i + 1 < n)
    def _next_gather():
        _gather_token_tiles(yb_hbm, dn_ref, TOP_K * tk, buf.at[1 - slot], sem.at[1 - slot], priorities=2)

    pltpu.make_async_copy(buf.at[slot], buf.at[slot], sem.at[slot]).wait()
    ffn = buf[slot, 0:rows]
    for k in range(1, TOP_K):
        ffn = ffn + buf[slot, k * rows:(k + 1) * rows]
    ffn_ref[...] = ffn
    o_ref[...] = _layer_norm_rows(DEEPNORM_ALPHA * x_ref[...] + _load_token_tiles(ffn_ref, tk),
                                  g_ref[...], b_ref[...])


def _moe_combine_ln(xn, dest, yb_tiles, g, b):
    t, d = xn.shape
    tk = MOE_TK
    n = t // tk
    dest_tiles = (dest * TILE_ROWS).reshape(n, tk, TOP_K).transpose(0, 2, 1).reshape(n, 1, TOP_K * tk)
    smem_rows = lambda imap: pl.BlockSpec((None, 1, TOP_K * tk), imap, memory_space=pltpu.SMEM)
    row = lambda i: (i, 0)
    fixed = lambda i: (0, 0)
    return pl.pallas_call(
        _moe_combine_kernel,
        grid=(n,),
        in_specs=[smem_rows(lambda i: (i, 0, 0)), smem_rows(lambda i: (jnp.minimum(i + 1, n - 1), 0, 0)),
                  pl.BlockSpec((tk, d), row), pl.BlockSpec(memory_space=pl.ANY),
                  pl.BlockSpec((1, d), fixed), pl.BlockSpec((1, d), fixed)],
        out_specs=pl.BlockSpec((tk, d), row),
        out_shape=jax.ShapeDtypeStruct((t, d), F32),
        scratch_shapes=[pltpu.VMEM((2, TOP_K * tk * TILE_ROWS, LANES), F32),
                        pltpu.VMEM((tk * TILE_ROWS, LANES), F32), pltpu.SemaphoreType.DMA((2,))],
        compiler_params=_cparams("arbitrary"),
        name="moe_combine_ln",
    )(dest_tiles, dest_tiles, xn, yb_tiles, g.reshape(1, d), b.reshape(1, d))


def _moe_ffn_ln(xn, x_tiles, route, counts, layer, w_in, b_in, w_out, b_out, g, b):
    t, d = xn.shape
    tm = MOE_TM
    flat_e = route[:TOP_K].astype(jnp.int32).T.reshape(-1)
    rank = route[TOP_K:2 * TOP_K].astype(jnp.int32).T.reshape(-1)
    gate = route[2 * TOP_K:3 * TOP_K].T.reshape(-1)
    counts = counts[:, 0].astype(jnp.int32)
    blocks_per_e = (counts + tm - 1) // tm
    blk_end = jnp.cumsum(blocks_per_e)
    blk_start = blk_end - blocks_per_e
    row_start = jnp.cumsum(counts) - counts
    dest = blk_start[flat_e] * tm + rank
    n_blocks = (t * TOP_K) // tm + N_EXPERTS
    block_expert = jnp.minimum(
        jnp.sum((blk_end[None, :] <= jnp.arange(n_blocks, dtype=jnp.int32)[:, None]).astype(jnp.int32), axis=1),
        N_EXPERTS - 1)
    n_real = blk_end[-1:].astype(jnp.int32)
    order = jnp.argsort(flat_e, stable=True).astype(jnp.int32)
    blk = jnp.arange(n_blocks, dtype=jnp.int32)
    rank_row = ((blk - blk_start[block_expert]) * tm)[:, None] + jnp.arange(tm, dtype=jnp.int32)[None, :]
    valid = rank_row < counts[block_expert][:, None]
    pair = order[jnp.clip(row_start[block_expert][:, None] + rank_row, 0, t * TOP_K - 1)]
    row_tok = jnp.where(valid, pair // TOP_K, 0)
    row_gate = jnp.where(valid, gate[pair], 0.0)
    yb_tiles = _moe_experts(x_tiles, row_tok, row_gate, block_expert, n_real, layer, w_in, b_in, w_out, b_out)
    return _moe_combine_ln(xn, dest, yb_tiles, g, b)


def _softplus(x):
    return jnp.maximum(x, 0.0) + jnp.log(1.0 + jnp.exp(-jnp.abs(x)))


def _gelu_tanh(x):
    return 0.5 * x * (1.0 + jnp.tanh(0.7978845608028654 * (x + 0.044715 * (x * x * x))))


def _lru_kernel(xa_ref, ga_ref, cw_ref, cb_ref, wr_ref, br_ref, wi_ref, bi_ref, lam_ref, o_ref,
                tail_ref, h_ref, a_ref, u_ref):
    tc, bsz, w = xa_ref.shape

    @pl.when(pl.program_id(0) == 0)
    def _init():
        tail_ref[...] = jnp.zeros_like(tail_ref)
        h_ref[...] = jnp.zeros_like(h_ref)

    xa = xa_ref[...]
    ext = jnp.concatenate([tail_ref[...], xa], axis=0)
    xc = cb_ref[...] + ext[0:tc] * cw_ref[0]
    for tap in range(1, CONV_WIDTH):
        xc = xc + ext[tap:tap + tc] * cw_ref[tap]
    tail_ref[...] = xa[tc - (CONV_WIDTH - 1):]
    x2 = xc.reshape(tc * bsz, w)
    xb = x2.astype(BF16)
    r = jax.nn.sigmoid(_dot(xb, wr_ref[...]) + br_ref[...])
    ig = jax.nn.sigmoid(_dot(xb, wi_ref[...]) + bi_ref[...])
    log_a = (-LRU_C) * r * _softplus(-lam_ref[...])
    a_ref[...] = jnp.exp(log_a).reshape(tc, bsz, w)
    u_ref[...] = (jnp.sqrt(1.0 - jnp.exp(2.0 * log_a)) * (ig * x2)).reshape(tc, bsz, w)

    def step(t, h):
        h = a_ref[t] * h + u_ref[t]
        u_ref[t] = h
        return h

    h_ref[...] = lax.fori_loop(0, tc, step, h_ref[...], unroll=8)
    o_ref[...] = (u_ref[...] * _gelu_tanh(ga_ref[...])).astype(o_ref.dtype)


def _block_diag(w):
    h, i, j = w.shape
    return jnp.einsum('hij,hk->hikj', w, jnp.eye(h, dtype=w.dtype)).reshape(h * i, h * j)


def _rg_lru(proj, conv_w, conv_b, gate_r_w, gate_r_b, gate_i_w, gate_i_b, lru_lambda):
    seq, bsz, _ = proj.shape
    w = A_WIDTH
    tc = LRU_TC
    fixed2 = lambda i: (0, 0)
    fixed3 = lambda i: (0, 0, 0)
    return pl.pallas_call(
        _lru_kernel,
        grid=(seq // tc,),
        in_specs=[pl.BlockSpec((tc, bsz, w), lambda i: (i, 0, 0)),
                  pl.BlockSpec((tc, bsz, w), lambda i: (i, 0, 1)),
                  pl.BlockSpec((CONV_WIDTH, 1, w), fixed3), pl.BlockSpec((1, w), fixed2),
                  pl.BlockSpec((w, w), fixed2), pl.BlockSpec((1, w), fixed2),
                  pl.BlockSpec((w, w), fixed2), pl.BlockSpec((1, w), fixed2),
                  pl.BlockSpec((1, w), fixed2)],
        out_specs=pl.BlockSpec((tc, bsz, w), lambda i: (i, 0, 0)),
        out_shape=jax.ShapeDtypeStruct((seq, bsz, w), BF16),
        scratch_shapes=[pltpu.VMEM((CONV_WIDTH - 1, bsz, w), F32), pltpu.VMEM((bsz, w), F32),
                        pltpu.VMEM((tc, bsz, w), F32), pltpu.VMEM((tc, bsz, w), F32)],
        compiler_params=_cparams("arbitrary"),
        name="rg_lru",
    )(proj, proj, conv_w.reshape(CONV_WIDTH, 1, w), conv_b.reshape(1, w),
      _block_diag(gate_r_w).astype(BF16), gate_r_b.reshape(1, w),
      _block_diag(gate_i_w).astype(BF16), gate_i_b.reshape(1, w), lru_lambda.reshape(1, w))


def _s5_kernel(u_ref, wbr_ref, wbi_ref, ar_ref, ai_ref, ccr_ref, cci_ref, d_ref, gw_ref, gb_ref, o_ref,
               xr_ref, xi_ref, sr_ref, si_ref):
    tc, bsz, w = u_ref.shape
    nstate = xr_ref.shape[-1]
    sh = nstate // S5_HALF
    wh = w // S5_HALF

    @pl.when(pl.program_id(0) == 0)
    def _init():
        sr_ref[...] = jnp.zeros_like(sr_ref)
        si_ref[...] = jnp.zeros_like(si_ref)

    u2 = u_ref[...].reshape(tc * bsz, w)
    ub = u2.astype(BF16)
    for hf in range(S5_HALF):
        uh = ub[:, hf * wh:(hf + 1) * wh]
        xr_ref[:, :, hf * sh:(hf + 1) * sh] = _dot(uh, wbr_ref[hf]).reshape(tc, bsz, sh)
        xi_ref[:, :, hf * sh:(hf + 1) * sh] = _dot(uh, wbi_ref[hf]).reshape(tc, bsz, sh)

    for hf in range(S5_HALF):
        lo, hi = hf * sh, (hf + 1) * sh
        ar = jnp.broadcast_to(ar_ref[:, lo:hi], (bsz, sh))
        ai = jnp.broadcast_to(ai_ref[:, lo:hi], (bsz, sh))

        def step(t, carry, lo=lo, hi=hi, ar=ar, ai=ai):
            xr, xi = carry
            nxr = ar * xr - ai * xi + xr_ref[t, :, lo:hi]
            nxi = ar * xi + ai * xr + xi_ref[t, :, lo:hi]
            xr_ref[t, :, lo:hi] = nxr
            xi_ref[t, :, lo:hi] = nxi
            return nxr, nxi

        xr, xi = lax.fori_loop(0, tc, step, (sr_ref[:, lo:hi], si_ref[:, lo:hi]), unroll=4)
        sr_ref[:, lo:hi] = xr
        si_ref[:, lo:hi] = xi

    ys = []
    for hf in range(S5_HALF):
        lo, hi = hf * sh, (hf + 1) * sh
        xrh = xr_ref[:, :, lo:hi].reshape(tc * bsz, sh).astype(BF16)
        xih = xi_ref[:, :, lo:hi].reshape(tc * bsz, sh).astype(BF16)
        ys.append(_dot(xrh, ccr_ref[hf]) - _dot(xih, cci_ref[hf]))
    y = jnp.concatenate(ys, axis=1) + d_ref[...] * u2
    s = _gelu_tanh(y)
    yb = s * jax.nn.sigmoid(_dot(s.astype(BF16), gw_ref[...]) + gb_ref[...])
    o_ref[...] = yb.reshape(tc, bsz, w).astype(o_ref.dtype)


def _s5_glu(proj, lam_re, lam_im, log_step, b_re, b_im, c_re, c_im, d_skip, glu_w, glu_b):
    seq, bsz, _ = proj.shape
    w = B_WIDTH
    tc = S5_TC
    nstate = S5_GROUPS * S5_STATE
    gh = S5_GROUPS // S5_HALF
    lr = jnp.minimum(lam_re, -1e-4)
    li = lam_im
    step = jnp.exp(log_step)[:, None]
    mag = jnp.exp(lr * step)
    ar = mag * jnp.cos(li * step)
    ai = mag * jnp.sin(li * step)
    inv = 1.0 / (lr * lr + li * li)
    zr = ((ar - 1.0) * lr + ai * li) * inv
    zi = (ai * lr - (ar - 1.0) * li) * inv
    bbr = zr[..., None] * b_re - zi[..., None] * b_im
    bbi = zr[..., None] * b_im + zi[..., None] * b_re
    eye = jnp.eye(gh, dtype=F32)

    def expand_in(bb):
        bb = bb.reshape(S5_HALF, gh, S5_STATE, S5_GROUP_CH)
        return jnp.einsum('fgph,gk->fghkp', bb, eye).reshape(
            S5_HALF, gh * S5_GROUP_CH, gh * S5_STATE).astype(BF16)

    def expand_out(c):
        c = c.reshape(S5_HALF, gh, S5_GROUP_CH, S5_STATE)
        return jnp.einsum('fghp,gk->fgpkh', c, eye).reshape(
            S5_HALF, gh * S5_STATE, gh * S5_GROUP_CH).astype(BF16)

    fixed2 = lambda i: (0, 0)
    fixed3 = lambda i: (0, 0, 0)
    return pl.pallas_call(
        _s5_kernel,
        grid=(seq // tc,),
        in_specs=[pl.BlockSpec((tc, bsz, w), lambda i: (i, 0, 2)),
                  pl.BlockSpec((S5_HALF, w // S5_HALF, nstate // S5_HALF), fixed3),
                  pl.BlockSpec((S5_HALF, w // S5_HALF, nstate // S5_HALF), fixed3),
                  pl.BlockSpec((1, nstate), fixed2), pl.BlockSpec((1, nstate), fixed2),
                  pl.BlockSpec((S5_HALF, nstate // S5_HALF, w // S5_HALF), fixed3),
                  pl.BlockSpec((S5_HALF, nstate // S5_HALF, w // S5_HALF), fixed3),
                  pl.BlockSpec((1, w), fixed2), pl.BlockSpec((w, w), fixed2), pl.BlockSpec((1, w), fixed2)],
        out_specs=pl.BlockSpec((tc, bsz, w), lambda i: (i, 0, 0)),
        out_shape=jax.ShapeDtypeStruct((seq, bsz, w), BF16),
        scratch_shapes=[pltpu.VMEM((tc, bsz, nstate), F32), pltpu.VMEM((tc, bsz, nstate), F32),
                        pltpu.VMEM((bsz, nstate), F32), pltpu.VMEM((bsz, nstate), F32)],
        compiler_params=_cparams("arbitrary"),
        name="s5_glu",
    )(proj, expand_in(bbr), expand_in(bbi), ar.reshape(1, nstate), ai.reshape(1, nstate),
      expand_out(c_re), expand_out(c_im), d_skip.reshape(1, w), glu_w.astype(BF16), glu_b.reshape(1, w))


def _cd_prep_kernel(xc_ref, q_ref, k_ref, pos_ref, cw_ref, cb_ref, wq_ref, wk_ref, wv_ref, wif_ref,
                    bif_ref, invf_ref, mq_ref, mk_ref, mv_ref, gt_ref, qd_ref, kd_ref, tail_ref):
    tl, w = xc_ref.shape
    pad = tail_ref.shape[0]

    @pl.when(pl.program_id(1) == 0)
    def _init():
        tail_ref[...] = jnp.zeros_like(tail_ref)

    xc = xc_ref[...]
    ext = jnp.concatenate([tail_ref[...], xc], axis=0)
    conv = cb_ref[...]
    for tap in range(CONV_WIDTH):
        off = pad - (CONV_WIDTH - 1) + tap
        conv = conv + ext[off:off + tl] * cw_ref[tap]
    tail_ref[...] = xc[tl - pad:]
    xconv = (conv * jax.nn.sigmoid(conv)).astype(BF16)
    q = _dot(xconv, wq_ref[...]).astype(BF16)
    k = (_dot(xconv, wk_ref[...]) * (MLSTM_HEAD_DIM ** -0.5)).astype(BF16)
    v = _dot(xc.astype(BF16), wv_ref[...]).astype(BF16)
    mq_ref[...] = q
    mk_ref[...] = k
    mv_ref[...] = v
    gt_ref[...] = (_dot(q, wif_ref[0]) + _dot(k, wif_ref[1]) + _dot(v, wif_ref[2]) + bif_ref[...])

    ang = pos_ref[...].astype(F32) * invf_ref[...]
    cos = jnp.cos(ang)
    sin = jnp.sin(ang)
    lane = lax.broadcasted_iota(jnp.int32, (1, LANES), 1)
    half = ROPE_DIMS // 2
    sin_lo = jnp.where(lane < half, -sin, 0.0)
    sin_hi = jnp.where((lane >= half) & (lane < ROPE_DIMS), sin, 0.0)
    for src, dst in ((q_ref, qd_ref), (k_ref, kd_ref)):
        for h in range(DSW_HEADS):
            sl = slice(h * DSW_HEAD_DIM, (h + 1) * DSW_HEAD_DIM)
            t = src[:, sl]
            rot = (t * cos + pltpu.roll(t, LANES - half, axis=1) * sin_lo
                   + pltpu.roll(t, half, axis=1) * sin_hi)
            dst[:, sl] = rot


def _cd_prep(proj, positions, conv_w, conv_b, w_q, w_k, w_v, w_if, b_if):
    bsz, seq, _ = proj.shape
    w = C_WIDTH
    tl = PROJ_TL
    half = ROPE_DIMS // 2
    inv_freq = ROPE_THETA ** (-jnp.arange(half, dtype=F32) / half)
    invf = jnp.zeros((1, LANES), F32).at[0, :half].set(inv_freq).at[0, half:ROPE_DIMS].set(inv_freq)
    wif = jnp.zeros((3, w, LANES), F32).at[:, :, :2 * MLSTM_HEADS].set(w_if.reshape(3, w, 2 * MLSTM_HEADS))
    bif = jnp.zeros((1, LANES), F32).at[0, :2 * MLSTM_HEADS].set(b_if)
    col = lambda c: (lambda b, i: (b, i, c))
    fixed2 = lambda b, i: (0, 0)
    fixed3 = lambda b, i: (0, 0, 0)
    tok_bf = jax.ShapeDtypeStruct((bsz, seq, w), BF16)
    return pl.pallas_call(
        _cd_prep_kernel,
        grid=(bsz, seq // tl),
        in_specs=[pl.BlockSpec((None, tl, w), col(0)), pl.BlockSpec((None, tl, w), col(2)),
                  pl.BlockSpec((None, tl, w), col(3)), pl.BlockSpec((None, tl, 1), col(0)),
                  pl.BlockSpec((CONV_WIDTH, 1, w), fixed3), pl.BlockSpec((1, w), fixed2),
                  pl.BlockSpec((w, w), fixed2), pl.BlockSpec((w, w), fixed2), pl.BlockSpec((w, w), fixed2),
                  pl.BlockSpec((3, w, LANES), fixed3), pl.BlockSpec((1, LANES), fixed2),
                  pl.BlockSpec((1, LANES), fixed2)],
        out_specs=[pl.BlockSpec((None, tl, w), col(0))] * 3 + [pl.BlockSpec((None, tl, LANES), col(0))]
                  + [pl.BlockSpec((None, tl, w), col(0))] * 2,
        out_shape=[tok_bf, tok_bf, tok_bf, jax.ShapeDtypeStruct((bsz, seq, LANES), F32),
                   jax.ShapeDtypeStruct((bsz, seq, w), F32), jax.ShapeDtypeStruct((bsz, seq, w), F32)],
        scratch_shapes=[pltpu.VMEM((8, w), F32)],
        compiler_params=_cparams("arbitrary", "arbitrary"),
        name="cd_prep",
    )(proj, proj, proj, positions.reshape(bsz, seq, 1), conv_w.reshape(CONV_WIDTH, 1, w),
      conv_b.reshape(1, w), _block_diag(w_q).astype(BF16), _block_diag(w_k).astype(BF16),
      _block_diag(w_v).astype(BF16), wif.astype(BF16), bif, invf)


def _cumsum_rows(x):
    n = x.shape[0]
    row = lax.broadcasted_iota(jnp.int32, x.shape, 0)
    shift = 1
    while shift < n:
        x = x + jnp.where(row >= shift, pltpu.roll(x, shift, axis=0), 0.0)
        shift *= 2
    return x


def _mlstm_kernel(q_ref, k_ref, v_ref, g_ref, op_ref, o_ref, c_ref, n_ref, m_ref):
    cs = q_ref.shape[0]
    dh = MLSTM_HEAD_DIM

    @pl.when(pl.program_id(1) == 0)
    def _init():
        c_ref[...] = jnp.zeros_like(c_ref)
        n_ref[...] = jnp.zeros_like(n_ref)
        m_ref[...] = jnp.full_like(m_ref, NEG_INF)

    gates = g_ref[...]
    cum = _cumsum_rows(jax.nn.log_sigmoid(gates))
    gates_t = gates.T
    cum_t = cum.T
    causal = (lax.broadcasted_iota(jnp.int32, (cs, cs), 0) >= lax.broadcasted_iota(jnp.int32, (cs, cs), 1))
    for h in range(MLSTM_HEADS):
        sl = slice(h * dh, (h + 1) * dh)
        fcol = MLSTM_HEADS + h
        li_c, li_r = gates[:, h:h + 1], gates_t[h:h + 1, :]
        cum_c, cum_r = cum[:, fcol:fcol + 1], cum_t[fcol:fcol + 1, :]
        m_prev = m_ref[h][:, 0:1]
        q, k, v = q_ref[:, sl], k_ref[:, sl], v_ref[:, sl]
        log_d = jnp.where(causal, cum_c - cum_r + li_r, NEG_INF)
        log_inter = cum_c + m_prev
        m_s = jnp.maximum(jnp.max(log_d, axis=1, keepdims=True), log_inter)
        s = _dot_nt(q, k) * jnp.exp(log_d - m_s)
        inter = jnp.exp(log_inter - m_s)
        c_prev = c_ref[h]
        n_prev = n_ref[h]
        num = _dot(s.astype(BF16), v) + inter * _dot(q, c_prev.astype(BF16))
        den = (jnp.sum(s, axis=1, keepdims=True)
               + inter * jnp.sum(q.astype(F32) * n_prev, axis=1, keepdims=True))
        hh = num / jnp.maximum(jnp.abs(den), jnp.exp(-m_s))
        o_ref[:, sl] = (jax.nn.sigmoid(op_ref[:, sl]) * hh).astype(o_ref.dtype)
        chunk_f = cum_c[cs - 1:cs, :]
        to_end = chunk_f - cum_c + li_c
        m_c = jnp.max(to_end, axis=0, keepdims=True)
        m_new = jnp.maximum(chunk_f + m_prev, m_c)
        s_old = jnp.exp(chunk_f + m_prev - m_new)
        s_new = jnp.exp(m_c - m_new)
        kw = k.astype(F32) * jnp.exp(to_end - m_c)
        c_ref[h] = s_old * c_prev + s_new * _dot(kw.T.astype(BF16), v)
        n_ref[h] = s_old * n_prev + s_new * jnp.sum(kw, axis=0, keepdims=True)
        m_ref[h] = jnp.broadcast_to(m_new, (1, LANES))


def _mlstm(mq, mk, mv, gates, proj):
    bsz, seq, w = mq.shape
    cs = MLSTM_CS
    tok = lambda b, i: (b, i, 0)
    return pl.pallas_call(
        _mlstm_kernel,
        grid=(bsz, seq // cs),
        in_specs=[pl.BlockSpec((None, cs, w), tok)] * 3
                 + [pl.BlockSpec((None, cs, LANES), tok), pl.BlockSpec((None, cs, w), lambda b, i: (b, i, 1))],
        out_specs=pl.BlockSpec((None, cs, w), tok),
        out_shape=jax.ShapeDtypeStruct((bsz, seq, w), BF16),
        scratch_shapes=[pltpu.VMEM((MLSTM_HEADS, MLSTM_HEAD_DIM, MLSTM_HEAD_DIM), F32),
                        pltpu.VMEM((MLSTM_HEADS, 1, MLSTM_HEAD_DIM), F32),
                        pltpu.VMEM((MLSTM_HEADS, 1, LANES), F32)],
        compiler_params=_cparams("arbitrary", "arbitrary"),
        name="mlstm",
    )(mq, mk, mv, gates, proj)


def _dsw_kernel(q_ref, k_ref, v_ref, y_ref, o_scr, lse_scr):
    seq, dh = q_ref.shape
    blk = DSW_BLOCK
    qi = lax.broadcasted_iota(jnp.int32, (blk, 2 * blk), 0)
    ki = lax.broadcasted_iota(jnp.int32, (blk, 2 * blk), 1)
    dist = qi + blk - ki
    scale = dh ** -0.5
    units = seq // blk
    for g, (window, dil) in enumerate(DSW_CONFIGS):
        nsub = seq // dil // blk
        band = (dist >= 0) & (dist <= window // dil)

        def unit(u, carry, g=g, dil=dil, nsub=nsub, band=band):
            r = u // nsub
            sb = u % nsub
            cur = pl.ds(r + sb * (blk * dil), blk, stride=dil)
            prev = pl.ds(r + jnp.maximum(sb - 1, 0) * (blk * dil), blk, stride=dil)
            kk = jnp.concatenate([k_ref[prev, :], k_ref[cur, :]], axis=0).astype(BF16)
            vv = jnp.concatenate([v_ref[prev, :], v_ref[cur, :]], axis=0).astype(BF16)
            s = _dot_nt(q_ref[cur, :].astype(BF16), kk) * scale
            s = jnp.where(band & ((ki >= blk) | (sb > 0)), s, NEG_INF)
            m = jnp.max(s, axis=1, keepdims=True)
            p = jnp.exp(s - m)
            l = jnp.sum(p, axis=1, keepdims=True)
            o_scr[g, cur, :] = _dot(p.astype(BF16), vv) / l
            lse_scr[g, cur, :] = jnp.broadcast_to(m + jnp.log(l), (blk, dh))
            return carry

        lax.fori_loop(0, units, unit, 0, unroll=32)

    lses = [lse_scr[g] for g in range(len(DSW_CONFIGS))]
    mx = functools.reduce(jnp.maximum, lses)
    ws = [jnp.exp(l - mx) for l in lses]
    acc = sum(ws[g] * o_scr[g] for g in range(len(DSW_CONFIGS)))
    y_ref[...] = (acc / sum(ws)).astype(y_ref.dtype)


def _dsw_attention(qd, kd, proj):
    bsz, seq, w = qd.shape
    dh = DSW_HEAD_DIM
    v_col = (proj.shape[-1] - w) // dh
    head = lambda b, h: (b, 0, h)
    return pl.pallas_call(
        _dsw_kernel,
        grid=(bsz, DSW_HEADS),
        in_specs=[pl.BlockSpec((None, seq, dh), head), pl.BlockSpec((None, seq, dh), head),
                  pl.BlockSpec((None, seq, dh), lambda b, h: (b, 0, v_col + h))],
        out_specs=pl.BlockSpec((None, seq, dh), head),
        out_shape=jax.ShapeDtypeStruct((bsz, seq, w), BF16),
        scratch_shapes=[pltpu.VMEM((len(DSW_CONFIGS), seq, dh), F32),
                        pltpu.VMEM((len(DSW_CONFIGS), seq, dh), F32)],
        compiler_params=_cparams("arbitrary", "arbitrary"),
        name="dsw_attention",
    )(qd, kd, proj)


def kernel(x, positions, ab_w_in, ab_conv_w, ab_conv_b, ab_gate_r_w, ab_gate_r_b, ab_gate_i_w, ab_gate_i_b, ab_lru_lambda, ab_s5_lambda_re, ab_s5_lambda_im, ab_s5_log_step, ab_s5_b_re, ab_s5_b_im, ab_s5_c_re, ab_s5_c_im, ab_s5_d, ab_glu_w, ab_glu_b, ab_w_out, cd_w_in, cd_conv_w, cd_conv_b, cd_w_q, cd_w_k, cd_w_v, cd_w_if, cd_b_if, cd_w_out, ln_mix_g, ln_mix_b, ln_ffn_g, ln_ffn_b, moe_router_w, moe_router_b, moe_w_in, moe_b_in, moe_w_out, moe_b_out):
    bsz, seq, d = x.shape
    t = bsz * seq
    for layer in range(DEPTH):
        j = layer // 2
        if layer % 2 == 0:
            proj = _inproj_time_major(x, ab_w_in[j].astype(BF16))
            ya = _rg_lru(proj, ab_conv_w[j], ab_conv_b[j], ab_gate_r_w[j], ab_gate_r_b[j],
                         ab_gate_i_w[j], ab_gate_i_b[j], ab_lru_lambda[j])
            yb = _s5_glu(proj, ab_s5_lambda_re[j], ab_s5_lambda_im[j], ab_s5_log_step[j], ab_s5_b_re[j],
                         ab_s5_b_im[j], ab_s5_c_re[j], ab_s5_c_im[j], ab_s5_d[j], ab_glu_w[j], ab_glu_b[j])
            time_major, w_out = True, ab_w_out[j]
        else:
            proj = _inproj(x, cd_w_in[j].astype(BF16))
            mq, mk, mv, gates, qd, kd = _cd_prep(proj, positions, cd_conv_w[j], cd_conv_b[j], cd_w_q[j],
                                                 cd_w_k[j], cd_w_v[j], cd_w_if[j], cd_b_if[j])
            ya = _mlstm(mq, mk, mv, gates, proj)
            yb = _dsw_attention(qd, kd, proj)
            time_major, w_out = False, cd_w_out[j]
        xn, x_tiles, route, counts = _outproj_ln_router(
            ya, yb, time_major, w_out.astype(BF16), x, ln_mix_g[layer], ln_mix_b[layer],
            moe_router_w[layer], moe_router_b[layer])
        x = _moe_ffn_ln(xn.reshape(t, d), x_tiles, route, counts, layer, moe_w_in, moe_b_in,
                        moe_w_out, moe_b_out, ln_ffn_g[layer], ln_ffn_b[layer]).reshape(bsz, seq, d)
    return x
```
